```python
import jax, jax.numpy as jnp
from jax import lax
import numpy as np

D_MODEL = 4096
BATCH = 8
SEQ = 4096
DEPTH = 1

MIX_WIDTH = D_MODEL
HEAD_DIM = 128
A_WIDTH = MIX_WIDTH // 2
A_HEADS = A_WIDTH // HEAD_DIM
A_KEY_DIM = 128
A_KEY_WIDTH = A_HEADS * A_KEY_DIM
B_WIDTH = MIX_WIDTH - A_WIDTH
B_GROUP_DIM = 128
B_GROUPS = B_WIDTH // B_GROUP_DIM
GMLP_CHUNK = 128
GLA_CHUNK = 64
IN_COLS = 2 * A_KEY_WIDTH + 2 * A_WIDTH + 2 * B_WIDTH
D_FF = ((8 * D_MODEL + 3 * 256 - 1) // (3 * 256)) * 256
PLE_DIM = 256
EPS = 1e-6

kernel_name = "hymba_hgrn2_gmlp_hybrid"


def _rmsnorm(x, w):
    xf = x.astype(jnp.float32)
    y = xf * lax.rsqrt(jnp.mean(xf * xf, axis=-1, keepdims=True) + EPS)
    return (y * w.astype(jnp.float32)).astype(x.dtype)


def _hgrn2(q, f_pre, i_in, g, lb, norm_w):
    bsz, t, _ = q.shape
    n = t // GLA_CHUNK
    f32 = jnp.float32
    lbf = lb.astype(f32)
    qf = jax.nn.silu(q.astype(f32))
    f = lbf + (1.0 - lbf) * jax.nn.sigmoid(f_pre.astype(f32))
    kf = 1.0 - f
    logf = jnp.log(jnp.maximum(f, 1e-30))

    def to_chunks(a, d):
        return a.reshape(bsz, n, GLA_CHUNK, A_HEADS, d).transpose(1, 0, 3, 2, 4)

    qc = to_chunks(qf, A_KEY_DIM)
    kc = to_chunks(kf, A_KEY_DIM)
    vc = to_chunks(i_in.astype(f32), HEAD_DIM)
    bc = jnp.cumsum(to_chunks(logf, A_KEY_DIM), axis=3)
    causal = jnp.tril(jnp.ones((GLA_CHUNK, GLA_CHUNK), dtype=bool))[:, :, None]

    def step(state, inp):
        q_c, k_c, v_c, b_c = inp
        inter = jnp.einsum('bhtk,bhkv->bhtv', q_c * jnp.exp(b_c), state)
        diff = b_c[:, :, :, None, :] - b_c[:, :, None, :, :]
        decay = jnp.exp(jnp.where(causal, diff, -jnp.inf))
        scores = jnp.einsum('bhtk,bhsk,bhtsk->bhts', q_c, k_c, decay)
        intra = jnp.einsum('bhts,bhsv->bhtv', scores, v_c)
        b_end = b_c[:, :, -1:, :]
        new_state = (jnp.exp(b_end[:, :, 0, :])[..., None] * state
                     + jnp.einsum('bhsk,bhsv->bhkv', k_c * jnp.exp(b_end - b_c), v_c))
        return new_state, inter + intra

    s0 = jnp.zeros((bsz, A_HEADS, A_KEY_DIM, HEAD_DIM), f32)
    _, o = lax.scan(step, s0, (qc, kc, vc, bc))
    o = o.transpose(1, 0, 3, 2, 4).reshape(bsz, t, A_HEADS, HEAD_DIM)
    o = o * lax.rsqrt(jnp.mean(o * o, axis=-1, keepdims=True) + EPS)
    o = o.reshape(bsz, t, A_WIDTH) * norm_w.astype(f32) * jax.nn.silu(g.astype(f32))
    return o.astype(q.dtype)


def _gmlp(u, v, ln_w, ln_b, w_s, b_s):
    bsz, t, _ = u.shape
    n = t // GMLP_CHUNK
    f32 = jnp.float32
    uf = jax.nn.gelu(u.astype(f32), approximate=False)
    vf = jax.nn.gelu(v.astype(f32), approximate=False)
    mu = jnp.mean(vf, axis=-1, keepdims=True)
    var = jnp.mean(jnp.square(vf - mu), axis=-1, keepdims=True)
    vf = (vf - mu) * lax.rsqrt(var + EPS) * ln_w.astype(f32) + ln_b.astype(f32)
    vc = vf.reshape(bsz, n, GMLP_CHUNK, B_GROUPS, B_GROUP_DIM)
    tril = jnp.tril(jnp.ones((GMLP_CHUNK, GMLP_CHUNK), f32))
    w = w_s.astype(f32) * tril
    z = jnp.einsum('gts,bnsgd->bntgd', w, vc) + b_s.astype(f32).T[None, None, :, :, None]
    return (uf * z.reshape(bsz, t, B_WIDTH)).astype(u.dtype)


def _fwd_setup_inputs(seed: int = 0) -> dict:
    key = jax.random.key(seed)
    ks = jax.random.split(key, 20)
    nrm = jax.random.normal
    f32 = jnp.float32

    def gain(k, shape):
        return 1.0 + 0.01 * nrm(k, shape, f32)

    return {
        "x": nrm(ks[0], (BATCH, SEQ, D_MODEL), f32),
        "p": nrm(ks[1], (DEPTH, BATCH, SEQ, PLE_DIM), f32),
        "pre_mix_w": gain(ks[2], (DEPTH, D_MODEL)),
        "w_in": nrm(ks[3], (DEPTH, D_MODEL, IN_COLS), f32) * D_MODEL ** -0.5,
        "lb_param": nrm(ks[4], (DEPTH + 1, A_KEY_WIDTH), f32) * 0.5,
        "a_norm_w": gain(ks[5], (DEPTH, A_WIDTH)),
        "gmlp_ln_w": gain(ks[6], (DEPTH, B_WIDTH)),
        "gmlp_ln_b": 0.01 * nrm(ks[7], (DEPTH, B_WIDTH), f32),
        "w_spatial": nrm(ks[8], (DEPTH, B_GROUPS, GMLP_CHUNK, GMLP_CHUNK), f32) * GMLP_CHUNK ** -0.5,
        "b_spatial": gain(ks[9], (DEPTH, B_GROUPS, GMLP_CHUNK)),
        "w_out": nrm(ks[10], (DEPTH, MIX_WIDTH, D_MODEL), f32) * MIX_WIDTH ** -0.5,
        "post_mix_w": gain(ks[11], (DEPTH, D_MODEL)),
        "pre_ffn_w": gain(ks[12], (DEPTH, D_MODEL)),
        "w_gate": nrm(ks[13], (DEPTH, D_MODEL, D_FF), f32) * D_MODEL ** -0.5,
        "w_up": nrm(ks[14], (DEPTH, D_MODEL, D_FF), f32) * D_MODEL ** -0.5,
        "w_down": nrm(ks[15], (DEPTH, D_FF, D_MODEL), f32) * D_FF ** -0.5,
        "post_ffn_w": gain(ks[16], (DEPTH, D_MODEL)),
        "w_ple": nrm(ks[17], (DEPTH, PLE_DIM, D_MODEL), f32) * PLE_DIM ** -0.5,
        "w_ple_gate": nrm(ks[18], (DEPTH, D_MODEL, D_MODEL), f32) * D_MODEL ** -0.5,
        "post_ple_w": gain(ks[19], (DEPTH, D_MODEL)),
    }


def _fwd_reference(x, p, pre_mix_w, w_in, lb_param, a_norm_w, gmlp_ln_w, gmlp_ln_b, w_spatial, b_spatial,
              w_out, post_mix_w, pre_ffn_w, w_gate, w_up, w_down, post_ffn_w, w_ple, w_ple_gate, post_ple_w):
    lower_bounds = jnp.cumsum(jax.nn.softmax(lb_param.astype(jnp.float32), axis=0), axis=0)
    splits = [A_KEY_WIDTH,
              2 * A_KEY_WIDTH,
              2 * A_KEY_WIDTH + A_WIDTH,
              2 * A_KEY_WIDTH + 2 * A_WIDTH,
              2 * A_KEY_WIDTH + 2 * A_WIDTH + B_WIDTH]
    for l in range(DEPTH):
        h = _rmsnorm(x, pre_mix_w[l])
        proj = h @ w_in[l]
        q, f_pre, i_in, g, u, v = jnp.split(proj, splits, axis=-1)
        a_out = _hgrn2(q, f_pre, i_in, g, lower_bounds[l], a_norm_w[l])
        b_out = _gmlp(u, v, gmlp_ln_w[l], gmlp_ln_b[l], w_spatial[l], b_spatial[l])
        mix = jnp.concatenate([a_out, b_out], axis=-1) @ w_out[l]
        x = x + _rmsnorm(mix, post_mix_w[l])
        h = _rmsnorm(x, pre_ffn_w[l])
        ff = (jax.nn.silu(h @ w_gate[l]) * (h @ w_up[l])) @ w_down[l]
        x = x + _rmsnorm(ff, post_ffn_w[l])
        gate = jax.nn.sigmoid(x @ w_ple_gate[l])
        x = x + _rmsnorm((p[l] @ w_ple[l]) * gate, post_ple_w[l])
    return x


import jax as _jax
import jax.numpy as _jnp

TWIN_FORMAT = 'train_step'
FWD_PARAMS = ['x', 'p', 'pre_mix_w', 'w_in', 'lb_param', 'a_norm_w', 'gmlp_ln_w', 'gmlp_ln_b', 'w_spatial', 'b_spatial', 'w_out', 'post_mix_w', 'pre_ffn_w', 'w_gate', 'w_up', 'w_down', 'post_ffn_w', 'w_ple', 'w_ple_gate', 'post_ple_w']
TWIN_WEIGHTS = ['pre_mix_w', 'w_in', 'lb_param', 'a_norm_w', 'gmlp_ln_w', 'gmlp_ln_b', 'w_spatial', 'b_spatial', 'w_out', 'post_mix_w', 'pre_ffn_w', 'w_gate', 'w_up', 'w_down', 'post_ffn_w', 'w_ple', 'w_ple_gate', 'post_ple_w']
TWIN_DIFF_INPUT = 'x'
TWIN_INPUTS = ['x', 'p', 'pre_mix_w', 'w_in', 'lb_param', 'a_norm_w', 'gmlp_ln_w', 'gmlp_ln_b', 'w_spatial', 'b_spatial', 'w_out', 'post_mix_w', 'pre_ffn_w', 'w_gate', 'w_up', 'w_down', 'post_ffn_w', 'w_ple', 'w_ple_gate', 'post_ple_w', 'loss_target', 'm_pre_mix_w', 'm_w_in', 'm_lb_param', 'm_a_norm_w', 'm_gmlp_ln_w', 'm_gmlp_ln_b', 'm_w_spatial', 'm_b_spatial', 'm_w_out', 'm_post_mix_w', 'm_pre_ffn_w', 'm_w_gate', 'm_w_up', 'm_w_down', 'm_post_ffn_w', 'm_w_ple', 'm_w_ple_gate', 'm_post_ple_w', 'v_pre_mix_w', 'v_w_in', 'v_lb_param', 'v_a_norm_w', 'v_gmlp_ln_w', 'v_gmlp_ln_b', 'v_w_spatial', 'v_b_spatial', 'v_w_out', 'v_post_mix_w', 'v_pre_ffn_w', 'v_w_gate', 'v_w_up', 'v_w_down', 'v_post_ffn_w', 'v_w_ple', 'v_w_ple_gate', 'v_post_ple_w']
TWIN_OUTPUTS = ['loss', 'grad_x', 'grad_pre_mix_w', 'grad_w_in', 'grad_lb_param', 'grad_a_norm_w', 'grad_gmlp_ln_w', 'grad_gmlp_ln_b', 'grad_w_spatial', 'grad_b_spatial', 'grad_w_out', 'grad_post_mix_w', 'grad_pre_ffn_w', 'grad_w_gate', 'grad_w_up', 'grad_w_down', 'grad_post_ffn_w', 'grad_w_ple', 'grad_w_ple_gate', 'grad_post_ple_w', 'delta_pre_mix_w', 'delta_w_in', 'delta_lb_param', 'delta_a_norm_w', 'delta_gmlp_ln_w', 'delta_gmlp_ln_b', 'delta_w_spatial', 'delta_b_spatial', 'delta_w_out', 'delta_post_mix_w', 'delta_pre_ffn_w', 'delta_w_gate', 'delta_w_up', 'delta_w_down', 'delta_post_ffn_w', 'delta_w_ple', 'delta_w_ple_gate', 'delta_post_ple_w', 'new_m_pre_mix_w', 'new_m_w_in', 'new_m_lb_param', 'new_m_a_norm_w', 'new_m_gmlp_ln_w', 'new_m_gmlp_ln_b', 'new_m_w_spatial', 'new_m_b_spatial', 'new_m_w_out', 'new_m_post_mix_w', 'new_m_pre_ffn_w', 'new_m_w_gate', 'new_m_w_up', 'new_m_w_down', 'new_m_post_ffn_w', 'new_m_w_ple', 'new_m_w_ple_gate', 'new_m_post_ple_w', 'new_v_pre_mix_w', 'new_v_w_in', 'new_v_lb_param', 'new_v_a_norm_w', 'new_v_gmlp_ln_w', 'new_v_gmlp_ln_b', 'new_v_w_spatial', 'new_v_b_spatial', 'new_v_w_out', 'new_v_post_mix_w', 'new_v_pre_ffn_w', 'new_v_w_gate', 'new_v_w_up', 'new_v_w_down', 'new_v_post_ffn_w', 'new_v_w_ple', 'new_v_w_ple_gate', 'new_v_post_ple_w']
TWIN_LEAF_KINDS = {'loss': 'loss', 'grad_x': 'grad_x', 'grad_pre_mix_w': 'grad_w', 'grad_w_in': 'grad_w', 'grad_lb_param': 'grad_w', 'grad_a_norm_w': 'grad_w', 'grad_gmlp_ln_w': 'grad_w', 'grad_gmlp_ln_b': 'grad_w', 'grad_w_spatial': 'grad_w', 'grad_b_spatial': 'grad_w', 'grad_w_out': 'grad_w', 'grad_post_mix_w': 'grad_w', 'grad_pre_ffn_w': 'grad_w', 'grad_w_gate': 'grad_w', 'grad_w_up': 'grad_w', 'grad_w_down': 'grad_w', 'grad_post_ffn_w': 'grad_w', 'grad_w_ple': 'grad_w', 'grad_w_ple_gate': 'grad_w', 'grad_post_ple_w': 'grad_w', 'delta_pre_mix_w': 'delta_w', 'delta_w_in': 'delta_w', 'delta_lb_param': 'delta_w', 'delta_a_norm_w': 'delta_w', 'delta_gmlp_ln_w': 'delta_w', 'delta_gmlp_ln_b': 'delta_w', 'delta_w_spatial': 'delta_w', 'delta_b_spatial': 'delta_w', 'delta_w_out': 'delta_w', 'delta_post_mix_w': 'delta_w', 'delta_pre_ffn_w': 'delta_w', 'delta_w_gate': 'delta_w', 'delta_w_up': 'delta_w', 'delta_w_down': 'delta_w', 'delta_post_ffn_w': 'delta_w', 'delta_w_ple': 'delta_w', 'delta_w_ple_gate': 'delta_w', 'delta_post_ple_w': 'delta_w', 'new_m_pre_mix_w': 'new_m', 'new_m_w_in': 'new_m', 'new_m_lb_param': 'new_m', 'new_m_a_norm_w': 'new_m', 'new_m_gmlp_ln_w': 'new_m', 'new_m_gmlp_ln_b': 'new_m', 'new_m_w_spatial': 'new_m', 'new_m_b_spatial': 'new_m', 'new_m_w_out': 'new_m', 'new_m_post_mix_w': 'new_m', 'new_m_pre_ffn_w': 'new_m', 'new_m_w_gate': 'new_m', 'new_m_w_up': 'new_m', 'new_m_w_down': 'new_m', 'new_m_post_ffn_w': 'new_m', 'new_m_w_ple': 'new_m', 'new_m_w_ple_gate': 'new_m', 'new_m_post_ple_w': 'new_m', 'new_v_pre_mix_w': 'new_v', 'new_v_w_in': 'new_v', 'new_v_lb_param': 'new_v', 'new_v_a_norm_w': 'new_v', 'new_v_gmlp_ln_w': 'new_v', 'new_v_gmlp_ln_b': 'new_v', 'new_v_w_spatial': 'new_v', 'new_v_b_spatial': 'new_v', 'new_v_w_out': 'new_v', 'new_v_post_mix_w': 'new_v', 'new_v_pre_ffn_w': 'new_v', 'new_v_w_gate': 'new_v', 'new_v_w_up': 'new_v', 'new_v_w_down': 'new_v', 'new_v_post_ffn_w': 'new_v', 'new_v_w_ple': 'new_v', 'new_v_w_ple_gate': 'new_v', 'new_v_post_ple_w': 'new_v'}


def _forward(args):
    return _fwd_reference(*[args[k] for k in FWD_PARAMS])


def _output_shape():
    out = _jax.eval_shape(lambda: _forward(_fwd_setup_inputs(0)))
    return out.shape, out.dtype

N_MICROBATCH = 1
ADAM_LR = 0.001
ADAM_B1 = 0.9
ADAM_B2 = 0.999
ADAM_EPS = 1e-08
ADAM_WD = 0.01
ADAM_STEP = 10
PER_EXAMPLE_BATCH_AXIS = {'x': 0, 'p': 1, 'loss_target': 0}
SHARED_INPUTS = []
_WEIGHT_DTYPES = {'pre_mix_w': _jnp.float32, 'w_in': _jnp.float32, 'lb_param': _jnp.float32, 'a_norm_w': _jnp.float32, 'gmlp_ln_w': _jnp.float32, 'gmlp_ln_b': _jnp.float32, 'w_spatial': _jnp.float32, 'b_spatial': _jnp.float32, 'w_out': _jnp.float32, 'post_mix_w': _jnp.float32, 'pre_ffn_w': _jnp.float32, 'w_gate': _jnp.float32, 'w_up': _jnp.float32, 'w_down': _jnp.float32, 'post_ffn_w': _jnp.float32, 'w_ple': _jnp.float32, 'w_ple_gate': _jnp.float32, 'post_ple_w': _jnp.float32}
MOMENT_SCALE = {'pre_mix_w': 1.762974e-01, 'w_in': 1.023335e-01, 'lb_param': 1.037771e-02, 'a_norm_w': 1.233151e-01, 'gmlp_ln_w': 8.514783e-02, 'gmlp_ln_b': 8.473430e-02, 'w_spatial': 8.432201e-02, 'b_spatial': 1.239549e-01, 'w_out': 2.121810e-01, 'post_mix_w': 8.047315e+00, 'pre_ffn_w': 1.768917e-01, 'w_gate': 6.663029e-02, 'w_up': 9.080710e-02, 'w_down': 1.476755e-01, 'post_ffn_w': 7.977495e+00, 'w_ple': 8.848665e-02, 'w_ple_gate': 4.778077e-02, 'post_ple_w': 8.091838e+00}


def _to_microbatches(a, axis):
    t = _jnp.moveaxis(a, axis, 0)
    t = t.reshape((N_MICROBATCH, t.shape[0] // N_MICROBATCH) + t.shape[1:])
    return _jnp.moveaxis(t, 1, axis + 1)


def setup_inputs(seed: int = 0) -> dict:
    inp = _fwd_setup_inputs(seed)
    key = _jax.random.fold_in(_jax.random.key(seed), 7919)
    shape, _ = _output_shape()
    out = dict(inp)
    out["loss_target"] = _jax.random.normal(_jax.random.fold_in(key, 0), shape, _jnp.float32)
    for i, name in enumerate(TWIN_WEIGHTS):
        w = inp[name].astype(_jnp.float32)
        if MOMENT_SCALE is None:
            s = _jnp.sqrt(_jnp.mean(_jnp.square(w)) + 1e-30)
        else:
            s = MOMENT_SCALE[name]
        km, kv = _jax.random.split(_jax.random.fold_in(key, i + 1))
        out[name] = w
        out["m_" + name] = s * _jax.random.normal(km, w.shape, _jnp.float32)
        out["v_" + name] = (s * s) * _jax.random.uniform(kv, w.shape, _jnp.float32, 0.5, 1.5)
    if N_MICROBATCH > 1:
        for name, axis in PER_EXAMPLE_BATCH_AXIS.items():
            out[name] = _to_microbatches(out[name], axis)
    return {'x': out['x'], 'p': out['p'], 'pre_mix_w': out['pre_mix_w'], 'w_in': out['w_in'], 'lb_param': out['lb_param'], 'a_norm_w': out['a_norm_w'], 'gmlp_ln_w': out['gmlp_ln_w'], 'gmlp_ln_b': out['gmlp_ln_b'], 'w_spatial': out['w_spatial'], 'b_spatial': out['b_spatial'], 'w_out': out['w_out'], 'post_mix_w': out['post_mix_w'], 'pre_ffn_w': out['pre_ffn_w'], 'w_gate': out['w_gate'], 'w_up': out['w_up'], 'w_down': out['w_down'], 'post_ffn_w': out['post_ffn_w'], 'w_ple': out['w_ple'], 'w_ple_gate': out['w_ple_gate'], 'post_ple_w': out['post_ple_w'], 'loss_target': out['loss_target'], 'm_pre_mix_w': out['m_pre_mix_w'], 'm_w_in': out['m_w_in'], 'm_lb_param': out['m_lb_param'], 'm_a_norm_w': out['m_a_norm_w'], 'm_gmlp_ln_w': out['m_gmlp_ln_w'], 'm_gmlp_ln_b': out['m_gmlp_ln_b'], 'm_w_spatial': out['m_w_spatial'], 'm_b_spatial': out['m_b_spatial'], 'm_w_out': out['m_w_out'], 'm_post_mix_w': out['m_post_mix_w'], 'm_pre_ffn_w': out['m_pre_ffn_w'], 'm_w_gate': out['m_w_gate'], 'm_w_up': out['m_w_up'], 'm_w_down': out['m_w_down'], 'm_post_ffn_w': out['m_post_ffn_w'], 'm_w_ple': out['m_w_ple'], 'm_w_ple_gate': out['m_w_ple_gate'], 'm_post_ple_w': out['m_post_ple_w'], 'v_pre_mix_w': out['v_pre_mix_w'], 'v_w_in': out['v_w_in'], 'v_lb_param': out['v_lb_param'], 'v_a_norm_w': out['v_a_norm_w'], 'v_gmlp_ln_w': out['v_gmlp_ln_w'], 'v_gmlp_ln_b': out['v_gmlp_ln_b'], 'v_w_spatial': out['v_w_spatial'], 'v_b_spatial': out['v_b_spatial'], 'v_w_out': out['v_w_out'], 'v_post_mix_w': out['v_post_mix_w'], 'v_pre_ffn_w': out['v_pre_ffn_w'], 'v_w_gate': out['v_w_gate'], 'v_w_up': out['v_w_up'], 'v_w_down': out['v_w_down'], 'v_post_ffn_w': out['v_post_ffn_w'], 'v_w_ple': out['v_w_ple'], 'v_w_ple_gate': out['v_w_ple_gate'], 'v_post_ple_w': out['v_post_ple_w']}


def _loss(weights, diff, rest, loss_target):
    with _jax.named_scope("forward"):
        args = {**rest, TWIN_DIFF_INPUT: diff, **{k: w.astype(_WEIGHT_DTYPES[k]) for k, w in weights.items()}}
        y = _forward(args)
    with _jax.named_scope("loss_head"):
        err = _jnp.square(y.astype(_jnp.float32) - loss_target)
        return 0.5 * _jnp.sum(_jnp.mean(err, axis=-1)) if err.ndim else 0.5 * err


def _adamw(w, g, m, v):
    m = ADAM_B1 * m + (1.0 - ADAM_B1) * g
    v = ADAM_B2 * v + (1.0 - ADAM_B2) * _jnp.square(g)
    m_hat = m / (1.0 - ADAM_B1 ** ADAM_STEP)
    v_hat = v / (1.0 - ADAM_B2 ** ADAM_STEP)
    delta = -ADAM_LR * (m_hat / (_jnp.sqrt(v_hat) + ADAM_EPS) + ADAM_WD * w)
    return delta, m, v


def reference(x, p, pre_mix_w, w_in, lb_param, a_norm_w, gmlp_ln_w, gmlp_ln_b, w_spatial, b_spatial, w_out, post_mix_w, pre_ffn_w, w_gate, w_up, w_down, post_ffn_w, w_ple, w_ple_gate, post_ple_w, loss_target, m_pre_mix_w, m_w_in, m_lb_param, m_a_norm_w, m_gmlp_ln_w, m_gmlp_ln_b, m_w_spatial, m_b_spatial, m_w_out, m_post_mix_w, m_pre_ffn_w, m_w_gate, m_w_up, m_w_down, m_post_ffn_w, m_w_ple, m_w_ple_gate, m_post_ple_w, v_pre_mix_w, v_w_in, v_lb_param, v_a_norm_w, v_gmlp_ln_w, v_gmlp_ln_b, v_w_spatial, v_b_spatial, v_w_out, v_post_mix_w, v_pre_ffn_w, v_w_gate, v_w_up, v_w_down, v_post_ffn_w, v_w_ple, v_w_ple_gate, v_post_ple_w):
    given = dict(x=x, p=p, pre_mix_w=pre_mix_w, w_in=w_in, lb_param=lb_param, a_norm_w=a_norm_w, gmlp_ln_w=gmlp_ln_w, gmlp_ln_b=gmlp_ln_b, w_spatial=w_spatial, b_spatial=b_spatial, w_out=w_out, post_mix_w=post_mix_w, pre_ffn_w=pre_ffn_w, w_gate=w_gate, w_up=w_up, w_down=w_down, post_ffn_w=post_ffn_w, w_ple=w_ple, w_ple_gate=w_ple_gate, post_ple_w=post_ple_w, loss_target=loss_target, m_pre_mix_w=m_pre_mix_w, m_w_in=m_w_in, m_lb_param=m_lb_param, m_a_norm_w=m_a_norm_w, m_gmlp_ln_w=m_gmlp_ln_w, m_gmlp_ln_b=m_gmlp_ln_b, m_w_spatial=m_w_spatial, m_b_spatial=m_b_spatial, m_w_out=m_w_out, m_post_mix_w=m_post_mix_w, m_pre_ffn_w=m_pre_ffn_w, m_w_gate=m_w_gate, m_w_up=m_w_up, m_w_down=m_w_down, m_post_ffn_w=m_post_ffn_w, m_w_ple=m_w_ple, m_w_ple_gate=m_w_ple_gate, m_post_ple_w=m_post_ple_w, v_pre_mix_w=v_pre_mix_w, v_w_in=v_w_in, v_lb_param=v_lb_param, v_a_norm_w=v_a_norm_w, v_gmlp_ln_w=v_gmlp_ln_w, v_gmlp_ln_b=v_gmlp_ln_b, v_w_spatial=v_w_spatial, v_b_spatial=v_b_spatial, v_w_out=v_w_out, v_post_mix_w=v_post_mix_w, v_pre_ffn_w=v_pre_ffn_w, v_w_gate=v_w_gate, v_w_up=v_w_up, v_w_down=v_w_down, v_post_ffn_w=v_post_ffn_w, v_w_ple=v_w_ple, v_w_ple_gate=v_w_ple_gate, v_post_ple_w=v_post_ple_w)
    weights = {n: given[n] for n in TWIN_WEIGHTS}
    shared = {n: given[n] for n in SHARED_INPUTS}
    per_example = {n: given[n] for n in ['x', 'p']}
    grad_fn = _jax.value_and_grad(_loss, argnums=(0, 1))

    def one_microbatch(ex, loss_target):
        ex = dict(ex)
        diff = ex.pop(TWIN_DIFF_INPUT)
        return grad_fn(weights, diff, {**shared, **ex}, loss_target)

    if N_MICROBATCH == 1:
        loss, (grad_w, grad_x) = one_microbatch(per_example, given["loss_target"])
    else:
        def body(carry, xs):
            loss_sum, grad_sum = carry
            l_k, (gw_k, gx_k) = one_microbatch(xs[0], xs[1])
            with _jax.named_scope("update"):
                return (loss_sum + l_k, _jax.tree.map(_jnp.add, grad_sum, gw_k)), gx_k

        init = (_jnp.zeros((), _jnp.float32), _jax.tree.map(_jnp.zeros_like, weights))
        (loss, grad_w), grad_x = _jax.lax.scan(body, init, (per_example, given["loss_target"]))
    with _jax.named_scope("update"):
        delta_w, new_m, new_v = {}, {}, {}
        for n in TWIN_WEIGHTS:
            delta_w[n], new_m[n], new_v[n] = _adamw(weights[n], grad_w[n], given["m_" + n], given["v_" + n])
    return (loss, grad_x, *[grad_w[n] for n in TWIN_WEIGHTS], *[delta_w[n] for n in TWIN_WEIGHTS],
            *[new_m[n] for n in TWIN_WEIGHTS], *[new_v[n] for n in TWIN_WEIGHTS])
```

```python
import functools

import numpy as np
import jax
import jax.numpy as jnp
from jax import lax
from jax.experimental import pallas as pl
from jax.experimental.pallas import tpu as pltpu

F32 = jnp.float32
BF16 = jnp.bfloat16

EPS = 1e-6
HEAD = 128
GLA_CHUNK = 64
GMLP_CHUNK = 128
N_DEV = 8
N_CHIP = 4
LANE = 128
VMEM_LIMIT = 56 * 1024 * 1024
HGRN_ROWS = 512
ROW_TILE = 128

ADAM_LR = 0.001
ADAM_B1 = 0.9
ADAM_B2 = 0.999
ADAM_EPS = 1e-08
ADAM_WD = 0.01
ADAM_STEP = 10

MESH = pl.DeviceIdType.MESH
ANY = pl.BlockSpec(memory_space=pl.ANY)

_DIMS = {
    "nn": (((1,), (0,)), ((), ())),
    "nt": (((1,), (1,)), ((), ())),
    "tn": (((0,), (0,)), ((), ())),
}


def _tile(dim, pref):
    return pref if dim % pref == 0 else dim


def _rows(r, bytes_per_row, budget=4 * 1024 * 1024, mult=16):
    best = None
    for cand in range(mult, r + 1, mult):
        if r % cand == 0 and cand * bytes_per_row <= budget:
            best = cand
    return best if best is not None else r


def _cparams(n_axes):
    return pltpu.CompilerParams(dimension_semantics=("arbitrary",) * n_axes, vmem_limit_bytes=VMEM_LIMIT)


def _dot(a, b, form="nn"):
    return lax.dot_general(a.astype(BF16), b.astype(BF16), _DIMS[form], preferred_element_type=F32)


def _split3(x):
    hi = x.astype(BF16)
    r = x - hi.astype(F32)
    mid = r.astype(BF16)
    lo = (r - mid.astype(F32)).astype(BF16)
    return hi, mid, lo


def _dot_exact_l(c, x):
    hi, mid, lo = _split3(x)
    d = lambda y: lax.dot_general(c, y, _DIMS["nn"], preferred_element_type=F32)
    return d(hi) + d(mid) + d(lo)


def _dot_exact_r(x, c):
    hi, mid, lo = _split3(x)
    d = lambda y: lax.dot_general(y, c, _DIMS["nn"], preferred_element_type=F32)
    return d(hi) + d(mid) + d(lo)


def _sigmoid(x):
    return 1.0 / (1.0 + jnp.exp(-x))


def _gelu(x):
    return 0.5 * x * (1.0 + lax.erf(x * 0.7071067811865476))


def _gelu_grad(x):
    cdf = 0.5 * (1.0 + lax.erf(x * 0.7071067811865476))
    pdf = jnp.exp(-0.5 * x * x) * 0.3989422804014327
    return cdf + x * pdf


def _matmul(name, operands, pairs, outs, grid, acc_shape, n_slots=1, epilogue=None):
    used = sorted({i for p in pairs for i in p[:2]})
    n_op = len(operands)
    n_out = len(outs)
    k_axis = len(grid) - 1
    n_k = grid[-1]

    def body(*refs):
        ops = refs[:n_op]
        out_refs = refs[n_op:n_op + n_out]
        acc = refs[n_op + n_out]
        k = pl.program_id(k_axis)

        @pl.when(k == 0)
        def _():
            acc[...] = jnp.zeros_like(acc)

        vals = {i: ops[i][...] for i in used}
        vals = {i: (v if v.dtype == BF16 else v.astype(BF16)) for i, v in vals.items()}
        for s in range(n_slots):
            tot = None
            for ia, ib, form, slot in pairs:
                if slot != s:
                    continue
                d = lax.dot_general(vals[ia], vals[ib], _DIMS[form], preferred_element_type=F32)
                tot = d if tot is None else tot + d
            acc[s] += tot

        @pl.when(k == n_k - 1)
        def _():
            accs = [acc[s] for s in range(n_slots)]
            extra = [ops[i][...] for i in range(n_op) if i not in used]
            res = epilogue(accs, *extra) if epilogue is not None else accs
            for o, v in zip(out_refs, res):
                o[...] = v.astype(o.dtype)

    return pl.pallas_call(
        body,
        name=name,
        grid=grid,
        in_specs=[s for _, s in operands],
        out_specs=[s for _, s in outs],
        out_shape=[s for s, _ in outs],
        scratch_shapes=[pltpu.VMEM((n_slots,) + tuple(acc_shape), F32)],
        compiler_params=_cparams(len(grid)),
    )(*[a for a, _ in operands])


def _mm_plain(name, a, b, form, out_dtype, tm, tn, tk, extra=None, epilogue=None):
    if form == "nn":
        (M, K), N = a.shape, b.shape[1]
    elif form == "nt":
        (M, K), N = a.shape, b.shape[0]
    else:
        (K, M), N = a.shape, b.shape[1]
    tm, tn, tk = _tile(M, tm), _tile(N, tn), _tile(K, tk)
    a_spec = pl.BlockSpec((tk, tm), lambda m, n, k: (k, m)) if form == "tn" else pl.BlockSpec((tm, tk), lambda m, n, k: (m, k))
    b_spec = pl.BlockSpec((tn, tk), lambda m, n, k: (n, k)) if form == "nt" else pl.BlockSpec((tk, tn), lambda m, n, k: (k, n))
    operands = [(a, a_spec), (b, b_spec)]
    if extra is not None:
        operands.append((extra, pl.BlockSpec((tm, tn), lambda m, n, k: (m, n))))
    out = (jax.ShapeDtypeStruct((M, N), out_dtype), pl.BlockSpec((tm, tn), lambda m, n, k: (m, n)))
    return _matmul(name, operands, [(0, 1, form, 0)], [out], (M // tm, N // tn, K // tk), (tm, tn), epilogue=epilogue)[0]


def _cast_bf16(name, w):
    r, c = w.shape
    tr = _rows(r, 6 * c)

    def body(w_ref, o_ref):
        o_ref[...] = w_ref[...].astype(BF16)

    return pl.pallas_call(
        body, name=name, grid=(r // tr,),
        in_specs=[pl.BlockSpec((tr, c), lambda i: (i, 0))],
        out_specs=pl.BlockSpec((tr, c), lambda i: (i, 0)),
        out_shape=jax.ShapeDtypeStruct((r, c), BF16),
        compiler_params=_cparams(1),
    )(w)


def _rms_stats(x):
    r = lax.rsqrt(jnp.mean(x * x, axis=-1, keepdims=True) + EPS)
    return x * r, r


def _rms_bwd(xhat, r, w, dy):
    dxh = dy * w
    return r * (dxh - xhat * jnp.mean(dxh * xhat, axis=-1, keepdims=True))


def _row_spec(tr, d):
    return pl.BlockSpec((tr, d), lambda i: (i, 0))


def _vec_spec(d):
    return pl.BlockSpec((1, d), lambda i: (0, 0))


def _rms_fwd(name, x, w):
    t, d = x.shape
    tr = _tile(t, ROW_TILE)

    def body(x_ref, w_ref, h_ref):
        xh, _ = _rms_stats(x_ref[...])
        h_ref[...] = (xh * w_ref[...]).astype(BF16)

    return pl.pallas_call(
        body, name=name, grid=(t // tr,),
        in_specs=[_row_spec(tr, d), _vec_spec(d)],
        out_specs=_row_spec(tr, d),
        out_shape=jax.ShapeDtypeStruct((t, d), BF16),
        compiler_params=_cparams(1),
    )(x, w)


def _resid_rms(name, xres, y, w_post, w_next):
    t, d = xres.shape
    tr = _tile(t, ROW_TILE)
    has_next = w_next is not None

    def body(*refs):
        if has_next:
            x_ref, y_ref, wp_ref, wn_ref, xo_ref, h_ref = refs
        else:
            x_ref, y_ref, wp_ref, xo_ref, h_ref = refs
        yh, _ = _rms_stats(y_ref[...])
        xn = x_ref[...] + yh * wp_ref[...]
        xo_ref[...] = xn
        if has_next:
            xh, _ = _rms_stats(xn)
            h_ref[...] = (xh * wn_ref[...]).astype(BF16)
        else:
            h_ref[...] = xn.astype(BF16)

    ins = [xres, y, w_post] + ([w_next] if has_next else [])
    in_specs = [_row_spec(tr, d), _row_spec(tr, d), _vec_spec(d)] + ([_vec_spec(d)] if has_next else [])
    return pl.pallas_call(
        body, name=name, grid=(t // tr,),
        in_specs=in_specs,
        out_specs=[_row_spec(tr, d), _row_spec(tr, d)],
        out_shape=[jax.ShapeDtypeStruct((t, d), F32), jax.ShapeDtypeStruct((t, d), BF16)],
        compiler_params=_cparams(1),
    )(*ins)


def _ple_loss(name, x2, pe, pgl, w_pp, tgt):
    t, d = x2.shape
    tr = _tile(t, ROW_TILE)

    def body(x2_ref, pe_ref, pgl_ref, w_ref, tgt_ref, loss_ref, d3_ref, dpe_ref, dpgl_ref, dw_ref):
        @pl.when(pl.program_id(0) == 0)
        def _():
            loss_ref[...] = jnp.zeros_like(loss_ref)
            dw_ref[...] = jnp.zeros_like(dw_ref)

        pe_v = pe_ref[...]
        s = _sigmoid(pgl_ref[...])
        y = pe_v * s
        yh, r = _rms_stats(y)
        w = w_ref[...]
        err = x2_ref[...] + yh * w - tgt_ref[...]
        loss_ref[...] += 0.5 * jnp.sum(jnp.mean(err * err, axis=-1, keepdims=True), axis=0, keepdims=True)
        d3 = err * (1.0 / d)
        d3_ref[...] = d3
        dw_ref[...] += jnp.sum(d3 * yh, axis=0, keepdims=True)
        dy = _rms_bwd(yh, r, w, d3)
        dpe_ref[...] = (dy * s).astype(BF16)
        dpgl_ref[...] = (dy * pe_v * s * (1.0 - s)).astype(BF16)

    return pl.pallas_call(
        body, name=name, grid=(t // tr,),
        in_specs=[_row_spec(tr, d), _row_spec(tr, d), _row_spec(tr, d), _vec_spec(d), _row_spec(tr, d)],
        out_specs=[pl.BlockSpec((1, 1), lambda i: (0, 0)), _row_spec(tr, d), _row_spec(tr, d), _row_spec(tr, d), _vec_spec(d)],
        out_shape=[jax.ShapeDtypeStruct((1, 1), F32), jax.ShapeDtypeStruct((t, d), F32),
                   jax.ShapeDtypeStruct((t, d), BF16), jax.ShapeDtypeStruct((t, d), BF16),
                   jax.ShapeDtypeStruct((1, d), F32)],
        compiler_params=_cparams(1),
    )(x2, pe, pgl, w_pp, tgt)


def _norm_bwd(name, dres, y, w_post):
    t, d = dres.shape
    tr = _tile(t, ROW_TILE)

    def body(d_ref, y_ref, w_ref, dy_ref, dw_ref):
        @pl.when(pl.program_id(0) == 0)
        def _():
            dw_ref[...] = jnp.zeros_like(dw_ref)

        dv = d_ref[...]
        yh, r = _rms_stats(y_ref[...])
        dw_ref[...] += jnp.sum(dv * yh, axis=0, keepdims=True)
        dy_ref[...] = _rms_bwd(yh, r, w_ref[...], dv).astype(BF16)

    return pl.pallas_call(
        body, name=name, grid=(t // tr,),
        in_specs=[_row_spec(tr, d), _row_spec(tr, d), _vec_spec(d)],
        out_specs=[_row_spec(tr, d), _vec_spec(d)],
        out_shape=[jax.ShapeDtypeStruct((t, d), BF16), jax.ShapeDtypeStruct((1, d), F32)],
        compiler_params=_cparams(1),
    )(dres, y, w_post)


def _prenorm_bwd(name, dres, dh, xin, w_pre, y=None, w_post=None):
    t, d = dres.shape
    tr = _tile(t, ROW_TILE)
    two = y is not None

    def body(*refs):
        if two:
            d_ref, dh_ref, x_ref, wpre_ref, y_ref, wpost_ref, do_ref, dwpre_ref, dy_ref, dwpost_ref = refs
        else:
            d_ref, dh_ref, x_ref, wpre_ref, do_ref, dwpre_ref = refs

        @pl.when(pl.program_id(0) == 0)
        def _():
            dwpre_ref[...] = jnp.zeros_like(dwpre_ref)
            if two:
                dwpost_ref[...] = jnp.zeros_like(dwpost_ref)

        dhv = dh_ref[...]
        xh, r = _rms_stats(x_ref[...])
        dwpre_ref[...] += jnp.sum(dhv * xh, axis=0, keepdims=True)
        dout = d_ref[...] + _rms_bwd(xh, r, wpre_ref[...], dhv)
        do_ref[...] = dout
        if two:
            yh, ry = _rms_stats(y_ref[...])
            dwpost_ref[...] += jnp.sum(dout * yh, axis=0, keepdims=True)
            dy_ref[...] = _rms_bwd(yh, ry, wpost_ref[...], dout).astype(BF16)

    ins = [dres, dh, xin, w_pre] + ([y, w_post] if two else [])
    in_specs = [_row_spec(tr, d)] * 3 + [_vec_spec(d)] + ([_row_spec(tr, d), _vec_spec(d)] if two else [])
    out_specs = [_row_spec(tr, d), _vec_spec(d)] + ([_row_spec(tr, d), _vec_spec(d)] if two else [])
    out_shape = [jax.ShapeDtypeStruct((t, d), F32), jax.ShapeDtypeStruct((1, d), F32)]
    if two:
        out_shape += [jax.ShapeDtypeStruct((t, d), BF16), jax.ShapeDtypeStruct((1, d), F32)]
    return pl.pallas_call(
        body, name=name, grid=(t // tr,),
        in_specs=in_specs, out_specs=out_specs, out_shape=out_shape,
        compiler_params=_cparams(1),
    )(*ins)


_LEVELS = (32, 16, 8, 4, 2, 1)
_N_CUM = 3 + 2 * len(_LEVELS)


def _hgrn_constants():
    c = GLA_CHUNK
    idx = np.arange(c)
    t, r = idx[:, None], idx[None, :]
    mats = [(r <= t), (r > t), np.ones((c, c), bool)]
    lq, lk, masks = [], [], []
    for h in _LEVELS:
        blk, pos = idx // (2 * h), idx % (2 * h)
        mid = blk * 2 * h + h - 1
        upper, lower = pos >= h, pos < h
        lq.append(upper[:, None] & (r > mid[:, None]) & (r <= t))
        lk.append(lower[:, None] & (r > t) & (r <= mid[:, None]))
        masks.append((blk[:, None] == blk[None, :]) & upper[:, None] & lower[None, :])
    cum = np.concatenate(mats + lq + lk, axis=0).astype(np.float32)
    rev = (r >= t).astype(np.float32)
    return (jnp.asarray(cum, BF16), jnp.asarray(rev, BF16), jnp.asarray(np.stack(masks).astype(np.float32)))


def _hgrn_gates(qp, fp, lb):
    sq = _sigmoid(qp)
    q = qp * sq
    sg = _sigmoid(fp)
    f = lb + (1.0 - lb) * sg
    k = 1.0 - f
    logf = jnp.log(jnp.maximum(f, 1e-30))
    return q, sq, sg, f, k, logf


def _hgrn_decays(cum_ref, logf):
    c = GLA_CHUNK
    e = jnp.exp(_dot_exact_l(cum_ref[...], logf))
    part = lambda i: e[i * c:(i + 1) * c]
    n = len(_LEVELS)
    return part(0), part(1), part(2), [part(3 + i) for i in range(n)], [part(3 + n + i) for i in range(n)]


def _hgrn_fwd(proj, lb, nw, n_heads):
    t = proj.shape[0]
    aw = n_heads * HEAD
    rb = _tile(t, HGRN_ROWS)
    c = GLA_CHUNK
    n_sub = rb // c
    cum, _, masks = _hgrn_constants()

    def body(q_ref, f_ref, i_ref, g_ref, lb_ref, nw_ref, cum_ref, m_ref, a_ref, o_ref, s_ref, sc_ref, st):
        @pl.when(pl.program_id(1) == 0)
        def _():
            st[...] = jnp.zeros_like(st)

        lbv = lb_ref[...]
        nwv = nw_ref[...]
        eye = (lax.broadcasted_iota(jnp.int32, (c, c), 0) == lax.broadcasted_iota(jnp.int32, (c, c), 1)).astype(F32)

        def chunk(j, carry):
            rows = pl.ds(pl.multiple_of(j * c, c), c)
            q, _, _, _, k, logf = _hgrn_gates(q_ref[rows, :], f_ref[rows, :], lbv)
            v = i_ref[rows, :]
            eb, ebe, eend, eq, ek = _hgrn_decays(cum_ref, logf)
            s_ref[j] = st[...]
            inter = _dot(q * eb, st[...], "nt")
            scores = eye * jnp.sum(q * k, axis=-1, keepdims=True)
            for lvl in range(len(_LEVELS)):
                scores = scores + m_ref[lvl] * _dot(q * eq[lvl], k * ek[lvl], "nt")
            sc_ref[rows, :] = scores
            o = inter + _dot(scores, v)
            st[...] = st[...] * eend[0:1] + _dot(v, k * ebe, "tn")
            o_ref[rows, :] = o
            r = lax.rsqrt(jnp.mean(o * o, axis=-1, keepdims=True) + EPS)
            gv = g_ref[rows, :]
            a_ref[rows, :] = (o * r * nwv * (gv * _sigmoid(gv))).astype(BF16)
            return carry

        lax.fori_loop(0, n_sub, chunk, 0)

    col = lambda base: pl.BlockSpec((rb, HEAD), lambda h, r: (r, base * n_heads + h))
    vec = pl.BlockSpec((1, HEAD), lambda h, r: (0, h))
    return pl.pallas_call(
        body, name="hgrn2_fwd", grid=(n_heads, t // rb),
        in_specs=[col(0), col(1), col(2), col(3), vec, vec,
                  pl.BlockSpec(cum.shape, lambda h, r: (0, 0)), pl.BlockSpec(masks.shape, lambda h, r: (0, 0, 0))],
        out_specs=[pl.BlockSpec((rb, HEAD), lambda h, r: (r, h)), pl.BlockSpec((rb, HEAD), lambda h, r: (r, h)),
                   pl.BlockSpec((None, n_sub, HEAD, HEAD), lambda h, r: (h, r, 0, 0)),
                   pl.BlockSpec((None, rb, c), lambda h, r: (h, r, 0))],
        out_shape=[jax.ShapeDtypeStruct((t, aw), BF16), jax.ShapeDtypeStruct((t, aw), F32),
                   jax.ShapeDtypeStruct((n_heads, t // c, HEAD, HEAD), F32),
                   jax.ShapeDtypeStruct((n_heads, t, c), F32)],
        scratch_shapes=[pltpu.VMEM((HEAD, HEAD), F32)],
        compiler_params=_cparams(2),
    )(proj, proj, proj, proj, lb, nw, cum, masks)


def _hgrn_bwd(proj, lb, nw, o_raw, states, scores, dab, n_heads):
    t = proj.shape[0]
    aw = n_heads * HEAD
    rb = _tile(t, HGRN_ROWS)
    c = GLA_CHUNK
    n_sub = rb // c
    n_rb = t // rb
    cum, rev, masks = _hgrn_constants()

    def body(q_ref, f_ref, i_ref, g_ref, lb_ref, nw_ref, o_ref, s_ref, sc_ref, da_ref, cum_ref, rev_ref, m_ref,
             dq_ref, df_ref, di_ref, dg_ref, dlb_ref, dnw_ref, dst):
        @pl.when(pl.program_id(1) == 0)
        def _():
            dst[...] = jnp.zeros_like(dst)
            dlb_ref[...] = jnp.zeros_like(dlb_ref)
            dnw_ref[...] = jnp.zeros_like(dnw_ref)

        lbv = lb_ref[...]
        nwv = nw_ref[...]
        ri = lax.broadcasted_iota(jnp.int32, (c, c), 0)
        ci = lax.broadcasted_iota(jnp.int32, (c, c), 1)
        eye = (ri == ci).astype(F32)
        causal = (ci <= ri).astype(F32)
        last_row = (lax.broadcasted_iota(jnp.int32, (c, HEAD), 0) == c - 1).astype(F32)

        def chunk(jj, carry):
            j = n_sub - 1 - jj
            rows = pl.ds(pl.multiple_of(j * c, c), c)
            qp = q_ref[rows, :]
            q, sq, sg, f, k, logf = _hgrn_gates(qp, f_ref[rows, :], lbv)
            v = i_ref[rows, :]
            gv = g_ref[rows, :]
            eb, ebe, eend, eq, ek = _hgrn_decays(cum_ref, logf)
            s_in = s_ref[j]
            a_sc = sc_ref[rows, :]
            dsn = dst[...]
            o = o_ref[rows, :]
            r = lax.rsqrt(jnp.mean(o * o, axis=-1, keepdims=True) + EPS)
            oh = o * r
            sgg = _sigmoid(gv)
            sil = gv * sgg
            da = da_ref[rows, :]
            dg_ref[rows, :] = (da * oh * nwv * (sgg * (1.0 + gv * (1.0 - sgg)))).astype(BF16)
            dnw_ref[...] += jnp.sum(da * oh * sil, axis=0, keepdims=True)
            doh = da * nwv * sil
            do = r * (doh - oh * jnp.mean(doh * oh, axis=-1, keepdims=True))
            kt = k * ebe
            qt = q * eb
            di_ref[rows, :] = (_dot(a_sc, do, "tn") + _dot(kt, dsn, "nt")).astype(BF16)
            d_sc = _dot(do, v, "nt") * causal
            dqt = _dot(do, s_in)
            dkt = _dot(v, dsn)
            dst[...] = dsn * eend[0:1] + _dot(do, qt, "tn")
            diag = jnp.sum(d_sc * eye, axis=-1, keepdims=True)
            dq = dqt * eb
            dk = dkt * ebe
            db = q * dq - k * dk
            dq = dq + diag * k
            dk = dk + diag * q
            for lvl in range(len(_LEVELS)):
                dm = (m_ref[lvl] * d_sc).astype(BF16)
                ql = (q * eq[lvl]).astype(BF16)
                kl = (k * ek[lvl]).astype(BF16)
                gq = _dot(dm, kl)
                gk = _dot(dm, ql, "tn")
                dq = dq + gq * eq[lvl]
                dk = dk + gk * ek[lvl]
                db = db + ql.astype(F32) * gq - kl.astype(F32) * gk
            extra = jnp.sum(dkt * kt, axis=0, keepdims=True) + eend[0:1] * jnp.sum(s_in * dsn, axis=0, keepdims=True)
            db = db + last_row * extra
            dlogf = _dot_exact_l(rev_ref[...], db)
            dfv = jnp.where(f > 1e-30, dlogf / f, 0.0) - dk
            df_ref[rows, :] = (dfv * (1.0 - lbv) * sg * (1.0 - sg)).astype(BF16)
            dlb_ref[...] += jnp.sum(dfv * (1.0 - sg), axis=0, keepdims=True)
            dq_ref[rows, :] = (dq * (sq * (1.0 + qp * (1.0 - sq)))).astype(BF16)
            return carry

        lax.fori_loop(0, n_sub, chunk, 0)

    col = lambda base: pl.BlockSpec((rb, HEAD), lambda h, r: (n_rb - 1 - r, base * n_heads + h))
    blk = pl.BlockSpec((rb, HEAD), lambda h, r: (n_rb - 1 - r, h))
    vec = pl.BlockSpec((1, HEAD), lambda h, r: (0, h))
    const = lambda a: pl.BlockSpec(a.shape, lambda h, r: (0,) * a.ndim)
    outs = pl.pallas_call(
        body, name="hgrn2_bwd", grid=(n_heads, n_rb),
        in_specs=[col(0), col(1), col(2), col(3), vec, vec, blk,
                  pl.BlockSpec((None, n_sub, HEAD, HEAD), lambda h, r: (h, n_rb - 1 - r, 0, 0)),
                  pl.BlockSpec((None, rb, c), lambda h, r: (h, n_rb - 1 - r, 0)),
                  blk, const(cum), const(rev), const(masks)],
        out_specs=[blk, blk, blk, blk, vec, vec],
        out_shape=[jax.ShapeDtypeStruct((t, aw), BF16)] * 4 + [jax.ShapeDtypeStruct((1, aw), F32)] * 2,
        scratch_shapes=[pltpu.VMEM((HEAD, HEAD), F32)],
        compiler_params=_cparams(2),
    )(proj, proj, proj, proj, lb, nw, o_raw, states, scores, dab, cum, rev, masks)
    return outs


def _lb_fwd(lb_param):
    def body(p_ref, o_ref):
        p = p_ref[...]
        e = jnp.exp(p - jnp.max(p, axis=0, keepdims=True))
        o_ref[...] = e[0:1] / jnp.sum(e, axis=0, keepdims=True)

    return pl.pallas_call(body, name="lb_fwd", out_shape=jax.ShapeDtypeStruct((1, lb_param.shape[1]), F32))(lb_param)


def _lb_bwd(lb_param, dlb):
    def body(p_ref, d_ref, o_ref):
        p = p_ref[...]
        e = jnp.exp(p - jnp.max(p, axis=0, keepdims=True))
        s = e / jnp.sum(e, axis=0, keepdims=True)
        first = (lax.broadcasted_iota(jnp.int32, p.shape, 0) == 0).astype(F32)
        o_ref[...] = d_ref[...] * s[0:1] * (first - s)

    return pl.pallas_call(body, name="lb_bwd", out_shape=jax.ShapeDtypeStruct(lb_param.shape, F32))(lb_param, dlb)


def _gmlp_norm(v, lnw, lnb):
    vf = _gelu(v)
    mu = jnp.mean(vf, axis=-1, keepdims=True)
    cen = vf - mu
    rstd = lax.rsqrt(jnp.mean(cen * cen, axis=-1, keepdims=True) + EPS)
    xh = cen * rstd
    return xh, rstd, xh * lnw + lnb


def _tril(n):
    return (lax.broadcasted_iota(jnp.int32, (n, n), 1) <= lax.broadcasted_iota(jnp.int32, (n, n), 0)).astype(F32)


def _gmlp_fwd(proj, lnw, lnb, w_sp, bs_t, n_groups, col_base):
    t = proj.shape[0]
    bw = n_groups * HEAD
    c = GMLP_CHUNK

    def body(u_ref, v_ref, lnw_ref, lnb_ref, w_ref, bs_ref, o_ref):
        tri = _tril(c)
        uf = _gelu(u_ref[...])
        _, _, vn = _gmlp_norm(v_ref[...], lnw_ref[...], lnb_ref[...])
        for g in range(n_groups):
            cols = slice(g * HEAD, (g + 1) * HEAD)
            z = _dot(w_ref[g] * tri, vn[:, cols]) + bs_ref[:, g:g + 1]
            o_ref[:, cols] = (uf[:, cols] * z).astype(BF16)

    blk = lambda b: pl.BlockSpec((c, bw), lambda n: (n, b))
    const = lambda a: pl.BlockSpec(a.shape, lambda n: (0,) * a.ndim)
    return pl.pallas_call(
        body, name="gmlp_fwd", grid=(t // c,),
        in_specs=[blk(col_base), blk(col_base + 1), const(lnw), const(lnb), const(w_sp), const(bs_t)],
        out_specs=pl.BlockSpec((c, bw), lambda n: (n, 0)),
        out_shape=jax.ShapeDtypeStruct((t, bw), BF16),
        compiler_params=_cparams(1),
    )(proj, proj, lnw, lnb, w_sp, bs_t)


def _gmlp_bwd(proj, lnw, lnb, w_sp, bs_t, dab, n_groups, col_base):
    t = proj.shape[0]
    bw = n_groups * HEAD
    c = GMLP_CHUNK
    n_steps = t // c
    sel = jnp.asarray((np.arange(bw)[:, None] // HEAD == np.arange(n_groups)[None, :]).astype(np.float32), BF16)

    def body(u_ref, v_ref, lnw_ref, lnb_ref, w_ref, bs_ref, d_ref, sel_ref,
             du_ref, dv_ref, dlnw_ref, dlnb_ref, dw_ref, dbs_ref, dz_acc, dvn_scr):
        step = pl.program_id(0)

        @pl.when(step == 0)
        def _():
            dlnw_ref[...] = jnp.zeros_like(dlnw_ref)
            dlnb_ref[...] = jnp.zeros_like(dlnb_ref)
            dw_ref[...] = jnp.zeros_like(dw_ref)
            dz_acc[...] = jnp.zeros_like(dz_acc)

        tri = _tril(c)
        u = u_ref[...]
        v = v_ref[...]
        uf = _gelu(u)
        lnw_v = lnw_ref[...]
        xh, rstd, vn = _gmlp_norm(v, lnw_v, lnb_ref[...])
        dbo = d_ref[...]
        dz = dbo * uf
        dz_acc[...] += dz
        for g in range(n_groups):
            cols = slice(g * HEAD, (g + 1) * HEAD)
            wg = w_ref[g] * tri
            z = _dot(wg, vn[:, cols]) + bs_ref[:, g:g + 1]
            du_ref[:, cols] = (dbo[:, cols] * z * _gelu_grad(u[:, cols])).astype(BF16)
            dvn_scr[:, cols] = _dot(wg, dz[:, cols], "tn")
            dw_ref[g] += tri * _dot(dz[:, cols], vn[:, cols], "nt")
        dvn = dvn_scr[...]
        dlnw_ref[...] += jnp.sum(dvn * xh, axis=0, keepdims=True)
        dlnb_ref[...] += jnp.sum(dvn, axis=0, keepdims=True)
        dxh = dvn * lnw_v
        dvf = rstd * (dxh - jnp.mean(dxh, axis=-1, keepdims=True) - xh * jnp.mean(dxh * xh, axis=-1, keepdims=True))
        dv_ref[...] = (dvf * _gelu_grad(v)).astype(BF16)

        @pl.when(step == n_steps - 1)
        def _():
            dbs_ref[...] = _dot_exact_r(dz_acc[...], sel_ref[...])

    blk = lambda b: pl.BlockSpec((c, bw), lambda n: (n, b))
    const = lambda a: pl.BlockSpec(a.shape, lambda n: (0,) * a.ndim)
    row = pl.BlockSpec((c, bw), lambda n: (n, 0))
    vec = pl.BlockSpec((1, bw), lambda n: (0, 0))
    return pl.pallas_call(
        body, name="gmlp_bwd", grid=(n_steps,),
        in_specs=[blk(col_base), blk(col_base + 1), const(lnw), const(lnb), const(w_sp), const(bs_t), blk(1), const(sel)],
        out_specs=[row, row, vec, vec, const(w_sp), const(bs_t)],
        out_shape=[jax.ShapeDtypeStruct((t, bw), BF16), jax.ShapeDtypeStruct((t, bw), BF16),
                   jax.ShapeDtypeStruct((1, bw), F32), jax.ShapeDtypeStruct((1, bw), F32),
                   jax.ShapeDtypeStruct(w_sp.shape, F32), jax.ShapeDtypeStruct(bs_t.shape, F32)],
        scratch_shapes=[pltpu.VMEM((c, bw), F32), pltpu.VMEM((c, bw), F32)],
        compiler_params=_cparams(1),
    )(proj, proj, lnw, lnb, w_sp, bs_t, dab, sel)


def _position():
    return lax.axis_index("x"), lax.axis_index("y"), lax.axis_index("c")


def _all_gather(name, shards, in_vmem=False):
    n = len(shards)
    per = N_DEV - 1

    def body(*refs):
        ins, outs = refs[:n], refs[n:2 * n]
        send_sems, recv_sems, local_sems = refs[2 * n:]
        x, y, c = _position()
        me, sibling = (x, y, c), (x, y, 1 - c)
        chips = [(1 - x, y), (x, 1 - y), (1 - x, 1 - y)]

        def idx(px, py, pc):
            return 4 * px + 2 * py + pc

        waits = []
        for a in range(n):
            out = outs[a]

            def copy(k, block, to, src=None, out=out, a=a):
                dst = out.at[idx(*block)]
                return pltpu.make_async_remote_copy(
                    src_ref=dst if src is None else src, dst_ref=dst,
                    send_sem=send_sems.at[a * per + k], recv_sem=recv_sems.at[a * per + k],
                    device_id=to, device_id_type=MESH)

            mine = pltpu.make_async_copy(ins[a], out.at[idx(*me)], local_sems.at[a])
            mine.start()
            first = [copy(0, me, sibling, src=ins[a])]
            first += [copy(1 + j, me, (*chip, c), src=ins[a]) for j, chip in enumerate(chips)]
            for cp in first:
                cp.start()
            waits.append((copy, first, mine))
        for a in range(n):
            copy, first, mine = waits[a]
            passed = [copy(4 + j, (*chip, c), sibling) for j, chip in enumerate(chips)]
            for j, chip in enumerate(chips):
                copy(1 + j, (*chip, c), me).wait_recv()
                passed[j].start()
            waits[a] = (copy, first + passed, mine)
        for a in range(n):
            copy, sent, mine = waits[a]
            copy(0, sibling, me).wait_recv()
            for j, chip in enumerate(chips):
                copy(4 + j, (*chip, 1 - c), me).wait_recv()
            for cp in sent:
                cp.wait_send()
            mine.wait()

    spec = pl.BlockSpec(memory_space=pltpu.VMEM) if in_vmem else ANY
    return pl.pallas_call(
        body, name=name,
        in_specs=[spec] * n, out_specs=[spec] * n,
        out_shape=[jax.ShapeDtypeStruct((N_DEV,) + s.shape, s.dtype) for s in shards],
        scratch_shapes=[pltpu.SemaphoreType.DMA((n * per,)), pltpu.SemaphoreType.DMA((n * per,)),
                        pltpu.SemaphoreType.DMA((n,))],
    )(*shards)


def _pair_exchange(name, grads):
    n = len(grads)

    def body(*refs):
        ins, outs = refs[:n], refs[n:2 * n]
        send_sems, recv_sems = refs[2 * n:]
        x, y, c = _position()
        cps = []
        for a in range(n):
            for k in range(N_CHIP):
                cp = pltpu.make_async_remote_copy(
                    src_ref=ins[a].at[k, 1 - c], dst_ref=outs[a].at[k],
                    send_sem=send_sems.at[a * N_CHIP + k], recv_sem=recv_sems.at[a * N_CHIP + k],
                    device_id=(x, y, 1 - c), device_id_type=MESH)
                cp.start()
                cps.append(cp)
        for cp in cps:
            cp.wait()

    return pl.pallas_call(
        body, name=name,
        in_specs=[ANY] * n, out_specs=[ANY] * n,
        out_shape=[jax.ShapeDtypeStruct((N_CHIP,) + g.shape[2:], g.dtype) for g in grads],
        scratch_shapes=[pltpu.SemaphoreType.DMA((n * N_CHIP,)), pltpu.SemaphoreType.DMA((n * N_CHIP,))],
    )(*grads)


def _pair_sum(name, grad, other, core):
    _, _, r, c = grad.shape
    tr = _rows(r, 6 * c)

    def body(core_ref, g_ref, o_ref, out_ref):
        out_ref[...] = (g_ref[...].astype(F32) + o_ref[...].astype(F32)).astype(BF16)

    return pl.pallas_call(
        body, name=name,
        grid_spec=pltpu.PrefetchScalarGridSpec(
            num_scalar_prefetch=1, grid=(N_CHIP, r // tr),
            in_specs=[pl.BlockSpec((None, None, tr, c), lambda k, i, core_ref: (k, core_ref[0], i, 0)),
                      pl.BlockSpec((None, tr, c), lambda k, i, core_ref: (k, i, 0))],
            out_specs=pl.BlockSpec((None, tr, c), lambda k, i, core_ref: (k, i, 0))),
        out_shape=jax.ShapeDtypeStruct((N_CHIP, r, c), BF16),
        compiler_params=_cparams(2),
    )(core, grad, other)


def _chip_exchange(name, sums):
    n = len(sums)
    per = N_CHIP - 1

    def body(*refs):
        ins, outs = refs[:n], refs[n:2 * n]
        send_sems, recv_sems, local_sems = refs[2 * n:]
        x, y, c = _position()
        my_chip = 2 * x + y
        chips = [(1 - x, y), (x, 1 - y), (1 - x, 1 - y)]
        cps = []
        for a in range(n):
            mine = pltpu.make_async_copy(ins[a].at[my_chip], outs[a].at[my_chip], local_sems.at[a])
            mine.start()
            cps.append(mine)
            for j, (px, py) in enumerate(chips):
                cp = pltpu.make_async_remote_copy(
                    src_ref=ins[a].at[2 * px + py], dst_ref=outs[a].at[my_chip],
                    send_sem=send_sems.at[a * per + j], recv_sem=recv_sems.at[a * per + j],
                    device_id=(px, py, c), device_id_type=MESH)
                cp.start()
                cps.append(cp)
        for cp in cps:
            cp.wait()

    return pl.pallas_call(
        body, name=name,
        in_specs=[ANY] * n, out_specs=[ANY] * n,
        out_shape=[jax.ShapeDtypeStruct(s.shape, s.dtype) for s in sums],
        scratch_shapes=[pltpu.SemaphoreType.DMA((n * per,)), pltpu.SemaphoreType.DMA((n * per,)),
                        pltpu.SemaphoreType.DMA((n,))],
    )(*sums)


def _adamw_math(w, g, m, v):
    m = ADAM_B1 * m + (1.0 - ADAM_B1) * g
    v = ADAM_B2 * v + (1.0 - ADAM_B2) * (g * g)
    m_hat = m / (1.0 - ADAM_B1 ** ADAM_STEP)
    v_hat = v / (1.0 - ADAM_B2 ** ADAM_STEP)
    delta = -ADAM_LR * (m_hat / (jnp.sqrt(v_hat) + ADAM_EPS) + ADAM_WD * w)
    return delta, m, v


def _adamw(name, parts, w, m, v):
    n_parts, r, c = parts.shape
    tr = _rows(r, c * (n_parts * parts.dtype.itemsize + 28), mult=8)

    def body(p_ref, w_ref, m_ref, v_ref, g_ref, d_ref, mo_ref, vo_ref):
        g = p_ref[0].astype(F32)
        for i in range(1, n_parts):
            g = g + p_ref[i].astype(F32)
        g_ref[...] = g
        d_ref[...], mo_ref[...], vo_ref[...] = _adamw_math(w_ref[...], g, m_ref[...], v_ref[...])

    row = pl.BlockSpec((tr, c), lambda i: (i, 0))
    return pl.pallas_call(
        body, name=name, grid=(r // tr,),
        in_specs=[pl.BlockSpec((n_parts, tr, c), lambda i: (0, i, 0)), row, row, row],
        out_specs=[row] * 4,
        out_shape=[jax.ShapeDtypeStruct((r, c), F32)] * 4,
        compiler_params=_cparams(1),
    )(parts, w, m, v)


def kernel(x, p, pre_mix_w, w_in, lb_param, a_norm_w, gmlp_ln_w, gmlp_ln_b, w_spatial, b_spatial, w_out, post_mix_w, pre_ffn_w, w_gate, w_up, w_down, post_ffn_w, w_ple, w_ple_gate, post_ple_w, loss_target, m_pre_mix_w, m_w_in, m_lb_param, m_a_norm_w, m_gmlp_ln_w, m_gmlp_ln_b, m_w_spatial, m_b_spatial, m_w_out, m_post_mix_w, m_pre_ffn_w, m_w_gate, m_w_up, m_w_down, m_post_ffn_w, m_w_ple, m_w_ple_gate, m_post_ple_w, v_pre_mix_w, v_w_in, v_lb_param, v_a_norm_w, v_gmlp_ln_w, v_gmlp_ln_b, v_w_spatial, v_b_spatial, v_w_out, v_post_mix_w, v_pre_ffn_w, v_w_gate, v_w_up, v_w_down, v_post_ffn_w, v_w_ple, v_w_ple_gate, v_post_ple_w):
    big_names = ["w_in", "w_out", "w_gate", "w_up", "w_down", "w_ple", "w_ple_gate"]
    small_names = ["pre_mix_w", "lb_param", "a_norm_w", "gmlp_ln_w", "gmlp_ln_b", "w_spatial", "b_spatial",
                   "post_mix_w", "pre_ffn_w", "post_ffn_w", "post_ple_w"]
    all_names = ["pre_mix_w", "w_in", "lb_param", "a_norm_w", "gmlp_ln_w", "gmlp_ln_b", "w_spatial", "b_spatial",
                 "w_out", "post_mix_w", "pre_ffn_w", "w_gate", "w_up", "w_down", "post_ffn_w", "w_ple", "w_ple_gate",
                 "post_ple_w"]
    env = dict(locals())
    W = {n: env[n] for n in all_names}
    M = {n: env["m_" + n] for n in all_names}
    V = {n: env["v_" + n] for n in all_names}

    xs = x[0]
    ps = p[0, 0]
    tgt = loss_target[0]
    t, d = xs.shape
    aw = a_norm_w.shape[1]
    bw = gmlp_ln_w.shape[1]
    n_heads, n_groups = aw // HEAD, bw // HEAD
    core = lax.axis_index("c").astype(jnp.int32).reshape(1)

    shard = {n: W[n][0] for n in big_names}
    bf = {n: _cast_bf16("cast_" + n, shard[n]) for n in big_names}
    gathered = dict(zip(big_names, _all_gather("ag_weights", [bf[n] for n in big_names])))
    win_g = gathered["w_in"]
    wgate_g, wup_g = gathered["w_gate"], gathered["w_up"]
    wdown_g = gathered["w_down"]
    wple_g = gathered["w_ple"]
    wout_f = gathered["w_out"].reshape(d, d)
    wpg_f = gathered["w_ple_gate"].reshape(d, d)
    n_in = win_g.shape[2]
    ffl = wgate_g.shape[2]
    n_ple = wple_g.shape[2]
    ple = ps.shape[1]

    TM, TK = 1024, 512
    tm = _tile(t, TM)
    tkd = _tile(d, TK)

    h1 = _rms_fwd("rms_pre_mix", xs, pre_mix_w)
    proj = _matmul(
        "mm_proj",
        [(h1, pl.BlockSpec((tm, tkd), lambda m, n, k: (m, k))), (win_g, pl.BlockSpec((None, tkd, n_in), lambda m, n, k: (n, k, 0)))],
        [(0, 1, "nn", 0)],
        [(jax.ShapeDtypeStruct((t, N_DEV * n_in), F32), pl.BlockSpec((tm, n_in), lambda m, n, k: (m, n)))],
        (t // tm, N_DEV, d // tkd), (tm, n_in))[0]
    lb = _lb_fwd(lb_param)
    a_out, o_raw, states, scores = _hgrn_fwd(proj, lb, a_norm_w, n_heads)
    bs_t = b_spatial[0].T
    w_sp = w_spatial[0]
    col_u = (4 * aw) // bw
    b_out = _gmlp_fwd(proj, gmlp_ln_w, gmlp_ln_b, w_sp, bs_t, n_groups, col_u)
    ab = jnp.concatenate([a_out, b_out], axis=1)
    mix = _mm_plain("mm_mix", ab, wout_f, "nn", F32, TM, 2048, TK)
    x1, h2 = _resid_rms("resid_mix", xs, mix, post_mix_w, pre_ffn_w)

    def swiglu(accs):
        gate, up = accs
        return gate, up, gate * _sigmoid(gate) * up

    tmf = _tile(t, 512)
    blk3 = lambda: pl.BlockSpec((None, tmf, ffl), lambda j, m, k: (j, m, 0))
    gate, up, act = _matmul(
        "mm_ffn_up",
        [(h2, pl.BlockSpec((tmf, tkd), lambda j, m, k: (m, k))),
         (wgate_g, pl.BlockSpec((None, tkd, ffl), lambda j, m, k: (j, k, 0))),
         (wup_g, pl.BlockSpec((None, tkd, ffl), lambda j, m, k: (j, k, 0)))],
        [(0, 1, "nn", 0), (0, 2, "nn", 1)],
        [(jax.ShapeDtypeStruct((N_DEV, t, ffl), BF16), blk3()) for _ in range(3)],
        (N_DEV, t // tmf, d // tkd), (tmf, ffl), n_slots=2, epilogue=swiglu)
    tn_d = _tile(d, 2048)
    ff = _matmul(
        "mm_ffn_down",
        [(act, pl.BlockSpec((None, tm, ffl), lambda m, n, k: (k, m, 0))),
         (wdown_g, pl.BlockSpec((None, ffl, tn_d), lambda m, n, k: (k, 0, n)))],
        [(0, 1, "nn", 0)],
        [(jax.ShapeDtypeStruct((t, d), F32), pl.BlockSpec((tm, tn_d), lambda m, n, k: (m, n)))],
        (t // tm, d // tn_d, N_DEV), (tm, tn_d))[0]
    x2, x2b = _resid_rms("resid_ffn", x1, ff, post_ffn_w, None)

    pgl = _mm_plain("mm_ple_gate", x2b, wpg_f, "nn", F32, TM, 2048, TK)
    pe = _matmul(
        "mm_ple",
        [(ps, pl.BlockSpec((tm, ple), lambda m, n, k: (m, 0))), (wple_g, pl.BlockSpec((None, ple, n_ple), lambda m, n, k: (n, 0, 0)))],
        [(0, 1, "nn", 0)],
        [(jax.ShapeDtypeStruct((t, N_DEV * n_ple), F32), pl.BlockSpec((tm, n_ple), lambda m, n, k: (m, n)))],
        (t // tm, N_DEV, 1), (tm, n_ple))[0]
    loss_part, d3, dpe, dpgl, g_post_ple = _ple_loss("ple_loss", x2, pe, pgl, post_ple_w, tgt)

    tkt = _tile(t, TK)
    g_wple = _matmul(
        "mm_dw_ple",
        [(ps, pl.BlockSpec((tkt, ple), lambda n, k: (k, 0))), (dpe, pl.BlockSpec((tkt, n_ple), lambda n, k: (k, n)))],
        [(0, 1, "tn", 0)],
        [(jax.ShapeDtypeStruct((N_DEV, ple, n_ple), BF16), pl.BlockSpec((None, ple, n_ple), lambda n, k: (n, 0, 0)))],
        (N_DEV, t // tkt), (ple, n_ple))[0]
    g_wpg = _mm_plain("mm_dw_ple_gate", x2b, dpgl, "tn", BF16, TM, 2048, TK)
    d2 = _mm_plain("mm_d_x2", dpgl, wpg_f, "nt", F32, TM, 1024, TK, extra=d3, epilogue=lambda accs, e: [accs[0] + e])

    dff, g_post_ffn = _norm_bwd("norm_bwd_ffn", d2, ff, post_ffn_w)
    g_wdown = _matmul(
        "mm_dw_down",
        [(act, pl.BlockSpec((None, tkt, ffl), lambda j, n, k: (j, k, 0))), (dff, pl.BlockSpec((tkt, tn_d), lambda j, n, k: (k, n)))],
        [(0, 1, "tn", 0)],
        [(jax.ShapeDtypeStruct((N_DEV, ffl, d), BF16), pl.BlockSpec((None, ffl, tn_d), lambda j, n, k: (j, 0, n)))],
        (N_DEV, d // tn_d, t // tkt), (ffl, tn_d))[0]

    def swiglu_bwd(accs, gate_v, up_v):
        dact = accs[0]
        gf = gate_v.astype(F32)
        sg = _sigmoid(gf)
        return dact * up_v.astype(F32) * (sg * (1.0 + gf * (1.0 - sg))), dact * (gf * sg)

    dgate, dup = _matmul(
        "mm_d_act",
        [(dff, pl.BlockSpec((tmf, tkd), lambda j, m, k: (m, k))),
         (wdown_g, pl.BlockSpec((None, ffl, tkd), lambda j, m, k: (j, 0, k))),
         (gate, blk3()), (up, blk3())],
        [(0, 1, "nt", 0)],
        [(jax.ShapeDtypeStruct((N_DEV, t, ffl), BF16), blk3()) for _ in range(2)],
        (N_DEV, t // tmf, d // tkd), (tmf, ffl), epilogue=swiglu_bwd)
    tmd = _tile(d, TM)
    g_wgate, g_wup = _matmul(
        "mm_dw_gate_up",
        [(h2, pl.BlockSpec((tkt, tmd), lambda j, m, k: (k, m))),
         (dgate, pl.BlockSpec((None, tkt, ffl), lambda j, m, k: (j, k, 0))),
         (dup, pl.BlockSpec((None, tkt, ffl), lambda j, m, k: (j, k, 0)))],
        [(0, 1, "tn", 0), (0, 2, "tn", 1)],
        [(jax.ShapeDtypeStruct((N_DEV, d, ffl), BF16), pl.BlockSpec((None, tmd, ffl), lambda j, m, k: (j, m, 0))) for _ in range(2)],
        (N_DEV, d // tmd, t // tkt), (tmd, ffl), n_slots=2)
    tn1 = _tile(d, 1024)
    dh2 = _matmul(
        "mm_d_h2",
        [(dgate, pl.BlockSpec((None, tm, ffl), lambda m, n, k: (k, m, 0))),
         (wgate_g, pl.BlockSpec((None, tn1, ffl), lambda m, n, k: (k, n, 0))),
         (dup, pl.BlockSpec((None, tm, ffl), lambda m, n, k: (k, m, 0))),
         (wup_g, pl.BlockSpec((None, tn1, ffl), lambda m, n, k: (k, n, 0)))],
        [(0, 1, "nt", 0), (2, 3, "nt", 0)],
        [(jax.ShapeDtypeStruct((t, d), F32), pl.BlockSpec((tm, tn1), lambda m, n, k: (m, n)))],
        (t // tm, d // tn1, N_DEV), (tm, tn1))[0]
    d1, g_pre_ffn, dmix, g_post_mix = _prenorm_bwd("prenorm_bwd_ffn", d2, dh2, x1, pre_ffn_w, mix, post_mix_w)

    g_wout = _mm_plain("mm_dw_out", ab, dmix, "tn", BF16, TM, 2048, TK)
    dab = _mm_plain("mm_d_ab", dmix, wout_f, "nt", F32, TM, 2048, TK)
    dq, df, di, dg, dlb, g_a_norm = _hgrn_bwd(proj, lb, a_norm_w, o_raw, states, scores, dab, n_heads)
    du, dv, g_ln_w, g_ln_b, g_wsp, g_bs_t = _gmlp_bwd(proj, gmlp_ln_w, gmlp_ln_b, w_sp, bs_t, dab, n_groups, col_u)
    dproj = jnp.concatenate([dq, df, di, dg, du, dv], axis=1)
    g_win = _matmul(
        "mm_dw_in",
        [(h1, pl.BlockSpec((tkt, tmd), lambda j, m, k: (k, m))), (dproj, pl.BlockSpec((tkt, n_in), lambda j, m, k: (k, j)))],
        [(0, 1, "tn", 0)],
        [(jax.ShapeDtypeStruct((N_DEV, d, n_in), BF16), pl.BlockSpec((None, tmd, n_in), lambda j, m, k: (j, m, 0)))],
        (N_DEV, d // tmd, t // tkt), (tmd, n_in))[0]
    dh1 = _matmul(
        "mm_d_h1",
        [(dproj, pl.BlockSpec((tm, n_in), lambda m, n, k: (m, k))), (win_g, pl.BlockSpec((None, tn1, n_in), lambda m, n, k: (k, n, 0)))],
        [(0, 1, "nt", 0)],
        [(jax.ShapeDtypeStruct((t, d), F32), pl.BlockSpec((tm, tn1), lambda m, n, k: (m, n)))],
        (t // tm, d // tn1, N_DEV), (tm, tn1))[0]
    grad_x, g_pre_mix = _prenorm_bwd("prenorm_bwd_mix", d1, dh1, xs, pre_mix_w)

    blocked = {
        "w_in": g_win, "w_out": g_wout.reshape(N_DEV, d // N_DEV, d), "w_gate": g_wgate, "w_up": g_wup,
        "w_down": g_wdown, "w_ple": g_wple, "w_ple_gate": g_wpg.reshape(N_DEV, d // N_DEV, d),
    }
    by_chip = [blocked[n].reshape((N_CHIP, 2) + blocked[n].shape[1:]) for n in big_names]
    from_sibling = _pair_exchange("rs_pair", by_chip)
    chip_sums = [_pair_sum("pair_sum_" + n, g, o, core) for n, g, o in zip(big_names, by_chip, from_sibling)]
    reduced = dict(zip(big_names, _chip_exchange("rs_chips", chip_sums)))

    grads, deltas, new_m, new_v = {}, {}, {}, {}
    for n in big_names:
        g, dl, mo, vo = _adamw("adamw_" + n, reduced[n], shard[n], M[n][0], V[n][0])
        grads[n], deltas[n], new_m[n], new_v[n] = g[None], dl[None], mo[None], vo[None]

    small_grad = {
        "pre_mix_w": g_pre_mix, "lb_param": _lb_bwd(lb_param, dlb), "a_norm_w": g_a_norm, "gmlp_ln_w": g_ln_w,
        "gmlp_ln_b": g_ln_b, "w_spatial": g_wsp, "b_spatial": g_bs_t.T, "post_mix_w": g_post_mix,
        "pre_ffn_w": g_pre_ffn, "post_ffn_w": g_post_ffn, "post_ple_w": g_post_ple,
    }
    pack = lambda get: jnp.concatenate([get(n).reshape(-1, LANE) for n in small_names], axis=0)
    g_all = _all_gather("ag_small", [pack(lambda n: small_grad[n])], in_vmem=True)[0]
    sg, sd, sm, sv = _adamw("adamw_small", g_all, pack(lambda n: W[n]), pack(lambda n: M[n]), pack(lambda n: V[n]))
    off = 0
    for n in small_names:
        rows = W[n].size // LANE
        for src, dst in ((sg, grads), (sd, deltas), (sm, new_m), (sv, new_v)):
            dst[n] = src[off:off + rows].reshape(W[n].shape)
        off += rows

    loss = lax.psum(loss_part[0, 0], ("x", "y", "c"))
    return (loss, grad_x[None], *[grads[n] for n in all_names], *[deltas[n] for n in all_names],
            *[new_m[n] for n in all_names], *[new_v[n] for n in all_names])
```

```python
import functools

import numpy as np
import jax
import jax.numpy as jnp
from jax import lax
from jax.experimental import pallas as pl
from jax.experimental.pallas import tpu as pltpu

F32 = jnp.float32
BF16 = jnp.bfloat16

EPS = 1e-6
HEAD = 128
GLA_CHUNK = 64
GMLP_CHUNK = 128
N_DEV = 8
N_CHIP = 4
LANE = 128
VMEM_LIMIT = 56 * 1024 * 1024
HGRN_ROWS = 512
ROW_TILE = 128

ADAM_LR = 0.001
ADAM_B1 = 0.9
ADAM_B2 = 0.999
ADAM_EPS = 1e-08
ADAM_WD = 0.01
ADAM_STEP = 10

MESH = pl.DeviceIdType.MESH
ANY = pl.BlockSpec(memory_space=pl.ANY)

_DIMS = {
    "nn": (((1,), (0,)), ((), ())),
    "nt": (((1,), (1,)), ((), ())),
    "tn": (((0,), (0,)), ((), ())),
}


def _tile(dim, pref):
    return pref if dim % pref == 0 else dim


def _rows(r, bytes_per_row, budget=4 * 1024 * 1024, mult=16):
    best = None
    for cand in range(mult, r + 1, mult):
        if r % cand == 0 and cand * bytes_per_row <= budget:
            best = cand
    return best if best is not None else r


def _cparams(n_axes):
    return pltpu.CompilerParams(dimension_semantics=("arbitrary",) * n_axes, vmem_limit_bytes=VMEM_LIMIT)


def _dot(a, b, form="nn"):
    return lax.dot_general(a.astype(BF16), b.astype(BF16), _DIMS[form], preferred_element_type=F32)


def _split3(x):
    hi = x.astype(BF16)
    r = x - hi.astype(F32)
    mid = r.astype(BF16)
    lo = (r - mid.astype(F32)).astype(BF16)
    return hi, mid, lo


def _dot_exact_l(c, x):
    hi, mid, lo = _split3(x)
    d = lambda y: lax.dot_general(c, y, _DIMS["nn"], preferred_element_type=F32)
    return d(hi) + d(mid) + d(lo)


def _dot_exact_r(x, c):
    hi, mid, lo = _split3(x)
    d = lambda y: lax.dot_general(y, c, _DIMS["nn"], preferred_element_type=F32)
    return d(hi) + d(mid) + d(lo)


def _sigmoid(x):
    return 1.0 / (1.0 + jnp.exp(-x))


def _gelu(x):
    return 0.5 * x * (1.0 + lax.erf(x * 0.7071067811865476))


def _gelu_grad(x):
    cdf = 0.5 * (1.0 + lax.erf(x * 0.7071067811865476))
    pdf = jnp.exp(-0.5 * x * x) * 0.3989422804014327
    return cdf + x * pdf


def _position():
    return lax.axis_index("x"), lax.axis_index("y"), lax.axis_index("c")


def _linear_step(grid):
    step = 0
    for ax, n in enumerate(grid):
        step = step * n + pl.program_id(ax)
    return step


class _Comm:
    def __init__(self, arrays, out_shapes, sem_shapes, phases):
        self.arrays, self.out_shapes, self.sem_shapes, self.phases = list(arrays), list(out_shapes), list(sem_shapes), phases
        self.results = None


class _CommLayout:
    def __init__(self, comms, space=pl.ANY):
        self.comms = list(comms)
        self.arrays = [a for c in self.comms for a in c.arrays]
        self.out_shapes = [s for c in self.comms for s in c.out_shapes]
        self.sem_shapes = [s for c in self.comms for s in c.sem_shapes]
        self.n_in, self.n_out = len(self.arrays), len(self.out_shapes)
        self.in_specs = [pl.BlockSpec(memory_space=space)] * self.n_in
        self.out_specs = [pl.BlockSpec(memory_space=space)] * self.n_out

    def run(self, cin, cout, csem, step, n_steps, post):
        i = o = s = 0
        for c in self.comms:
            ins, outs, sems = cin[i:i + len(c.arrays)], cout[o:o + len(c.out_shapes)], csem[s:s + len(c.sem_shapes)]
            i, o, s = i + len(c.arrays), o + len(c.out_shapes), s + len(c.sem_shapes)
            for frac, fn in c.phases:
                if (frac is None) != post:
                    continue
                due = n_steps - 1 if frac is None else max(0, min(int(frac * n_steps), n_steps - 2))
                if n_steps == 1:
                    fn(ins, outs, sems)
                else:
                    pl.when(step == due)(functools.partial(fn, ins, outs, sems))

    def deliver(self, results):
        o = 0
        for c in self.comms:
            c.results = list(results[o:o + len(c.out_shapes)])
            o += len(c.out_shapes)


def _comm_only(name, comms, in_vmem=False):
    lay = _CommLayout(comms, pltpu.VMEM if in_vmem else pl.ANY)

    def body(*refs):
        cin, cout, csem = refs[:lay.n_in], refs[lay.n_in:lay.n_in + lay.n_out], refs[lay.n_in + lay.n_out:]
        lay.run(cin, cout, csem, 0, 1, post=False)
        lay.run(cin, cout, csem, 0, 1, post=True)

    res = pl.pallas_call(
        body, name=name, in_specs=lay.in_specs, out_specs=lay.out_specs, out_shape=lay.out_shapes,
        scratch_shapes=lay.sem_shapes,
    )(*lay.arrays)
    lay.deliver(res)


def _ag_comm(shards, mid_frac=0.0):
    n = len(shards)
    per = N_DEV - 1

    def tools(ins, outs, sems):
        send_sems, recv_sems, local_sems = sems
        x, y, c = _position()
        me, sibling = (x, y, c), (x, y, 1 - c)
        chips = [(1 - x, y), (x, 1 - y), (1 - x, 1 - y)]

        def copy(a, k, block, to, from_shard=False):
            dst = outs[a].at[4 * block[0] + 2 * block[1] + block[2]]
            return pltpu.make_async_remote_copy(
                src_ref=ins[a] if from_shard else dst, dst_ref=dst,
                send_sem=send_sems.at[a * per + k], recv_sem=recv_sems.at[a * per + k],
                device_id=to, device_id_type=MESH)

        def local(a):
            return pltpu.make_async_copy(ins[a], outs[a].at[4 * x + 2 * y + c], local_sems.at[a])

        return me, sibling, chips, c, copy, local

    def first(ins, outs, sems):
        me, sibling, chips, c, copy, local = tools(ins, outs, sems)
        for a in range(n):
            local(a).start()
            copy(a, 0, me, sibling, True).start()
            for j, chip in enumerate(chips):
                copy(a, 1 + j, me, (*chip, c), True).start()

    def middle(ins, outs, sems):
        me, sibling, chips, c, copy, local = tools(ins, outs, sems)
        for a in range(n):
            for j, chip in enumerate(chips):
                copy(a, 1 + j, (*chip, c), me).wait_recv()
                copy(a, 4 + j, (*chip, c), sibling).start()

    def last(ins, outs, sems):
        me, sibling, chips, c, copy, local = tools(ins, outs, sems)
        for a in range(n):
            copy(a, 0, sibling, me).wait_recv()
            copy(a, 0, me, sibling, True).wait_send()
            for j, chip in enumerate(chips):
                copy(a, 4 + j, (*chip, 1 - c), me).wait_recv()
                copy(a, 1 + j, me, (*chip, c), True).wait_send()
                copy(a, 4 + j, (*chip, c), sibling).wait_send()
            local(a).wait()

    return _Comm(
        shards, [jax.ShapeDtypeStruct((N_DEV,) + s.shape, s.dtype) for s in shards],
        [pltpu.SemaphoreType.DMA((n * per,)), pltpu.SemaphoreType.DMA((n * per,)), pltpu.SemaphoreType.DMA((n,))],
        [(0.0, first), (mid_frac, middle), (None, last)])


def _pair_comm(grads):
    n = len(grads)

    def copies(ins, outs, sems):
        send_sems, recv_sems = sems
        x, y, c = _position()
        return [pltpu.make_async_remote_copy(
            src_ref=ins[a].at[k, 1 - c], dst_ref=outs[a].at[k],
            send_sem=send_sems.at[a * N_CHIP + k], recv_sem=recv_sems.at[a * N_CHIP + k],
            device_id=(x, y, 1 - c), device_id_type=MESH) for a in range(n) for k in range(N_CHIP)]

    def first(ins, outs, sems):
        for cp in copies(ins, outs, sems):
            cp.start()

    def last(ins, outs, sems):
        for cp in copies(ins, outs, sems):
            cp.wait()

    return _Comm(
        grads, [jax.ShapeDtypeStruct((N_CHIP,) + g.shape[2:], g.dtype) for g in grads],
        [pltpu.SemaphoreType.DMA((n * N_CHIP,)), pltpu.SemaphoreType.DMA((n * N_CHIP,))],
        [(0.0, first), (None, last)])


def _chip_comm(sums):
    n = len(sums)
    per = N_CHIP - 1

    def copies(ins, outs, sems):
        send_sems, recv_sems, local_sems = sems
        x, y, c = _position()
        my_chip = 2 * x + y
        cps = []
        for a in range(n):
            cps.append(pltpu.make_async_copy(ins[a].at[my_chip], outs[a].at[my_chip], local_sems.at[a]))
            for j, (px, py) in enumerate([(1 - x, y), (x, 1 - y), (1 - x, 1 - y)]):
                cps.append(pltpu.make_async_remote_copy(
                    src_ref=ins[a].at[2 * px + py], dst_ref=outs[a].at[my_chip],
                    send_sem=send_sems.at[a * per + j], recv_sem=recv_sems.at[a * per + j],
                    device_id=(px, py, c), device_id_type=MESH))
        return cps

    def first(ins, outs, sems):
        for cp in copies(ins, outs, sems):
            cp.start()

    def last(ins, outs, sems):
        for cp in copies(ins, outs, sems):
            cp.wait()

    return _Comm(
        sums, [jax.ShapeDtypeStruct(s.shape, s.dtype) for s in sums],
        [pltpu.SemaphoreType.DMA((n * per,)), pltpu.SemaphoreType.DMA((n * per,)), pltpu.SemaphoreType.DMA((n,))],
        [(0.0, first), (None, last)])


def _matmul(name, operands, pairs, outs, grid, acc_shape, n_slots=1, epilogue=None, comms=()):
    used = sorted({i for p in pairs for i in p[:2]})
    n_op = len(operands)
    n_out = len(outs)
    k_axis = len(grid) - 1
    n_k = grid[-1]
    lay = _CommLayout(comms)

    def body(*refs):
        ops = refs[:n_op]
        out_refs = refs[n_op + lay.n_in:n_op + lay.n_in + n_out]
        acc = refs[n_op + lay.n_in + n_out + lay.n_out]
        k = pl.program_id(k_axis)
        step = _linear_step(grid)
        cin = refs[n_op:n_op + lay.n_in]
        cout = refs[n_op + lay.n_in + n_out:n_op + lay.n_in + n_out + lay.n_out]
        csem = refs[n_op + lay.n_in + n_out + lay.n_out + 1:]
        lay.run(cin, cout, csem, step, int(np.prod(grid)), post=False)

        @pl.when(k == 0)
        def _():
            acc[...] = jnp.zeros_like(acc)

        vals = {i: ops[i][...] for i in used}
        vals = {i: (v if v.dtype == BF16 else v.astype(BF16)) for i, v in vals.items()}
        for s in range(n_slots):
            tot = None
            for ia, ib, form, slot in pairs:
                if slot != s:
                    continue
                d = lax.dot_general(vals[ia], vals[ib], _DIMS[form], preferred_element_type=F32)
                tot = d if tot is None else tot + d
            acc[s] += tot

        @pl.when(k == n_k - 1)
        def _():
            accs = [acc[s] for s in range(n_slots)]
            extra = [ops[i][...] for i in range(n_op) if i not in used]
            res = epilogue(accs, *extra) if epilogue is not None else accs
            for o, v in zip(out_refs, res):
                o[...] = v.astype(o.dtype)

        lay.run(cin, cout, csem, step, int(np.prod(grid)), post=True)

    res = pl.pallas_call(
        body,
        name=name,
        grid=grid,
        in_specs=[s for _, s in operands] + lay.in_specs,
        out_specs=[s for _, s in outs] + lay.out_specs,
        out_shape=[s for s, _ in outs] + lay.out_shapes,
        scratch_shapes=[pltpu.VMEM((n_slots,) + tuple(acc_shape), F32)] + lay.sem_shapes,
        compiler_params=_cparams(len(grid)),
    )(*[a for a, _ in operands], *lay.arrays)
    lay.deliver(res[n_out:])
    return res[:n_out]


def _mm_plain(name, a, b, form, out_dtype, tm, tn, tk, extra=None, epilogue=None, comms=()):
    if form == "nn":
        (M, K), N = a.shape, b.shape[1]
    elif form == "nt":
        (M, K), N = a.shape, b.shape[0]
    else:
        (K, M), N = a.shape, b.shape[1]
    tm, tn, tk = _tile(M, tm), _tile(N, tn), _tile(K, tk)
    a_spec = pl.BlockSpec((tk, tm), lambda m, n, k: (k, m)) if form == "tn" else pl.BlockSpec((tm, tk), lambda m, n, k: (m, k))
    b_spec = pl.BlockSpec((tn, tk), lambda m, n, k: (n, k)) if form == "nt" else pl.BlockSpec((tk, tn), lambda m, n, k: (k, n))
    operands = [(a, a_spec), (b, b_spec)]
    if extra is not None:
        operands.append((extra, pl.BlockSpec((tm, tn), lambda m, n, k: (m, n))))
    out = (jax.ShapeDtypeStruct((M, N), out_dtype), pl.BlockSpec((tm, tn), lambda m, n, k: (m, n)))
    return _matmul(name, operands, [(0, 1, form, 0)], [out], (M // tm, N // tn, K // tk), (tm, tn), epilogue=epilogue,
                   comms=comms)[0]


def _cast_bf16(name, w):
    r, c = w.shape
    tr = _rows(r, 6 * c)

    def body(w_ref, o_ref):
        o_ref[...] = w_ref[...].astype(BF16)

    return pl.pallas_call(
        body, name=name, grid=(r // tr,),
        in_specs=[pl.BlockSpec((tr, c), lambda i: (i, 0))],
        out_specs=pl.BlockSpec((tr, c), lambda i: (i, 0)),
        out_shape=jax.ShapeDtypeStruct((r, c), BF16),
        compiler_params=_cparams(1),
    )(w)


def _rms_stats(x):
    r = lax.rsqrt(jnp.mean(x * x, axis=-1, keepdims=True) + EPS)
    return x * r, r


def _rms_bwd(xhat, r, w, dy):
    dxh = dy * w
    return r * (dxh - xhat * jnp.mean(dxh * xhat, axis=-1, keepdims=True))


def _row_spec(tr, d):
    return pl.BlockSpec((tr, d), lambda i: (i, 0))


def _vec_spec(d):
    return pl.BlockSpec((1, d), lambda i: (0, 0))


def _rms_fwd(name, x, w):
    t, d = x.shape
    tr = _tile(t, ROW_TILE)

    def body(x_ref, w_ref, h_ref):
        xh, _ = _rms_stats(x_ref[...])
        h_ref[...] = (xh * w_ref[...]).astype(BF16)

    return pl.pallas_call(
        body, name=name, grid=(t // tr,),
        in_specs=[_row_spec(tr, d), _vec_spec(d)],
        out_specs=_row_spec(tr, d),
        out_shape=jax.ShapeDtypeStruct((t, d), BF16),
        compiler_params=_cparams(1),
    )(x, w)


def _resid_rms(name, xres, y, w_post, w_next):
    t, d = xres.shape
    tr = _tile(t, ROW_TILE)
    has_next = w_next is not None

    def body(*refs):
        if has_next:
            x_ref, y_ref, wp_ref, wn_ref, xo_ref, h_ref = refs
        else:
            x_ref, y_ref, wp_ref, xo_ref, h_ref = refs
        yh, _ = _rms_stats(y_ref[...])
        xn = x_ref[...] + yh * wp_ref[...]
        xo_ref[...] = xn
        if has_next:
            xh, _ = _rms_stats(xn)
            h_ref[...] = (xh * wn_ref[...]).astype(BF16)
        else:
            h_ref[...] = xn.astype(BF16)

    ins = [xres, y, w_post] + ([w_next] if has_next else [])
    in_specs = [_row_spec(tr, d), _row_spec(tr, d), _vec_spec(d)] + ([_vec_spec(d)] if has_next else [])
    return pl.pallas_call(
        body, name=name, grid=(t // tr,),
        in_specs=in_specs,
        out_specs=[_row_spec(tr, d), _row_spec(tr, d)],
        out_shape=[jax.ShapeDtypeStruct((t, d), F32), jax.ShapeDtypeStruct((t, d), BF16)],
        compiler_params=_cparams(1),
    )(*ins)


def _ple_loss(name, x2, pe, pgl, w_pp, tgt):
    t, d = x2.shape
    tr = _tile(t, ROW_TILE)

    def body(x2_ref, pe_ref, pgl_ref, w_ref, tgt_ref, loss_ref, d3_ref, dpe_ref, dpgl_ref, dw_ref):
        @pl.when(pl.program_id(0) == 0)
        def _():
            loss_ref[...] = jnp.zeros_like(loss_ref)
            dw_ref[...] = jnp.zeros_like(dw_ref)

        pe_v = pe_ref[...]
        s = _sigmoid(pgl_ref[...])
        y = pe_v * s
        yh, r = _rms_stats(y)
        w = w_ref[...]
        err = x2_ref[...] + yh * w - tgt_ref[...]
        loss_ref[...] += 0.5 * jnp.sum(jnp.mean(err * err, axis=-1, keepdims=True), axis=0, keepdims=True)
        d3 = err * (1.0 / d)
        d3_ref[...] = d3
        dw_ref[...] += jnp.sum(d3 * yh, axis=0, keepdims=True)
        dy = _rms_bwd(yh, r, w, d3)
        dpe_ref[...] = (dy * s).astype(BF16)
        dpgl_ref[...] = (dy * pe_v * s * (1.0 - s)).astype(BF16)

    return pl.pallas_call(
        body, name=name, grid=(t // tr,),
        in_specs=[_row_spec(tr, d), _row_spec(tr, d), _row_spec(tr, d), _vec_spec(d), _row_spec(tr, d)],
        out_specs=[pl.BlockSpec((1, 1), lambda i: (0, 0)), _row_spec(tr, d), _row_spec(tr, d), _row_spec(tr, d), _vec_spec(d)],
        out_shape=[jax.ShapeDtypeStruct((1, 1), F32), jax.ShapeDtypeStruct((t, d), F32),
                   jax.ShapeDtypeStruct((t, d), BF16), jax.ShapeDtypeStruct((t, d), BF16),
                   jax.ShapeDtypeStruct((1, d), F32)],
        compiler_params=_cparams(1),
    )(x2, pe, pgl, w_pp, tgt)


def _norm_bwd(name, dres, y, w_post):
    t, d = dres.shape
    tr = _tile(t, ROW_TILE)

    def body(d_ref, y_ref, w_ref, dy_ref, dw_ref):
        @pl.when(pl.program_id(0) == 0)
        def _():
            dw_ref[...] = jnp.zeros_like(dw_ref)

        dv = d_ref[...]
        yh, r = _rms_stats(y_ref[...])
        dw_ref[...] += jnp.sum(dv * yh, axis=0, keepdims=True)
        dy_ref[...] = _rms_bwd(yh, r, w_ref[...], dv).astype(BF16)

    return pl.pallas_call(
        body, name=name, grid=(t // tr,),
        in_specs=[_row_spec(tr, d), _row_spec(tr, d), _vec_spec(d)],
        out_specs=[_row_spec(tr, d), _vec_spec(d)],
        out_shape=[jax.ShapeDtypeStruct((t, d), BF16), jax.ShapeDtypeStruct((1, d), F32)],
        compiler_params=_cparams(1),
    )(dres, y, w_post)


def _prenorm_bwd(name, dres, dh, xin, w_pre, y=None, w_post=None):
    t, d = dres.shape
    tr = _tile(t, ROW_TILE)
    two = y is not None

    def body(*refs):
        if two:
            d_ref, dh_ref, x_ref, wpre_ref, y_ref, wpost_ref, do_ref, dwpre_ref, dy_ref, dwpost_ref = refs
        else:
            d_ref, dh_ref, x_ref, wpre_ref, do_ref, dwpre_ref = refs

        @pl.when(pl.program_id(0) == 0)
        def _():
            dwpre_ref[...] = jnp.zeros_like(dwpre_ref)
            if two:
                dwpost_ref[...] = jnp.zeros_like(dwpost_ref)

        dhv = dh_ref[...]
        xh, r = _rms_stats(x_ref[...])
        dwpre_ref[...] += jnp.sum(dhv * xh, axis=0, keepdims=True)
        dout = d_ref[...] + _rms_bwd(xh, r, wpre_ref[...], dhv)
        do_ref[...] = dout
        if two:
            yh, ry = _rms_stats(y_ref[...])
            dwpost_ref[...] += jnp.sum(dout * yh, axis=0, keepdims=True)
            dy_ref[...] = _rms_bwd(yh, ry, wpost_ref[...], dout).astype(BF16)

    ins = [dres, dh, xin, w_pre] + ([y, w_post] if two else [])
    in_specs = [_row_spec(tr, d)] * 3 + [_vec_spec(d)] + ([_row_spec(tr, d), _vec_spec(d)] if two else [])
    out_specs = [_row_spec(tr, d), _vec_spec(d)] + ([_row_spec(tr, d), _vec_spec(d)] if two else [])
    out_shape = [jax.ShapeDtypeStruct((t, d), F32), jax.ShapeDtypeStruct((1, d), F32)]
    if two:
        out_shape += [jax.ShapeDtypeStruct((t, d), BF16), jax.ShapeDtypeStruct((1, d), F32)]
    return pl.pallas_call(
        body, name=name, grid=(t // tr,),
        in_specs=in_specs, out_specs=out_specs, out_shape=out_shape,
        compiler_params=_cparams(1),
    )(*ins)


_LEVELS = (32, 16, 8, 4, 2, 1)
_N_CUM = 3 + 2 * len(_LEVELS)


def _hgrn_constants():
    c = GLA_CHUNK
    idx = np.arange(c)
    t, r = idx[:, None], idx[None, :]
    mats = [(r <= t), (r > t), np.ones((c, c), bool)]
    lq, lk, masks = [], [], []
    for h in _LEVELS:
        blk, pos = idx // (2 * h), idx % (2 * h)
        mid = blk * 2 * h + h - 1
        upper, lower = pos >= h, pos < h
        lq.append(upper[:, None] & (r > mid[:, None]) & (r <= t))
        lk.append(lower[:, None] & (r > t) & (r <= mid[:, None]))
        masks.append((blk[:, None] == blk[None, :]) & upper[:, None] & lower[None, :])
    cum = np.concatenate(mats + lq + lk, axis=0).astype(np.float32)
    rev = (r >= t).astype(np.float32)
    return (jnp.asarray(cum, BF16), jnp.asarray(rev, BF16), jnp.asarray(np.stack(masks).astype(np.float32)))


def _hgrn_gates(qp, fp, lb):
    sq = _sigmoid(qp)
    q = qp * sq
    sg = _sigmoid(fp)
    f = lb + (1.0 - lb) * sg
    k = 1.0 - f
    logf = jnp.log(jnp.maximum(f, 1e-30))
    return q, sq, sg, f, k, logf


def _hgrn_decays(cum_ref, logf):
    c = GLA_CHUNK
    e = jnp.exp(_dot_exact_l(cum_ref[...], logf))
    part = lambda i: e[i * c:(i + 1) * c]
    n = len(_LEVELS)
    return part(0), part(1), part(2), [part(3 + i) for i in range(n)], [part(3 + n + i) for i in range(n)]


def _hgrn_fwd(proj, lb, nw, n_heads, comms=()):
    t = proj.shape[0]
    aw = n_heads * HEAD
    rb = _tile(t, HGRN_ROWS)
    c = GLA_CHUNK
    n_sub = rb // c
    cum, _, masks = _hgrn_constants()
    lay = _CommLayout(comms)
    grid = (n_heads, t // rb)

    def body(*refs):
        q_ref, f_ref, i_ref, g_ref, lb_ref, nw_ref, cum_ref, m_ref = refs[:8]
        cin = refs[8:8 + lay.n_in]
        a_ref, o_ref, s_ref, sc_ref = refs[8 + lay.n_in:12 + lay.n_in]
        cout = refs[12 + lay.n_in:12 + lay.n_in + lay.n_out]
        st = refs[12 + lay.n_in + lay.n_out]
        csem = refs[13 + lay.n_in + lay.n_out:]
        step = _linear_step(grid)
        lay.run(cin, cout, csem, step, grid[0] * grid[1], post=False)

        @pl.when(pl.program_id(1) == 0)
        def _():
            st[...] = jnp.zeros_like(st)

        lbv = lb_ref[...]
        nwv = nw_ref[...]
        eye = (lax.broadcasted_iota(jnp.int32, (c, c), 0) == lax.broadcasted_iota(jnp.int32, (c, c), 1)).astype(F32)

        def chunk(j, carry):
            rows = pl.ds(pl.multiple_of(j * c, c), c)
            q, _, _, _, k, logf = _hgrn_gates(q_ref[rows, :], f_ref[rows, :], lbv)
            v = i_ref[rows, :]
            eb, ebe, eend, eq, ek = _hgrn_decays(cum_ref, logf)
            s_ref[j] = st[...]
            inter = _dot(q * eb, st[...], "nt")
            scores = eye * jnp.sum(q * k, axis=-1, keepdims=True)
            for lvl in range(len(_LEVELS)):
                scores = scores + m_ref[lvl] * _dot(q * eq[lvl], k * ek[lvl], "nt")
            sc_ref[rows, :] = scores
            o = inter + _dot(scores, v)
            st[...] = st[...] * eend[0:1] + _dot(v, k * ebe, "tn")
            o_ref[rows, :] = o
            r = lax.rsqrt(jnp.mean(o * o, axis=-1, keepdims=True) + EPS)
            gv = g_ref[rows, :]
            a_ref[rows, :] = (o * r * nwv * (gv * _sigmoid(gv))).astype(BF16)
            return carry

        lax.fori_loop(0, n_sub, chunk, 0)
        lay.run(cin, cout, csem, step, grid[0] * grid[1], post=True)

    col = lambda base: pl.BlockSpec((rb, HEAD), lambda h, r: (r, base * n_heads + h))
    vec = pl.BlockSpec((1, HEAD), lambda h, r: (0, h))
    res = pl.pallas_call(
        body, name="hgrn2_fwd", grid=grid,
        in_specs=[col(0), col(1), col(2), col(3), vec, vec,
                  pl.BlockSpec(cum.shape, lambda h, r: (0, 0)), pl.BlockSpec(masks.shape, lambda h, r: (0, 0, 0))] + lay.in_specs,
        out_specs=[pl.BlockSpec((rb, HEAD), lambda h, r: (r, h)), pl.BlockSpec((rb, HEAD), lambda h, r: (r, h)),
                   pl.BlockSpec((None, n_sub, HEAD, HEAD), lambda h, r: (h, r, 0, 0)),
                   pl.BlockSpec((None, rb, c), lambda h, r: (h, r, 0))] + lay.out_specs,
        out_shape=[jax.ShapeDtypeStruct((t, aw), BF16), jax.ShapeDtypeStruct((t, aw), F32),
                   jax.ShapeDtypeStruct((n_heads, t // c, HEAD, HEAD), F32),
                   jax.ShapeDtypeStruct((n_heads, t, c), F32)] + lay.out_shapes,
        scratch_shapes=[pltpu.VMEM((HEAD, HEAD), F32)] + lay.sem_shapes,
        compiler_params=_cparams(2),
    )(proj, proj, proj, proj, lb, nw, cum, masks, *lay.arrays)
    lay.deliver(res[4:])
    return res[:4]


def _hgrn_bwd(proj, lb, nw, o_raw, states, scores, dab, n_heads, comms=()):
    t = proj.shape[0]
    aw = n_heads * HEAD
    rb = _tile(t, HGRN_ROWS)
    c = GLA_CHUNK
    n_sub = rb // c
    n_rb = t // rb
    cum, rev, masks = _hgrn_constants()
    lay = _CommLayout(comms)
    grid = (n_heads, n_rb)

    def body(*refs):
        q_ref, f_ref, i_ref, g_ref, lb_ref, nw_ref, o_ref, s_ref, sc_ref, da_ref, cum_ref, rev_ref, m_ref = refs[:13]
        cin = refs[13:13 + lay.n_in]
        dq_ref, df_ref, di_ref, dg_ref, dlb_ref, dnw_ref = refs[13 + lay.n_in:19 + lay.n_in]
        cout = refs[19 + lay.n_in:19 + lay.n_in + lay.n_out]
        dst = refs[19 + lay.n_in + lay.n_out]
        csem = refs[20 + lay.n_in + lay.n_out:]
        step = _linear_step(grid)
        lay.run(cin, cout, csem, step, grid[0] * grid[1], post=False)

        @pl.when(pl.program_id(1) == 0)
        def _():
            dst[...] = jnp.zeros_like(dst)
            dlb_ref[...] = jnp.zeros_like(dlb_ref)
            dnw_ref[...] = jnp.zeros_like(dnw_ref)

        lbv = lb_ref[...]
        nwv = nw_ref[...]
        ri = lax.broadcasted_iota(jnp.int32, (c, c), 0)
        ci = lax.broadcasted_iota(jnp.int32, (c, c), 1)
        eye = (ri == ci).astype(F32)
        causal = (ci <= ri).astype(F32)
        last_row = (lax.broadcasted_iota(jnp.int32, (c, HEAD), 0) == c - 1).astype(F32)

        def chunk(jj, carry):
            j = n_sub - 1 - jj
            rows = pl.ds(pl.multiple_of(j * c, c), c)
            qp = q_ref[rows, :]
            q, sq, sg, f, k, logf = _hgrn_gates(qp, f_ref[rows, :], lbv)
            v = i_ref[rows, :]
            gv = g_ref[rows, :]
            eb, ebe, eend, eq, ek = _hgrn_decays(cum_ref, logf)
            s_in = s_ref[j]
            a_sc = sc_ref[rows, :]
            dsn = dst[...]
            o = o_ref[rows, :]
            r = lax.rsqrt(jnp.mean(o * o, axis=-1, keepdims=True) + EPS)
            oh = o * r
            sgg = _sigmoid(gv)
            sil = gv * sgg
            da = da_ref[rows, :]
            dg_ref[rows, :] = (da * oh * nwv * (sgg * (1.0 + gv * (1.0 - sgg)))).astype(BF16)
            dnw_ref[...] += jnp.sum(da * oh * sil, axis=0, keepdims=True)
            doh = da * nwv * sil
            do = r * (doh - oh * jnp.mean(doh * oh, axis=-1, keepdims=True))
            kt = k * ebe
            qt = q * eb
            di_ref[rows, :] = (_dot(a_sc, do, "tn") + _dot(kt, dsn, "nt")).astype(BF16)
            d_sc = _dot(do, v, "nt") * causal
            dqt = _dot(do, s_in)
            dkt = _dot(v, dsn)
            dst[...] = dsn * eend[0:1] + _dot(do, qt, "tn")
            diag = jnp.sum(d_sc * eye, axis=-1, keepdims=True)
            dq = dqt * eb
            dk = dkt * ebe
            db = q * dq - k * dk
            dq = dq + diag * k
            dk = dk + diag * q
            for lvl in range(len(_LEVELS)):
                dm = (m_ref[lvl] * d_sc).astype(BF16)
                ql = (q * eq[lvl]).astype(BF16)
                kl = (k * ek[lvl]).astype(BF16)
                gq = _dot(dm, kl)
                gk = _dot(dm, ql, "tn")
                dq = dq + gq * eq[lvl]
                dk = dk + gk * ek[lvl]
                db = db + ql.astype(F32) * gq - kl.astype(F32) * gk
            extra = jnp.sum(dkt * kt, axis=0, keepdims=True) + eend[0:1] * jnp.sum(s_in * dsn, axis=0, keepdims=True)
            db = db + last_row * extra
            dlogf = _dot_exact_l(rev_ref[...], db)
            dfv = jnp.where(f > 1e-30, dlogf / f, 0.0) - dk
            df_ref[rows, :] = (dfv * (1.0 - lbv) * sg * (1.0 - sg)).astype(BF16)
            dlb_ref[...] += jnp.sum(dfv * (1.0 - sg), axis=0, keepdims=True)
            dq_ref[rows, :] = (dq * (sq * (1.0 + qp * (1.0 - sq)))).astype(BF16)
            return carry

        lax.fori_loop(0, n_sub, chunk, 0)
        lay.run(cin, cout, csem, step, grid[0] * grid[1], post=True)

    col = lambda base: pl.BlockSpec((rb, HEAD), lambda h, r: (n_rb - 1 - r, base * n_heads + h))
    blk = pl.BlockSpec((rb, HEAD), lambda h, r: (n_rb - 1 - r, h))
    vec = pl.BlockSpec((1, HEAD), lambda h, r: (0, h))
    const = lambda a: pl.BlockSpec(a.shape, lambda h, r: (0,) * a.ndim)
    res = pl.pallas_call(
        body, name="hgrn2_bwd", grid=grid,
        in_specs=[col(0), col(1), col(2), col(3), vec, vec, blk,
                  pl.BlockSpec((None, n_sub, HEAD, HEAD), lambda h, r: (h, n_rb - 1 - r, 0, 0)),
                  pl.BlockSpec((None, rb, c), lambda h, r: (h, n_rb - 1 - r, 0)),
                  blk, const(cum), const(rev), const(masks)] + lay.in_specs,
        out_specs=[blk, blk, blk, blk, vec, vec] + lay.out_specs,
        out_shape=[jax.ShapeDtypeStruct((t, aw), BF16)] * 4 + [jax.ShapeDtypeStruct((1, aw), F32)] * 2 + lay.out_shapes,
        scratch_shapes=[pltpu.VMEM((HEAD, HEAD), F32)] + lay.sem_shapes,
        compiler_params=_cparams(2),
    )(proj, proj, proj, proj, lb, nw, o_raw, states, scores, dab, cum, rev, masks, *lay.arrays)
    lay.deliver(res[6:])
    return res[:6]


def _lb_fwd(lb_param):
    def body(p_ref, o_ref):
        p = p_ref[...]
        e = jnp.exp(p - jnp.max(p, axis=0, keepdims=True))
        o_ref[...] = e[0:1] / jnp.sum(e, axis=0, keepdims=True)

    return pl.pallas_call(body, name="lb_fwd", out_shape=jax.ShapeDtypeStruct((1, lb_param.shape[1]), F32))(lb_param)


def _lb_bwd(lb_param, dlb):
    def body(p_ref, d_ref, o_ref):
        p = p_ref[...]
        e = jnp.exp(p - jnp.max(p, axis=0, keepdims=True))
        s = e / jnp.sum(e, axis=0, keepdims=True)
        first = (lax.broadcasted_iota(jnp.int32, p.shape, 0) == 0).astype(F32)
        o_ref[...] = d_ref[...] * s[0:1] * (first - s)

    return pl.pallas_call(body, name="lb_bwd", out_shape=jax.ShapeDtypeStruct(lb_param.shape, F32))(lb_param, dlb)


def _gmlp_norm(v, lnw, lnb):
    vf = _gelu(v)
    mu = jnp.mean(vf, axis=-1, keepdims=True)
    cen = vf - mu
    rstd = lax.rsqrt(jnp.mean(cen * cen, axis=-1, keepdims=True) + EPS)
    xh = cen * rstd
    return xh, rstd, xh * lnw + lnb


def _tril(n):
    return (lax.broadcasted_iota(jnp.int32, (n, n), 1) <= lax.broadcasted_iota(jnp.int32, (n, n), 0)).astype(F32)


def _gmlp_fwd(proj, lnw, lnb, w_sp, bs_t, n_groups, col_base):
    t = proj.shape[0]
    bw = n_groups * HEAD
    c = GMLP_CHUNK

    def body(u_ref, v_ref, lnw_ref, lnb_ref, w_ref, bs_ref, o_ref):
        tri = _tril(c)
        uf = _gelu(u_ref[...])
        _, _, vn = _gmlp_norm(v_ref[...], lnw_ref[...], lnb_ref[...])
        for g in range(n_groups):
            cols = slice(g * HEAD, (g + 1) * HEAD)
            z = _dot(w_ref[g] * tri, vn[:, cols]) + bs_ref[:, g:g + 1]
            o_ref[:, cols] = (uf[:, cols] * z).astype(BF16)

    blk = lambda b: pl.BlockSpec((c, bw), lambda n: (n, b))
    const = lambda a: pl.BlockSpec(a.shape, lambda n: (0,) * a.ndim)
    return pl.pallas_call(
        body, name="gmlp_fwd", grid=(t // c,),
        in_specs=[blk(col_base), blk(col_base + 1), const(lnw), const(lnb), const(w_sp), const(bs_t)],
        out_specs=pl.BlockSpec((c, bw), lambda n: (n, 0)),
        out_shape=jax.ShapeDtypeStruct((t, bw), BF16),
        compiler_params=_cparams(1),
    )(proj, proj, lnw, lnb, w_sp, bs_t)


def _gmlp_bwd(proj, lnw, lnb, w_sp, bs_t, dab, n_groups, col_base):
    t = proj.shape[0]
    bw = n_groups * HEAD
    c = GMLP_CHUNK
    n_steps = t // c
    sel = jnp.asarray((np.arange(bw)[:, None] // HEAD == np.arange(n_groups)[None, :]).astype(np.float32), BF16)

    def body(u_ref, v_ref, lnw_ref, lnb_ref, w_ref, bs_ref, d_ref, sel_ref,
             du_ref, dv_ref, dlnw_ref, dlnb_ref, dw_ref, dbs_ref, dz_acc, dvn_scr):
        step = pl.program_id(0)

        @pl.when(step == 0)
        def _():
            dlnw_ref[...] = jnp.zeros_like(dlnw_ref)
            dlnb_ref[...] = jnp.zeros_like(dlnb_ref)
            dw_ref[...] = jnp.zeros_like(dw_ref)
            dz_acc[...] = jnp.zeros_like(dz_acc)

        tri = _tril(c)
        u = u_ref[...]
        v = v_ref[...]
        uf = _gelu(u)
        lnw_v = lnw_ref[...]
        xh, rstd, vn = _gmlp_norm(v, lnw_v, lnb_ref[...])
        dbo = d_ref[...]
        dz = dbo * uf
        dz_acc[...] += dz
        for g in range(n_groups):
            cols = slice(g * HEAD, (g + 1) * HEAD)
            wg = w_ref[g] * tri
            z = _dot(wg, vn[:, cols]) + bs_ref[:, g:g + 1]
            du_ref[:, cols] = (dbo[:, cols] * z * _gelu_grad(u[:, cols])).astype(BF16)
            dvn_scr[:, cols] = _dot(wg, dz[:, cols], "tn")
            dw_ref[g] += tri * _dot(dz[:, cols], vn[:, cols], "nt")
        dvn = dvn_scr[...]
        dlnw_ref[...] += jnp.sum(dvn * xh, axis=0, keepdims=True)
        dlnb_ref[...] += jnp.sum(dvn, axis=0, keepdims=True)
        dxh = dvn * lnw_v
        dvf = rstd * (dxh - jnp.mean(dxh, axis=-1, keepdims=True) - xh * jnp.mean(dxh * xh, axis=-1, keepdims=True))
        dv_ref[...] = (dvf * _gelu_grad(v)).astype(BF16)

        @pl.when(step == n_steps - 1)
        def _():
            dbs_ref[...] = _dot_exact_r(dz_acc[...], sel_ref[...])

    blk = lambda b: pl.BlockSpec((c, bw), lambda n: (n, b))
    const = lambda a: pl.BlockSpec(a.shape, lambda n: (0,) * a.ndim)
    row = pl.BlockSpec((c, bw), lambda n: (n, 0))
    vec = pl.BlockSpec((1, bw), lambda n: (0, 0))
    return pl.pallas_call(
        body, name="gmlp_bwd", grid=(n_steps,),
        in_specs=[blk(col_base), blk(col_base + 1), const(lnw), const(lnb), const(w_sp), const(bs_t), blk(1), const(sel)],
        out_specs=[row, row, vec, vec, const(w_sp), const(bs_t)],
        out_shape=[jax.ShapeDtypeStruct((t, bw), BF16), jax.ShapeDtypeStruct((t, bw), BF16),
                   jax.ShapeDtypeStruct((1, bw), F32), jax.ShapeDtypeStruct((1, bw), F32),
                   jax.ShapeDtypeStruct(w_sp.shape, F32), jax.ShapeDtypeStruct(bs_t.shape, F32)],
        scratch_shapes=[pltpu.VMEM((c, bw), F32), pltpu.VMEM((c, bw), F32)],
        compiler_params=_cparams(1),
    )(proj, proj, lnw, lnb, w_sp, bs_t, dab, sel)


def _pair_sum(name, grad, other, core):
    _, _, r, c = grad.shape
    tr = _rows(r, 6 * c)

    def body(core_ref, g_ref, o_ref, out_ref):
        out_ref[...] = (g_ref[...].astype(F32) + o_ref[...].astype(F32)).astype(BF16)

    return pl.pallas_call(
        body, name=name,
        grid_spec=pltpu.PrefetchScalarGridSpec(
            num_scalar_prefetch=1, grid=(N_CHIP, r // tr),
            in_specs=[pl.BlockSpec((None, None, tr, c), lambda k, i, core_ref: (k, core_ref[0], i, 0)),
                      pl.BlockSpec((None, tr, c), lambda k, i, core_ref: (k, i, 0))],
            out_specs=pl.BlockSpec((None, tr, c), lambda k, i, core_ref: (k, i, 0))),
        out_shape=jax.ShapeDtypeStruct((N_CHIP, r, c), BF16),
        compiler_params=_cparams(2),
    )(core, grad, other)


def _adamw_math(w, g, m, v):
    m = ADAM_B1 * m + (1.0 - ADAM_B1) * g
    v = ADAM_B2 * v + (1.0 - ADAM_B2) * (g * g)
    m_hat = m / (1.0 - ADAM_B1 ** ADAM_STEP)
    v_hat = v / (1.0 - ADAM_B2 ** ADAM_STEP)
    delta = -ADAM_LR * (m_hat / (jnp.sqrt(v_hat) + ADAM_EPS) + ADAM_WD * w)
    return delta, m, v


def _adamw(name, parts, w, m, v):
    n_parts, r, c = parts.shape
    tr = _rows(r, c * (n_parts * parts.dtype.itemsize + 28), mult=8)

    def body(p_ref, w_ref, m_ref, v_ref, g_ref, d_ref, mo_ref, vo_ref):
        g = p_ref[0].astype(F32)
        for i in range(1, n_parts):
            g = g + p_ref[i].astype(F32)
        g_ref[...] = g
        d_ref[...], mo_ref[...], vo_ref[...] = _adamw_math(w_ref[...], g, m_ref[...], v_ref[...])

    row = pl.BlockSpec((tr, c), lambda i: (i, 0))
    return pl.pallas_call(
        body, name=name, grid=(r // tr,),
        in_specs=[pl.BlockSpec((n_parts, tr, c), lambda i: (0, i, 0)), row, row, row],
        out_specs=[row] * 4,
        out_shape=[jax.ShapeDtypeStruct((r, c), F32)] * 4,
        compiler_params=_cparams(1),
    )(parts, w, m, v)


def kernel(x, p, pre_mix_w, w_in, lb_param, a_norm_w, gmlp_ln_w, gmlp_ln_b, w_spatial, b_spatial, w_out, post_mix_w, pre_ffn_w, w_gate, w_up, w_down, post_ffn_w, w_ple, w_ple_gate, post_ple_w, loss_target, m_pre_mix_w, m_w_in, m_lb_param, m_a_norm_w, m_gmlp_ln_w, m_gmlp_ln_b, m_w_spatial, m_b_spatial, m_w_out, m_post_mix_w, m_pre_ffn_w, m_w_gate, m_w_up, m_w_down, m_post_ffn_w, m_w_ple, m_w_ple_gate, m_post_ple_w, v_pre_mix_w, v_w_in, v_lb_param, v_a_norm_w, v_gmlp_ln_w, v_gmlp_ln_b, v_w_spatial, v_b_spatial, v_w_out, v_post_mix_w, v_pre_ffn_w, v_w_gate, v_w_up, v_w_down, v_post_ffn_w, v_w_ple, v_w_ple_gate, v_post_ple_w):
    big_names = ["w_in", "w_out", "w_gate", "w_up", "w_down", "w_ple", "w_ple_gate"]
    small_names = ["pre_mix_w", "lb_param", "a_norm_w", "gmlp_ln_w", "gmlp_ln_b", "w_spatial", "b_spatial",
                   "post_mix_w", "pre_ffn_w", "post_ffn_w", "post_ple_w"]
    all_names = ["pre_mix_w", "w_in", "lb_param", "a_norm_w", "gmlp_ln_w", "gmlp_ln_b", "w_spatial", "b_spatial",
                 "w_out", "post_mix_w", "pre_ffn_w", "w_gate", "w_up", "w_down", "post_ffn_w", "w_ple", "w_ple_gate",
                 "post_ple_w"]
    env = dict(locals())
    W = {n: env[n] for n in all_names}
    M = {n: env["m_" + n] for n in all_names}
    V = {n: env["v_" + n] for n in all_names}

    xs = x[0]
    ps = p[0, 0]
    tgt = loss_target[0]
    t, d = xs.shape
    aw = a_norm_w.shape[1]
    bw = gmlp_ln_w.shape[1]
    n_heads, n_groups = aw // HEAD, bw // HEAD
    core = lax.axis_index("c").astype(jnp.int32).reshape(1)

    shard = {n: W[n][0] for n in big_names}
    bf = {n: _cast_bf16("cast_" + n, shard[n]) for n in big_names}
    ag_in = _ag_comm([bf["w_in"]])
    _comm_only("ag_w_in", [ag_in])
    win_g = ag_in.results[0]
    n_in = win_g.shape[2]
    ffl = bf["w_gate"].shape[1]
    n_ple = bf["w_ple"].shape[1]
    ple = ps.shape[1]

    TM, TK = 1024, 512
    tm = _tile(t, TM)
    tkd = _tile(d, TK)

    h1 = _rms_fwd("rms_pre_mix", xs, pre_mix_w)
    ag_a = _ag_comm([bf["w_out"], bf["w_gate"]], mid_frac=0.9)
    proj = _matmul(
        "mm_proj",
        [(h1, pl.BlockSpec((tm, tkd), lambda m, n, k: (m, k))), (win_g, pl.BlockSpec((None, tkd, n_in), lambda m, n, k: (n, k, 0)))],
        [(0, 1, "nn", 0)],
        [(jax.ShapeDtypeStruct((t, N_DEV * n_in), F32), pl.BlockSpec((tm, n_in), lambda m, n, k: (m, n)))],
        (t // tm, N_DEV, d // tkd), (tm, n_in), comms=[ag_a])[0]
    wout_f = ag_a.results[0].reshape(d, d)
    wgate_g = ag_a.results[1]
    lb = _lb_fwd(lb_param)
    ag_b = _ag_comm([bf["w_up"]], mid_frac=0.7)
    a_out, o_raw, states, scores = _hgrn_fwd(proj, lb, a_norm_w, n_heads, comms=[ag_b])
    wup_g = ag_b.results[0]
    bs_t = b_spatial[0].T
    w_sp = w_spatial[0]
    col_u = (4 * aw) // bw
    b_out = _gmlp_fwd(proj, gmlp_ln_w, gmlp_ln_b, w_sp, bs_t, n_groups, col_u)
    ab = jnp.concatenate([a_out, b_out], axis=1)
    mix = _mm_plain("mm_mix", ab, wout_f, "nn", F32, TM, 2048, TK)
    x1, h2 = _resid_rms("resid_mix", xs, mix, post_mix_w, pre_ffn_w)

    def swiglu(accs):
        gate, up = accs
        return gate, up, gate * _sigmoid(gate) * up

    tmf = _tile(t, 512)
    blk3 = lambda: pl.BlockSpec((None, tmf, ffl), lambda j, m, k: (j, m, 0))
    ag_c = _ag_comm([bf["w_down"], bf["w_ple_gate"], bf["w_ple"]], mid_frac=0.7)
    gate, up, act = _matmul(
        "mm_ffn_up",
        [(h2, pl.BlockSpec((tmf, tkd), lambda j, m, k: (m, k))),
         (wgate_g, pl.BlockSpec((None, tkd, ffl), lambda j, m, k: (j, k, 0))),
         (wup_g, pl.BlockSpec((None, tkd, ffl), lambda j, m, k: (j, k, 0)))],
        [(0, 1, "nn", 0), (0, 2, "nn", 1)],
        [(jax.ShapeDtypeStruct((N_DEV, t, ffl), BF16), blk3()) for _ in range(3)],
        (N_DEV, t // tmf, d // tkd), (tmf, ffl), n_slots=2, epilogue=swiglu, comms=[ag_c])
    wdown_g = ag_c.results[0]
    wpg_f = ag_c.results[1].reshape(d, d)
    wple_g = ag_c.results[2]
    tn_d = _tile(d, 2048)
    ff = _matmul(
        "mm_ffn_down",
        [(act, pl.BlockSpec((None, tm, ffl), lambda m, n, k: (k, m, 0))),
         (wdown_g, pl.BlockSpec((None, ffl, tn_d), lambda m, n, k: (k, 0, n)))],
        [(0, 1, "nn", 0)],
        [(jax.ShapeDtypeStruct((t, d), F32), pl.BlockSpec((tm, tn_d), lambda m, n, k: (m, n)))],
        (t // tm, d // tn_d, N_DEV), (tm, tn_d))[0]
    x2, x2b = _resid_rms("resid_ffn", x1, ff, post_ffn_w, None)

    pgl = _mm_plain("mm_ple_gate", x2b, wpg_f, "nn", F32, TM, 2048, TK)
    pe = _matmul(
        "mm_ple",
        [(ps, pl.BlockSpec((tm, ple), lambda m, n, k: (m, 0))), (wple_g, pl.BlockSpec((None, ple, n_ple), lambda m, n, k: (n, 0, 0)))],
        [(0, 1, "nn", 0)],
        [(jax.ShapeDtypeStruct((t, N_DEV * n_ple), F32), pl.BlockSpec((tm, n_ple), lambda m, n, k: (m, n)))],
        (t // tm, N_DEV, 1), (tm, n_ple))[0]
    loss_part, d3, dpe, dpgl, g_post_ple = _ple_loss("ple_loss", x2, pe, pgl, post_ple_w, tgt)

    tkt = _tile(t, TK)
    g_wple = _matmul(
        "mm_dw_ple",
        [(ps, pl.BlockSpec((tkt, ple), lambda n, k: (k, 0))), (dpe, pl.BlockSpec((tkt, n_ple), lambda n, k: (k, n)))],
        [(0, 1, "tn", 0)],
        [(jax.ShapeDtypeStruct((N_DEV, ple, n_ple), BF16), pl.BlockSpec((None, ple, n_ple), lambda n, k: (n, 0, 0)))],
        (N_DEV, t // tkt), (ple, n_ple))[0]
    g_wpg = _mm_plain("mm_dw_ple_gate", x2b, dpgl, "tn", BF16, TM, 2048, TK)

    def by_chip(g):
        return g.reshape((N_CHIP, 2) + g.shape[-2:])

    def pair_sums(names, comm):
        return [_pair_sum("pair_sum_" + n, g, o, core) for n, g, o in zip(names, comm.arrays, comm.results)]

    r1_p = _pair_comm([by_chip(g_wpg.reshape(N_DEV, d // N_DEV, d)), by_chip(g_wple)])
    d2 = _mm_plain("mm_d_x2", dpgl, wpg_f, "nt", F32, TM, 1024, TK, extra=d3, epilogue=lambda accs, e: [accs[0] + e],
                   comms=[r1_p])
    r2_p = _chip_comm(pair_sums(["w_ple_gate", "w_ple"], r1_p))

    dff, g_post_ffn = _norm_bwd("norm_bwd_ffn", d2, ff, post_ffn_w)
    g_wdown = _matmul(
        "mm_dw_down",
        [(act, pl.BlockSpec((None, tkt, ffl), lambda j, n, k: (j, k, 0))), (dff, pl.BlockSpec((tkt, tn_d), lambda j, n, k: (k, n)))],
        [(0, 1, "tn", 0)],
        [(jax.ShapeDtypeStruct((N_DEV, ffl, d), BF16), pl.BlockSpec((None, ffl, tn_d), lambda j, n, k: (j, 0, n)))],
        (N_DEV, d // tn_d, t // tkt), (ffl, tn_d), comms=[r2_p])[0]
    r1_d = _pair_comm([by_chip(g_wdown)])

    def swiglu_bwd(accs, gate_v, up_v):
        dact = accs[0]
        gf = gate_v.astype(F32)
        sg = _sigmoid(gf)
        return dact * up_v.astype(F32) * (sg * (1.0 + gf * (1.0 - sg))), dact * (gf * sg)

    dgate, dup = _matmul(
        "mm_d_act",
        [(dff, pl.BlockSpec((tmf, tkd), lambda j, m, k: (m, k))),
         (wdown_g, pl.BlockSpec((None, ffl, tkd), lambda j, m, k: (j, 0, k))),
         (gate, blk3()), (up, blk3())],
        [(0, 1, "nt", 0)],
        [(jax.ShapeDtypeStruct((N_DEV, t, ffl), BF16), blk3()) for _ in range(2)],
        (N_DEV, t // tmf, d // tkd), (tmf, ffl), epilogue=swiglu_bwd, comms=[r1_d])
    r2_d = _chip_comm(pair_sums(["w_down"], r1_d))
    tmd = _tile(d, TM)
    g_wgate, g_wup = _matmul(
        "mm_dw_gate_up",
        [(h2, pl.BlockSpec((tkt, tmd), lambda j, m, k: (k, m))),
         (dgate, pl.BlockSpec((None, tkt, ffl), lambda j, m, k: (j, k, 0))),
         (dup, pl.BlockSpec((None, tkt, ffl), lambda j, m, k: (j, k, 0)))],
        [(0, 1, "tn", 0), (0, 2, "tn", 1)],
        [(jax.ShapeDtypeStruct((N_DEV, d, ffl), BF16), pl.BlockSpec((None, tmd, ffl), lambda j, m, k: (j, m, 0))) for _ in range(2)],
        (N_DEV, d // tmd, t // tkt), (tmd, ffl), n_slots=2, comms=[r2_d])
    r1_gu = _pair_comm([by_chip(g_wgate), by_chip(g_wup)])
    tn1 = _tile(d, 1024)
    dh2 = _matmul(
        "mm_d_h2",
        [(dgate, pl.BlockSpec((None, tm, ffl), lambda m, n, k: (k, m, 0))),
         (wgate_g, pl.BlockSpec((None, tn1, ffl), lambda m, n, k: (k, n, 0))),
         (dup, pl.BlockSpec((None, tm, ffl), lambda m, n, k: (k, m, 0))),
         (wup_g, pl.BlockSpec((None, tn1, ffl), lambda m, n, k: (k, n, 0)))],
        [(0, 1, "nt", 0), (2, 3, "nt", 0)],
        [(jax.ShapeDtypeStruct((t, d), F32), pl.BlockSpec((tm, tn1), lambda m, n, k: (m, n)))],
        (t // tm, d // tn1, N_DEV), (tm, tn1), comms=[r1_gu])[0]
    s_gate, s_up = pair_sums(["w_gate", "w_up"], r1_gu)
    d1, g_pre_ffn, dmix, g_post_mix = _prenorm_bwd("prenorm_bwd_ffn", d2, dh2, x1, pre_ffn_w, mix, post_mix_w)

    g_wout = _mm_plain("mm_dw_out", ab, dmix, "tn", BF16, TM, 2048, TK)
    r1_o = _pair_comm([by_chip(g_wout.reshape(N_DEV, d // N_DEV, d))])
    dab = _mm_plain("mm_d_ab", dmix, wout_f, "nt", F32, TM, 2048, TK, comms=[r1_o])
    r2_guo = _chip_comm([s_gate, s_up] + pair_sums(["w_out"], r1_o))
    dq, df, di, dg, dlb, g_a_norm = _hgrn_bwd(proj, lb, a_norm_w, o_raw, states, scores, dab, n_heads, comms=[r2_guo])
    du, dv, g_ln_w, g_ln_b, g_wsp, g_bs_t = _gmlp_bwd(proj, gmlp_ln_w, gmlp_ln_b, w_sp, bs_t, dab, n_groups, col_u)
    dproj = jnp.concatenate([dq, df, di, dg, du, dv], axis=1)
    g_win = _matmul(
        "mm_dw_in",
        [(h1, pl.BlockSpec((tkt, tmd), lambda j, m, k: (k, m))), (dproj, pl.BlockSpec((tkt, n_in), lambda j, m, k: (k, j)))],
        [(0, 1, "tn", 0)],
        [(jax.ShapeDtypeStruct((N_DEV, d, n_in), BF16), pl.BlockSpec((None, tmd, n_in), lambda j, m, k: (j, m, 0)))],
        (N_DEV, d // tmd, t // tkt), (tmd, n_in))[0]
    r1_in = _pair_comm([by_chip(g_win)])
    dh1 = _matmul(
        "mm_d_h1",
        [(dproj, pl.BlockSpec((tm, n_in), lambda m, n, k: (m, k))), (win_g, pl.BlockSpec((None, tn1, n_in), lambda m, n, k: (k, n, 0)))],
        [(0, 1, "nt", 0)],
        [(jax.ShapeDtypeStruct((t, d), F32), pl.BlockSpec((tm, tn1), lambda m, n, k: (m, n)))],
        (t // tm, d // tn1, N_DEV), (tm, tn1), comms=[r1_in])[0]
    r2_in = _chip_comm(pair_sums(["w_in"], r1_in))
    _comm_only("rs_w_in", [r2_in])
    grad_x, g_pre_mix = _prenorm_bwd("prenorm_bwd_mix", d1, dh1, xs, pre_mix_w)

    reduced = {
        "w_in": r2_in.results[0], "w_out": r2_guo.results[2], "w_gate": r2_guo.results[0], "w_up": r2_guo.results[1],
        "w_down": r2_d.results[0], "w_ple": r2_p.results[1], "w_ple_gate": r2_p.results[0],
    }
    grads, deltas, new_m, new_v = {}, {}, {}, {}
    for n in big_names:
        g, dl, mo, vo = _adamw("adamw_" + n, reduced[n], shard[n], M[n][0], V[n][0])
        grads[n], deltas[n], new_m[n], new_v[n] = g[None], dl[None], mo[None], vo[None]

    small_grad = {
        "pre_mix_w": g_pre_mix, "lb_param": _lb_bwd(lb_param, dlb), "a_norm_w": g_a_norm, "gmlp_ln_w": g_ln_w,
        "gmlp_ln_b": g_ln_b, "w_spatial": g_wsp, "b_spatial": g_bs_t.T, "post_mix_w": g_post_mix,
        "pre_ffn_w": g_pre_ffn, "post_ffn_w": g_post_ffn, "post_ple_w": g_post_ple,
    }
    pack = lambda get: jnp.concatenate([get(n).reshape(-1, LANE) for n in small_names], axis=0)
    ag_small = _ag_comm([pack(lambda n: small_grad[n])])
    _comm_only("ag_small", [ag_small], in_vmem=True)
    g_all = ag_small.results[0]
    sg, sd, sm, sv = _adamw("adamw_small", g_all, pack(lambda n: W[n]), pack(lambda n: M[n]), pack(lambda n: V[n]))
    off = 0
    for n in small_names:
        rows = W[n].size // LANE
        for src, dst in ((sg, grads), (sd, deltas), (sm, new_m), (sv, new_v)):
            dst[n] = src[off:off + rows].reshape(W[n].shape)
        off += rows

    loss = lax.psum(loss_part[0, 0], ("x", "y", "c"))
    return (loss, grad_x[None], *[grads[n] for n in all_names], *[deltas[n] for n in all_names],
            *[new_m[n] for n in all_names], *[new_v[n] for n in all_names])
```

```python
import functools

import numpy as np
import jax
import jax.numpy as jnp
from jax import lax
from jax.experimental import pallas as pl
from jax.experimental.pallas import tpu as pltpu

F32 = jnp.float32
BF16 = jnp.bfloat16

EPS = 1e-6
HEAD = 128
GLA_CHUNK = 64
GMLP_CHUNK = 128
N_DEV = 8
N_CHIP = 4
LANE = 128
VMEM_LIMIT = 56 * 1024 * 1024
HGRN_ROWS = 512
ROW_TILE = 128
EPILOGUE_ROWS = 256
HGRN_UNROLL = 2

ADAM_LR = 0.001
ADAM_B1 = 0.9
ADAM_B2 = 0.999
ADAM_EPS = 1e-08
ADAM_WD = 0.01
ADAM_STEP = 10

MESH = pl.DeviceIdType.MESH
ANY = pl.BlockSpec(memory_space=pl.ANY)

_DIMS = {
    "nn": (((1,), (0,)), ((), ())),
    "nt": (((1,), (1,)), ((), ())),
    "tn": (((0,), (0,)), ((), ())),
}


def _tile(dim, pref):
    return pref if dim % pref == 0 else dim


def _rows(r, bytes_per_row, budget=4 * 1024 * 1024, mult=16):
    best = None
    for cand in range(mult, r + 1, mult):
        if r % cand == 0 and cand * bytes_per_row <= budget:
            best = cand
    return best if best is not None else r


def _cparams(n_axes):
    return pltpu.CompilerParams(dimension_semantics=("arbitrary",) * n_axes, vmem_limit_bytes=VMEM_LIMIT)


def _dot(a, b, form="nn"):
    return lax.dot_general(a.astype(BF16), b.astype(BF16), _DIMS[form], preferred_element_type=F32)


def _split3(x):
    hi = x.astype(BF16)
    r = x - hi.astype(F32)
    mid = r.astype(BF16)
    lo = (r - mid.astype(F32)).astype(BF16)
    return hi, mid, lo


def _dot_exact_l(c, x):
    hi, mid, lo = _split3(x)
    d = lambda y: lax.dot_general(c, y, _DIMS["nn"], preferred_element_type=F32)
    return d(hi) + d(mid) + d(lo)


def _dot_exact_r(x, c):
    hi, mid, lo = _split3(x)
    d = lambda y: lax.dot_general(y, c, _DIMS["nn"], preferred_element_type=F32)
    return d(hi) + d(mid) + d(lo)


def _sigmoid(x):
    return 1.0 / (1.0 + jnp.exp(-x))


def _gelu(x):
    return 0.5 * x * (1.0 + lax.erf(x * 0.7071067811865476))


def _gelu_grad(x):
    cdf = 0.5 * (1.0 + lax.erf(x * 0.7071067811865476))
    pdf = jnp.exp(-0.5 * x * x) * 0.3989422804014327
    return cdf + x * pdf


def _position():
    return lax.axis_index("x"), lax.axis_index("y"), lax.axis_index("c")


def _linear_step(grid):
    step = 0
    for ax, n in enumerate(grid):
        step = step * n + pl.program_id(ax)
    return step


class _Comm:
    def __init__(self, arrays, out_shapes, sem_shapes, phases):
        self.arrays, self.out_shapes, self.sem_shapes, self.phases = list(arrays), list(out_shapes), list(sem_shapes), phases
        self.results = None


class _CommLayout:
    def __init__(self, comms, space=pl.ANY):
        self.comms = list(comms)
        self.arrays = [a for c in self.comms for a in c.arrays]
        self.out_shapes = [s for c in self.comms for s in c.out_shapes]
        self.sem_shapes = [s for c in self.comms for s in c.sem_shapes]
        self.n_in, self.n_out = len(self.arrays), len(self.out_shapes)
        self.in_specs = [pl.BlockSpec(memory_space=space)] * self.n_in
        self.out_specs = [pl.BlockSpec(memory_space=space)] * self.n_out

    def run(self, cin, cout, csem, step, n_steps, post):
        i = o = s = 0
        for c in self.comms:
            ins, outs, sems = cin[i:i + len(c.arrays)], cout[o:o + len(c.out_shapes)], csem[s:s + len(c.sem_shapes)]
            i, o, s = i + len(c.arrays), o + len(c.out_shapes), s + len(c.sem_shapes)
            for frac, fn in c.phases:
                if (frac is None) != post:
                    continue
                due = n_steps - 1 if frac is None else max(0, min(int(frac * n_steps), n_steps - 2))
                if n_steps == 1:
                    fn(ins, outs, sems)
                else:
                    pl.when(step == due)(functools.partial(fn, ins, outs, sems))

    def deliver(self, results):
        o = 0
        for c in self.comms:
            c.results = list(results[o:o + len(c.out_shapes)])
            o += len(c.out_shapes)


def _comm_only(name, comms, in_vmem=False):
    lay = _CommLayout(comms, pltpu.VMEM if in_vmem else pl.ANY)

    def body(*refs):
        cin, cout, csem = refs[:lay.n_in], refs[lay.n_in:lay.n_in + lay.n_out], refs[lay.n_in + lay.n_out:]
        lay.run(cin, cout, csem, 0, 1, post=False)
        lay.run(cin, cout, csem, 0, 1, post=True)

    res = pl.pallas_call(
        body, name=name, in_specs=lay.in_specs, out_specs=lay.out_specs, out_shape=lay.out_shapes,
        scratch_shapes=lay.sem_shapes,
    )(*lay.arrays)
    lay.deliver(res)


def _ag_comm(shards, mid_frac=0.0):
    n = len(shards)
    per = N_DEV - 1

    def tools(ins, outs, sems):
        send_sems, recv_sems, local_sems = sems
        x, y, c = _position()
        me, sibling = (x, y, c), (x, y, 1 - c)
        chips = [(1 - x, y), (x, 1 - y), (1 - x, 1 - y)]

        def copy(a, k, block, to, from_shard=False):
            dst = outs[a].at[4 * block[0] + 2 * block[1] + block[2]]
            return pltpu.make_async_remote_copy(
                src_ref=ins[a] if from_shard else dst, dst_ref=dst,
                send_sem=send_sems.at[a * per + k], recv_sem=recv_sems.at[a * per + k],
                device_id=to, device_id_type=MESH)

        def local(a):
            return pltpu.make_async_copy(ins[a], outs[a].at[4 * x + 2 * y + c], local_sems.at[a])

        return me, sibling, chips, c, copy, local

    def first(ins, outs, sems):
        me, sibling, chips, c, copy, local = tools(ins, outs, sems)
        for a in range(n):
            local(a).start()
            copy(a, 0, me, sibling, True).start()
            for j, chip in enumerate(chips):
                copy(a, 1 + j, me, (*chip, c), True).start()

    def middle(ins, outs, sems):
        me, sibling, chips, c, copy, local = tools(ins, outs, sems)
        for a in range(n):
            for j, chip in enumerate(chips):
                copy(a, 1 + j, (*chip, c), me).wait_recv()
                copy(a, 4 + j, (*chip, c), sibling).start()

    def last(ins, outs, sems):
        me, sibling, chips, c, copy, local = tools(ins, outs, sems)
        for a in range(n):
            copy(a, 0, sibling, me).wait_recv()
            copy(a, 0, me, sibling, True).wait_send()
            for j, chip in enumerate(chips):
                copy(a, 4 + j, (*chip, 1 - c), me).wait_recv()
                copy(a, 1 + j, me, (*chip, c), True).wait_send()
                copy(a, 4 + j, (*chip, c), sibling).wait_send()
            local(a).wait()

    return _Comm(
        shards, [jax.ShapeDtypeStruct((N_DEV,) + s.shape, s.dtype) for s in shards],
        [pltpu.SemaphoreType.DMA((n * per,)), pltpu.SemaphoreType.DMA((n * per,)), pltpu.SemaphoreType.DMA((n,))],
        [(0.0, first), (mid_frac, middle), (None, last)])


def _pair_comm(grads):
    n = len(grads)

    def copies(ins, outs, sems):
        send_sems, recv_sems = sems
        x, y, c = _position()
        return [pltpu.make_async_remote_copy(
            src_ref=ins[a].at[k, 1 - c], dst_ref=outs[a].at[k],
            send_sem=send_sems.at[a * N_CHIP + k], recv_sem=recv_sems.at[a * N_CHIP + k],
            device_id=(x, y, 1 - c), device_id_type=MESH) for a in range(n) for k in range(N_CHIP)]

    def first(ins, outs, sems):
        for cp in copies(ins, outs, sems):
            cp.start()

    def last(ins, outs, sems):
        for cp in copies(ins, outs, sems):
            cp.wait()

    return _Comm(
        grads, [jax.ShapeDtypeStruct((N_CHIP,) + g.shape[2:], g.dtype) for g in grads],
        [pltpu.SemaphoreType.DMA((n * N_CHIP,)), pltpu.SemaphoreType.DMA((n * N_CHIP,))],
        [(0.0, first), (None, last)])


def _chip_comm(sums):
    n = len(sums)
    per = N_CHIP - 1

    def copies(ins, outs, sems):
        send_sems, recv_sems, local_sems = sems
        x, y, c = _position()
        my_chip = 2 * x + y
        cps = []
        for a in range(n):
            cps.append(pltpu.make_async_copy(ins[a].at[my_chip], outs[a].at[my_chip], local_sems.at[a]))
            for j, (px, py) in enumerate([(1 - x, y), (x, 1 - y), (1 - x, 1 - y)]):
                cps.append(pltpu.make_async_remote_copy(
                    src_ref=ins[a].at[2 * px + py], dst_ref=outs[a].at[my_chip],
                    send_sem=send_sems.at[a * per + j], recv_sem=recv_sems.at[a * per + j],
                    device_id=(px, py, c), device_id_type=MESH))
        return cps

    def first(ins, outs, sems):
        for cp in copies(ins, outs, sems):
            cp.start()

    def last(ins, outs, sems):
        for cp in copies(ins, outs, sems):
            cp.wait()

    return _Comm(
        sums, [jax.ShapeDtypeStruct(s.shape, s.dtype) for s in sums],
        [pltpu.SemaphoreType.DMA((n * per,)), pltpu.SemaphoreType.DMA((n * per,)), pltpu.SemaphoreType.DMA((n,))],
        [(0.0, first), (None, last)])


def _matmul(name, operands, pairs, outs, grid, acc_shape, n_slots=1, epilogue=None, comms=()):
    used = sorted({i for p in pairs for i in p[:2]})
    n_op = len(operands)
    n_out = len(outs)
    k_axis = len(grid) - 1
    n_k = grid[-1]
    lay = _CommLayout(comms)

    def body(*refs):
        ops = refs[:n_op]
        out_refs = refs[n_op + lay.n_in:n_op + lay.n_in + n_out]
        acc = refs[n_op + lay.n_in + n_out + lay.n_out]
        k = pl.program_id(k_axis)
        step = _linear_step(grid)
        cin = refs[n_op:n_op + lay.n_in]
        cout = refs[n_op + lay.n_in + n_out:n_op + lay.n_in + n_out + lay.n_out]
        csem = refs[n_op + lay.n_in + n_out + lay.n_out + 1:]
        lay.run(cin, cout, csem, step, int(np.prod(grid)), post=False)

        @pl.when(k == 0)
        def _():
            acc[...] = jnp.zeros_like(acc)

        vals = {i: ops[i][...] for i in used}
        vals = {i: (v if v.dtype == BF16 else v.astype(BF16)) for i, v in vals.items()}
        for s in range(n_slots):
            tot = None
            for ia, ib, form, slot in pairs:
                if slot != s:
                    continue
                d = lax.dot_general(vals[ia], vals[ib], _DIMS[form], preferred_element_type=F32)
                tot = d if tot is None else tot + d
            acc[s] += tot

        @pl.when(k == n_k - 1)
        def _():
            rows = acc_shape[0]
            chunk = EPILOGUE_ROWS if (epilogue is not None and rows % EPILOGUE_ROWS == 0) else rows
            for r0 in range(0, rows, chunk):
                sl = slice(r0, r0 + chunk)
                accs = [acc[s, sl, :] for s in range(n_slots)]
                extra = [ops[i][sl, :] for i in range(n_op) if i not in used]
                res = epilogue(accs, *extra) if epilogue is not None else accs
                for o, v in zip(out_refs, res):
                    o[sl, :] = v.astype(o.dtype)

        lay.run(cin, cout, csem, step, int(np.prod(grid)), post=True)

    res = pl.pallas_call(
        body,
        name=name,
        grid=grid,
        in_specs=[s for _, s in operands] + lay.in_specs,
        out_specs=[s for _, s in outs] + lay.out_specs,
        out_shape=[s for s, _ in outs] + lay.out_shapes,
        scratch_shapes=[pltpu.VMEM((n_slots,) + tuple(acc_shape), F32)] + lay.sem_shapes,
        compiler_params=_cparams(len(grid)),
    )(*[a for a, _ in operands], *lay.arrays)
    lay.deliver(res[n_out:])
    return res[:n_out]


def _mm_plain(name, a, b, form, out_dtype, tm, tn, tk, extra=None, epilogue=None, comms=()):
    if form == "nn":
        (M, K), N = a.shape, b.shape[1]
    elif form == "nt":
        (M, K), N = a.shape, b.shape[0]
    else:
        (K, M), N = a.shape, b.shape[1]
    tm, tn, tk = _tile(M, tm), _tile(N, tn), _tile(K, tk)
    a_spec = pl.BlockSpec((tk, tm), lambda m, n, k: (k, m)) if form == "tn" else pl.BlockSpec((tm, tk), lambda m, n, k: (m, k))
    b_spec = pl.BlockSpec((tn, tk), lambda m, n, k: (n, k)) if form == "nt" else pl.BlockSpec((tk, tn), lambda m, n, k: (k, n))
    operands = [(a, a_spec), (b, b_spec)]
    if extra is not None:
        operands.append((extra, pl.BlockSpec((tm, tn), lambda m, n, k: (m, n))))
    out = (jax.ShapeDtypeStruct((M, N), out_dtype), pl.BlockSpec((tm, tn), lambda m, n, k: (m, n)))
    return _matmul(name, operands, [(0, 1, form, 0)], [out], (M // tm, N // tn, K // tk), (tm, tn), epilogue=epilogue,
                   comms=comms)[0]


def _cast_bf16(name, w):
    r, c = w.shape
    tr = _rows(r, 6 * c)

    def body(w_ref, o_ref):
        o_ref[...] = w_ref[...].astype(BF16)

    return pl.pallas_call(
        body, name=name, grid=(r // tr,),
        in_specs=[pl.BlockSpec((tr, c), lambda i: (i, 0))],
        out_specs=pl.BlockSpec((tr, c), lambda i: (i, 0)),
        out_shape=jax.ShapeDtypeStruct((r, c), BF16),
        compiler_params=_cparams(1),
    )(w)


def _rms_stats(x):
    r = lax.rsqrt(jnp.mean(x * x, axis=-1, keepdims=True) + EPS)
    return x * r, r


def _rms_bwd(xhat, r, w, dy):
    dxh = dy * w
    return r * (dxh - xhat * jnp.mean(dxh * xhat, axis=-1, keepdims=True))


def _row_spec(tr, d):
    return pl.BlockSpec((tr, d), lambda i: (i, 0))


def _vec_spec(d):
    return pl.BlockSpec((1, d), lambda i: (0, 0))


def _rms_fwd(name, x, w):
    t, d = x.shape
    tr = _tile(t, ROW_TILE)

    def body(x_ref, w_ref, h_ref):
        xh, _ = _rms_stats(x_ref[...])
        h_ref[...] = (xh * w_ref[...]).astype(BF16)

    return pl.pallas_call(
        body, name=name, grid=(t // tr,),
        in_specs=[_row_spec(tr, d), _vec_spec(d)],
        out_specs=_row_spec(tr, d),
        out_shape=jax.ShapeDtypeStruct((t, d), BF16),
        compiler_params=_cparams(1),
    )(x, w)


def _resid_rms(name, xres, y, w_post, w_next):
    t, d = xres.shape
    tr = _tile(t, ROW_TILE)
    has_next = w_next is not None

    def body(*refs):
        if has_next:
            x_ref, y_ref, wp_ref, wn_ref, xo_ref, h_ref = refs
        else:
            x_ref, y_ref, wp_ref, xo_ref, h_ref = refs
        yh, _ = _rms_stats(y_ref[...])
        xn = x_ref[...] + yh * wp_ref[...]
        xo_ref[...] = xn
        if has_next:
            xh, _ = _rms_stats(xn)
            h_ref[...] = (xh * wn_ref[...]).astype(BF16)
        else:
            h_ref[...] = xn.astype(BF16)

    ins = [xres, y, w_post] + ([w_next] if has_next else [])
    in_specs = [_row_spec(tr, d), _row_spec(tr, d), _vec_spec(d)] + ([_vec_spec(d)] if has_next else [])
    return pl.pallas_call(
        body, name=name, grid=(t // tr,),
        in_specs=in_specs,
        out_specs=[_row_spec(tr, d), _row_spec(tr, d)],
        out_shape=[jax.ShapeDtypeStruct((t, d), F32), jax.ShapeDtypeStruct((t, d), BF16)],
        compiler_params=_cparams(1),
    )(*ins)


def _ple_loss(name, x2, pe, pgl, w_pp, tgt):
    t, d = x2.shape
    tr = _tile(t, ROW_TILE)

    def body(x2_ref, pe_ref, pgl_ref, w_ref, tgt_ref, loss_ref, d3_ref, dpe_ref, dpgl_ref, dw_ref):
        @pl.when(pl.program_id(0) == 0)
        def _():
            loss_ref[...] = jnp.zeros_like(loss_ref)
            dw_ref[...] = jnp.zeros_like(dw_ref)

        pe_v = pe_ref[...]
        s = _sigmoid(pgl_ref[...])
        y = pe_v * s
        yh, r = _rms_stats(y)
        w = w_ref[...]
        err = x2_ref[...] + yh * w - tgt_ref[...]
        loss_ref[...] += 0.5 * jnp.sum(jnp.mean(err * err, axis=-1, keepdims=True), axis=0, keepdims=True)
        d3 = err * (1.0 / d)
        d3_ref[...] = d3
        dw_ref[...] += jnp.sum(d3 * yh, axis=0, keepdims=True)
        dy = _rms_bwd(yh, r, w, d3)
        dpe_ref[...] = (dy * s).astype(BF16)
        dpgl_ref[...] = (dy * pe_v * s * (1.0 - s)).astype(BF16)

    return pl.pallas_call(
        body, name=name, grid=(t // tr,),
        in_specs=[_row_spec(tr, d), _row_spec(tr, d), _row_spec(tr, d), _vec_spec(d), _row_spec(tr, d)],
        out_specs=[pl.BlockSpec((1, 1), lambda i: (0, 0)), _row_spec(tr, d), _row_spec(tr, d), _row_spec(tr, d), _vec_spec(d)],
        out_shape=[jax.ShapeDtypeStruct((1, 1), F32), jax.ShapeDtypeStruct((t, d), F32),
                   jax.ShapeDtypeStruct((t, d), BF16), jax.ShapeDtypeStruct((t, d), BF16),
                   jax.ShapeDtypeStruct((1, d), F32)],
        compiler_params=_cparams(1),
    )(x2, pe, pgl, w_pp, tgt)


def _norm_bwd(name, dres, y, w_post):
    t, d = dres.shape
    tr = _tile(t, ROW_TILE)

    def body(d_ref, y_ref, w_ref, dy_ref, dw_ref):
        @pl.when(pl.program_id(0) == 0)
        def _():
            dw_ref[...] = jnp.zeros_like(dw_ref)

        dv = d_ref[...]
        yh, r = _rms_stats(y_ref[...])
        dw_ref[...] += jnp.sum(dv * yh, axis=0, keepdims=True)
        dy_ref[...] = _rms_bwd(yh, r, w_ref[...], dv).astype(BF16)

    return pl.pallas_call(
        body, name=name, grid=(t // tr,),
        in_specs=[_row_spec(tr, d), _row_spec(tr, d), _vec_spec(d)],
        out_specs=[_row_spec(tr, d), _vec_spec(d)],
        out_shape=[jax.ShapeDtypeStruct((t, d), BF16), jax.ShapeDtypeStruct((1, d), F32)],
        compiler_params=_cparams(1),
    )(dres, y, w_post)


def _prenorm_bwd(name, dres, dh, xin, w_pre, y=None, w_post=None):
    t, d = dres.shape
    tr = _tile(t, ROW_TILE)
    two = y is not None

    def body(*refs):
        if two:
            d_ref, dh_ref, x_ref, wpre_ref, y_ref, wpost_ref, do_ref, dwpre_ref, dy_ref, dwpost_ref = refs
        else:
            d_ref, dh_ref, x_ref, wpre_ref, do_ref, dwpre_ref = refs

        @pl.when(pl.program_id(0) == 0)
        def _():
            dwpre_ref[...] = jnp.zeros_like(dwpre_ref)
            if two:
                dwpost_ref[...] = jnp.zeros_like(dwpost_ref)

        dhv = dh_ref[...]
        xh, r = _rms_stats(x_ref[...])
        dwpre_ref[...] += jnp.sum(dhv * xh, axis=0, keepdims=True)
        dout = d_ref[...] + _rms_bwd(xh, r, wpre_ref[...], dhv)
        do_ref[...] = dout
        if two:
            yh, ry = _rms_stats(y_ref[...])
            dwpost_ref[...] += jnp.sum(dout * yh, axis=0, keepdims=True)
            dy_ref[...] = _rms_bwd(yh, ry, wpost_ref[...], dout).astype(BF16)

    ins = [dres, dh, xin, w_pre] + ([y, w_post] if two else [])
    in_specs = [_row_spec(tr, d)] * 3 + [_vec_spec(d)] + ([_row_spec(tr, d), _vec_spec(d)] if two else [])
    out_specs = [_row_spec(tr, d), _vec_spec(d)] + ([_row_spec(tr, d), _vec_spec(d)] if two else [])
    out_shape = [jax.ShapeDtypeStruct((t, d), F32), jax.ShapeDtypeStruct((1, d), F32)]
    if two:
        out_shape += [jax.ShapeDtypeStruct((t, d), BF16), jax.ShapeDtypeStruct((1, d), F32)]
    return pl.pallas_call(
        body, name=name, grid=(t // tr,),
        in_specs=in_specs, out_specs=out_specs, out_shape=out_shape,
        compiler_params=_cparams(1),
    )(*ins)


_LEVELS = (32, 16, 8, 4, 2, 1)
_N_CUM = 3 + 2 * len(_LEVELS)


def _hgrn_constants():
    c = GLA_CHUNK
    idx = np.arange(c)
    t, r = idx[:, None], idx[None, :]
    mats = [(r <= t), (r > t), np.ones((c, c), bool)]
    lq, lk, masks = [], [], []
    for h in _LEVELS:
        blk, pos = idx // (2 * h), idx % (2 * h)
        mid = blk * 2 * h + h - 1
        upper, lower = pos >= h, pos < h
        lq.append(upper[:, None] & (r > mid[:, None]) & (r <= t))
        lk.append(lower[:, None] & (r > t) & (r <= mid[:, None]))
        masks.append((blk[:, None] == blk[None, :]) & upper[:, None] & lower[None, :])
    cum = np.concatenate(mats + lq + lk, axis=0).astype(np.float32)
    rev = (r >= t).astype(np.float32)
    return (jnp.asarray(cum, BF16), jnp.asarray(rev, BF16), jnp.asarray(np.stack(masks).astype(np.float32)))


def _hgrn_gates(qp, fp, lb):
    sq = _sigmoid(qp)
    q = qp * sq
    sg = _sigmoid(fp)
    f = lb + (1.0 - lb) * sg
    k = 1.0 - f
    logf = jnp.log(jnp.maximum(f, 1e-30))
    return q, sq, sg, f, k, logf


def _hgrn_decays(cum_ref, logf):
    c = GLA_CHUNK
    e = jnp.exp(_dot_exact_l(cum_ref[...], logf))
    part = lambda i: e[i * c:(i + 1) * c]
    n = len(_LEVELS)
    return part(0), part(1), part(2), [part(3 + i) for i in range(n)], [part(3 + n + i) for i in range(n)]


def _hgrn_fwd(proj, lb, nw, n_heads, comms=()):
    t = proj.shape[0]
    aw = n_heads * HEAD
    rb = _tile(t, HGRN_ROWS)
    c = GLA_CHUNK
    n_sub = rb // c
    cum, _, masks = _hgrn_constants()
    lay = _CommLayout(comms)
    grid = (n_heads, t // rb)

    def body(*refs):
        q_ref, f_ref, i_ref, g_ref, lb_ref, nw_ref, cum_ref, m_ref = refs[:8]
        cin = refs[8:8 + lay.n_in]
        a_ref, o_ref, s_ref, sc_ref = refs[8 + lay.n_in:12 + lay.n_in]
        cout = refs[12 + lay.n_in:12 + lay.n_in + lay.n_out]
        st = refs[12 + lay.n_in + lay.n_out]
        csem = refs[13 + lay.n_in + lay.n_out:]
        step = _linear_step(grid)
        lay.run(cin, cout, csem, step, grid[0] * grid[1], post=False)

        @pl.when(pl.program_id(1) == 0)
        def _():
            st[...] = jnp.zeros_like(st)

        lbv = lb_ref[...]
        nwv = nw_ref[...]
        eye = (lax.broadcasted_iota(jnp.int32, (c, c), 0) == lax.broadcasted_iota(jnp.int32, (c, c), 1)).astype(F32)

        def chunk(j, carry):
            rows = pl.ds(pl.multiple_of(j * c, c), c)
            q, _, _, _, k, logf = _hgrn_gates(q_ref[rows, :], f_ref[rows, :], lbv)
            v = i_ref[rows, :]
            eb, ebe, eend, eq, ek = _hgrn_decays(cum_ref, logf)
            s_ref[j] = st[...]
            inter = _dot(q * eb, st[...], "nt")
            scores = eye * jnp.sum(q * k, axis=-1, keepdims=True)
            for lvl in range(len(_LEVELS)):
                scores = scores + m_ref[lvl] * _dot(q * eq[lvl], k * ek[lvl], "nt")
            sc_ref[rows, :] = scores
            o = inter + _dot(scores, v)
            st[...] = st[...] * eend[0:1] + _dot(v, k * ebe, "tn")
            o_ref[rows, :] = o
            r = lax.rsqrt(jnp.mean(o * o, axis=-1, keepdims=True) + EPS)
            gv = g_ref[rows, :]
            a_ref[rows, :] = (o * r * nwv * (gv * _sigmoid(gv))).astype(BF16)
            return carry

        lax.fori_loop(0, n_sub, chunk, 0, unroll=HGRN_UNROLL)
        lay.run(cin, cout, csem, step, grid[0] * grid[1], post=True)

    col = lambda base: pl.BlockSpec((rb, HEAD), lambda h, r: (r, base * n_heads + h))
    vec = pl.BlockSpec((1, HEAD), lambda h, r: (0, h))
    res = pl.pallas_call(
        body, name="hgrn2_fwd", grid=grid,
        in_specs=[col(0), col(1), col(2), col(3), vec, vec,
                  pl.BlockSpec(cum.shape, lambda h, r: (0, 0)), pl.BlockSpec(masks.shape, lambda h, r: (0, 0, 0))] + lay.in_specs,
        out_specs=[pl.BlockSpec((rb, HEAD), lambda h, r: (r, h)), pl.BlockSpec((rb, HEAD), lambda h, r: (r, h)),
                   pl.BlockSpec((None, n_sub, HEAD, HEAD), lambda h, r: (h, r, 0, 0)),
                   pl.BlockSpec((None, rb, c), lambda h, r: (h, r, 0))] + lay.out_specs,
        out_shape=[jax.ShapeDtypeStruct((t, aw), BF16), jax.ShapeDtypeStruct((t, aw), F32),
                   jax.ShapeDtypeStruct((n_heads, t // c, HEAD, HEAD), F32),
                   jax.ShapeDtypeStruct((n_heads, t, c), F32)] + lay.out_shapes,
        scratch_shapes=[pltpu.VMEM((HEAD, HEAD), F32)] + lay.sem_shapes,
        compiler_params=_cparams(2),
    )(proj, proj, proj, proj, lb, nw, cum, masks, *lay.arrays)
    lay.deliver(res[4:])
    return res[:4]


def _hgrn_bwd(proj, lb, nw, o_raw, states, scores, dab, n_heads, comms=()):
    t = proj.shape[0]
    aw = n_heads * HEAD
    rb = _tile(t, HGRN_ROWS)
    c = GLA_CHUNK
    n_sub = rb // c
    n_rb = t // rb
    cum, rev, masks = _hgrn_constants()
    lay = _CommLayout(comms)
    grid = (n_heads, n_rb)

    def body(*refs):
        q_ref, f_ref, i_ref, g_ref, lb_ref, nw_ref, o_ref, s_ref, sc_ref, da_ref, cum_ref, rev_ref, m_ref = refs[:13]
        cin = refs[13:13 + lay.n_in]
        dq_ref, df_ref, di_ref, dg_ref, dlb_ref, dnw_ref = refs[13 + lay.n_in:19 + lay.n_in]
        cout = refs[19 + lay.n_in:19 + lay.n_in + lay.n_out]
        dst = refs[19 + lay.n_in + lay.n_out]
        csem = refs[20 + lay.n_in + lay.n_out:]
        step = _linear_step(grid)
        lay.run(cin, cout, csem, step, grid[0] * grid[1], post=False)

        @pl.when(pl.program_id(1) == 0)
        def _():
            dst[...] = jnp.zeros_like(dst)
            dlb_ref[...] = jnp.zeros_like(dlb_ref)
            dnw_ref[...] = jnp.zeros_like(dnw_ref)

        lbv = lb_ref[...]
        nwv = nw_ref[...]
        ri = lax.broadcasted_iota(jnp.int32, (c, c), 0)
        ci = lax.broadcasted_iota(jnp.int32, (c, c), 1)
        eye = (ri == ci).astype(F32)
        causal = (ci <= ri).astype(F32)
        last_row = (lax.broadcasted_iota(jnp.int32, (c, HEAD), 0) == c - 1).astype(F32)

        def chunk(jj, carry):
            j = n_sub - 1 - jj
            rows = pl.ds(pl.multiple_of(j * c, c), c)
            qp = q_ref[rows, :]
            q, sq, sg, f, k, logf = _hgrn_gates(qp, f_ref[rows, :], lbv)
            v = i_ref[rows, :]
            gv = g_ref[rows, :]
            eb, ebe, eend, eq, ek = _hgrn_decays(cum_ref, logf)
            s_in = s_ref[j]
            a_sc = sc_ref[rows, :]
            dsn = dst[...]
            o = o_ref[rows, :]
            r = lax.rsqrt(jnp.mean(o * o, axis=-1, keepdims=True) + EPS)
            oh = o * r
            sgg = _sigmoid(gv)
            sil = gv * sgg
            da = da_ref[rows, :]
            dg_ref[rows, :] = (da * oh * nwv * (sgg * (1.0 + gv * (1.0 - sgg)))).astype(BF16)
            dnw_ref[...] += jnp.sum(da * oh * sil, axis=0, keepdims=True)
            doh = da * nwv * sil
            do = r * (doh - oh * jnp.mean(doh * oh, axis=-1, keepdims=True))
            kt = k * ebe
            qt = q * eb
            di_ref[rows, :] = (_dot(a_sc, do, "tn") + _dot(kt, dsn, "nt")).astype(BF16)
            d_sc = _dot(do, v, "nt") * causal
            dqt = _dot(do, s_in)
            dkt = _dot(v, dsn)
            dst[...] = dsn * eend[0:1] + _dot(do, qt, "tn")
            diag = jnp.sum(d_sc * eye, axis=-1, keepdims=True)
            dq = dqt * eb
            dk = dkt * ebe
            db = q * dq - k * dk
            dq = dq + diag * k
            dk = dk + diag * q
            for lvl in range(len(_LEVELS)):
                dm = (m_ref[lvl] * d_sc).astype(BF16)
                ql = (q * eq[lvl]).astype(BF16)
                kl = (k * ek[lvl]).astype(BF16)
                gq = _dot(dm, kl)
                gk = _dot(dm, ql, "tn")
                dq = dq + gq * eq[lvl]
                dk = dk + gk * ek[lvl]
                db = db + ql.astype(F32) * gq - kl.astype(F32) * gk
            extra = jnp.sum(dkt * kt, axis=0, keepdims=True) + eend[0:1] * jnp.sum(s_in * dsn, axis=0, keepdims=True)
            db = db + last_row * extra
            dlogf = _dot_exact_l(rev_ref[...], db)
            dfv = jnp.where(f > 1e-30, dlogf / f, 0.0) - dk
            df_ref[rows, :] = (dfv * (1.0 - lbv) * sg * (1.0 - sg)).astype(BF16)
            dlb_ref[...] += jnp.sum(dfv * (1.0 - sg), axis=0, keepdims=True)
            dq_ref[rows, :] = (dq * (sq * (1.0 + qp * (1.0 - sq)))).astype(BF16)
            return carry

        lax.fori_loop(0, n_sub, chunk, 0, unroll=HGRN_UNROLL)
        lay.run(cin, cout, csem, step, grid[0] * grid[1], post=True)

    col = lambda base: pl.BlockSpec((rb, HEAD), lambda h, r: (n_rb - 1 - r, base * n_heads + h))
    blk = pl.BlockSpec((rb, HEAD), lambda h, r: (n_rb - 1 - r, h))
    vec = pl.BlockSpec((1, HEAD), lambda h, r: (0, h))
    const = lambda a: pl.BlockSpec(a.shape, lambda h, r: (0,) * a.ndim)
    res = pl.pallas_call(
        body, name="hgrn2_bwd", grid=grid,
        in_specs=[col(0), col(1), col(2), col(3), vec, vec, blk,
                  pl.BlockSpec((None, n_sub, HEAD, HEAD), lambda h, r: (h, n_rb - 1 - r, 0, 0)),
                  pl.BlockSpec((None, rb, c), lambda h, r: (h, n_rb - 1 - r, 0)),
                  blk, const(cum), const(rev), const(masks)] + lay.in_specs,
        out_specs=[blk, blk, blk, blk, vec, vec] + lay.out_specs,
        out_shape=[jax.ShapeDtypeStruct((t, aw), BF16)] * 4 + [jax.ShapeDtypeStruct((1, aw), F32)] * 2 + lay.out_shapes,
        scratch_shapes=[pltpu.VMEM((HEAD, HEAD), F32)] + lay.sem_shapes,
        compiler_params=_cparams(2),
    )(proj, proj, proj, proj, lb, nw, o_raw, states, scores, dab, cum, rev, masks, *lay.arrays)
    lay.deliver(res[6:])
    return res[:6]


def _lb_fwd(lb_param):
    def body(p_ref, o_ref):
        p = p_ref[...]
        e = jnp.exp(p - jnp.max(p, axis=0, keepdims=True))
        o_ref[...] = e[0:1] / jnp.sum(e, axis=0, keepdims=True)

    return pl.pallas_call(body, name="lb_fwd", out_shape=jax.ShapeDtypeStruct((1, lb_param.shape[1]), F32))(lb_param)


def _lb_bwd(lb_param, dlb):
    def body(p_ref, d_ref, o_ref):
        p = p_ref[...]
        e = jnp.exp(p - jnp.max(p, axis=0, keepdims=True))
        s = e / jnp.sum(e, axis=0, keepdims=True)
        first = (lax.broadcasted_iota(jnp.int32, p.shape, 0) == 0).astype(F32)
        o_ref[...] = d_ref[...] * s[0:1] * (first - s)

    return pl.pallas_call(body, name="lb_bwd", out_shape=jax.ShapeDtypeStruct(lb_param.shape, F32))(lb_param, dlb)


def _gmlp_norm(v, lnw, lnb):
    vf = _gelu(v)
    mu = jnp.mean(vf, axis=-1, keepdims=True)
    cen = vf - mu
    rstd = lax.rsqrt(jnp.mean(cen * cen, axis=-1, keepdims=True) + EPS)
    xh = cen * rstd
    return xh, rstd, xh * lnw + lnb


def _tril(n):
    return (lax.broadcasted_iota(jnp.int32, (n, n), 1) <= lax.broadcasted_iota(jnp.int32, (n, n), 0)).astype(F32)


def _gmlp_fwd(proj, lnw, lnb, w_sp, bs_t, n_groups, col_base):
    t = proj.shape[0]
    bw = n_groups * HEAD
    c = GMLP_CHUNK

    def body(u_ref, v_ref, lnw_ref, lnb_ref, w_ref, bs_ref, o_ref):
        tri = _tril(c)
        uf = _gelu(u_ref[...])
        _, _, vn = _gmlp_norm(v_ref[...], lnw_ref[...], lnb_ref[...])
        for g in range(n_groups):
            cols = slice(g * HEAD, (g + 1) * HEAD)
            z = _dot(w_ref[g] * tri, vn[:, cols]) + bs_ref[:, g:g + 1]
            o_ref[:, cols] = (uf[:, cols] * z).astype(BF16)

    blk = lambda b: pl.BlockSpec((c, bw), lambda n: (n, b))
    const = lambda a: pl.BlockSpec(a.shape, lambda n: (0,) * a.ndim)
    return pl.pallas_call(
        body, name="gmlp_fwd", grid=(t // c,),
        in_specs=[blk(col_base), blk(col_base + 1), const(lnw), const(lnb), const(w_sp), const(bs_t)],
        out_specs=pl.BlockSpec((c, bw), lambda n: (n, 0)),
        out_shape=jax.ShapeDtypeStruct((t, bw), BF16),
        compiler_params=_cparams(1),
    )(proj, proj, lnw, lnb, w_sp, bs_t)


def _gmlp_bwd(proj, lnw, lnb, w_sp, bs_t, dab, n_groups, col_base):
    t = proj.shape[0]
    bw = n_groups * HEAD
    c = GMLP_CHUNK
    n_steps = t // c
    sel = jnp.asarray((np.arange(bw)[:, None] // HEAD == np.arange(n_groups)[None, :]).astype(np.float32), BF16)

    def body(u_ref, v_ref, lnw_ref, lnb_ref, w_ref, bs_ref, d_ref, sel_ref,
             du_ref, dv_ref, dlnw_ref, dlnb_ref, dw_ref, dbs_ref, dz_acc, dvn_scr):
        step = pl.program_id(0)

        @pl.when(step == 0)
        def _():
            dlnw_ref[...] = jnp.zeros_like(dlnw_ref)
            dlnb_ref[...] = jnp.zeros_like(dlnb_ref)
            dw_ref[...] = jnp.zeros_like(dw_ref)
            dz_acc[...] = jnp.zeros_like(dz_acc)

        tri = _tril(c)
        u = u_ref[...]
        v = v_ref[...]
        uf = _gelu(u)
        lnw_v = lnw_ref[...]
        xh, rstd, vn = _gmlp_norm(v, lnw_v, lnb_ref[...])
        dbo = d_ref[...]
        dz = dbo * uf
        dz_acc[...] += dz
        for g in range(n_groups):
            cols = slice(g * HEAD, (g + 1) * HEAD)
            wg = w_ref[g] * tri
            z = _dot(wg, vn[:, cols]) + bs_ref[:, g:g + 1]
            du_ref[:, cols] = (dbo[:, cols] * z * _gelu_grad(u[:, cols])).astype(BF16)
            dvn_scr[:, cols] = _dot(wg, dz[:, cols], "tn")
            dw_ref[g] += tri * _dot(dz[:, cols], vn[:, cols], "nt")
        dvn = dvn_scr[...]
        dlnw_ref[...] += jnp.sum(dvn * xh, axis=0, keepdims=True)
        dlnb_ref[...] += jnp.sum(dvn, axis=0, keepdims=True)
        dxh = dvn * lnw_v
        dvf = rstd * (dxh - jnp.mean(dxh, axis=-1, keepdims=True) - xh * jnp.mean(dxh * xh, axis=-1, keepdims=True))
        dv_ref[...] = (dvf * _gelu_grad(v)).astype(BF16)

        @pl.when(step == n_steps - 1)
        def _():
            dbs_ref[...] = _dot_exact_r(dz_acc[...], sel_ref[...])

    blk = lambda b: pl.BlockSpec((c, bw), lambda n: (n, b))
    const = lambda a: pl.BlockSpec(a.shape, lambda n: (0,) * a.ndim)
    row = pl.BlockSpec((c, bw), lambda n: (n, 0))
    vec = pl.BlockSpec((1, bw), lambda n: (0, 0))
    return pl.pallas_call(
        body, name="gmlp_bwd", grid=(n_steps,),
        in_specs=[blk(col_base), blk(col_base + 1), const(lnw), const(lnb), const(w_sp), const(bs_t), blk(1), const(sel)],
        out_specs=[row, row, vec, vec, const(w_sp), const(bs_t)],
        out_shape=[jax.ShapeDtypeStruct((t, bw), BF16), jax.ShapeDtypeStruct((t, bw), BF16),
                   jax.ShapeDtypeStruct((1, bw), F32), jax.ShapeDtypeStruct((1, bw), F32),
                   jax.ShapeDtypeStruct(w_sp.shape, F32), jax.ShapeDtypeStruct(bs_t.shape, F32)],
        scratch_shapes=[pltpu.VMEM((c, bw), F32), pltpu.VMEM((c, bw), F32)],
        compiler_params=_cparams(1),
    )(proj, proj, lnw, lnb, w_sp, bs_t, dab, sel)


def _pair_sum(name, grad, other, core):
    _, _, r, c = grad.shape
    tr = _rows(r, 6 * c)

    def body(core_ref, g_ref, o_ref, out_ref):
        out_ref[...] = (g_ref[...].astype(F32) + o_ref[...].astype(F32)).astype(BF16)

    return pl.pallas_call(
        body, name=name,
        grid_spec=pltpu.PrefetchScalarGridSpec(
            num_scalar_prefetch=1, grid=(N_CHIP, r // tr),
            in_specs=[pl.BlockSpec((None, None, tr, c), lambda k, i, core_ref: (k, core_ref[0], i, 0)),
                      pl.BlockSpec((None, tr, c), lambda k, i, core_ref: (k, i, 0))],
            out_specs=pl.BlockSpec((None, tr, c), lambda k, i, core_ref: (k, i, 0))),
        out_shape=jax.ShapeDtypeStruct((N_CHIP, r, c), BF16),
        compiler_params=_cparams(2),
    )(core, grad, other)


def _adamw_math(w, g, m, v):
    m = ADAM_B1 * m + (1.0 - ADAM_B1) * g
    v = ADAM_B2 * v + (1.0 - ADAM_B2) * (g * g)
    m_hat = m / (1.0 - ADAM_B1 ** ADAM_STEP)
    v_hat = v / (1.0 - ADAM_B2 ** ADAM_STEP)
    delta = -ADAM_LR * (m_hat / (jnp.sqrt(v_hat) + ADAM_EPS) + ADAM_WD * w)
    return delta, m, v


def _adamw(name, parts, w, m, v):
    n_parts, r, c = parts.shape
    tr = _rows(r, c * (n_parts * parts.dtype.itemsize + 28), mult=8)

    def body(p_ref, w_ref, m_ref, v_ref, g_ref, d_ref, mo_ref, vo_ref):
        g = p_ref[0].astype(F32)
        for i in range(1, n_parts):
            g = g + p_ref[i].astype(F32)
        g_ref[...] = g
        d_ref[...], mo_ref[...], vo_ref[...] = _adamw_math(w_ref[...], g, m_ref[...], v_ref[...])

    row = pl.BlockSpec((tr, c), lambda i: (i, 0))
    return pl.pallas_call(
        body, name=name, grid=(r // tr,),
        in_specs=[pl.BlockSpec((n_parts, tr, c), lambda i: (0, i, 0)), row, row, row],
        out_specs=[row] * 4,
        out_shape=[jax.ShapeDtypeStruct((r, c), F32)] * 4,
        compiler_params=_cparams(1),
    )(parts, w, m, v)


def kernel(x, p, pre_mix_w, w_in, lb_param, a_norm_w, gmlp_ln_w, gmlp_ln_b, w_spatial, b_spatial, w_out, post_mix_w, pre_ffn_w, w_gate, w_up, w_down, post_ffn_w, w_ple, w_ple_gate, post_ple_w, loss_target, m_pre_mix_w, m_w_in, m_lb_param, m_a_norm_w, m_gmlp_ln_w, m_gmlp_ln_b, m_w_spatial, m_b_spatial, m_w_out, m_post_mix_w, m_pre_ffn_w, m_w_gate, m_w_up, m_w_down, m_post_ffn_w, m_w_ple, m_w_ple_gate, m_post_ple_w, v_pre_mix_w, v_w_in, v_lb_param, v_a_norm_w, v_gmlp_ln_w, v_gmlp_ln_b, v_w_spatial, v_b_spatial, v_w_out, v_post_mix_w, v_pre_ffn_w, v_w_gate, v_w_up, v_w_down, v_post_ffn_w, v_w_ple, v_w_ple_gate, v_post_ple_w):
    big_names = ["w_in", "w_out", "w_gate", "w_up", "w_down", "w_ple", "w_ple_gate"]
    small_names = ["pre_mix_w", "lb_param", "a_norm_w", "gmlp_ln_w", "gmlp_ln_b", "w_spatial", "b_spatial",
                   "post_mix_w", "pre_ffn_w", "post_ffn_w", "post_ple_w"]
    all_names = ["pre_mix_w", "w_in", "lb_param", "a_norm_w", "gmlp_ln_w", "gmlp_ln_b", "w_spatial", "b_spatial",
                 "w_out", "post_mix_w", "pre_ffn_w", "w_gate", "w_up", "w_down", "post_ffn_w", "w_ple", "w_ple_gate",
                 "post_ple_w"]
    env = dict(locals())
    W = {n: env[n] for n in all_names}
    M = {n: env["m_" + n] for n in all_names}
    V = {n: env["v_" + n] for n in all_names}

    xs = x[0]
    ps = p[0, 0]
    tgt = loss_target[0]
    t, d = xs.shape
    aw = a_norm_w.shape[1]
    bw = gmlp_ln_w.shape[1]
    n_heads, n_groups = aw // HEAD, bw // HEAD
    core = lax.axis_index("c").astype(jnp.int32).reshape(1)

    shard = {n: W[n][0] for n in big_names}
    bf = {n: _cast_bf16("cast_" + n, shard[n]) for n in big_names}
    ag_in = _ag_comm([bf["w_in"]])
    _comm_only("ag_w_in", [ag_in])
    win_g = ag_in.results[0]
    n_in = win_g.shape[2]
    ffl = bf["w_gate"].shape[1]
    n_ple = bf["w_ple"].shape[1]
    ple = ps.shape[1]

    TM, TK = 1024, 512
    tm = _tile(t, TM)
    tkd = _tile(d, TK)

    h1 = _rms_fwd("rms_pre_mix", xs, pre_mix_w)
    ag_a = _ag_comm([bf["w_gate"]], mid_frac=0.85)
    proj = _matmul(
        "mm_proj",
        [(h1, pl.BlockSpec((tm, tkd), lambda m, n, k: (m, k))), (win_g, pl.BlockSpec((None, tkd, n_in), lambda m, n, k: (n, k, 0)))],
        [(0, 1, "nn", 0)],
        [(jax.ShapeDtypeStruct((t, N_DEV * n_in), F32), pl.BlockSpec((tm, n_in), lambda m, n, k: (m, n)))],
        (t // tm, N_DEV, d // tkd), (tm, n_in), comms=[ag_a])[0]
    wgate_g = ag_a.results[0]
    lb = _lb_fwd(lb_param)
    ag_b = _ag_comm([bf["w_up"], bf["w_out"]], mid_frac=0.75)
    a_out, o_raw, states, scores = _hgrn_fwd(proj, lb, a_norm_w, n_heads, comms=[ag_b])
    wup_g = ag_b.results[0]
    wout_f = ag_b.results[1].reshape(d, d)
    bs_t = b_spatial[0].T
    w_sp = w_spatial[0]
    col_u = (4 * aw) // bw
    b_out = _gmlp_fwd(proj, gmlp_ln_w, gmlp_ln_b, w_sp, bs_t, n_groups, col_u)
    ab = jnp.concatenate([a_out, b_out], axis=1)
    mix = _mm_plain("mm_mix", ab, wout_f, "nn", F32, TM, 2048, TK)
    x1, h2 = _resid_rms("resid_mix", xs, mix, post_mix_w, pre_ffn_w)

    def swiglu(accs):
        gate, up = accs
        return gate, up, gate * _sigmoid(gate) * up

    tmf = tm
    blk3 = lambda: pl.BlockSpec((None, tmf, ffl), lambda j, m, k: (j, m, 0))
    ag_c = _ag_comm([bf["w_down"], bf["w_ple_gate"], bf["w_ple"]], mid_frac=0.7)
    gate, up, act = _matmul(
        "mm_ffn_up",
        [(h2, pl.BlockSpec((tmf, tkd), lambda j, m, k: (m, k))),
         (wgate_g, pl.BlockSpec((None, tkd, ffl), lambda j, m, k: (j, k, 0))),
         (wup_g, pl.BlockSpec((None, tkd, ffl), lambda j, m, k: (j, k, 0)))],
        [(0, 1, "nn", 0), (0, 2, "nn", 1)],
        [(jax.ShapeDtypeStruct((N_DEV, t, ffl), BF16), blk3()) for _ in range(3)],
        (N_DEV, t // tmf, d // tkd), (tmf, ffl), n_slots=2, epilogue=swiglu, comms=[ag_c])
    wdown_g = ag_c.results[0]
    wpg_f = ag_c.results[1].reshape(d, d)
    wple_g = ag_c.results[2]
    tn_d = _tile(d, 2048)
    ff = _matmul(
        "mm_ffn_down",
        [(act, pl.BlockSpec((None, tm, ffl), lambda m, n, k: (k, m, 0))),
         (wdown_g, pl.BlockSpec((None, ffl, tn_d), lambda m, n, k: (k, 0, n)))],
        [(0, 1, "nn", 0)],
        [(jax.ShapeDtypeStruct((t, d), F32), pl.BlockSpec((tm, tn_d), lambda m, n, k: (m, n)))],
        (t // tm, d // tn_d, N_DEV), (tm, tn_d))[0]
    x2, x2b = _resid_rms("resid_ffn", x1, ff, post_ffn_w, None)

    pgl = _mm_plain("mm_ple_gate", x2b, wpg_f, "nn", F32, TM, 2048, TK)
    pe = _matmul(
        "mm_ple",
        [(ps, pl.BlockSpec((tm, ple), lambda m, n, k: (m, 0))), (wple_g, pl.BlockSpec((None, ple, n_ple), lambda m, n, k: (n, 0, 0)))],
        [(0, 1, "nn", 0)],
        [(jax.ShapeDtypeStruct((t, N_DEV * n_ple), F32), pl.BlockSpec((tm, n_ple), lambda m, n, k: (m, n)))],
        (t // tm, N_DEV, 1), (tm, n_ple))[0]
    loss_part, d3, dpe, dpgl, g_post_ple = _ple_loss("ple_loss", x2, pe, pgl, post_ple_w, tgt)

    tkt = _tile(t, TK)
    g_wple = _matmul(
        "mm_dw_ple",
        [(ps, pl.BlockSpec((tkt, ple), lambda n, k: (k, 0))), (dpe, pl.BlockSpec((tkt, n_ple), lambda n, k: (k, n)))],
        [(0, 1, "tn", 0)],
        [(jax.ShapeDtypeStruct((N_DEV, ple, n_ple), BF16), pl.BlockSpec((None, ple, n_ple), lambda n, k: (n, 0, 0)))],
        (N_DEV, t // tkt), (ple, n_ple))[0]
    g_wpg = _mm_plain("mm_dw_ple_gate", x2b, dpgl, "tn", BF16, TM, 2048, TK)

    def by_chip(g):
        return g.reshape((N_CHIP, 2) + g.shape[-2:])

    def pair_sums(names, comm):
        return [_pair_sum("pair_sum_" + n, g, o, core) for n, g, o in zip(names, comm.arrays, comm.results)]

    r1_p = _pair_comm([by_chip(g_wpg.reshape(N_DEV, d // N_DEV, d)), by_chip(g_wple)])
    d2 = _mm_plain("mm_d_x2", dpgl, wpg_f, "nt", F32, TM, 1024, TK, extra=d3, epilogue=lambda accs, e: [accs[0] + e],
                   comms=[r1_p])
    r2_p = _chip_comm(pair_sums(["w_ple_gate", "w_ple"], r1_p))

    dff, g_post_ffn = _norm_bwd("norm_bwd_ffn", d2, ff, post_ffn_w)
    g_wdown = _matmul(
        "mm_dw_down",
        [(act, pl.BlockSpec((None, tkt, ffl), lambda j, n, k: (j, k, 0))), (dff, pl.BlockSpec((tkt, tn_d), lambda j, n, k: (k, n)))],
        [(0, 1, "tn", 0)],
        [(jax.ShapeDtypeStruct((N_DEV, ffl, d), BF16), pl.BlockSpec((None, ffl, tn_d), lambda j, n, k: (j, 0, n)))],
        (N_DEV, d // tn_d, t // tkt), (ffl, tn_d), comms=[r2_p])[0]
    r1_d = _pair_comm([by_chip(g_wdown)])

    def swiglu_bwd(accs, gate_v, up_v):
        dact = accs[0]
        gf = gate_v.astype(F32)
        sg = _sigmoid(gf)
        return dact * up_v.astype(F32) * (sg * (1.0 + gf * (1.0 - sg))), dact * (gf * sg)

    dgate, dup = _matmul(
        "mm_d_act",
        [(dff, pl.BlockSpec((tmf, tkd), lambda j, m, k: (m, k))),
         (wdown_g, pl.BlockSpec((None, ffl, tkd), lambda j, m, k: (j, 0, k))),
         (gate, blk3()), (up, blk3())],
        [(0, 1, "nt", 0)],
        [(jax.ShapeDtypeStruct((N_DEV, t, ffl), BF16), blk3()) for _ in range(2)],
        (N_DEV, t // tmf, d // tkd), (tmf, ffl), epilogue=swiglu_bwd, comms=[r1_d])
    r2_d = _chip_comm(pair_sums(["w_down"], r1_d))
    tmd = _tile(d, TM)
    g_wgate, g_wup = _matmul(
        "mm_dw_gate_up",
        [(h2, pl.BlockSpec((tkt, tmd), lambda j, m, k: (k, m))),
         (dgate, pl.BlockSpec((None, tkt, ffl), lambda j, m, k: (j, k, 0))),
         (dup, pl.BlockSpec((None, tkt, ffl), lambda j, m, k: (j, k, 0)))],
        [(0, 1, "tn", 0), (0, 2, "tn", 1)],
        [(jax.ShapeDtypeStruct((N_DEV, d, ffl), BF16), pl.BlockSpec((None, tmd, ffl), lambda j, m, k: (j, m, 0))) for _ in range(2)],
        (N_DEV, d // tmd, t // tkt), (tmd, ffl), n_slots=2, comms=[r2_d])
    r1_gu = _pair_comm([by_chip(g_wgate), by_chip(g_wup)])
    tn1 = _tile(d, 1024)
    dh2 = _matmul(
        "mm_d_h2",
        [(dgate, pl.BlockSpec((None, tm, ffl), lambda m, n, k: (k, m, 0))),
         (wgate_g, pl.BlockSpec((None, tn1, ffl), lambda m, n, k: (k, n, 0))),
         (dup, pl.BlockSpec((None, tm, ffl), lambda m, n, k: (k, m, 0))),
         (wup_g, pl.BlockSpec((None, tn1, ffl), lambda m, n, k: (k, n, 0)))],
        [(0, 1, "nt", 0), (2, 3, "nt", 0)],
        [(jax.ShapeDtypeStruct((t, d), F32), pl.BlockSpec((tm, tn1), lambda m, n, k: (m, n)))],
        (t // tm, d // tn1, N_DEV), (tm, tn1), comms=[r1_gu])[0]
    s_gate, s_up = pair_sums(["w_gate", "w_up"], r1_gu)
    d1, g_pre_ffn, dmix, g_post_mix = _prenorm_bwd("prenorm_bwd_ffn", d2, dh2, x1, pre_ffn_w, mix, post_mix_w)

    g_wout = _mm_plain("mm_dw_out", ab, dmix, "tn", BF16, TM, 2048, TK)
    r1_o = _pair_comm([by_chip(g_wout.reshape(N_DEV, d // N_DEV, d))])
    dab = _mm_plain("mm_d_ab", dmix, wout_f, "nt", F32, TM, 2048, TK, comms=[r1_o])
    r2_gu = _chip_comm([s_gate, s_up])
    r2_o = _chip_comm(pair_sums(["w_out"], r1_o))
    dq, df, di, dg, dlb, g_a_norm = _hgrn_bwd(proj, lb, a_norm_w, o_raw, states, scores, dab, n_heads, comms=[r2_gu])
    du, dv, g_ln_w, g_ln_b, g_wsp, g_bs_t = _gmlp_bwd(proj, gmlp_ln_w, gmlp_ln_b, w_sp, bs_t, dab, n_groups, col_u)
    dproj = jnp.concatenate([dq, df, di, dg, du, dv], axis=1)
    g_win = _matmul(
        "mm_dw_in",
        [(h1, pl.BlockSpec((tkt, tmd), lambda j, m, k: (k, m))), (dproj, pl.BlockSpec((tkt, n_in), lambda j, m, k: (k, j)))],
        [(0, 1, "tn", 0)],
        [(jax.ShapeDtypeStruct((N_DEV, d, n_in), BF16), pl.BlockSpec((None, tmd, n_in), lambda j, m, k: (j, m, 0)))],
        (N_DEV, d // tmd, t // tkt), (tmd, n_in), comms=[r2_o])[0]
    r1_in = _pair_comm([by_chip(g_win)])
    _comm_only("rs_pair_w_in", [r1_in])
    r2_in = _chip_comm(pair_sums(["w_in"], r1_in))
    dh1 = _matmul(
        "mm_d_h1",
        [(dproj, pl.BlockSpec((tm, n_in), lambda m, n, k: (m, k))), (win_g, pl.BlockSpec((None, tn1, n_in), lambda m, n, k: (k, n, 0)))],
        [(0, 1, "nt", 0)],
        [(jax.ShapeDtypeStruct((t, d), F32), pl.BlockSpec((tm, tn1), lambda m, n, k: (m, n)))],
        (t // tm, d // tn1, N_DEV), (tm, tn1), comms=[r2_in])[0]
    grad_x, g_pre_mix = _prenorm_bwd("prenorm_bwd_mix", d1, dh1, xs, pre_mix_w)

    reduced = {
        "w_in": r2_in.results[0], "w_out": r2_o.results[0], "w_gate": r2_gu.results[0], "w_up": r2_gu.results[1],
        "w_down": r2_d.results[0], "w_ple": r2_p.results[1], "w_ple_gate": r2_p.results[0],
    }
    grads, deltas, new_m, new_v = {}, {}, {}, {}
    for n in big_names:
        g, dl, mo, vo = _adamw("adamw_" + n, reduced[n], shard[n], M[n][0], V[n][0])
        grads[n], deltas[n], new_m[n], new_v[n] = g[None], dl[None], mo[None], vo[None]

    small_grad = {
        "pre_mix_w": g_pre_mix, "lb_param": _lb_bwd(lb_param, dlb), "a_norm_w": g_a_norm, "gmlp_ln_w": g_ln_w,
        "gmlp_ln_b": g_ln_b, "w_spatial": g_wsp, "b_spatial": g_bs_t.T, "post_mix_w": g_post_mix,
        "pre_ffn_w": g_pre_ffn, "post_ffn_w": g_post_ffn, "post_ple_w": g_post_ple,
    }
    pack = lambda get: jnp.concatenate([get(n).reshape(-1, LANE) for n in small_names], axis=0)
    ag_small = _ag_comm([pack(lambda n: small_grad[n])])
    _comm_only("ag_small", [ag_small], in_vmem=True)
    g_all = ag_small.results[0]
    sg, sd, sm, sv = _adamw("adamw_small", g_all, pack(lambda n: W[n]), pack(lambda n: M[n]), pack(lambda n: V[n]))
    off = 0
    for n in small_names:
        rows = W[n].size // LANE
        for src, dst in ((sg, grads), (sd, deltas), (sm, new_m), (sv, new_v)):
            dst[n] = src[off:off + rows].reshape(W[n].shape)
        off += rows

    loss = lax.psum(loss_part[0, 0], ("x", "y", "c"))
    return (loss, grad_x[None], *[grads[n] for n in all_names], *[deltas[n] for n in all_names],
            *[new_m[n] for n in all_names], *[new_v[n] for n in all_names])
```

```python
import functools

import numpy as np
import jax
import jax.numpy as jnp
from jax import lax
from jax.experimental import pallas as pl
from jax.experimental.pallas import tpu as pltpu

F32 = jnp.float32
BF16 = jnp.bfloat16

EPS = 1e-6
HEAD = 128
GLA_CHUNK = 64
GMLP_CHUNK = 128
N_DEV = 8
N_CHIP = 4
LANE = 128
VMEM_LIMIT = 56 * 1024 * 1024
HGRN_ROWS = 512
ROW_TILE = 128
EPILOGUE_ROWS = 256
HGRN_UNROLL = 1
HGRN_HEADS = 8

ADAM_LR = 0.001
ADAM_B1 = 0.9
ADAM_B2 = 0.999
ADAM_EPS = 1e-08
ADAM_WD = 0.01
ADAM_STEP = 10

MESH = pl.DeviceIdType.MESH
ANY = pl.BlockSpec(memory_space=pl.ANY)

_DIMS = {
    "nn": (((1,), (0,)), ((), ())),
    "nt": (((1,), (1,)), ((), ())),
    "tn": (((0,), (0,)), ((), ())),
}


def _tile(dim, pref):
    return pref if dim % pref == 0 else dim


def _rows(r, bytes_per_row, budget=16 * 1024 * 1024, mult=16):
    best = None
    for cand in range(mult, r + 1, mult):
        if r % cand == 0 and cand * bytes_per_row <= budget:
            best = cand
    return best if best is not None else r


def _cparams(n_axes):
    return pltpu.CompilerParams(dimension_semantics=("arbitrary",) * n_axes, vmem_limit_bytes=VMEM_LIMIT)


def _dot(a, b, form="nn"):
    return lax.dot_general(a.astype(BF16), b.astype(BF16), _DIMS[form], preferred_element_type=F32)


def _split3(x):
    hi = x.astype(BF16)
    r = x - hi.astype(F32)
    mid = r.astype(BF16)
    lo = (r - mid.astype(F32)).astype(BF16)
    return hi, mid, lo


def _dot_exact_l(c, x):
    hi, mid, lo = _split3(x)
    d = lambda y: lax.dot_general(c, y, _DIMS["nn"], preferred_element_type=F32)
    return d(hi) + d(mid) + d(lo)


def _dot_exact_r(x, c):
    hi, mid, lo = _split3(x)
    d = lambda y: lax.dot_general(y, c, _DIMS["nn"], preferred_element_type=F32)
    return d(hi) + d(mid) + d(lo)


def _sigmoid(x):
    return 1.0 / (1.0 + jnp.exp(-x))


def _gelu(x):
    return 0.5 * x * (1.0 + lax.erf(x * 0.7071067811865476))


def _gelu_grad(x):
    cdf = 0.5 * (1.0 + lax.erf(x * 0.7071067811865476))
    pdf = jnp.exp(-0.5 * x * x) * 0.3989422804014327
    return cdf + x * pdf


def _position():
    return lax.axis_index("x"), lax.axis_index("y"), lax.axis_index("c")


def _linear_step(grid):
    step = 0
    for ax, n in enumerate(grid):
        step = step * n + pl.program_id(ax)
    return step


class _Comm:
    def __init__(self, arrays, out_shapes, sem_shapes, phases):
        self.arrays, self.out_shapes, self.sem_shapes, self.phases = list(arrays), list(out_shapes), list(sem_shapes), phases
        self.results = None


class _CommLayout:
    def __init__(self, comms, space=pl.ANY):
        self.comms = list(comms)
        self.arrays = [a for c in self.comms for a in c.arrays]
        self.out_shapes = [s for c in self.comms for s in c.out_shapes]
        self.sem_shapes = [s for c in self.comms for s in c.sem_shapes]
        self.n_in, self.n_out = len(self.arrays), len(self.out_shapes)
        self.in_specs = [pl.BlockSpec(memory_space=space)] * self.n_in
        self.out_specs = [pl.BlockSpec(memory_space=space)] * self.n_out

    def run(self, cin, cout, csem, step, n_steps, post):
        i = o = s = 0
        for c in self.comms:
            ins, outs, sems = cin[i:i + len(c.arrays)], cout[o:o + len(c.out_shapes)], csem[s:s + len(c.sem_shapes)]
            i, o, s = i + len(c.arrays), o + len(c.out_shapes), s + len(c.sem_shapes)
            for frac, fn in c.phases:
                if (frac is None) != post:
                    continue
                due = n_steps - 1 if frac is None else max(0, min(int(frac * n_steps), n_steps - 2))
                if n_steps == 1:
                    fn(ins, outs, sems)
                else:
                    pl.when(step == due)(functools.partial(fn, ins, outs, sems))

    def deliver(self, results):
        o = 0
        for c in self.comms:
            c.results = list(results[o:o + len(c.out_shapes)])
            o += len(c.out_shapes)


def _comm_only(name, comms, in_vmem=False):
    lay = _CommLayout(comms, pltpu.VMEM if in_vmem else pl.ANY)

    def body(*refs):
        cin, cout, csem = refs[:lay.n_in], refs[lay.n_in:lay.n_in + lay.n_out], refs[lay.n_in + lay.n_out:]
        lay.run(cin, cout, csem, 0, 1, post=False)
        lay.run(cin, cout, csem, 0, 1, post=True)

    res = pl.pallas_call(
        body, name=name, in_specs=lay.in_specs, out_specs=lay.out_specs, out_shape=lay.out_shapes,
        scratch_shapes=lay.sem_shapes,
    )(*lay.arrays)
    lay.deliver(res)


def _ag_comm(shards, mid_frac=0.0):
    n = len(shards)
    per = N_DEV - 1

    def tools(ins, outs, sems):
        send_sems, recv_sems, local_sems = sems
        x, y, c = _position()
        me, sibling = (x, y, c), (x, y, 1 - c)
        chips = [(1 - x, y), (x, 1 - y), (1 - x, 1 - y)]

        def copy(a, k, block, to, from_shard=False):
            dst = outs[a].at[4 * block[0] + 2 * block[1] + block[2]]
            return pltpu.make_async_remote_copy(
                src_ref=ins[a] if from_shard else dst, dst_ref=dst,
                send_sem=send_sems.at[a * per + k], recv_sem=recv_sems.at[a * per + k],
                device_id=to, device_id_type=MESH)

        def local(a):
            return pltpu.make_async_copy(ins[a], outs[a].at[4 * x + 2 * y + c], local_sems.at[a])

        return me, sibling, chips, c, copy, local

    def first(ins, outs, sems):
        me, sibling, chips, c, copy, local = tools(ins, outs, sems)
        for a in range(n):
            local(a).start()
            copy(a, 0, me, sibling, True).start()
            for j, chip in enumerate(chips):
                copy(a, 1 + j, me, (*chip, c), True).start()

    def middle(ins, outs, sems):
        me, sibling, chips, c, copy, local = tools(ins, outs, sems)
        for a in range(n):
            for j, chip in enumerate(chips):
                copy(a, 1 + j, (*chip, c), me).wait_recv()
                copy(a, 4 + j, (*chip, c), sibling).start()

    def last(ins, outs, sems):
        me, sibling, chips, c, copy, local = tools(ins, outs, sems)
        for a in range(n):
            copy(a, 0, sibling, me).wait_recv()
            copy(a, 0, me, sibling, True).wait_send()
            for j, chip in enumerate(chips):
                copy(a, 4 + j, (*chip, 1 - c), me).wait_recv()
                copy(a, 1 + j, me, (*chip, c), True).wait_send()
                copy(a, 4 + j, (*chip, c), sibling).wait_send()
            local(a).wait()

    return _Comm(
        shards, [jax.ShapeDtypeStruct((N_DEV,) + s.shape, s.dtype) for s in shards],
        [pltpu.SemaphoreType.DMA((n * per,)), pltpu.SemaphoreType.DMA((n * per,)), pltpu.SemaphoreType.DMA((n,))],
        [(0.0, first), (mid_frac, middle), (None, last)])


def _pair_comm(grads):
    n = len(grads)

    def copies(ins, outs, sems):
        send_sems, recv_sems = sems
        x, y, c = _position()
        return [pltpu.make_async_remote_copy(
            src_ref=ins[a].at[k, 1 - c], dst_ref=outs[a].at[k],
            send_sem=send_sems.at[a * N_CHIP + k], recv_sem=recv_sems.at[a * N_CHIP + k],
            device_id=(x, y, 1 - c), device_id_type=MESH) for a in range(n) for k in range(N_CHIP)]

    def first(ins, outs, sems):
        for cp in copies(ins, outs, sems):
            cp.start()

    def last(ins, outs, sems):
        for cp in copies(ins, outs, sems):
            cp.wait()

    return _Comm(
        grads, [jax.ShapeDtypeStruct((N_CHIP,) + g.shape[2:], g.dtype) for g in grads],
        [pltpu.SemaphoreType.DMA((n * N_CHIP,)), pltpu.SemaphoreType.DMA((n * N_CHIP,))],
        [(0.0, first), (None, last)])


def _chip_comm(sums):
    n = len(sums)
    per = N_CHIP - 1

    def copies(ins, outs, sems):
        send_sems, recv_sems, local_sems = sems
        x, y, c = _position()
        my_chip = 2 * x + y
        cps = []
        for a in range(n):
            cps.append(pltpu.make_async_copy(ins[a].at[my_chip], outs[a].at[my_chip], local_sems.at[a]))
            for j, (px, py) in enumerate([(1 - x, y), (x, 1 - y), (1 - x, 1 - y)]):
                cps.append(pltpu.make_async_remote_copy(
                    src_ref=ins[a].at[2 * px + py], dst_ref=outs[a].at[my_chip],
                    send_sem=send_sems.at[a * per + j], recv_sem=recv_sems.at[a * per + j],
                    device_id=(px, py, c), device_id_type=MESH))
        return cps

    def first(ins, outs, sems):
        for cp in copies(ins, outs, sems):
            cp.start()

    def last(ins, outs, sems):
        for cp in copies(ins, outs, sems):
            cp.wait()

    return _Comm(
        sums, [jax.ShapeDtypeStruct(s.shape, s.dtype) for s in sums],
        [pltpu.SemaphoreType.DMA((n * per,)), pltpu.SemaphoreType.DMA((n * per,)), pltpu.SemaphoreType.DMA((n,))],
        [(0.0, first), (None, last)])


def _matmul(name, operands, pairs, outs, grid, acc_shape, n_slots=1, epilogue=None, comms=()):
    used = sorted({i for p in pairs for i in p[:2]})
    n_op = len(operands)
    n_out = len(outs)
    k_axis = len(grid) - 1
    n_k = grid[-1]
    lay = _CommLayout(comms)

    def body(*refs):
        ops = refs[:n_op]
        out_refs = refs[n_op + lay.n_in:n_op + lay.n_in + n_out]
        acc = refs[n_op + lay.n_in + n_out + lay.n_out]
        k = pl.program_id(k_axis)
        step = _linear_step(grid)
        cin = refs[n_op:n_op + lay.n_in]
        cout = refs[n_op + lay.n_in + n_out:n_op + lay.n_in + n_out + lay.n_out]
        csem = refs[n_op + lay.n_in + n_out + lay.n_out + 1:]
        lay.run(cin, cout, csem, step, int(np.prod(grid)), post=False)

        @pl.when(k == 0)
        def _():
            acc[...] = jnp.zeros_like(acc)

        vals = {i: ops[i][...] for i in used}
        vals = {i: (v if v.dtype == BF16 else v.astype(BF16)) for i, v in vals.items()}
        for s in range(n_slots):
            tot = None
            for ia, ib, form, slot in pairs:
                if slot != s:
                    continue
                d = lax.dot_general(vals[ia], vals[ib], _DIMS[form], preferred_element_type=F32)
                tot = d if tot is None else tot + d
            acc[s] += tot

        @pl.when(k == n_k - 1)
        def _():
            rows = acc_shape[0]
            chunk = EPILOGUE_ROWS if (epilogue is not None and rows % EPILOGUE_ROWS == 0) else rows
            for r0 in range(0, rows, chunk):
                sl = slice(r0, r0 + chunk)
                accs = [acc[s, sl, :] for s in range(n_slots)]
                extra = [ops[i][sl, :] for i in range(n_op) if i not in used]
                res = epilogue(accs, *extra) if epilogue is not None else accs
                for o, v in zip(out_refs, res):
                    o[sl, :] = v.astype(o.dtype)

        lay.run(cin, cout, csem, step, int(np.prod(grid)), post=True)

    res = pl.pallas_call(
        body,
        name=name,
        grid=grid,
        in_specs=[s for _, s in operands] + lay.in_specs,
        out_specs=[s for _, s in outs] + lay.out_specs,
        out_shape=[s for s, _ in outs] + lay.out_shapes,
        scratch_shapes=[pltpu.VMEM((n_slots,) + tuple(acc_shape), F32)] + lay.sem_shapes,
        compiler_params=_cparams(len(grid)),
    )(*[a for a, _ in operands], *lay.arrays)
    lay.deliver(res[n_out:])
    return res[:n_out]


def _mm_plain(name, a, b, form, out_dtype, tm, tn, tk, extra=None, epilogue=None, comms=()):
    if form == "nn":
        (M, K), N = a.shape, b.shape[1]
    elif form == "nt":
        (M, K), N = a.shape, b.shape[0]
    else:
        (K, M), N = a.shape, b.shape[1]
    tm, tn, tk = _tile(M, tm), _tile(N, tn), _tile(K, tk)
    a_spec = pl.BlockSpec((tk, tm), lambda m, n, k: (k, m)) if form == "tn" else pl.BlockSpec((tm, tk), lambda m, n, k: (m, k))
    b_spec = pl.BlockSpec((tn, tk), lambda m, n, k: (n, k)) if form == "nt" else pl.BlockSpec((tk, tn), lambda m, n, k: (k, n))
    operands = [(a, a_spec), (b, b_spec)]
    if extra is not None:
        operands.append((extra, pl.BlockSpec((tm, tn), lambda m, n, k: (m, n))))
    out = (jax.ShapeDtypeStruct((M, N), out_dtype), pl.BlockSpec((tm, tn), lambda m, n, k: (m, n)))
    return _matmul(name, operands, [(0, 1, form, 0)], [out], (M // tm, N // tn, K // tk), (tm, tn), epilogue=epilogue,
                   comms=comms)[0]


def _cast_bf16(name, w):
    r, c = w.shape
    tr = _rows(r, 6 * c)

    def body(w_ref, o_ref):
        o_ref[...] = w_ref[...].astype(BF16)

    return pl.pallas_call(
        body, name=name, grid=(r // tr,),
        in_specs=[pl.BlockSpec((tr, c), lambda i: (i, 0))],
        out_specs=pl.BlockSpec((tr, c), lambda i: (i, 0)),
        out_shape=jax.ShapeDtypeStruct((r, c), BF16),
        compiler_params=_cparams(1),
    )(w)


def _rms_stats(x):
    r = lax.rsqrt(jnp.mean(x * x, axis=-1, keepdims=True) + EPS)
    return x * r, r


def _rms_bwd(xhat, r, w, dy):
    dxh = dy * w
    return r * (dxh - xhat * jnp.mean(dxh * xhat, axis=-1, keepdims=True))


def _row_spec(tr, d):
    return pl.BlockSpec((tr, d), lambda i: (i, 0))


def _vec_spec(d):
    return pl.BlockSpec((1, d), lambda i: (0, 0))


def _rms_fwd(name, x, w):
    t, d = x.shape
    tr = _tile(t, ROW_TILE)

    def body(x_ref, w_ref, h_ref):
        xh, _ = _rms_stats(x_ref[...])
        h_ref[...] = (xh * w_ref[...]).astype(BF16)

    return pl.pallas_call(
        body, name=name, grid=(t // tr,),
        in_specs=[_row_spec(tr, d), _vec_spec(d)],
        out_specs=_row_spec(tr, d),
        out_shape=jax.ShapeDtypeStruct((t, d), BF16),
        compiler_params=_cparams(1),
    )(x, w)


def _resid_rms(name, xres, y, w_post, w_next):
    t, d = xres.shape
    tr = _tile(t, ROW_TILE)
    has_next = w_next is not None

    def body(*refs):
        if has_next:
            x_ref, y_ref, wp_ref, wn_ref, xo_ref, h_ref = refs
        else:
            x_ref, y_ref, wp_ref, xo_ref, h_ref = refs
        yh, _ = _rms_stats(y_ref[...])
        xn = x_ref[...] + yh * wp_ref[...]
        xo_ref[...] = xn
        if has_next:
            xh, _ = _rms_stats(xn)
            h_ref[...] = (xh * wn_ref[...]).astype(BF16)
        else:
            h_ref[...] = xn.astype(BF16)

    ins = [xres, y, w_post] + ([w_next] if has_next else [])
    in_specs = [_row_spec(tr, d), _row_spec(tr, d), _vec_spec(d)] + ([_vec_spec(d)] if has_next else [])
    return pl.pallas_call(
        body, name=name, grid=(t // tr,),
        in_specs=in_specs,
        out_specs=[_row_spec(tr, d), _row_spec(tr, d)],
        out_shape=[jax.ShapeDtypeStruct((t, d), F32), jax.ShapeDtypeStruct((t, d), BF16)],
        compiler_params=_cparams(1),
    )(*ins)


def _ple_loss(name, x2, pe, pgl, w_pp, tgt):
    t, d = x2.shape
    tr = _tile(t, ROW_TILE)

    def body(x2_ref, pe_ref, pgl_ref, w_ref, tgt_ref, loss_ref, d3_ref, dpe_ref, dpgl_ref, dw_ref):
        @pl.when(pl.program_id(0) == 0)
        def _():
            loss_ref[...] = jnp.zeros_like(loss_ref)
            dw_ref[...] = jnp.zeros_like(dw_ref)

        pe_v = pe_ref[...]
        s = _sigmoid(pgl_ref[...])
        y = pe_v * s
        yh, r = _rms_stats(y)
        w = w_ref[...]
        err = x2_ref[...] + yh * w - tgt_ref[...]
        loss_ref[...] += 0.5 * jnp.sum(jnp.mean(err * err, axis=-1, keepdims=True), axis=0, keepdims=True)
        d3 = err * (1.0 / d)
        d3_ref[...] = d3
        dw_ref[...] += jnp.sum(d3 * yh, axis=0, keepdims=True)
        dy = _rms_bwd(yh, r, w, d3)
        dpe_ref[...] = (dy * s).astype(BF16)
        dpgl_ref[...] = (dy * pe_v * s * (1.0 - s)).astype(BF16)

    return pl.pallas_call(
        body, name=name, grid=(t // tr,),
        in_specs=[_row_spec(tr, d), _row_spec(tr, d), _row_spec(tr, d), _vec_spec(d), _row_spec(tr, d)],
        out_specs=[pl.BlockSpec((1, 1), lambda i: (0, 0)), _row_spec(tr, d), _row_spec(tr, d), _row_spec(tr, d), _vec_spec(d)],
        out_shape=[jax.ShapeDtypeStruct((1, 1), F32), jax.ShapeDtypeStruct((t, d), F32),
                   jax.ShapeDtypeStruct((t, d), BF16), jax.ShapeDtypeStruct((t, d), BF16),
                   jax.ShapeDtypeStruct((1, d), F32)],
        compiler_params=_cparams(1),
    )(x2, pe, pgl, w_pp, tgt)


def _norm_bwd(name, dres, y, w_post):
    t, d = dres.shape
    tr = _tile(t, ROW_TILE)

    def body(d_ref, y_ref, w_ref, dy_ref, dw_ref):
        @pl.when(pl.program_id(0) == 0)
        def _():
            dw_ref[...] = jnp.zeros_like(dw_ref)

        dv = d_ref[...]
        yh, r = _rms_stats(y_ref[...])
        dw_ref[...] += jnp.sum(dv * yh, axis=0, keepdims=True)
        dy_ref[...] = _rms_bwd(yh, r, w_ref[...], dv).astype(BF16)

    return pl.pallas_call(
        body, name=name, grid=(t // tr,),
        in_specs=[_row_spec(tr, d), _row_spec(tr, d), _vec_spec(d)],
        out_specs=[_row_spec(tr, d), _vec_spec(d)],
        out_shape=[jax.ShapeDtypeStruct((t, d), BF16), jax.ShapeDtypeStruct((1, d), F32)],
        compiler_params=_cparams(1),
    )(dres, y, w_post)


def _prenorm_bwd(name, dres, dh, xin, w_pre, y=None, w_post=None):
    t, d = dres.shape
    tr = _tile(t, ROW_TILE)
    two = y is not None

    def body(*refs):
        if two:
            d_ref, dh_ref, x_ref, wpre_ref, y_ref, wpost_ref, do_ref, dwpre_ref, dy_ref, dwpost_ref = refs
        else:
            d_ref, dh_ref, x_ref, wpre_ref, do_ref, dwpre_ref = refs

        @pl.when(pl.program_id(0) == 0)
        def _():
            dwpre_ref[...] = jnp.zeros_like(dwpre_ref)
            if two:
                dwpost_ref[...] = jnp.zeros_like(dwpost_ref)

        dhv = dh_ref[...]
        xh, r = _rms_stats(x_ref[...])
        dwpre_ref[...] += jnp.sum(dhv * xh, axis=0, keepdims=True)
        dout = d_ref[...] + _rms_bwd(xh, r, wpre_ref[...], dhv)
        do_ref[...] = dout
        if two:
            yh, ry = _rms_stats(y_ref[...])
            dwpost_ref[...] += jnp.sum(dout * yh, axis=0, keepdims=True)
            dy_ref[...] = _rms_bwd(yh, ry, wpost_ref[...], dout).astype(BF16)

    ins = [dres, dh, xin, w_pre] + ([y, w_post] if two else [])
    in_specs = [_row_spec(tr, d)] * 3 + [_vec_spec(d)] + ([_row_spec(tr, d), _vec_spec(d)] if two else [])
    out_specs = [_row_spec(tr, d), _vec_spec(d)] + ([_row_spec(tr, d), _vec_spec(d)] if two else [])
    out_shape = [jax.ShapeDtypeStruct((t, d), F32), jax.ShapeDtypeStruct((1, d), F32)]
    if two:
        out_shape += [jax.ShapeDtypeStruct((t, d), BF16), jax.ShapeDtypeStruct((1, d), F32)]
    return pl.pallas_call(
        body, name=name, grid=(t // tr,),
        in_specs=in_specs, out_specs=out_specs, out_shape=out_shape,
        compiler_params=_cparams(1),
    )(*ins)


_LEVELS = (32, 16, 8, 4, 2, 1)
_N_CUM = 3 + 2 * len(_LEVELS)


def _hgrn_constants():
    c = GLA_CHUNK
    idx = np.arange(c)
    t, r = idx[:, None], idx[None, :]
    mats = [(r <= t), (r > t), np.ones((c, c), bool)]
    lq, lk, masks = [], [], []
    for h in _LEVELS:
        blk, pos = idx // (2 * h), idx % (2 * h)
        mid = blk * 2 * h + h - 1
        upper, lower = pos >= h, pos < h
        lq.append(upper[:, None] & (r > mid[:, None]) & (r <= t))
        lk.append(lower[:, None] & (r > t) & (r <= mid[:, None]))
        masks.append((blk[:, None] == blk[None, :]) & upper[:, None] & lower[None, :])
    cum = np.concatenate(mats + lq + lk, axis=0).astype(np.float32)
    rev = (r >= t).astype(np.float32)
    return (jnp.asarray(cum, BF16), jnp.asarray(rev, BF16), jnp.asarray(np.stack(masks).astype(np.float32)))


def _hgrn_gates(qp, fp, lb):
    sq = _sigmoid(qp)
    q = qp * sq
    sg = _sigmoid(fp)
    f = lb + (1.0 - lb) * sg
    k = 1.0 - f
    logf = jnp.log(jnp.maximum(f, 1e-30))
    return q, sq, sg, f, k, logf


def _hgrn_decays(cum_ref, logf):
    c = GLA_CHUNK
    e = jnp.exp(_dot_exact_l(cum_ref[...], logf))
    part = lambda i: e[i * c:(i + 1) * c]
    n = len(_LEVELS)
    return part(0), part(1), part(2), [part(3 + i) for i in range(n)], [part(3 + n + i) for i in range(n)]


def _hgrn_fwd(proj, lb, nw, n_heads, comms=()):
    t = proj.shape[0]
    aw = n_heads * HEAD
    rb = _tile(t, HGRN_ROWS)
    c = GLA_CHUNK
    n_sub = rb // c
    cum, _, masks = _hgrn_constants()
    lay = _CommLayout(comms)
    hp = HGRN_HEADS if n_heads % HGRN_HEADS == 0 else 1
    wd = hp * HEAD
    grid = (n_heads // hp, t // rb)

    def body(*refs):
        q_ref, f_ref, i_ref, g_ref, lb_ref, nw_ref, cum_ref, m_ref = refs[:8]
        cin = refs[8:8 + lay.n_in]
        a_ref, o_ref, s_ref, sc_ref = refs[8 + lay.n_in:12 + lay.n_in]
        cout = refs[12 + lay.n_in:12 + lay.n_in + lay.n_out]
        st = refs[12 + lay.n_in + lay.n_out]
        csem = refs[13 + lay.n_in + lay.n_out:]
        step = _linear_step(grid)
        lay.run(cin, cout, csem, step, grid[0] * grid[1], post=False)

        @pl.when(pl.program_id(1) == 0)
        def _():
            st[...] = jnp.zeros_like(st)

        lbv = lb_ref[...]
        nwv = nw_ref[...]
        eye = (lax.broadcasted_iota(jnp.int32, (c, c), 0) == lax.broadcasted_iota(jnp.int32, (c, c), 1)).astype(F32)
        heads = range(hp)
        hs = lambda a, h: a[:, h * HEAD:(h + 1) * HEAD]

        def chunk(j, carry):
            rows = pl.ds(pl.multiple_of(j * c, c), c)
            q, _, _, _, k, logf = _hgrn_gates(q_ref[rows, :], f_ref[rows, :], lbv)
            v = i_ref[rows, :]
            eb, ebe, eend, eq, ek = _hgrn_decays(cum_ref, logf)
            qt, kt, qk = q * eb, k * ebe, q * k
            s_in = [st[h] for h in heads]
            for h in heads:
                s_ref[h, j] = s_in[h]
            inter = [_dot(hs(qt, h), s_in[h], "nt") for h in heads]
            for h in heads:
                st[h] = s_in[h] * hs(eend, h)[0:1] + _dot(hs(v, h), hs(kt, h), "tn")
            scores = [eye * jnp.sum(hs(qk, h), axis=-1, keepdims=True) for h in heads]
            for lvl in range(len(_LEVELS)):
                ql, kl = q * eq[lvl], k * ek[lvl]
                for h in heads:
                    scores[h] = scores[h] + m_ref[lvl] * _dot(hs(ql, h), hs(kl, h), "nt")
            gv = g_ref[rows, :]
            gate = nwv * (gv * _sigmoid(gv))
            for h in heads:
                sc_ref[h, rows, :] = scores[h]
                o = inter[h] + _dot(scores[h], hs(v, h))
                o_ref[rows, h * HEAD:(h + 1) * HEAD] = o
                r = lax.rsqrt(jnp.mean(o * o, axis=-1, keepdims=True) + EPS)
                a_ref[rows, h * HEAD:(h + 1) * HEAD] = (o * r * hs(gate, h)).astype(BF16)
            return carry

        lax.fori_loop(0, n_sub, chunk, 0, unroll=HGRN_UNROLL)
        lay.run(cin, cout, csem, step, grid[0] * grid[1], post=True)

    n_hb = n_heads // hp
    col = lambda base: pl.BlockSpec((rb, wd), lambda h, r: (r, base * n_hb + h))
    vec = pl.BlockSpec((1, wd), lambda h, r: (0, h))
    res = pl.pallas_call(
        body, name="hgrn2_fwd", grid=grid,
        in_specs=[col(0), col(1), col(2), col(3), vec, vec,
                  pl.BlockSpec(cum.shape, lambda h, r: (0, 0)), pl.BlockSpec(masks.shape, lambda h, r: (0, 0, 0))] + lay.in_specs,
        out_specs=[pl.BlockSpec((rb, wd), lambda h, r: (r, h)), pl.BlockSpec((rb, wd), lambda h, r: (r, h)),
                   pl.BlockSpec((hp, n_sub, HEAD, HEAD), lambda h, r: (h, r, 0, 0)),
                   pl.BlockSpec((hp, rb, c), lambda h, r: (h, r, 0))] + lay.out_specs,
        out_shape=[jax.ShapeDtypeStruct((t, aw), BF16), jax.ShapeDtypeStruct((t, aw), F32),
                   jax.ShapeDtypeStruct((n_heads, t // c, HEAD, HEAD), F32),
                   jax.ShapeDtypeStruct((n_heads, t, c), F32)] + lay.out_shapes,
        scratch_shapes=[pltpu.VMEM((hp, HEAD, HEAD), F32)] + lay.sem_shapes,
        compiler_params=_cparams(2),
    )(proj, proj, proj, proj, lb, nw, cum, masks, *lay.arrays)
    lay.deliver(res[4:])
    return res[:4]


def _hgrn_bwd(proj, lb, nw, o_raw, states, scores, dab, n_heads, comms=()):
    t = proj.shape[0]
    aw = n_heads * HEAD
    rb = _tile(t, HGRN_ROWS)
    c = GLA_CHUNK
    n_sub = rb // c
    n_rb = t // rb
    cum, rev, masks = _hgrn_constants()
    lay = _CommLayout(comms)
    hp = HGRN_HEADS if n_heads % HGRN_HEADS == 0 else 1
    wd = hp * HEAD
    grid = (n_heads // hp, n_rb)

    def body(*refs):
        q_ref, f_ref, i_ref, g_ref, lb_ref, nw_ref, o_ref, s_ref, sc_ref, da_ref, cum_ref, rev_ref, m_ref = refs[:13]
        cin = refs[13:13 + lay.n_in]
        dq_ref, df_ref, di_ref, dg_ref, dlb_ref, dnw_ref = refs[13 + lay.n_in:19 + lay.n_in]
        cout = refs[19 + lay.n_in:19 + lay.n_in + lay.n_out]
        dst = refs[19 + lay.n_in + lay.n_out]
        csem = refs[20 + lay.n_in + lay.n_out:]
        step = _linear_step(grid)
        lay.run(cin, cout, csem, step, grid[0] * grid[1], post=False)

        @pl.when(pl.program_id(1) == 0)
        def _():
            dst[...] = jnp.zeros_like(dst)
            dlb_ref[...] = jnp.zeros_like(dlb_ref)
            dnw_ref[...] = jnp.zeros_like(dnw_ref)

        lbv = lb_ref[...]
        nwv = nw_ref[...]
        ri = lax.broadcasted_iota(jnp.int32, (c, c), 0)
        ci = lax.broadcasted_iota(jnp.int32, (c, c), 1)
        eye = (ri == ci).astype(F32)
        causal = (ci <= ri).astype(F32)
        last_row = (lax.broadcasted_iota(jnp.int32, (c, wd), 0) == c - 1).astype(F32)
        heads = range(hp)
        hs = lambda a, h: a[:, h * HEAD:(h + 1) * HEAD]
        wide = lambda parts: parts[0] if hp == 1 else jnp.concatenate(parts, axis=1)

        def head_mean(a):
            return wide([jnp.broadcast_to(jnp.mean(hs(a, h), axis=-1, keepdims=True), (c, HEAD)) for h in heads])

        def chunk(jj, carry):
            j = n_sub - 1 - jj
            rows = pl.ds(pl.multiple_of(j * c, c), c)
            qp = q_ref[rows, :]
            q, sq, sg, f, k, logf = _hgrn_gates(qp, f_ref[rows, :], lbv)
            v = i_ref[rows, :]
            gv = g_ref[rows, :]
            eb, ebe, eend, eq, ek = _hgrn_decays(cum_ref, logf)
            s_in = [s_ref[h, j] for h in heads]
            a_sc = [sc_ref[h, rows, :] for h in heads]
            dsn = [dst[h] for h in heads]
            o = o_ref[rows, :]
            r = lax.rsqrt(head_mean(o * o) + EPS)
            oh = o * r
            sgg = _sigmoid(gv)
            sil = gv * sgg
            da = da_ref[rows, :]
            dg_ref[rows, :] = (da * oh * nwv * (sgg * (1.0 + gv * (1.0 - sgg)))).astype(BF16)
            dnw_ref[...] += jnp.sum(da * oh * sil, axis=0, keepdims=True)
            doh = da * nwv * sil
            do = r * (doh - oh * head_mean(doh * oh))
            kt = k * ebe
            qt = q * eb
            d_sc = [_dot(hs(do, h), hs(v, h), "nt") * causal for h in heads]
            dqt = wide([_dot(hs(do, h), s_in[h]) for h in heads])
            dkt = wide([_dot(hs(v, h), dsn[h]) for h in heads])
            for h in heads:
                dst[h] = dsn[h] * hs(eend, h)[0:1] + _dot(hs(do, h), hs(qt, h), "tn")
            di_ref[rows, :] = wide([_dot(a_sc[h], hs(do, h), "tn") + _dot(hs(kt, h), dsn[h], "nt") for h in heads]).astype(BF16)
            diag = wide([jnp.broadcast_to(jnp.sum(d_sc[h] * eye, axis=-1, keepdims=True), (c, HEAD)) for h in heads])
            dq = dqt * eb
            dk = dkt * ebe
            db = q * dq - k * dk
            dq = dq + diag * k
            dk = dk + diag * q
            for lvl in range(len(_LEVELS)):
                ql = (q * eq[lvl]).astype(BF16)
                kl = (k * ek[lvl]).astype(BF16)
                dm = [(m_ref[lvl] * d_sc[h]).astype(BF16) for h in heads]
                gq = wide([_dot(dm[h], hs(kl, h)) for h in heads])
                gk = wide([_dot(dm[h], hs(ql, h), "tn") for h in heads])
                dq = dq + gq * eq[lvl]
                dk = dk + gk * ek[lvl]
                db = db + ql.astype(F32) * gq - kl.astype(F32) * gk
            state_term = wide([jnp.sum(s_in[h] * dsn[h], axis=0, keepdims=True) for h in heads])
            extra = jnp.sum(dkt * kt, axis=0, keepdims=True) + eend[0:1] * state_term
            db = db + last_row * extra
            dlogf = _dot_exact_l(rev_ref[...], db)
            dfv = jnp.where(f > 1e-30, dlogf / f, 0.0) - dk
            df_ref[rows, :] = (dfv * (1.0 - lbv) * sg * (1.0 - sg)).astype(BF16)
            dlb_ref[...] += jnp.sum(dfv * (1.0 - sg), axis=0, keepdims=True)
            dq_ref[rows, :] = (dq * (sq * (1.0 + qp * (1.0 - sq)))).astype(BF16)
            return carry

        lax.fori_loop(0, n_sub, chunk, 0, unroll=HGRN_UNROLL)
        lay.run(cin, cout, csem, step, grid[0] * grid[1], post=True)

    n_hb = n_heads // hp
    col = lambda base: pl.BlockSpec((rb, wd), lambda h, r: (n_rb - 1 - r, base * n_hb + h))
    blk = pl.BlockSpec((rb, wd), lambda h, r: (n_rb - 1 - r, h))
    vec = pl.BlockSpec((1, wd), lambda h, r: (0, h))
    const = lambda a: pl.BlockSpec(a.shape, lambda h, r: (0,) * a.ndim)
    res = pl.pallas_call(
        body, name="hgrn2_bwd", grid=grid,
        in_specs=[col(0), col(1), col(2), col(3), vec, vec, blk,
                  pl.BlockSpec((hp, n_sub, HEAD, HEAD), lambda h, r: (h, n_rb - 1 - r, 0, 0)),
                  pl.BlockSpec((hp, rb, c), lambda h, r: (h, n_rb - 1 - r, 0)),
                  blk, const(cum), const(rev), const(masks)] + lay.in_specs,
        out_specs=[blk, blk, blk, blk, vec, vec] + lay.out_specs,
        out_shape=[jax.ShapeDtypeStruct((t, aw), BF16)] * 4 + [jax.ShapeDtypeStruct((1, aw), F32)] * 2 + lay.out_shapes,
        scratch_shapes=[pltpu.VMEM((hp, HEAD, HEAD), F32)] + lay.sem_shapes,
        compiler_params=_cparams(2),
    )(proj, proj, proj, proj, lb, nw, o_raw, states, scores, dab, cum, rev, masks, *lay.arrays)
    lay.deliver(res[6:])
    return res[:6]


def _lb_fwd(lb_param):
    def body(p_ref, o_ref):
        p = p_ref[...]
        e = jnp.exp(p - jnp.max(p, axis=0, keepdims=True))
        o_ref[...] = e[0:1] / jnp.sum(e, axis=0, keepdims=True)

    return pl.pallas_call(body, name="lb_fwd", out_shape=jax.ShapeDtypeStruct((1, lb_param.shape[1]), F32))(lb_param)


def _lb_bwd(lb_param, dlb):
    def body(p_ref, d_ref, o_ref):
        p = p_ref[...]
        e = jnp.exp(p - jnp.max(p, axis=0, keepdims=True))
        s = e / jnp.sum(e, axis=0, keepdims=True)
        first = (lax.broadcasted_iota(jnp.int32, p.shape, 0) == 0).astype(F32)
        o_ref[...] = d_ref[...] * s[0:1] * (first - s)

    return pl.pallas_call(body, name="lb_bwd", out_shape=jax.ShapeDtypeStruct(lb_param.shape, F32))(lb_param, dlb)


def _gmlp_norm(v, lnw, lnb):
    vf = _gelu(v)
    mu = jnp.mean(vf, axis=-1, keepdims=True)
    cen = vf - mu
    rstd = lax.rsqrt(jnp.mean(cen * cen, axis=-1, keepdims=True) + EPS)
    xh = cen * rstd
    return xh, rstd, xh * lnw + lnb


def _tril(n):
    return (lax.broadcasted_iota(jnp.int32, (n, n), 1) <= lax.broadcasted_iota(jnp.int32, (n, n), 0)).astype(F32)


def _gmlp_fwd(proj, lnw, lnb, w_sp, bs_t, n_groups, col_base):
    t = proj.shape[0]
    bw = n_groups * HEAD
    c = GMLP_CHUNK

    def body(u_ref, v_ref, lnw_ref, lnb_ref, w_ref, bs_ref, o_ref):
        tri = _tril(c)
        uf = _gelu(u_ref[...])
        _, _, vn = _gmlp_norm(v_ref[...], lnw_ref[...], lnb_ref[...])
        for g in range(n_groups):
            cols = slice(g * HEAD, (g + 1) * HEAD)
            z = _dot(w_ref[g] * tri, vn[:, cols]) + bs_ref[:, g:g + 1]
            o_ref[:, cols] = (uf[:, cols] * z).astype(BF16)

    blk = lambda b: pl.BlockSpec((c, bw), lambda n: (n, b))
    const = lambda a: pl.BlockSpec(a.shape, lambda n: (0,) * a.ndim)
    return pl.pallas_call(
        body, name="gmlp_fwd", grid=(t // c,),
        in_specs=[blk(col_base), blk(col_base + 1), const(lnw), const(lnb), const(w_sp), const(bs_t)],
        out_specs=pl.BlockSpec((c, bw), lambda n: (n, 0)),
        out_shape=jax.ShapeDtypeStruct((t, bw), BF16),
        compiler_params=_cparams(1),
    )(proj, proj, lnw, lnb, w_sp, bs_t)


def _gmlp_bwd(proj, lnw, lnb, w_sp, bs_t, dab, n_groups, col_base):
    t = proj.shape[0]
    bw = n_groups * HEAD
    c = GMLP_CHUNK
    n_steps = t // c
    sel = jnp.asarray((np.arange(bw)[:, None] // HEAD == np.arange(n_groups)[None, :]).astype(np.float32), BF16)

    def body(u_ref, v_ref, lnw_ref, lnb_ref, w_ref, bs_ref, d_ref, sel_ref,
             du_ref, dv_ref, dlnw_ref, dlnb_ref, dw_ref, dbs_ref, dz_acc, dvn_scr):
        step = pl.program_id(0)

        @pl.when(step == 0)
        def _():
            dlnw_ref[...] = jnp.zeros_like(dlnw_ref)
            dlnb_ref[...] = jnp.zeros_like(dlnb_ref)
            dw_ref[...] = jnp.zeros_like(dw_ref)
            dz_acc[...] = jnp.zeros_like(dz_acc)

        tri = _tril(c)
        u = u_ref[...]
        v = v_ref[...]
        uf = _gelu(u)
        lnw_v = lnw_ref[...]
        xh, rstd, vn = _gmlp_norm(v, lnw_v, lnb_ref[...])
        dbo = d_ref[...]
        dz = dbo * uf
        dz_acc[...] += dz
        for g in range(n_groups):
            cols = slice(g * HEAD, (g + 1) * HEAD)
            wg = w_ref[g] * tri
            z = _dot(wg, vn[:, cols]) + bs_ref[:, g:g + 1]
            du_ref[:, cols] = (dbo[:, cols] * z * _gelu_grad(u[:, cols])).astype(BF16)
            dvn_scr[:, cols] = _dot(wg, dz[:, cols], "tn")
            dw_ref[g] += tri * _dot(dz[:, cols], vn[:, cols], "nt")
        dvn = dvn_scr[...]
        dlnw_ref[...] += jnp.sum(dvn * xh, axis=0, keepdims=True)
        dlnb_ref[...] += jnp.sum(dvn, axis=0, keepdims=True)
        dxh = dvn * lnw_v
        dvf = rstd * (dxh - jnp.mean(dxh, axis=-1, keepdims=True) - xh * jnp.mean(dxh * xh, axis=-1, keepdims=True))
        dv_ref[...] = (dvf * _gelu_grad(v)).astype(BF16)

        @pl.when(step == n_steps - 1)
        def _():
            dbs_ref[...] = _dot_exact_r(dz_acc[...], sel_ref[...])

    blk = lambda b: pl.BlockSpec((c, bw), lambda n: (n, b))
    const = lambda a: pl.BlockSpec(a.shape, lambda n: (0,) * a.ndim)
    row = pl.BlockSpec((c, bw), lambda n: (n, 0))
    vec = pl.BlockSpec((1, bw), lambda n: (0, 0))
    return pl.pallas_call(
        body, name="gmlp_bwd", grid=(n_steps,),
        in_specs=[blk(col_base), blk(col_base + 1), const(lnw), const(lnb), const(w_sp), const(bs_t), blk(1), const(sel)],
        out_specs=[row, row, vec, vec, const(w_sp), const(bs_t)],
        out_shape=[jax.ShapeDtypeStruct((t, bw), BF16), jax.ShapeDtypeStruct((t, bw), BF16),
                   jax.ShapeDtypeStruct((1, bw), F32), jax.ShapeDtypeStruct((1, bw), F32),
                   jax.ShapeDtypeStruct(w_sp.shape, F32), jax.ShapeDtypeStruct(bs_t.shape, F32)],
        scratch_shapes=[pltpu.VMEM((c, bw), F32), pltpu.VMEM((c, bw), F32)],
        compiler_params=_cparams(1),
    )(proj, proj, lnw, lnb, w_sp, bs_t, dab, sel)


def _pair_sum(name, grad, other, core):
    _, _, r, c = grad.shape
    tr = _rows(r, 6 * c)

    def body(core_ref, g_ref, o_ref, out_ref):
        out_ref[...] = (g_ref[...].astype(F32) + o_ref[...].astype(F32)).astype(BF16)

    return pl.pallas_call(
        body, name=name,
        grid_spec=pltpu.PrefetchScalarGridSpec(
            num_scalar_prefetch=1, grid=(N_CHIP, r // tr),
            in_specs=[pl.BlockSpec((None, None, tr, c), lambda k, i, core_ref: (k, core_ref[0], i, 0)),
                      pl.BlockSpec((None, tr, c), lambda k, i, core_ref: (k, i, 0))],
            out_specs=pl.BlockSpec((None, tr, c), lambda k, i, core_ref: (k, i, 0))),
        out_shape=jax.ShapeDtypeStruct((N_CHIP, r, c), BF16),
        compiler_params=_cparams(2),
    )(core, grad, other)


def _adamw_math(w, g, m, v):
    m = ADAM_B1 * m + (1.0 - ADAM_B1) * g
    v = ADAM_B2 * v + (1.0 - ADAM_B2) * (g * g)
    m_hat = m / (1.0 - ADAM_B1 ** ADAM_STEP)
    v_hat = v / (1.0 - ADAM_B2 ** ADAM_STEP)
    delta = -ADAM_LR * (m_hat / (jnp.sqrt(v_hat) + ADAM_EPS) + ADAM_WD * w)
    return delta, m, v


def _adamw(name, parts, w, m, v):
    n_parts, r, c = parts.shape
    tr = _rows(r, c * (n_parts * parts.dtype.itemsize + 28), mult=8)

    def body(p_ref, w_ref, m_ref, v_ref, g_ref, d_ref, mo_ref, vo_ref):
        g = p_ref[0].astype(F32)
        for i in range(1, n_parts):
            g = g + p_ref[i].astype(F32)
        g_ref[...] = g
        d_ref[...], mo_ref[...], vo_ref[...] = _adamw_math(w_ref[...], g, m_ref[...], v_ref[...])

    row = pl.BlockSpec((tr, c), lambda i: (i, 0))
    return pl.pallas_call(
        body, name=name, grid=(r // tr,),
        in_specs=[pl.BlockSpec((n_parts, tr, c), lambda i: (0, i, 0)), row, row, row],
        out_specs=[row] * 4,
        out_shape=[jax.ShapeDtypeStruct((r, c), F32)] * 4,
        compiler_params=_cparams(1),
    )(parts, w, m, v)


def kernel(x, p, pre_mix_w, w_in, lb_param, a_norm_w, gmlp_ln_w, gmlp_ln_b, w_spatial, b_spatial, w_out, post_mix_w, pre_ffn_w, w_gate, w_up, w_down, post_ffn_w, w_ple, w_ple_gate, post_ple_w, loss_target, m_pre_mix_w, m_w_in, m_lb_param, m_a_norm_w, m_gmlp_ln_w, m_gmlp_ln_b, m_w_spatial, m_b_spatial, m_w_out, m_post_mix_w, m_pre_ffn_w, m_w_gate, m_w_up, m_w_down, m_post_ffn_w, m_w_ple, m_w_ple_gate, m_post_ple_w, v_pre_mix_w, v_w_in, v_lb_param, v_a_norm_w, v_gmlp_ln_w, v_gmlp_ln_b, v_w_spatial, v_b_spatial, v_w_out, v_post_mix_w, v_pre_ffn_w, v_w_gate, v_w_up, v_w_down, v_post_ffn_w, v_w_ple, v_w_ple_gate, v_post_ple_w):
    big_names = ["w_in", "w_out", "w_gate", "w_up", "w_down", "w_ple", "w_ple_gate"]
    small_names = ["pre_mix_w", "lb_param", "a_norm_w", "gmlp_ln_w", "gmlp_ln_b", "w_spatial", "b_spatial",
                   "post_mix_w", "pre_ffn_w", "post_ffn_w", "post_ple_w"]
    all_names = ["pre_mix_w", "w_in", "lb_param", "a_norm_w", "gmlp_ln_w", "gmlp_ln_b", "w_spatial", "b_spatial",
                 "w_out", "post_mix_w", "pre_ffn_w", "w_gate", "w_up", "w_down", "post_ffn_w", "w_ple", "w_ple_gate",
                 "post_ple_w"]
    env = dict(locals())
    W = {n: env[n] for n in all_names}
    M = {n: env["m_" + n] for n in all_names}
    V = {n: env["v_" + n] for n in all_names}

    xs = x[0]
    ps = p[0, 0]
    tgt = loss_target[0]
    t, d = xs.shape
    aw = a_norm_w.shape[1]
    bw = gmlp_ln_w.shape[1]
    n_heads, n_groups = aw // HEAD, bw // HEAD
    core = lax.axis_index("c").astype(jnp.int32).reshape(1)

    shard = {n: W[n][0] for n in big_names}
    bf = {n: _cast_bf16("cast_" + n, shard[n]) for n in big_names}
    ag_in = _ag_comm([bf["w_in"]])
    _comm_only("ag_w_in", [ag_in])
    win_g = ag_in.results[0]
    n_in = win_g.shape[2]
    ffl = bf["w_gate"].shape[1]
    n_ple = bf["w_ple"].shape[1]
    ple = ps.shape[1]

    TM, TK = 1024, 1024
    tm = _tile(t, TM)
    tkd = _tile(d, TK)

    h1 = _rms_fwd("rms_pre_mix", xs, pre_mix_w)
    ag_a = _ag_comm([bf["w_gate"]], mid_frac=0.85)
    proj = _matmul(
        "mm_proj",
        [(h1, pl.BlockSpec((tm, tkd), lambda m, n, k: (m, k))), (win_g, pl.BlockSpec((None, tkd, n_in), lambda m, n, k: (n, k, 0)))],
        [(0, 1, "nn", 0)],
        [(jax.ShapeDtypeStruct((t, N_DEV * n_in), F32), pl.BlockSpec((tm, n_in), lambda m, n, k: (m, n)))],
        (t // tm, N_DEV, d // tkd), (tm, n_in), comms=[ag_a])[0]
    wgate_g = ag_a.results[0]
    lb = _lb_fwd(lb_param)
    ag_b = _ag_comm([bf["w_out"]], mid_frac=0.6)
    a_out, o_raw, states, scores = _hgrn_fwd(proj, lb, a_norm_w, n_heads, comms=[ag_b])
    wout_f = ag_b.results[0].reshape(d, d)
    bs_t = b_spatial[0].T
    w_sp = w_spatial[0]
    col_u = (4 * aw) // bw
    b_out = _gmlp_fwd(proj, gmlp_ln_w, gmlp_ln_b, w_sp, bs_t, n_groups, col_u)
    ab = jnp.concatenate([a_out, b_out], axis=1)
    mix = _mm_plain("mm_mix", ab, wout_f, "nn", F32, TM, 2048, TK)
    x1, h2 = _resid_rms("resid_mix", xs, mix, post_mix_w, pre_ffn_w)

    def swiglu(accs, gate_v):
        gf = gate_v.astype(F32)
        return accs[0], gf * _sigmoid(gf) * accs[0]

    tmf = tm
    blk3 = lambda: pl.BlockSpec((None, tmf, ffl), lambda j, m, k: (j, m, 0))
    ag_c = _ag_comm([bf["w_up"]], mid_frac=0.85)
    gate = _matmul(
        "mm_ffn_gate",
        [(h2, pl.BlockSpec((tmf, tkd), lambda j, m, k: (m, k))),
         (wgate_g, pl.BlockSpec((None, tkd, ffl), lambda j, m, k: (j, k, 0)))],
        [(0, 1, "nn", 0)],
        [(jax.ShapeDtypeStruct((N_DEV, t, ffl), BF16), blk3())],
        (N_DEV, t // tmf, d // tkd), (tmf, ffl), comms=[ag_c])[0]
    wup_g = ag_c.results[0]
    ag_d = _ag_comm([bf["w_down"]], mid_frac=0.85)
    up, act = _matmul(
        "mm_ffn_up",
        [(h2, pl.BlockSpec((tmf, tkd), lambda j, m, k: (m, k))),
         (wup_g, pl.BlockSpec((None, tkd, ffl), lambda j, m, k: (j, k, 0))),
         (gate, blk3())],
        [(0, 1, "nn", 0)],
        [(jax.ShapeDtypeStruct((N_DEV, t, ffl), BF16), blk3()) for _ in range(2)],
        (N_DEV, t // tmf, d // tkd), (tmf, ffl), epilogue=swiglu, comms=[ag_d])
    wdown_g = ag_d.results[0]
    ag_e = _ag_comm([bf["w_ple_gate"], bf["w_ple"]], mid_frac=0.6)
    tn_d = _tile(d, 2048)
    ff = _matmul(
        "mm_ffn_down",
        [(act, pl.BlockSpec((None, tm, ffl), lambda m, n, k: (k, m, 0))),
         (wdown_g, pl.BlockSpec((None, ffl, tn_d), lambda m, n, k: (k, 0, n)))],
        [(0, 1, "nn", 0)],
        [(jax.ShapeDtypeStruct((t, d), F32), pl.BlockSpec((tm, tn_d), lambda m, n, k: (m, n)))],
        (t // tm, d // tn_d, N_DEV), (tm, tn_d), comms=[ag_e])[0]
    wpg_f = ag_e.results[0].reshape(d, d)
    wple_g = ag_e.results[1]
    x2, x2b = _resid_rms("resid_ffn", x1, ff, post_ffn_w, None)

    pgl = _mm_plain("mm_ple_gate", x2b, wpg_f, "nn", F32, TM, 2048, TK)
    pe = _matmul(
        "mm_ple",
        [(ps, pl.BlockSpec((tm, ple), lambda m, n, k: (m, 0))), (wple_g, pl.BlockSpec((None, ple, n_ple), lambda m, n, k: (n, 0, 0)))],
        [(0, 1, "nn", 0)],
        [(jax.ShapeDtypeStruct((t, N_DEV * n_ple), F32), pl.BlockSpec((tm, n_ple), lambda m, n, k: (m, n)))],
        (t // tm, N_DEV, 1), (tm, n_ple))[0]
    loss_part, d3, dpe, dpgl, g_post_ple = _ple_loss("ple_loss", x2, pe, pgl, post_ple_w, tgt)

    tkt = _tile(t, TK)
    g_wple = _matmul(
        "mm_dw_ple",
        [(ps, pl.BlockSpec((tkt, ple), lambda n, k: (k, 0))), (dpe, pl.BlockSpec((tkt, n_ple), lambda n, k: (k, n)))],
        [(0, 1, "tn", 0)],
        [(jax.ShapeDtypeStruct((N_DEV, ple, n_ple), BF16), pl.BlockSpec((None, ple, n_ple), lambda n, k: (n, 0, 0)))],
        (N_DEV, t // tkt), (ple, n_ple))[0]
    g_wpg = _mm_plain("mm_dw_ple_gate", x2b, dpgl, "tn", BF16, TM, 2048, TK)

    def by_chip(g):
        return g.reshape((N_CHIP, 2) + g.shape[-2:])

    def pair_sums(names, comm):
        return [_pair_sum("pair_sum_" + n, g, o, core) for n, g, o in zip(names, comm.arrays, comm.results)]

    r1_p = _pair_comm([by_chip(g_wpg.reshape(N_DEV, d // N_DEV, d)), by_chip(g_wple)])
    d2 = _mm_plain("mm_d_x2", dpgl, wpg_f, "nt", F32, TM, 1024, TK, extra=d3, epilogue=lambda accs, e: [accs[0] + e],
                   comms=[r1_p])
    r2_p = _chip_comm(pair_sums(["w_ple_gate", "w_ple"], r1_p))

    dff, g_post_ffn = _norm_bwd("norm_bwd_ffn", d2, ff, post_ffn_w)
    g_wdown = _matmul(
        "mm_dw_down",
        [(act, pl.BlockSpec((None, tkt, ffl), lambda j, n, k: (j, k, 0))), (dff, pl.BlockSpec((tkt, tn_d), lambda j, n, k: (k, n)))],
        [(0, 1, "tn", 0)],
        [(jax.ShapeDtypeStruct((N_DEV, ffl, d), BF16), pl.BlockSpec((None, ffl, tn_d), lambda j, n, k: (j, 0, n)))],
        (N_DEV, d // tn_d, t // tkt), (ffl, tn_d), comms=[r2_p])[0]
    r1_d = _pair_comm([by_chip(g_wdown)])

    def swiglu_bwd(accs, gate_v, up_v):
        dact = accs[0]
        gf = gate_v.astype(F32)
        sg = _sigmoid(gf)
        return dact * up_v.astype(F32) * (sg * (1.0 + gf * (1.0 - sg))), dact * (gf * sg)

    dgate, dup = _matmul(
        "mm_d_act",
        [(dff, pl.BlockSpec((tmf, tkd), lambda j, m, k: (m, k))),
         (wdown_g, pl.BlockSpec((None, ffl, tkd), lambda j, m, k: (j, 0, k))),
         (gate, blk3()), (up, blk3())],
        [(0, 1, "nt", 0)],
        [(jax.ShapeDtypeStruct((N_DEV, t, ffl), BF16), blk3()) for _ in range(2)],
        (N_DEV, t // tmf, d // tkd), (tmf, ffl), epilogue=swiglu_bwd, comms=[r1_d])
    r2_d = _chip_comm(pair_sums(["w_down"], r1_d))
    tmd = _tile(d, TM)
    def dw_ffn(name, dy, comms):
        return _matmul(
            name,
            [(h2, pl.BlockSpec((tkt, tmd), lambda j, m, k: (k, m))),
             (dy, pl.BlockSpec((None, tkt, ffl), lambda j, m, k: (j, k, 0)))],
            [(0, 1, "tn", 0)],
            [(jax.ShapeDtypeStruct((N_DEV, d, ffl), BF16), pl.BlockSpec((None, tmd, ffl), lambda j, m, k: (j, m, 0)))],
            (N_DEV, d // tmd, t // tkt), (tmd, ffl), comms=comms)[0]

    g_wgate = dw_ffn("mm_dw_gate", dgate, [r2_d])
    r1_g = _pair_comm([by_chip(g_wgate)])
    g_wup = dw_ffn("mm_dw_up", dup, [r1_g])
    r2_g = _chip_comm(pair_sums(["w_gate"], r1_g))
    r1_u = _pair_comm([by_chip(g_wup)])
    tn1 = _tile(d, 1024)
    dh2 = _matmul(
        "mm_d_h2",
        [(dgate, pl.BlockSpec((None, tm, ffl), lambda m, n, k: (k, m, 0))),
         (wgate_g, pl.BlockSpec((None, tn1, ffl), lambda m, n, k: (k, n, 0))),
         (dup, pl.BlockSpec((None, tm, ffl), lambda m, n, k: (k, m, 0))),
         (wup_g, pl.BlockSpec((None, tn1, ffl), lambda m, n, k: (k, n, 0)))],
        [(0, 1, "nt", 0), (2, 3, "nt", 0)],
        [(jax.ShapeDtypeStruct((t, d), F32), pl.BlockSpec((tm, tn1), lambda m, n, k: (m, n)))],
        (t // tm, d // tn1, N_DEV), (tm, tn1), comms=[r2_g, r1_u])[0]
    r2_u = _chip_comm(pair_sums(["w_up"], r1_u))
    d1, g_pre_ffn, dmix, g_post_mix = _prenorm_bwd("prenorm_bwd_ffn", d2, dh2, x1, pre_ffn_w, mix, post_mix_w)

    g_wout = _mm_plain("mm_dw_out", ab, dmix, "tn", BF16, TM, 2048, TK)
    r1_o = _pair_comm([by_chip(g_wout.reshape(N_DEV, d // N_DEV, d))])
    dab = _mm_plain("mm_d_ab", dmix, wout_f, "nt", F32, TM, 2048, TK, comms=[r1_o])
    r2_o = _chip_comm(pair_sums(["w_out"], r1_o))
    dq, df, di, dg, dlb, g_a_norm = _hgrn_bwd(proj, lb, a_norm_w, o_raw, states, scores, dab, n_heads, comms=[r2_u])
    du, dv, g_ln_w, g_ln_b, g_wsp, g_bs_t = _gmlp_bwd(proj, gmlp_ln_w, gmlp_ln_b, w_sp, bs_t, dab, n_groups, col_u)
    dproj = jnp.concatenate([dq, df, di, dg, du, dv], axis=1)
    g_win = _matmul(
        "mm_dw_in",
        [(h1, pl.BlockSpec((tkt, tmd), lambda j, m, k: (k, m))), (dproj, pl.BlockSpec((tkt, n_in), lambda j, m, k: (k, j)))],
        [(0, 1, "tn", 0)],
        [(jax.ShapeDtypeStruct((N_DEV, d, n_in), BF16), pl.BlockSpec((None, tmd, n_in), lambda j, m, k: (j, m, 0)))],
        (N_DEV, d // tmd, t // tkt), (tmd, n_in), comms=[r2_o])[0]
    r1_in = _pair_comm([by_chip(g_win)])
    _comm_only("rs_pair_w_in", [r1_in])
    r2_in = _chip_comm(pair_sums(["w_in"], r1_in))
    dh1 = _matmul(
        "mm_d_h1",
        [(dproj, pl.BlockSpec((tm, n_in), lambda m, n, k: (m, k))), (win_g, pl.BlockSpec((None, tn1, n_in), lambda m, n, k: (k, n, 0)))],
        [(0, 1, "nt", 0)],
        [(jax.ShapeDtypeStruct((t, d), F32), pl.BlockSpec((tm, tn1), lambda m, n, k: (m, n)))],
        (t // tm, d // tn1, N_DEV), (tm, tn1), comms=[r2_in])[0]
    grad_x, g_pre_mix = _prenorm_bwd("prenorm_bwd_mix", d1, dh1, xs, pre_mix_w)

    reduced = {
        "w_in": r2_in.results[0], "w_out": r2_o.results[0], "w_gate": r2_g.results[0], "w_up": r2_u.results[0],
        "w_down": r2_d.results[0], "w_ple": r2_p.results[1], "w_ple_gate": r2_p.results[0],
    }
    grads, deltas, new_m, new_v = {}, {}, {}, {}
    for n in big_names:
        g, dl, mo, vo = _adamw("adamw_" + n, reduced[n], shard[n], M[n][0], V[n][0])
        grads[n], deltas[n], new_m[n], new_v[n] = g[None], dl[None], mo[None], vo[None]

    small_grad = {
        "pre_mix_w": g_pre_mix, "lb_param": _lb_bwd(lb_param, dlb), "a_norm_w": g_a_norm, "gmlp_ln_w": g_ln_w,
        "gmlp_ln_b": g_ln_b, "w_spatial": g_wsp, "b_spatial": g_bs_t.T, "post_mix_w": g_post_mix,
        "pre_ffn_w": g_pre_ffn, "post_ffn_w": g_post_ffn, "post_ple_w": g_post_ple,
    }
    pack = lambda get: jnp.concatenate([get(n).reshape(-1, LANE) for n in small_names], axis=0)
    ag_small = _ag_comm([pack(lambda n: small_grad[n])])
    _comm_only("ag_small", [ag_small], in_vmem=True)
    g_all = ag_small.results[0]
    sg, sd, sm, sv = _adamw("adamw_small", g_all, pack(lambda n: W[n]), pack(lambda n: M[n]), pack(lambda n: V[n]))
    off = 0
    for n in small_names:
        rows = W[n].size // LANE
        for src, dst in ((sg, grads), (sd, deltas), (sm, new_m), (sv, new_v)):
            dst[n] = src[off:off + rows].reshape(W[n].shape)
        off += rows

    loss = lax.psum(loss_part[0, 0], ("x", "y", "c"))
    return (loss, grad_x[None], *[grads[n] for n in all_names], *[deltas[n] for n in all_names],
            *[new_m[n] for n in all_names], *[new_v[n] for n in all_names])
```

```python
import functools

import numpy as np
import jax
import jax.numpy as jnp
from jax import lax
from jax.experimental import pallas as pl
from jax.experimental.pallas import tpu as pltpu

F32 = jnp.float32
BF16 = jnp.bfloat16

EPS = 1e-6
HEAD = 128
GLA_CHUNK = 64
GMLP_CHUNK = 128
N_DEV = 8
N_CHIP = 4
LANE = 128
VMEM_LIMIT = 56 * 1024 * 1024
HGRN_ROWS = 512
ROW_TILE = 128
EPILOGUE_ROWS = 256
HGRN_UNROLL = 1
HGRN_HEADS = 8

ADAM_LR = 0.001
ADAM_B1 = 0.9
ADAM_B2 = 0.999
ADAM_EPS = 1e-08
ADAM_WD = 0.01
ADAM_STEP = 10

MESH = pl.DeviceIdType.MESH
ANY = pl.BlockSpec(memory_space=pl.ANY)

_DIMS = {
    "nn": (((1,), (0,)), ((), ())),
    "nt": (((1,), (1,)), ((), ())),
    "tn": (((0,), (0,)), ((), ())),
}


def _tile(dim, pref):
    return pref if dim % pref == 0 else dim


def _rows(r, bytes_per_row, budget=18 * 1024 * 1024, mult=16):
    best = None
    for cand in range(mult, r + 1, mult):
        if r % cand == 0 and cand * bytes_per_row <= budget:
            best = cand
    return best if best is not None else r


def _cparams(n_axes):
    return pltpu.CompilerParams(dimension_semantics=("arbitrary",) * n_axes, vmem_limit_bytes=VMEM_LIMIT)


def _dot(a, b, form="nn"):
    return lax.dot_general(a.astype(BF16), b.astype(BF16), _DIMS[form], preferred_element_type=F32)


def _split3(x):
    hi = x.astype(BF16)
    r = x - hi.astype(F32)
    mid = r.astype(BF16)
    lo = (r - mid.astype(F32)).astype(BF16)
    return hi, mid, lo


def _dot_exact_l(c, x):
    hi, mid, lo = _split3(x)
    d = lambda y: lax.dot_general(c, y, _DIMS["nn"], preferred_element_type=F32)
    return d(hi) + d(mid) + d(lo)


def _dot_exact_r(x, c):
    hi, mid, lo = _split3(x)
    d = lambda y: lax.dot_general(y, c, _DIMS["nn"], preferred_element_type=F32)
    return d(hi) + d(mid) + d(lo)


def _sigmoid(x):
    return 1.0 / (1.0 + jnp.exp(-x))


def _gelu(x):
    return 0.5 * x * (1.0 + lax.erf(x * 0.7071067811865476))


def _gelu_grad(x):
    cdf = 0.5 * (1.0 + lax.erf(x * 0.7071067811865476))
    pdf = jnp.exp(-0.5 * x * x) * 0.3989422804014327
    return cdf + x * pdf


def _position():
    return lax.axis_index("x"), lax.axis_index("y"), lax.axis_index("c")


def _linear_step(grid):
    step = 0
    for ax, n in enumerate(grid):
        step = step * n + pl.program_id(ax)
    return step


class _Comm:
    def __init__(self, arrays, out_shapes, sem_shapes, phases):
        self.arrays, self.out_shapes, self.sem_shapes, self.phases = list(arrays), list(out_shapes), list(sem_shapes), phases
        self.results = None


class _CommLayout:
    def __init__(self, comms, space=pl.ANY):
        self.comms = list(comms)
        self.arrays = [a for c in self.comms for a in c.arrays]
        self.out_shapes = [s for c in self.comms for s in c.out_shapes]
        self.sem_shapes = [s for c in self.comms for s in c.sem_shapes]
        self.n_in, self.n_out = len(self.arrays), len(self.out_shapes)
        self.in_specs = [pl.BlockSpec(memory_space=space)] * self.n_in
        self.out_specs = [pl.BlockSpec(memory_space=space)] * self.n_out

    def run(self, cin, cout, csem, step, n_steps, post):
        i = o = s = 0
        for c in self.comms:
            ins, outs, sems = cin[i:i + len(c.arrays)], cout[o:o + len(c.out_shapes)], csem[s:s + len(c.sem_shapes)]
            i, o, s = i + len(c.arrays), o + len(c.out_shapes), s + len(c.sem_shapes)
            for frac, fn in c.phases:
                if (frac is None) != post:
                    continue
                due = n_steps - 1 if frac is None else max(0, min(int(frac * n_steps), n_steps - 2))
                if n_steps == 1:
                    fn(ins, outs, sems)
                else:
                    pl.when(step == due)(functools.partial(fn, ins, outs, sems))

    def deliver(self, results):
        o = 0
        for c in self.comms:
            c.results = list(results[o:o + len(c.out_shapes)])
            o += len(c.out_shapes)


def _comm_only(name, comms, in_vmem=False):
    lay = _CommLayout(comms, pltpu.VMEM if in_vmem else pl.ANY)

    def body(*refs):
        cin, cout, csem = refs[:lay.n_in], refs[lay.n_in:lay.n_in + lay.n_out], refs[lay.n_in + lay.n_out:]
        lay.run(cin, cout, csem, 0, 1, post=False)
        lay.run(cin, cout, csem, 0, 1, post=True)

    res = pl.pallas_call(
        body, name=name, in_specs=lay.in_specs, out_specs=lay.out_specs, out_shape=lay.out_shapes,
        scratch_shapes=lay.sem_shapes,
    )(*lay.arrays)
    lay.deliver(res)


def _ag_comm(shards, mid_frac=0.0):
    n = len(shards)
    per = N_DEV - 1

    def tools(ins, outs, sems):
        send_sems, recv_sems, local_sems = sems
        x, y, c = _position()
        me, sibling = (x, y, c), (x, y, 1 - c)
        chips = [(1 - x, y), (x, 1 - y), (1 - x, 1 - y)]

        def copy(a, k, block, to, from_shard=False):
            dst = outs[a].at[4 * block[0] + 2 * block[1] + block[2]]
            return pltpu.make_async_remote_copy(
                src_ref=ins[a] if from_shard else dst, dst_ref=dst,
                send_sem=send_sems.at[a * per + k], recv_sem=recv_sems.at[a * per + k],
                device_id=to, device_id_type=MESH)

        def local(a):
            return pltpu.make_async_copy(ins[a], outs[a].at[4 * x + 2 * y + c], local_sems.at[a])

        return me, sibling, chips, c, copy, local

    def first(ins, outs, sems):
        me, sibling, chips, c, copy, local = tools(ins, outs, sems)
        for a in range(n):
            local(a).start()
            copy(a, 0, me, sibling, True).start()
            for j, chip in enumerate(chips):
                copy(a, 1 + j, me, (*chip, c), True).start()

    def middle(ins, outs, sems):
        me, sibling, chips, c, copy, local = tools(ins, outs, sems)
        for a in range(n):
            for j, chip in enumerate(chips):
                copy(a, 1 + j, (*chip, c), me).wait_recv()
                copy(a, 4 + j, (*chip, c), sibling).start()

    def last(ins, outs, sems):
        me, sibling, chips, c, copy, local = tools(ins, outs, sems)
        for a in range(n):
            copy(a, 0, sibling, me).wait_recv()
            copy(a, 0, me, sibling, True).wait_send()
            for j, chip in enumerate(chips):
                copy(a, 4 + j, (*chip, 1 - c), me).wait_recv()
                copy(a, 1 + j, me, (*chip, c), True).wait_send()
                copy(a, 4 + j, (*chip, c), sibling).wait_send()
            local(a).wait()

    return _Comm(
        shards, [jax.ShapeDtypeStruct((N_DEV,) + s.shape, s.dtype) for s in shards],
        [pltpu.SemaphoreType.DMA((n * per,)), pltpu.SemaphoreType.DMA((n * per,)), pltpu.SemaphoreType.DMA((n,))],
        [(0.0, first), (mid_frac, middle), (None, last)])


def _pair_comm(grads):
    n = len(grads)

    def copies(ins, outs, sems):
        send_sems, recv_sems = sems
        x, y, c = _position()
        return [pltpu.make_async_remote_copy(
            src_ref=ins[a].at[k, 1 - c], dst_ref=outs[a].at[k],
            send_sem=send_sems.at[a * N_CHIP + k], recv_sem=recv_sems.at[a * N_CHIP + k],
            device_id=(x, y, 1 - c), device_id_type=MESH) for a in range(n) for k in range(N_CHIP)]

    def first(ins, outs, sems):
        for cp in copies(ins, outs, sems):
            cp.start()

    def last(ins, outs, sems):
        for cp in copies(ins, outs, sems):
            cp.wait()

    return _Comm(
        grads, [jax.ShapeDtypeStruct((N_CHIP,) + g.shape[2:], g.dtype) for g in grads],
        [pltpu.SemaphoreType.DMA((n * N_CHIP,)), pltpu.SemaphoreType.DMA((n * N_CHIP,))],
        [(0.0, first), (None, last)])


def _chip_comm(sums):
    n = len(sums)
    per = N_CHIP - 1

    def copies(ins, outs, sems):
        send_sems, recv_sems, local_sems = sems
        x, y, c = _position()
        my_chip = 2 * x + y
        cps = []
        for a in range(n):
            cps.append(pltpu.make_async_copy(ins[a].at[my_chip], outs[a].at[my_chip], local_sems.at[a]))
            for j, (px, py) in enumerate([(1 - x, y), (x, 1 - y), (1 - x, 1 - y)]):
                cps.append(pltpu.make_async_remote_copy(
                    src_ref=ins[a].at[2 * px + py], dst_ref=outs[a].at[my_chip],
                    send_sem=send_sems.at[a * per + j], recv_sem=recv_sems.at[a * per + j],
                    device_id=(px, py, c), device_id_type=MESH))
        return cps

    def first(ins, outs, sems):
        for cp in copies(ins, outs, sems):
            cp.start()

    def last(ins, outs, sems):
        for cp in copies(ins, outs, sems):
            cp.wait()

    return _Comm(
        sums, [jax.ShapeDtypeStruct(s.shape, s.dtype) for s in sums],
        [pltpu.SemaphoreType.DMA((n * per,)), pltpu.SemaphoreType.DMA((n * per,)), pltpu.SemaphoreType.DMA((n,))],
        [(0.0, first), (None, last)])


def _matmul(name, operands, pairs, outs, grid, acc_shape, n_slots=1, epilogue=None, comms=()):
    used = sorted({i for p in pairs for i in p[:2]})
    n_op = len(operands)
    n_out = len(outs)
    k_axis = len(grid) - 1
    n_k = grid[-1]
    lay = _CommLayout(comms)

    def body(*refs):
        ops = refs[:n_op]
        out_refs = refs[n_op + lay.n_in:n_op + lay.n_in + n_out]
        acc = refs[n_op + lay.n_in + n_out + lay.n_out]
        k = pl.program_id(k_axis)
        step = _linear_step(grid)
        cin = refs[n_op:n_op + lay.n_in]
        cout = refs[n_op + lay.n_in + n_out:n_op + lay.n_in + n_out + lay.n_out]
        csem = refs[n_op + lay.n_in + n_out + lay.n_out + 1:]
        lay.run(cin, cout, csem, step, int(np.prod(grid)), post=False)

        @pl.when(k == 0)
        def _():
            acc[...] = jnp.zeros_like(acc)

        vals = {i: ops[i][...] for i in used}
        vals = {i: (v if v.dtype == BF16 else v.astype(BF16)) for i, v in vals.items()}
        for s in range(n_slots):
            tot = None
            for ia, ib, form, slot in pairs:
                if slot != s:
                    continue
                d = lax.dot_general(vals[ia], vals[ib], _DIMS[form], preferred_element_type=F32)
                tot = d if tot is None else tot + d
            acc[s] += tot

        @pl.when(k == n_k - 1)
        def _():
            rows = acc_shape[0]
            chunk = EPILOGUE_ROWS if (epilogue is not None and rows % EPILOGUE_ROWS == 0) else rows
            for r0 in range(0, rows, chunk):
                sl = slice(r0, r0 + chunk)
                accs = [acc[s, sl, :] for s in range(n_slots)]
                extra = [ops[i][sl, :] for i in range(n_op) if i not in used]
                res = epilogue(accs, *extra) if epilogue is not None else accs
                for o, v in zip(out_refs, res):
                    o[sl, :] = v.astype(o.dtype)

        lay.run(cin, cout, csem, step, int(np.prod(grid)), post=True)

    res = pl.pallas_call(
        body,
        name=name,
        grid=grid,
        in_specs=[s for _, s in operands] + lay.in_specs,
        out_specs=[s for _, s in outs] + lay.out_specs,
        out_shape=[s for s, _ in outs] + lay.out_shapes,
        scratch_shapes=[pltpu.VMEM((n_slots,) + tuple(acc_shape), F32)] + lay.sem_shapes,
        compiler_params=_cparams(len(grid)),
    )(*[a for a, _ in operands], *lay.arrays)
    lay.deliver(res[n_out:])
    return res[:n_out]


def _mm_plain(name, a, b, form, out_dtype, tm, tn, tk, extra=None, epilogue=None, comms=()):
    if form == "nn":
        (M, K), N = a.shape, b.shape[1]
    elif form == "nt":
        (M, K), N = a.shape, b.shape[0]
    else:
        (K, M), N = a.shape, b.shape[1]
    tm, tn, tk = _tile(M, tm), _tile(N, tn), _tile(K, tk)
    a_spec = pl.BlockSpec((tk, tm), lambda m, n, k: (k, m)) if form == "tn" else pl.BlockSpec((tm, tk), lambda m, n, k: (m, k))
    b_spec = pl.BlockSpec((tn, tk), lambda m, n, k: (n, k)) if form == "nt" else pl.BlockSpec((tk, tn), lambda m, n, k: (k, n))
    operands = [(a, a_spec), (b, b_spec)]
    if extra is not None:
        operands.append((extra, pl.BlockSpec((tm, tn), lambda m, n, k: (m, n))))
    out = (jax.ShapeDtypeStruct((M, N), out_dtype), pl.BlockSpec((tm, tn), lambda m, n, k: (m, n)))
    return _matmul(name, operands, [(0, 1, form, 0)], [out], (M // tm, N // tn, K // tk), (tm, tn), epilogue=epilogue,
                   comms=comms)[0]


def _cast_bf16(name, w):
    r, c = w.shape
    tr = _rows(r, 6 * c)

    def body(w_ref, o_ref):
        o_ref[...] = w_ref[...].astype(BF16)

    return pl.pallas_call(
        body, name=name, grid=(r // tr,),
        in_specs=[pl.BlockSpec((tr, c), lambda i: (i, 0))],
        out_specs=pl.BlockSpec((tr, c), lambda i: (i, 0)),
        out_shape=jax.ShapeDtypeStruct((r, c), BF16),
        compiler_params=_cparams(1),
    )(w)


def _rms_stats(x):
    r = lax.rsqrt(jnp.mean(x * x, axis=-1, keepdims=True) + EPS)
    return x * r, r


def _rms_bwd(xhat, r, w, dy):
    dxh = dy * w
    return r * (dxh - xhat * jnp.mean(dxh * xhat, axis=-1, keepdims=True))


def _row_spec(tr, d):
    return pl.BlockSpec((tr, d), lambda i: (i, 0))


def _vec_spec(d):
    return pl.BlockSpec((1, d), lambda i: (0, 0))


def _rms_fwd(name, x, w):
    t, d = x.shape
    tr = _tile(t, ROW_TILE)

    def body(x_ref, w_ref, h_ref):
        xh, _ = _rms_stats(x_ref[...])
        h_ref[...] = (xh * w_ref[...]).astype(BF16)

    return pl.pallas_call(
        body, name=name, grid=(t // tr,),
        in_specs=[_row_spec(tr, d), _vec_spec(d)],
        out_specs=_row_spec(tr, d),
        out_shape=jax.ShapeDtypeStruct((t, d), BF16),
        compiler_params=_cparams(1),
    )(x, w)


def _resid_rms(name, xres, y, w_post, w_next):
    t, d = xres.shape
    tr = _tile(t, ROW_TILE)
    has_next = w_next is not None

    def body(*refs):
        if has_next:
            x_ref, y_ref, wp_ref, wn_ref, xo_ref, h_ref = refs
        else:
            x_ref, y_ref, wp_ref, xo_ref, h_ref = refs
        yh, _ = _rms_stats(y_ref[...])
        xn = x_ref[...] + yh * wp_ref[...]
        xo_ref[...] = xn
        if has_next:
            xh, _ = _rms_stats(xn)
            h_ref[...] = (xh * wn_ref[...]).astype(BF16)
        else:
            h_ref[...] = xn.astype(BF16)

    ins = [xres, y, w_post] + ([w_next] if has_next else [])
    in_specs = [_row_spec(tr, d), _row_spec(tr, d), _vec_spec(d)] + ([_vec_spec(d)] if has_next else [])
    return pl.pallas_call(
        body, name=name, grid=(t // tr,),
        in_specs=in_specs,
        out_specs=[_row_spec(tr, d), _row_spec(tr, d)],
        out_shape=[jax.ShapeDtypeStruct((t, d), F32), jax.ShapeDtypeStruct((t, d), BF16)],
        compiler_params=_cparams(1),
    )(*ins)


def _ple_loss(name, x2, pe, pgl, w_pp, tgt):
    t, d = x2.shape
    tr = _tile(t, ROW_TILE)

    def body(x2_ref, pe_ref, pgl_ref, w_ref, tgt_ref, loss_ref, d3_ref, dpe_ref, dpgl_ref, dw_ref):
        @pl.when(pl.program_id(0) == 0)
        def _():
            loss_ref[...] = jnp.zeros_like(loss_ref)
            dw_ref[...] = jnp.zeros_like(dw_ref)

        pe_v = pe_ref[...]
        s = _sigmoid(pgl_ref[...])
        y = pe_v * s
        yh, r = _rms_stats(y)
        w = w_ref[...]
        err = x2_ref[...] + yh * w - tgt_ref[...]
        loss_ref[...] += 0.5 * jnp.sum(jnp.mean(err * err, axis=-1, keepdims=True), axis=0, keepdims=True)
        d3 = err * (1.0 / d)
        d3_ref[...] = d3
        dw_ref[...] += jnp.sum(d3 * yh, axis=0, keepdims=True)
        dy = _rms_bwd(yh, r, w, d3)
        dpe_ref[...] = (dy * s).astype(BF16)
        dpgl_ref[...] = (dy * pe_v * s * (1.0 - s)).astype(BF16)

    return pl.pallas_call(
        body, name=name, grid=(t // tr,),
        in_specs=[_row_spec(tr, d), _row_spec(tr, d), _row_spec(tr, d), _vec_spec(d), _row_spec(tr, d)],
        out_specs=[pl.BlockSpec((1, 1), lambda i: (0, 0)), _row_spec(tr, d), _row_spec(tr, d), _row_spec(tr, d), _vec_spec(d)],
        out_shape=[jax.ShapeDtypeStruct((1, 1), F32), jax.ShapeDtypeStruct((t, d), F32),
                   jax.ShapeDtypeStruct((t, d), BF16), jax.ShapeDtypeStruct((t, d), BF16),
                   jax.ShapeDtypeStruct((1, d), F32)],
        compiler_params=_cparams(1),
    )(x2, pe, pgl, w_pp, tgt)


def _norm_bwd(name, dres, y, w_post):
    t, d = dres.shape
    tr = _tile(t, ROW_TILE)

    def body(d_ref, y_ref, w_ref, dy_ref, dw_ref):
        @pl.when(pl.program_id(0) == 0)
        def _():
            dw_ref[...] = jnp.zeros_like(dw_ref)

        dv = d_ref[...]
        yh, r = _rms_stats(y_ref[...])
        dw_ref[...] += jnp.sum(dv * yh, axis=0, keepdims=True)
        dy_ref[...] = _rms_bwd(yh, r, w_ref[...], dv).astype(BF16)

    return pl.pallas_call(
        body, name=name, grid=(t // tr,),
        in_specs=[_row_spec(tr, d), _row_spec(tr, d), _vec_spec(d)],
        out_specs=[_row_spec(tr, d), _vec_spec(d)],
        out_shape=[jax.ShapeDtypeStruct((t, d), BF16), jax.ShapeDtypeStruct((1, d), F32)],
        compiler_params=_cparams(1),
    )(dres, y, w_post)


def _prenorm_bwd(name, dres, dh, xin, w_pre, y=None, w_post=None):
    t, d = dres.shape
    tr = _tile(t, ROW_TILE)
    two = y is not None

    def body(*refs):
        if two:
            d_ref, dh_ref, x_ref, wpre_ref, y_ref, wpost_ref, do_ref, dwpre_ref, dy_ref, dwpost_ref = refs
        else:
            d_ref, dh_ref, x_ref, wpre_ref, do_ref, dwpre_ref = refs

        @pl.when(pl.program_id(0) == 0)
        def _():
            dwpre_ref[...] = jnp.zeros_like(dwpre_ref)
            if two:
                dwpost_ref[...] = jnp.zeros_like(dwpost_ref)

        dhv = dh_ref[...]
        xh, r = _rms_stats(x_ref[...])
        dwpre_ref[...] += jnp.sum(dhv * xh, axis=0, keepdims=True)
        dout = d_ref[...] + _rms_bwd(xh, r, wpre_ref[...], dhv)
        do_ref[...] = dout
        if two:
            yh, ry = _rms_stats(y_ref[...])
            dwpost_ref[...] += jnp.sum(dout * yh, axis=0, keepdims=True)
            dy_ref[...] = _rms_bwd(yh, ry, wpost_ref[...], dout).astype(BF16)

    ins = [dres, dh, xin, w_pre] + ([y, w_post] if two else [])
    in_specs = [_row_spec(tr, d)] * 3 + [_vec_spec(d)] + ([_row_spec(tr, d), _vec_spec(d)] if two else [])
    out_specs = [_row_spec(tr, d), _vec_spec(d)] + ([_row_spec(tr, d), _vec_spec(d)] if two else [])
    out_shape = [jax.ShapeDtypeStruct((t, d), F32), jax.ShapeDtypeStruct((1, d), F32)]
    if two:
        out_shape += [jax.ShapeDtypeStruct((t, d), BF16), jax.ShapeDtypeStruct((1, d), F32)]
    return pl.pallas_call(
        body, name=name, grid=(t // tr,),
        in_specs=in_specs, out_specs=out_specs, out_shape=out_shape,
        compiler_params=_cparams(1),
    )(*ins)


_LEVELS = (32, 16, 8, 4, 2, 1)
_N_CUM = 3 + 2 * len(_LEVELS)


def _hgrn_constants():
    c = GLA_CHUNK
    idx = np.arange(c)
    t, r = idx[:, None], idx[None, :]
    mats = [(r <= t), (r > t), np.ones((c, c), bool)]
    lq, lk, masks = [], [], []
    for h in _LEVELS:
        blk, pos = idx // (2 * h), idx % (2 * h)
        mid = blk * 2 * h + h - 1
        upper, lower = pos >= h, pos < h
        lq.append(upper[:, None] & (r > mid[:, None]) & (r <= t))
        lk.append(lower[:, None] & (r > t) & (r <= mid[:, None]))
        masks.append((blk[:, None] == blk[None, :]) & upper[:, None] & lower[None, :])
    cum = np.concatenate(mats + lq + lk, axis=0).astype(np.float32)
    rev = (r >= t).astype(np.float32)
    return (jnp.asarray(cum, BF16), jnp.asarray(rev, BF16), jnp.asarray(np.stack(masks).astype(np.float32)))


def _hgrn_gates(qp, fp, lb):
    sq = _sigmoid(qp)
    q = qp * sq
    sg = _sigmoid(fp)
    f = lb + (1.0 - lb) * sg
    k = 1.0 - f
    logf = jnp.log(jnp.maximum(f, 1e-30))
    return q, sq, sg, f, k, logf


def _hgrn_decays(cum_ref, logf):
    c = GLA_CHUNK
    e = jnp.exp(_dot_exact_l(cum_ref[...], logf))
    part = lambda i: e[i * c:(i + 1) * c]
    n = len(_LEVELS)
    return part(0), part(1), part(2), [part(3 + i) for i in range(n)], [part(3 + n + i) for i in range(n)]


def _hgrn_fwd(proj, lb, nw, n_heads, comms=()):
    t = proj.shape[0]
    aw = n_heads * HEAD
    rb = _tile(t, HGRN_ROWS)
    c = GLA_CHUNK
    n_sub = rb // c
    cum, _, masks = _hgrn_constants()
    lay = _CommLayout(comms)
    hp = HGRN_HEADS if n_heads % HGRN_HEADS == 0 else 1
    wd = hp * HEAD
    grid = (n_heads // hp, t // rb)

    def body(*refs):
        q_ref, f_ref, i_ref, g_ref, lb_ref, nw_ref, cum_ref, m_ref = refs[:8]
        cin = refs[8:8 + lay.n_in]
        a_ref, o_ref, s_ref, sc_ref = refs[8 + lay.n_in:12 + lay.n_in]
        cout = refs[12 + lay.n_in:12 + lay.n_in + lay.n_out]
        st = refs[12 + lay.n_in + lay.n_out]
        csem = refs[13 + lay.n_in + lay.n_out:]
        step = _linear_step(grid)
        lay.run(cin, cout, csem, step, grid[0] * grid[1], post=False)

        @pl.when(pl.program_id(1) == 0)
        def _():
            st[...] = jnp.zeros_like(st)

        lbv = lb_ref[...]
        nwv = nw_ref[...]
        eye = (lax.broadcasted_iota(jnp.int32, (c, c), 0) == lax.broadcasted_iota(jnp.int32, (c, c), 1)).astype(F32)
        heads = range(hp)
        hs = lambda a, h: a[:, h * HEAD:(h + 1) * HEAD]

        def chunk(j, carry):
            rows = pl.ds(pl.multiple_of(j * c, c), c)
            q, _, _, _, k, logf = _hgrn_gates(q_ref[rows, :], f_ref[rows, :], lbv)
            v = i_ref[rows, :]
            eb, ebe, eend, eq, ek = _hgrn_decays(cum_ref, logf)
            qt, kt, qk = q * eb, k * ebe, q * k
            s_in = [st[h] for h in heads]
            for h in heads:
                s_ref[h, j] = s_in[h]
            inter = [_dot(hs(qt, h), s_in[h], "nt") for h in heads]
            for h in heads:
                st[h] = s_in[h] * hs(eend, h)[0:1] + _dot(hs(v, h), hs(kt, h), "tn")
            scores = [eye * jnp.sum(hs(qk, h), axis=-1, keepdims=True) for h in heads]
            for lvl in range(len(_LEVELS)):
                ql, kl = q * eq[lvl], k * ek[lvl]
                for h in heads:
                    scores[h] = scores[h] + m_ref[lvl] * _dot(hs(ql, h), hs(kl, h), "nt")
            gv = g_ref[rows, :]
            gate = nwv * (gv * _sigmoid(gv))
            for h in heads:
                sc_ref[h, rows, :] = scores[h]
                o = inter[h] + _dot(scores[h], hs(v, h))
                o_ref[rows, h * HEAD:(h + 1) * HEAD] = o
                r = lax.rsqrt(jnp.mean(o * o, axis=-1, keepdims=True) + EPS)
                a_ref[rows, h * HEAD:(h + 1) * HEAD] = (o * r * hs(gate, h)).astype(BF16)
            return carry

        lax.fori_loop(0, n_sub, chunk, 0, unroll=HGRN_UNROLL)
        lay.run(cin, cout, csem, step, grid[0] * grid[1], post=True)

    n_hb = n_heads // hp
    col = lambda base: pl.BlockSpec((rb, wd), lambda h, r: (r, base * n_hb + h))
    vec = pl.BlockSpec((1, wd), lambda h, r: (0, h))
    res = pl.pallas_call(
        body, name="hgrn2_fwd", grid=grid,
        in_specs=[col(0), col(1), col(2), col(3), vec, vec,
                  pl.BlockSpec(cum.shape, lambda h, r: (0, 0)), pl.BlockSpec(masks.shape, lambda h, r: (0, 0, 0))] + lay.in_specs,
        out_specs=[pl.BlockSpec((rb, wd), lambda h, r: (r, h)), pl.BlockSpec((rb, wd), lambda h, r: (r, h)),
                   pl.BlockSpec((hp, n_sub, HEAD, HEAD), lambda h, r: (h, r, 0, 0)),
                   pl.BlockSpec((hp, rb, c), lambda h, r: (h, r, 0))] + lay.out_specs,
        out_shape=[jax.ShapeDtypeStruct((t, aw), BF16), jax.ShapeDtypeStruct((t, aw), F32),
                   jax.ShapeDtypeStruct((n_heads, t // c, HEAD, HEAD), F32),
                   jax.ShapeDtypeStruct((n_heads, t, c), F32)] + lay.out_shapes,
        scratch_shapes=[pltpu.VMEM((hp, HEAD, HEAD), F32)] + lay.sem_shapes,
        compiler_params=_cparams(2),
    )(proj, proj, proj, proj, lb, nw, cum, masks, *lay.arrays)
    lay.deliver(res[4:])
    return res[:4]


def _hgrn_bwd(proj, lb, nw, o_raw, states, scores, dab, n_heads, comms=()):
    t = proj.shape[0]
    aw = n_heads * HEAD
    rb = _tile(t, HGRN_ROWS)
    c = GLA_CHUNK
    n_sub = rb // c
    n_rb = t // rb
    cum, rev, masks = _hgrn_constants()
    lay = _CommLayout(comms)
    hp = HGRN_HEADS if n_heads % HGRN_HEADS == 0 else 1
    wd = hp * HEAD
    grid = (n_heads // hp, n_rb)

    def body(*refs):
        q_ref, f_ref, i_ref, g_ref, lb_ref, nw_ref, o_ref, s_ref, sc_ref, da_ref, cum_ref, rev_ref, m_ref = refs[:13]
        cin = refs[13:13 + lay.n_in]
        dq_ref, df_ref, di_ref, dg_ref, dlb_ref, dnw_ref = refs[13 + lay.n_in:19 + lay.n_in]
        cout = refs[19 + lay.n_in:19 + lay.n_in + lay.n_out]
        dst = refs[19 + lay.n_in + lay.n_out]
        csem = refs[20 + lay.n_in + lay.n_out:]
        step = _linear_step(grid)
        lay.run(cin, cout, csem, step, grid[0] * grid[1], post=False)

        @pl.when(pl.program_id(1) == 0)
        def _():
            dst[...] = jnp.zeros_like(dst)
            dlb_ref[...] = jnp.zeros_like(dlb_ref)
            dnw_ref[...] = jnp.zeros_like(dnw_ref)

        lbv = lb_ref[...]
        nwv = nw_ref[...]
        ri = lax.broadcasted_iota(jnp.int32, (c, c), 0)
        ci = lax.broadcasted_iota(jnp.int32, (c, c), 1)
        eye = (ri == ci).astype(F32)
        causal = (ci <= ri).astype(F32)
        last_row = (lax.broadcasted_iota(jnp.int32, (c, wd), 0) == c - 1).astype(F32)
        heads = range(hp)
        hs = lambda a, h: a[:, h * HEAD:(h + 1) * HEAD]
        wide = lambda parts: parts[0] if hp == 1 else jnp.concatenate(parts, axis=1)

        def head_mean(a):
            return wide([jnp.broadcast_to(jnp.mean(hs(a, h), axis=-1, keepdims=True), (c, HEAD)) for h in heads])

        def chunk(jj, carry):
            j = n_sub - 1 - jj
            rows = pl.ds(pl.multiple_of(j * c, c), c)
            qp = q_ref[rows, :]
            q, sq, sg, f, k, logf = _hgrn_gates(qp, f_ref[rows, :], lbv)
            v = i_ref[rows, :]
            gv = g_ref[rows, :]
            eb, ebe, eend, eq, ek = _hgrn_decays(cum_ref, logf)
            s_in = [s_ref[h, j] for h in heads]
            a_sc = [sc_ref[h, rows, :] for h in heads]
            dsn = [dst[h] for h in heads]
            o = o_ref[rows, :]
            r = lax.rsqrt(head_mean(o * o) + EPS)
            oh = o * r
            sgg = _sigmoid(gv)
            sil = gv * sgg
            da = da_ref[rows, :]
            dg_ref[rows, :] = (da * oh * nwv * (sgg * (1.0 + gv * (1.0 - sgg)))).astype(BF16)
            dnw_ref[...] += jnp.sum(da * oh * sil, axis=0, keepdims=True)
            doh = da * nwv * sil
            do = r * (doh - oh * head_mean(doh * oh))
            kt = k * ebe
            qt = q * eb
            d_sc = [_dot(hs(do, h), hs(v, h), "nt") * causal for h in heads]
            dqt = wide([_dot(hs(do, h), s_in[h]) for h in heads])
            dkt = wide([_dot(hs(v, h), dsn[h]) for h in heads])
            for h in heads:
                dst[h] = dsn[h] * hs(eend, h)[0:1] + _dot(hs(do, h), hs(qt, h), "tn")
            di_ref[rows, :] = wide([_dot(a_sc[h], hs(do, h), "tn") + _dot(hs(kt, h), dsn[h], "nt") for h in heads]).astype(BF16)
            diag = wide([jnp.broadcast_to(jnp.sum(d_sc[h] * eye, axis=-1, keepdims=True), (c, HEAD)) for h in heads])
            dq = dqt * eb
            dk = dkt * ebe
            db = q * dq - k * dk
            dq = dq + diag * k
            dk = dk + diag * q
            for lvl in range(len(_LEVELS)):
                ql = (q * eq[lvl]).astype(BF16)
                kl = (k * ek[lvl]).astype(BF16)
                dm = [(m_ref[lvl] * d_sc[h]).astype(BF16) for h in heads]
                gq = wide([_dot(dm[h], hs(kl, h)) for h in heads])
                gk = wide([_dot(dm[h], hs(ql, h), "tn") for h in heads])
                dq = dq + gq * eq[lvl]
                dk = dk + gk * ek[lvl]
                db = db + ql.astype(F32) * gq - kl.astype(F32) * gk
            state_term = wide([jnp.sum(s_in[h] * dsn[h], axis=0, keepdims=True) for h in heads])
            extra = jnp.sum(dkt * kt, axis=0, keepdims=True) + eend[0:1] * state_term
            db = db + last_row * extra
            dlogf = _dot_exact_l(rev_ref[...], db)
            dfv = jnp.where(f > 1e-30, dlogf / f, 0.0) - dk
            df_ref[rows, :] = (dfv * (1.0 - lbv) * sg * (1.0 - sg)).astype(BF16)
            dlb_ref[...] += jnp.sum(dfv * (1.0 - sg), axis=0, keepdims=True)
            dq_ref[rows, :] = (dq * (sq * (1.0 + qp * (1.0 - sq)))).astype(BF16)
            return carry

        lax.fori_loop(0, n_sub, chunk, 0, unroll=HGRN_UNROLL)
        lay.run(cin, cout, csem, step, grid[0] * grid[1], post=True)

    n_hb = n_heads // hp
    col = lambda base: pl.BlockSpec((rb, wd), lambda h, r: (n_rb - 1 - r, base * n_hb + h))
    blk = pl.BlockSpec((rb, wd), lambda h, r: (n_rb - 1 - r, h))
    vec = pl.BlockSpec((1, wd), lambda h, r: (0, h))
    const = lambda a: pl.BlockSpec(a.shape, lambda h, r: (0,) * a.ndim)
    res = pl.pallas_call(
        body, name="hgrn2_bwd", grid=grid,
        in_specs=[col(0), col(1), col(2), col(3), vec, vec, blk,
                  pl.BlockSpec((hp, n_sub, HEAD, HEAD), lambda h, r: (h, n_rb - 1 - r, 0, 0)),
                  pl.BlockSpec((hp, rb, c), lambda h, r: (h, n_rb - 1 - r, 0)),
                  blk, const(cum), const(rev), const(masks)] + lay.in_specs,
        out_specs=[blk, blk, blk, blk, vec, vec] + lay.out_specs,
        out_shape=[jax.ShapeDtypeStruct((t, aw), BF16)] * 4 + [jax.ShapeDtypeStruct((1, aw), F32)] * 2 + lay.out_shapes,
        scratch_shapes=[pltpu.VMEM((hp, HEAD, HEAD), F32)] + lay.sem_shapes,
        compiler_params=_cparams(2),
    )(proj, proj, proj, proj, lb, nw, o_raw, states, scores, dab, cum, rev, masks, *lay.arrays)
    lay.deliver(res[6:])
    return res[:6]


def _lb_fwd(lb_param):
    def body(p_ref, o_ref):
        p = p_ref[...]
        e = jnp.exp(p - jnp.max(p, axis=0, keepdims=True))
        o_ref[...] = e[0:1] / jnp.sum(e, axis=0, keepdims=True)

    return pl.pallas_call(body, name="lb_fwd", out_shape=jax.ShapeDtypeStruct((1, lb_param.shape[1]), F32))(lb_param)


def _lb_bwd(lb_param, dlb):
    def body(p_ref, d_ref, o_ref):
        p = p_ref[...]
        e = jnp.exp(p - jnp.max(p, axis=0, keepdims=True))
        s = e / jnp.sum(e, axis=0, keepdims=True)
        first = (lax.broadcasted_iota(jnp.int32, p.shape, 0) == 0).astype(F32)
        o_ref[...] = d_ref[...] * s[0:1] * (first - s)

    return pl.pallas_call(body, name="lb_bwd", out_shape=jax.ShapeDtypeStruct(lb_param.shape, F32))(lb_param, dlb)


def _gmlp_norm(v, lnw, lnb):
    vf = _gelu(v)
    mu = jnp.mean(vf, axis=-1, keepdims=True)
    cen = vf - mu
    rstd = lax.rsqrt(jnp.mean(cen * cen, axis=-1, keepdims=True) + EPS)
    xh = cen * rstd
    return xh, rstd, xh * lnw + lnb


def _tril(n):
    return (lax.broadcasted_iota(jnp.int32, (n, n), 1) <= lax.broadcasted_iota(jnp.int32, (n, n), 0)).astype(F32)


def _gmlp_fwd(proj, lnw, lnb, w_sp, bs_t, n_groups, col_base):
    t = proj.shape[0]
    bw = n_groups * HEAD
    c = GMLP_CHUNK

    def body(u_ref, v_ref, lnw_ref, lnb_ref, w_ref, bs_ref, o_ref):
        tri = _tril(c)
        uf = _gelu(u_ref[...])
        _, _, vn = _gmlp_norm(v_ref[...], lnw_ref[...], lnb_ref[...])
        for g in range(n_groups):
            cols = slice(g * HEAD, (g + 1) * HEAD)
            z = _dot(w_ref[g] * tri, vn[:, cols]) + bs_ref[:, g:g + 1]
            o_ref[:, cols] = (uf[:, cols] * z).astype(BF16)

    blk = lambda b: pl.BlockSpec((c, bw), lambda n: (n, b))
    const = lambda a: pl.BlockSpec(a.shape, lambda n: (0,) * a.ndim)
    return pl.pallas_call(
        body, name="gmlp_fwd", grid=(t // c,),
        in_specs=[blk(col_base), blk(col_base + 1), const(lnw), const(lnb), const(w_sp), const(bs_t)],
        out_specs=pl.BlockSpec((c, bw), lambda n: (n, 0)),
        out_shape=jax.ShapeDtypeStruct((t, bw), BF16),
        compiler_params=_cparams(1),
    )(proj, proj, lnw, lnb, w_sp, bs_t)


def _gmlp_bwd(proj, lnw, lnb, w_sp, bs_t, dab, n_groups, col_base):
    t = proj.shape[0]
    bw = n_groups * HEAD
    c = GMLP_CHUNK
    n_steps = t // c
    sel = jnp.asarray((np.arange(bw)[:, None] // HEAD == np.arange(n_groups)[None, :]).astype(np.float32), BF16)

    def body(u_ref, v_ref, lnw_ref, lnb_ref, w_ref, bs_ref, d_ref, sel_ref,
             du_ref, dv_ref, dlnw_ref, dlnb_ref, dw_ref, dbs_ref, dz_acc, dvn_scr):
        step = pl.program_id(0)

        @pl.when(step == 0)
        def _():
            dlnw_ref[...] = jnp.zeros_like(dlnw_ref)
            dlnb_ref[...] = jnp.zeros_like(dlnb_ref)
            dw_ref[...] = jnp.zeros_like(dw_ref)
            dz_acc[...] = jnp.zeros_like(dz_acc)

        tri = _tril(c)
        u = u_ref[...]
        v = v_ref[...]
        uf = _gelu(u)
        lnw_v = lnw_ref[...]
        xh, rstd, vn = _gmlp_norm(v, lnw_v, lnb_ref[...])
        dbo = d_ref[...]
        dz = dbo * uf
        dz_acc[...] += dz
        for g in range(n_groups):
            cols = slice(g * HEAD, (g + 1) * HEAD)
            wg = w_ref[g] * tri
            z = _dot(wg, vn[:, cols]) + bs_ref[:, g:g + 1]
            du_ref[:, cols] = (dbo[:, cols] * z * _gelu_grad(u[:, cols])).astype(BF16)
            dvn_scr[:, cols] = _dot(wg, dz[:, cols], "tn")
            dw_ref[g] += tri * _dot(dz[:, cols], vn[:, cols], "nt")
        dvn = dvn_scr[...]
        dlnw_ref[...] += jnp.sum(dvn * xh, axis=0, keepdims=True)
        dlnb_ref[...] += jnp.sum(dvn, axis=0, keepdims=True)
        dxh = dvn * lnw_v
        dvf = rstd * (dxh - jnp.mean(dxh, axis=-1, keepdims=True) - xh * jnp.mean(dxh * xh, axis=-1, keepdims=True))
        dv_ref[...] = (dvf * _gelu_grad(v)).astype(BF16)

        @pl.when(step == n_steps - 1)
        def _():
            dbs_ref[...] = _dot_exact_r(dz_acc[...], sel_ref[...])

    blk = lambda b: pl.BlockSpec((c, bw), lambda n: (n, b))
    const = lambda a: pl.BlockSpec(a.shape, lambda n: (0,) * a.ndim)
    row = pl.BlockSpec((c, bw), lambda n: (n, 0))
    vec = pl.BlockSpec((1, bw), lambda n: (0, 0))
    return pl.pallas_call(
        body, name="gmlp_bwd", grid=(n_steps,),
        in_specs=[blk(col_base), blk(col_base + 1), const(lnw), const(lnb), const(w_sp), const(bs_t), blk(1), const(sel)],
        out_specs=[row, row, vec, vec, const(w_sp), const(bs_t)],
        out_shape=[jax.ShapeDtypeStruct((t, bw), BF16), jax.ShapeDtypeStruct((t, bw), BF16),
                   jax.ShapeDtypeStruct((1, bw), F32), jax.ShapeDtypeStruct((1, bw), F32),
                   jax.ShapeDtypeStruct(w_sp.shape, F32), jax.ShapeDtypeStruct(bs_t.shape, F32)],
        scratch_shapes=[pltpu.VMEM((c, bw), F32), pltpu.VMEM((c, bw), F32)],
        compiler_params=_cparams(1),
    )(proj, proj, lnw, lnb, w_sp, bs_t, dab, sel)


def _pair_sum(name, grad, other, core):
    _, _, r, c = grad.shape
    tr = _rows(r, 6 * c)

    def body(core_ref, g_ref, o_ref, out_ref):
        out_ref[...] = (g_ref[...].astype(F32) + o_ref[...].astype(F32)).astype(BF16)

    return pl.pallas_call(
        body, name=name,
        grid_spec=pltpu.PrefetchScalarGridSpec(
            num_scalar_prefetch=1, grid=(N_CHIP, r // tr),
            in_specs=[pl.BlockSpec((None, None, tr, c), lambda k, i, core_ref: (k, core_ref[0], i, 0)),
                      pl.BlockSpec((None, tr, c), lambda k, i, core_ref: (k, i, 0))],
            out_specs=pl.BlockSpec((None, tr, c), lambda k, i, core_ref: (k, i, 0))),
        out_shape=jax.ShapeDtypeStruct((N_CHIP, r, c), BF16),
        compiler_params=_cparams(2),
    )(core, grad, other)


def _adamw_math(w, g, m, v):
    m = ADAM_B1 * m + (1.0 - ADAM_B1) * g
    v = ADAM_B2 * v + (1.0 - ADAM_B2) * (g * g)
    m_hat = m / (1.0 - ADAM_B1 ** ADAM_STEP)
    v_hat = v / (1.0 - ADAM_B2 ** ADAM_STEP)
    delta = -ADAM_LR * (m_hat / (jnp.sqrt(v_hat) + ADAM_EPS) + ADAM_WD * w)
    return delta, m, v


def _adamw(name, parts, w, m, v, comms=()):
    n_parts, r, c = parts.shape
    tr = _rows(r, c * (n_parts * parts.dtype.itemsize + 28), mult=8)
    lay = _CommLayout(comms)
    grid = (r // tr,)

    def body(*refs):
        p_ref, w_ref, m_ref, v_ref = refs[:4]
        cin = refs[4:4 + lay.n_in]
        g_ref, d_ref, mo_ref, vo_ref = refs[4 + lay.n_in:8 + lay.n_in]
        cout = refs[8 + lay.n_in:8 + lay.n_in + lay.n_out]
        csem = refs[8 + lay.n_in + lay.n_out:]
        step = pl.program_id(0)
        lay.run(cin, cout, csem, step, grid[0], post=False)
        g = p_ref[0].astype(F32)
        for i in range(1, n_parts):
            g = g + p_ref[i].astype(F32)
        g_ref[...] = g
        d_ref[...], mo_ref[...], vo_ref[...] = _adamw_math(w_ref[...], g, m_ref[...], v_ref[...])
        lay.run(cin, cout, csem, step, grid[0], post=True)

    row = pl.BlockSpec((tr, c), lambda i: (i, 0))
    res = pl.pallas_call(
        body, name=name, grid=grid,
        in_specs=[pl.BlockSpec((n_parts, tr, c), lambda i: (0, i, 0)), row, row, row] + lay.in_specs,
        out_specs=[row] * 4 + lay.out_specs,
        out_shape=[jax.ShapeDtypeStruct((r, c), F32)] * 4 + lay.out_shapes,
        scratch_shapes=lay.sem_shapes,
        compiler_params=_cparams(1),
    )(parts, w, m, v, *lay.arrays)
    lay.deliver(res[4:])
    return res[:4]


def kernel(x, p, pre_mix_w, w_in, lb_param, a_norm_w, gmlp_ln_w, gmlp_ln_b, w_spatial, b_spatial, w_out, post_mix_w, pre_ffn_w, w_gate, w_up, w_down, post_ffn_w, w_ple, w_ple_gate, post_ple_w, loss_target, m_pre_mix_w, m_w_in, m_lb_param, m_a_norm_w, m_gmlp_ln_w, m_gmlp_ln_b, m_w_spatial, m_b_spatial, m_w_out, m_post_mix_w, m_pre_ffn_w, m_w_gate, m_w_up, m_w_down, m_post_ffn_w, m_w_ple, m_w_ple_gate, m_post_ple_w, v_pre_mix_w, v_w_in, v_lb_param, v_a_norm_w, v_gmlp_ln_w, v_gmlp_ln_b, v_w_spatial, v_b_spatial, v_w_out, v_post_mix_w, v_pre_ffn_w, v_w_gate, v_w_up, v_w_down, v_post_ffn_w, v_w_ple, v_w_ple_gate, v_post_ple_w):
    big_names = ["w_in", "w_out", "w_gate", "w_up", "w_down", "w_ple", "w_ple_gate"]
    small_names = ["pre_mix_w", "lb_param", "a_norm_w", "gmlp_ln_w", "gmlp_ln_b", "w_spatial", "b_spatial",
                   "post_mix_w", "pre_ffn_w", "post_ffn_w", "post_ple_w"]
    all_names = ["pre_mix_w", "w_in", "lb_param", "a_norm_w", "gmlp_ln_w", "gmlp_ln_b", "w_spatial", "b_spatial",
                 "w_out", "post_mix_w", "pre_ffn_w", "w_gate", "w_up", "w_down", "post_ffn_w", "w_ple", "w_ple_gate",
                 "post_ple_w"]
    env = dict(locals())
    W = {n: env[n] for n in all_names}
    M = {n: env["m_" + n] for n in all_names}
    V = {n: env["v_" + n] for n in all_names}

    xs = x[0]
    ps = p[0, 0]
    tgt = loss_target[0]
    t, d = xs.shape
    aw = a_norm_w.shape[1]
    bw = gmlp_ln_w.shape[1]
    n_heads, n_groups = aw // HEAD, bw // HEAD
    core = lax.axis_index("c").astype(jnp.int32).reshape(1)

    transposed = ("w_gate", "w_up")
    local = lambda a, n: jnp.swapaxes(a, 1, 2)[0] if n in transposed else a[0]
    unlocal = lambda a, n: jnp.swapaxes(a[None], 1, 2) if n in transposed else a[None]
    shard = {n: local(W[n], n) for n in big_names}
    bf = {n: _cast_bf16("cast_" + n, shard[n]) for n in big_names}
    ag_in = _ag_comm([bf["w_in"]])
    _comm_only("ag_w_in", [ag_in])
    win_g = ag_in.results[0]
    n_in = win_g.shape[2]
    ffl = bf["w_gate"].shape[0]
    n_ple = bf["w_ple"].shape[1]
    ple = ps.shape[1]

    TM, TK = 1024, 1024
    tm = _tile(t, TM)
    tkd = _tile(d, TK)

    h1 = _rms_fwd("rms_pre_mix", xs, pre_mix_w)
    ag_a = _ag_comm([bf["w_gate"]], mid_frac=0.85)
    proj = _matmul(
        "mm_proj",
        [(h1, pl.BlockSpec((tm, tkd), lambda m, n, k: (m, k))), (win_g, pl.BlockSpec((None, tkd, n_in), lambda m, n, k: (n, k, 0)))],
        [(0, 1, "nn", 0)],
        [(jax.ShapeDtypeStruct((t, N_DEV * n_in), F32), pl.BlockSpec((tm, n_in), lambda m, n, k: (m, n)))],
        (t // tm, N_DEV, d // tkd), (tm, n_in), comms=[ag_a])[0]
    wgate_g = ag_a.results[0]
    lb = _lb_fwd(lb_param)
    ag_b = _ag_comm([bf["w_out"]], mid_frac=0.6)
    a_out, o_raw, states, scores = _hgrn_fwd(proj, lb, a_norm_w, n_heads, comms=[ag_b])
    wout_f = ag_b.results[0].reshape(d, d)
    bs_t = b_spatial[0].T
    w_sp = w_spatial[0]
    col_u = (4 * aw) // bw
    b_out = _gmlp_fwd(proj, gmlp_ln_w, gmlp_ln_b, w_sp, bs_t, n_groups, col_u)
    ab = jnp.concatenate([a_out, b_out], axis=1)
    mix = _mm_plain("mm_mix", ab, wout_f, "nn", F32, TM, 2048, TK)
    x1, h2 = _resid_rms("resid_mix", xs, mix, post_mix_w, pre_ffn_w)

    def swiglu(accs, gate_v):
        gf = gate_v.astype(F32)
        return accs[0], gf * _sigmoid(gf) * accs[0]

    tmf = tm
    blk3 = lambda: pl.BlockSpec((None, tmf, ffl), lambda j, m, k: (j, m, 0))
    ag_c = _ag_comm([bf["w_up"]], mid_frac=0.85)
    gate = _matmul(
        "mm_ffn_gate",
        [(h2, pl.BlockSpec((tmf, tkd), lambda j, m, k: (m, k))),
         (wgate_g, pl.BlockSpec((None, ffl, tkd), lambda j, m, k: (j, 0, k)))],
        [(0, 1, "nt", 0)],
        [(jax.ShapeDtypeStruct((N_DEV, t, ffl), BF16), blk3())],
        (N_DEV, t // tmf, d // tkd), (tmf, ffl), comms=[ag_c])[0]
    wup_g = ag_c.results[0]
    ag_d = _ag_comm([bf["w_down"]], mid_frac=0.85)
    up, act = _matmul(
        "mm_ffn_up",
        [(h2, pl.BlockSpec((tmf, tkd), lambda j, m, k: (m, k))),
         (wup_g, pl.BlockSpec((None, ffl, tkd), lambda j, m, k: (j, 0, k))),
         (gate, blk3())],
        [(0, 1, "nt", 0)],
        [(jax.ShapeDtypeStruct((N_DEV, t, ffl), BF16), blk3()) for _ in range(2)],
        (N_DEV, t // tmf, d // tkd), (tmf, ffl), epilogue=swiglu, comms=[ag_d])
    wdown_g = ag_d.results[0]
    ag_e = _ag_comm([bf["w_ple_gate"], bf["w_ple"]], mid_frac=0.6)
    tn_d = _tile(d, 2048)
    ff = _matmul(
        "mm_ffn_down",
        [(act, pl.BlockSpec((None, tm, ffl), lambda m, n, k: (k, m, 0))),
         (wdown_g, pl.BlockSpec((None, ffl, tn_d), lambda m, n, k: (k, 0, n)))],
        [(0, 1, "nn", 0)],
        [(jax.ShapeDtypeStruct((t, d), F32), pl.BlockSpec((tm, tn_d), lambda m, n, k: (m, n)))],
        (t // tm, d // tn_d, N_DEV), (tm, tn_d), comms=[ag_e])[0]
    wpg_f = ag_e.results[0].reshape(d, d)
    wple_g = ag_e.results[1]
    x2, x2b = _resid_rms("resid_ffn", x1, ff, post_ffn_w, None)

    pgl = _mm_plain("mm_ple_gate", x2b, wpg_f, "nn", F32, TM, 2048, TK)
    pe = _matmul(
        "mm_ple",
        [(ps, pl.BlockSpec((tm, ple), lambda m, n, k: (m, 0))), (wple_g, pl.BlockSpec((None, ple, n_ple), lambda m, n, k: (n, 0, 0)))],
        [(0, 1, "nn", 0)],
        [(jax.ShapeDtypeStruct((t, N_DEV * n_ple), F32), pl.BlockSpec((tm, n_ple), lambda m, n, k: (m, n)))],
        (t // tm, N_DEV, 1), (tm, n_ple))[0]
    loss_part, d3, dpe, dpgl, g_post_ple = _ple_loss("ple_loss", x2, pe, pgl, post_ple_w, tgt)

    tkt = _tile(t, TK)
    g_wple = _matmul(
        "mm_dw_ple",
        [(ps, pl.BlockSpec((tkt, ple), lambda n, k: (k, 0))), (dpe, pl.BlockSpec((tkt, n_ple), lambda n, k: (k, n)))],
        [(0, 1, "tn", 0)],
        [(jax.ShapeDtypeStruct((N_DEV, ple, n_ple), BF16), pl.BlockSpec((None, ple, n_ple), lambda n, k: (n, 0, 0)))],
        (N_DEV, t // tkt), (ple, n_ple))[0]
    g_wpg = _mm_plain("mm_dw_ple_gate", x2b, dpgl, "tn", BF16, TM, 2048, TK)

    def by_chip(g):
        return g.reshape((N_CHIP, 2) + g.shape[-2:])

    def pair_sums(names, comm):
        return [_pair_sum("pair_sum_" + n, g, o, core) for n, g, o in zip(names, comm.arrays, comm.results)]

    r1_p = _pair_comm([by_chip(g_wpg.reshape(N_DEV, d // N_DEV, d)), by_chip(g_wple)])
    d2 = _mm_plain("mm_d_x2", dpgl, wpg_f, "nt", F32, TM, 1024, TK, extra=d3, epilogue=lambda accs, e: [accs[0] + e],
                   comms=[r1_p])
    r2_p = _chip_comm(pair_sums(["w_ple_gate", "w_ple"], r1_p))

    dff, g_post_ffn = _norm_bwd("norm_bwd_ffn", d2, ff, post_ffn_w)
    g_wdown = _matmul(
        "mm_dw_down",
        [(act, pl.BlockSpec((None, tkt, ffl), lambda j, n, k: (j, k, 0))), (dff, pl.BlockSpec((tkt, tn_d), lambda j, n, k: (k, n)))],
        [(0, 1, "tn", 0)],
        [(jax.ShapeDtypeStruct((N_DEV, ffl, d), BF16), pl.BlockSpec((None, ffl, tn_d), lambda j, n, k: (j, 0, n)))],
        (N_DEV, d // tn_d, t // tkt), (ffl, tn_d), comms=[r2_p])[0]
    r1_d = _pair_comm([by_chip(g_wdown)])

    def swiglu_bwd(accs, gate_v, up_v):
        dact = accs[0]
        gf = gate_v.astype(F32)
        sg = _sigmoid(gf)
        return dact * up_v.astype(F32) * (sg * (1.0 + gf * (1.0 - sg))), dact * (gf * sg)

    dgate, dup = _matmul(
        "mm_d_act",
        [(dff, pl.BlockSpec((tmf, tkd), lambda j, m, k: (m, k))),
         (wdown_g, pl.BlockSpec((None, ffl, tkd), lambda j, m, k: (j, 0, k))),
         (gate, blk3()), (up, blk3())],
        [(0, 1, "nt", 0)],
        [(jax.ShapeDtypeStruct((N_DEV, t, ffl), BF16), blk3()) for _ in range(2)],
        (N_DEV, t // tmf, d // tkd), (tmf, ffl), epilogue=swiglu_bwd, comms=[r1_d])
    r2_d = _chip_comm(pair_sums(["w_down"], r1_d))
    tmd = _tile(d, TM)
    def dw_ffn(name, dy, comms):
        return _matmul(
            name,
            [(dy, pl.BlockSpec((None, tkt, ffl), lambda j, n, k: (j, k, 0))),
             (h2, pl.BlockSpec((tkt, tn_d), lambda j, n, k: (k, n)))],
            [(0, 1, "tn", 0)],
            [(jax.ShapeDtypeStruct((N_DEV, ffl, d), BF16), pl.BlockSpec((None, ffl, tn_d), lambda j, n, k: (j, 0, n)))],
            (N_DEV, d // tn_d, t // tkt), (ffl, tn_d), comms=comms)[0]

    g_wgate = dw_ffn("mm_dw_gate", dgate, [r2_d])
    r1_g = _pair_comm([by_chip(g_wgate)])
    g_wup = dw_ffn("mm_dw_up", dup, [r1_g])
    r2_g = _chip_comm(pair_sums(["w_gate"], r1_g))
    r1_u = _pair_comm([by_chip(g_wup)])
    tn1 = _tile(d, 1024)
    dh2 = _matmul(
        "mm_d_h2",
        [(dgate, pl.BlockSpec((None, tm, ffl), lambda m, n, k: (k, m, 0))),
         (wgate_g, pl.BlockSpec((None, ffl, tn1), lambda m, n, k: (k, 0, n))),
         (dup, pl.BlockSpec((None, tm, ffl), lambda m, n, k: (k, m, 0))),
         (wup_g, pl.BlockSpec((None, ffl, tn1), lambda m, n, k: (k, 0, n)))],
        [(0, 1, "nn", 0), (2, 3, "nn", 0)],
        [(jax.ShapeDtypeStruct((t, d), F32), pl.BlockSpec((tm, tn1), lambda m, n, k: (m, n)))],
        (t // tm, d // tn1, N_DEV), (tm, tn1), comms=[r2_g, r1_u])[0]
    r2_u = _chip_comm(pair_sums(["w_up"], r1_u))
    d1, g_pre_ffn, dmix, g_post_mix = _prenorm_bwd("prenorm_bwd_ffn", d2, dh2, x1, pre_ffn_w, mix, post_mix_w)

    g_wout = _mm_plain("mm_dw_out", ab, dmix, "tn", BF16, TM, 2048, TK)
    r1_o = _pair_comm([by_chip(g_wout.reshape(N_DEV, d // N_DEV, d))])
    dab = _mm_plain("mm_d_ab", dmix, wout_f, "nt", F32, TM, 2048, TK, comms=[r1_o])
    r2_o = _chip_comm(pair_sums(["w_out"], r1_o))
    dq, df, di, dg, dlb, g_a_norm = _hgrn_bwd(proj, lb, a_norm_w, o_raw, states, scores, dab, n_heads, comms=[r2_u])
    du, dv, g_ln_w, g_ln_b, g_wsp, g_bs_t = _gmlp_bwd(proj, gmlp_ln_w, gmlp_ln_b, w_sp, bs_t, dab, n_groups, col_u)
    dproj = jnp.concatenate([dq, df, di, dg, du, dv], axis=1)
    g_win = _matmul(
        "mm_dw_in",
        [(h1, pl.BlockSpec((tkt, tmd), lambda j, m, k: (k, m))), (dproj, pl.BlockSpec((tkt, n_in), lambda j, m, k: (k, j)))],
        [(0, 1, "tn", 0)],
        [(jax.ShapeDtypeStruct((N_DEV, d, n_in), BF16), pl.BlockSpec((None, tmd, n_in), lambda j, m, k: (j, m, 0)))],
        (N_DEV, d // tmd, t // tkt), (tmd, n_in), comms=[r2_o])[0]
    grads, deltas, new_m, new_v = {}, {}, {}, {}

    def adam(n, parts, comms=()):
        res = _adamw("adamw_" + n, parts, shard[n], local(M[n], n), local(V[n], n), comms=comms)
        grads[n], deltas[n], new_m[n], new_v[n] = (unlocal(a, n) for a in res)

    r1_in = _pair_comm([by_chip(g_win)])
    adam("w_down", r2_d.results[0], [r1_in])
    r2_in = _chip_comm(pair_sums(["w_in"], r1_in))
    dh1 = _matmul(
        "mm_d_h1",
        [(dproj, pl.BlockSpec((tm, n_in), lambda m, n, k: (m, k))), (win_g, pl.BlockSpec((None, tn1, n_in), lambda m, n, k: (k, n, 0)))],
        [(0, 1, "nt", 0)],
        [(jax.ShapeDtypeStruct((t, d), F32), pl.BlockSpec((tm, tn1), lambda m, n, k: (m, n)))],
        (t // tm, d // tn1, N_DEV), (tm, tn1), comms=[r2_in])[0]
    grad_x, g_pre_mix = _prenorm_bwd("prenorm_bwd_mix", d1, dh1, xs, pre_mix_w)

    reduced = {
        "w_in": r2_in.results[0], "w_out": r2_o.results[0], "w_gate": r2_g.results[0], "w_up": r2_u.results[0],
        "w_ple": r2_p.results[1], "w_ple_gate": r2_p.results[0],
    }
    small_grad = {
        "pre_mix_w": g_pre_mix, "lb_param": _lb_bwd(lb_param, dlb), "a_norm_w": g_a_norm, "gmlp_ln_w": g_ln_w,
        "gmlp_ln_b": g_ln_b, "w_spatial": g_wsp, "b_spatial": g_bs_t.T, "post_mix_w": g_post_mix,
        "pre_ffn_w": g_pre_ffn, "post_ffn_w": g_post_ffn, "post_ple_w": g_post_ple,
    }
    pack = lambda get: jnp.concatenate([get(n).reshape(-1, LANE) for n in small_names], axis=0)
    ag_small = _ag_comm([pack(lambda n: small_grad[n])], mid_frac=0.5)
    for n in big_names:
        if n != "w_down":
            adam(n, reduced[n], [ag_small] if n == "w_in" else [])
    g_all = ag_small.results[0]
    sg, sd, sm, sv = _adamw("adamw_small", g_all, pack(lambda n: W[n]), pack(lambda n: M[n]), pack(lambda n: V[n]))
    off = 0
    for n in small_names:
        rows = W[n].size // LANE
        for src, dst in ((sg, grads), (sd, deltas), (sm, new_m), (sv, new_v)):
            dst[n] = src[off:off + rows].reshape(W[n].shape)
        off += rows

    loss = lax.psum(loss_part[0, 0], ("x", "y", "c"))
    return (loss, grad_x[None], *[grads[n] for n in all_names], *[deltas[n] for n in all_names],
            *[new_m[n] for n in all_names], *[new_v[n] for n in all_names])
```

```python
import functools

import numpy as np
import jax
import jax.numpy as jnp
from jax import lax
from jax.experimental import pallas as pl
from jax.experimental.pallas import tpu as pltpu

F32 = jnp.float32
BF16 = jnp.bfloat16

EPS = 1e-6
HEAD = 128
GLA_CHUNK = 64
GMLP_CHUNK = 128
N_DEV = 8
N_CHIP = 4
LANE = 128
VMEM_LIMIT = 56 * 1024 * 1024
HGRN_ROWS = 512
ROW_TILE = 128
EPILOGUE_ROWS = 256
HGRN_UNROLL = 1
HGRN_HEADS = 8

ADAM_LR = 0.001
ADAM_B1 = 0.9
ADAM_B2 = 0.999
ADAM_EPS = 1e-08
ADAM_WD = 0.01
ADAM_STEP = 10

MESH = pl.DeviceIdType.MESH
ANY = pl.BlockSpec(memory_space=pl.ANY)

_DIMS = {
    "nn": (((1,), (0,)), ((), ())),
    "nt": (((1,), (1,)), ((), ())),
    "tn": (((0,), (0,)), ((), ())),
}


def _tile(dim, pref):
    return pref if dim % pref == 0 else dim


def _rows(r, bytes_per_row, budget=18 * 1024 * 1024, mult=16):
    best = None
    for cand in range(mult, r + 1, mult):
        if r % cand == 0 and cand * bytes_per_row <= budget:
            best = cand
    return best if best is not None else r


def _cparams(n_axes):
    return pltpu.CompilerParams(dimension_semantics=("arbitrary",) * n_axes, vmem_limit_bytes=VMEM_LIMIT)


def _dot(a, b, form="nn"):
    return lax.dot_general(a.astype(BF16), b.astype(BF16), _DIMS[form], preferred_element_type=F32)


def _split3(x):
    hi = x.astype(BF16)
    r = x - hi.astype(F32)
    mid = r.astype(BF16)
    lo = (r - mid.astype(F32)).astype(BF16)
    return hi, mid, lo


def _dot_exact_l(c, x):
    hi, mid, lo = _split3(x)
    d = lambda y: lax.dot_general(c, y, _DIMS["nn"], preferred_element_type=F32)
    return d(hi) + d(mid) + d(lo)


def _dot_exact_r(x, c):
    hi, mid, lo = _split3(x)
    d = lambda y: lax.dot_general(y, c, _DIMS["nn"], preferred_element_type=F32)
    return d(hi) + d(mid) + d(lo)


def _sigmoid(x):
    return 1.0 / (1.0 + jnp.exp(-x))


def _gelu(x):
    return 0.5 * x * (1.0 + lax.erf(x * 0.7071067811865476))


def _gelu_grad(x):
    cdf = 0.5 * (1.0 + lax.erf(x * 0.7071067811865476))
    pdf = jnp.exp(-0.5 * x * x) * 0.3989422804014327
    return cdf + x * pdf


def _position():
    return lax.axis_index("x"), lax.axis_index("y"), lax.axis_index("c")


def _linear_step(grid):
    step = 0
    for ax, n in enumerate(grid):
        step = step * n + pl.program_id(ax)
    return step


class _Comm:
    def __init__(self, arrays, out_shapes, sem_shapes, phases):
        self.arrays, self.out_shapes, self.sem_shapes, self.phases = list(arrays), list(out_shapes), list(sem_shapes), phases
        self.results = None


class _CommLayout:
    def __init__(self, comms, space=pl.ANY):
        self.comms = list(comms)
        self.arrays = [a for c in self.comms for a in c.arrays]
        self.out_shapes = [s for c in self.comms for s in c.out_shapes]
        self.sem_shapes = [s for c in self.comms for s in c.sem_shapes]
        self.n_in, self.n_out = len(self.arrays), len(self.out_shapes)
        self.in_specs = [pl.BlockSpec(memory_space=space)] * self.n_in
        self.out_specs = [pl.BlockSpec(memory_space=space)] * self.n_out

    def run(self, cin, cout, csem, step, n_steps, post):
        i = o = s = 0
        for c in self.comms:
            ins, outs, sems = cin[i:i + len(c.arrays)], cout[o:o + len(c.out_shapes)], csem[s:s + len(c.sem_shapes)]
            i, o, s = i + len(c.arrays), o + len(c.out_shapes), s + len(c.sem_shapes)
            for frac, fn in c.phases:
                if (frac is None) != post:
                    continue
                due = n_steps - 1 if frac is None else max(0, min(int(frac * n_steps), n_steps - 2))
                if n_steps == 1:
                    fn(ins, outs, sems)
                else:
                    pl.when(step == due)(functools.partial(fn, ins, outs, sems))

    def deliver(self, results):
        o = 0
        for c in self.comms:
            c.results = list(results[o:o + len(c.out_shapes)])
            o += len(c.out_shapes)


def _comm_only(name, comms, in_vmem=False):
    lay = _CommLayout(comms, pltpu.VMEM if in_vmem else pl.ANY)

    def body(*refs):
        cin, cout, csem = refs[:lay.n_in], refs[lay.n_in:lay.n_in + lay.n_out], refs[lay.n_in + lay.n_out:]
        lay.run(cin, cout, csem, 0, 1, post=False)
        lay.run(cin, cout, csem, 0, 1, post=True)

    res = pl.pallas_call(
        body, name=name, in_specs=lay.in_specs, out_specs=lay.out_specs, out_shape=lay.out_shapes,
        scratch_shapes=lay.sem_shapes,
    )(*lay.arrays)
    lay.deliver(res)


def _ag_comm(shards, mid_frac=0.0, start_frac=0.0):
    n = len(shards)
    per = N_DEV - 1

    def tools(ins, outs, sems):
        send_sems, recv_sems, local_sems = sems
        x, y, c = _position()
        me, sibling = (x, y, c), (x, y, 1 - c)
        chips = [(1 - x, y), (x, 1 - y), (1 - x, 1 - y)]

        def copy(a, k, block, to, from_shard=False):
            dst = outs[a].at[4 * block[0] + 2 * block[1] + block[2]]
            return pltpu.make_async_remote_copy(
                src_ref=ins[a] if from_shard else dst, dst_ref=dst,
                send_sem=send_sems.at[a * per + k], recv_sem=recv_sems.at[a * per + k],
                device_id=to, device_id_type=MESH)

        def local(a):
            return pltpu.make_async_copy(ins[a], outs[a].at[4 * x + 2 * y + c], local_sems.at[a])

        return me, sibling, chips, c, copy, local

    def first(ins, outs, sems):
        me, sibling, chips, c, copy, local = tools(ins, outs, sems)
        for a in range(n):
            local(a).start()
            copy(a, 0, me, sibling, True).start()
            for j, chip in enumerate(chips):
                copy(a, 1 + j, me, (*chip, c), True).start()

    def middle(ins, outs, sems):
        me, sibling, chips, c, copy, local = tools(ins, outs, sems)
        for a in range(n):
            for j, chip in enumerate(chips):
                copy(a, 1 + j, (*chip, c), me).wait_recv()
                copy(a, 4 + j, (*chip, c), sibling).start()

    def last(ins, outs, sems):
        me, sibling, chips, c, copy, local = tools(ins, outs, sems)
        for a in range(n):
            copy(a, 0, sibling, me).wait_recv()
            copy(a, 0, me, sibling, True).wait_send()
            for j, chip in enumerate(chips):
                copy(a, 4 + j, (*chip, 1 - c), me).wait_recv()
                copy(a, 1 + j, me, (*chip, c), True).wait_send()
                copy(a, 4 + j, (*chip, c), sibling).wait_send()
            local(a).wait()

    return _Comm(
        shards, [jax.ShapeDtypeStruct((N_DEV,) + s.shape, s.dtype) for s in shards],
        [pltpu.SemaphoreType.DMA((n * per,)), pltpu.SemaphoreType.DMA((n * per,)), pltpu.SemaphoreType.DMA((n,))],
        [(start_frac, first), (max(mid_frac, start_frac), middle), (None, last)])


def _pair_comm(grads):
    n = len(grads)

    def copies(ins, outs, sems):
        send_sems, recv_sems = sems
        x, y, c = _position()
        return [pltpu.make_async_remote_copy(
            src_ref=ins[a].at[k, 1 - c], dst_ref=outs[a].at[k],
            send_sem=send_sems.at[a * N_CHIP + k], recv_sem=recv_sems.at[a * N_CHIP + k],
            device_id=(x, y, 1 - c), device_id_type=MESH) for a in range(n) for k in range(N_CHIP)]

    def first(ins, outs, sems):
        for cp in copies(ins, outs, sems):
            cp.start()

    def last(ins, outs, sems):
        for cp in copies(ins, outs, sems):
            cp.wait()

    return _Comm(
        grads, [jax.ShapeDtypeStruct((N_CHIP,) + g.shape[2:], g.dtype) for g in grads],
        [pltpu.SemaphoreType.DMA((n * N_CHIP,)), pltpu.SemaphoreType.DMA((n * N_CHIP,))],
        [(0.0, first), (None, last)])


def _chip_comm(sums):
    n = len(sums)
    per = N_CHIP - 1

    def copies(ins, outs, sems):
        send_sems, recv_sems, local_sems = sems
        x, y, c = _position()
        my_chip = 2 * x + y
        cps = []
        for a in range(n):
            cps.append(pltpu.make_async_copy(ins[a].at[my_chip], outs[a].at[my_chip], local_sems.at[a]))
            for j, (px, py) in enumerate([(1 - x, y), (x, 1 - y), (1 - x, 1 - y)]):
                cps.append(pltpu.make_async_remote_copy(
                    src_ref=ins[a].at[2 * px + py], dst_ref=outs[a].at[my_chip],
                    send_sem=send_sems.at[a * per + j], recv_sem=recv_sems.at[a * per + j],
                    device_id=(px, py, c), device_id_type=MESH))
        return cps

    def first(ins, outs, sems):
        for cp in copies(ins, outs, sems):
            cp.start()

    def last(ins, outs, sems):
        for cp in copies(ins, outs, sems):
            cp.wait()

    return _Comm(
        sums, [jax.ShapeDtypeStruct(s.shape, s.dtype) for s in sums],
        [pltpu.SemaphoreType.DMA((n * per,)), pltpu.SemaphoreType.DMA((n * per,)), pltpu.SemaphoreType.DMA((n,))],
        [(0.0, first), (None, last)])


def _matmul(name, operands, pairs, outs, grid, acc_shape, n_slots=1, epilogue=None, comms=()):
    used = sorted({i for p in pairs for i in p[:2]})
    n_op = len(operands)
    n_out = len(outs)
    k_axis = len(grid) - 1
    n_k = grid[-1]
    lay = _CommLayout(comms)

    direct = n_k == 1 and epilogue is None
    n_acc = 0 if direct else 1

    def body(*refs):
        ops = refs[:n_op]
        out_refs = refs[n_op + lay.n_in:n_op + lay.n_in + n_out]
        acc = None if direct else refs[n_op + lay.n_in + n_out + lay.n_out]
        k = pl.program_id(k_axis)
        step = _linear_step(grid)
        cin = refs[n_op:n_op + lay.n_in]
        cout = refs[n_op + lay.n_in + n_out:n_op + lay.n_in + n_out + lay.n_out]
        csem = refs[n_op + lay.n_in + n_out + lay.n_out + n_acc:]
        lay.run(cin, cout, csem, step, int(np.prod(grid)), post=False)

        if n_k > 1:
            @pl.when(k == 0)
            def _():
                acc[...] = jnp.zeros_like(acc)

        vals = {i: ops[i][...] for i in used}
        vals = {i: (v if v.dtype == BF16 else v.astype(BF16)) for i, v in vals.items()}
        for s in range(n_slots):
            tot = None
            for ia, ib, form, slot in pairs:
                if slot != s:
                    continue
                d = lax.dot_general(vals[ia], vals[ib], _DIMS[form], preferred_element_type=F32)
                tot = d if tot is None else tot + d
            if direct:
                out_refs[s][...] = tot.astype(out_refs[s].dtype)
            elif n_k == 1:
                acc[s] = tot
            else:
                acc[s] += tot

        def finish():
            rows = acc_shape[0]
            chunk = EPILOGUE_ROWS if (epilogue is not None and rows % EPILOGUE_ROWS == 0) else rows
            for r0 in range(0, rows, chunk):
                sl = slice(r0, r0 + chunk)
                accs = [acc[s, sl, :] for s in range(n_slots)]
                extra = [ops[i][sl, :] for i in range(n_op) if i not in used]
                res = epilogue(accs, *extra) if epilogue is not None else accs
                for o, v in zip(out_refs, res):
                    o[sl, :] = v.astype(o.dtype)

        if n_k > 1:
            pl.when(k == n_k - 1)(finish)
        elif not direct:
            finish()

        lay.run(cin, cout, csem, step, int(np.prod(grid)), post=True)

    res = pl.pallas_call(
        body,
        name=name,
        grid=grid,
        in_specs=[s for _, s in operands] + lay.in_specs,
        out_specs=[s for _, s in outs] + lay.out_specs,
        out_shape=[s for s, _ in outs] + lay.out_shapes,
        scratch_shapes=([] if direct else [pltpu.VMEM((n_slots,) + tuple(acc_shape), F32)]) + lay.sem_shapes,
        compiler_params=_cparams(len(grid)),
    )(*[a for a, _ in operands], *lay.arrays)
    lay.deliver(res[n_out:])
    return res[:n_out]


def _mm_plain(name, a, b, form, out_dtype, tm, tn, tk, extra=None, epilogue=None, comms=()):
    if form == "nn":
        (M, K), N = a.shape, b.shape[1]
    elif form == "nt":
        (M, K), N = a.shape, b.shape[0]
    else:
        (K, M), N = a.shape, b.shape[1]
    tm, tn, tk = _tile(M, tm), _tile(N, tn), _tile(K, tk)
    a_spec = pl.BlockSpec((tk, tm), lambda m, n, k: (k, m)) if form == "tn" else pl.BlockSpec((tm, tk), lambda m, n, k: (m, k))
    b_spec = pl.BlockSpec((tn, tk), lambda m, n, k: (n, k)) if form == "nt" else pl.BlockSpec((tk, tn), lambda m, n, k: (k, n))
    operands = [(a, a_spec), (b, b_spec)]
    if extra is not None:
        operands.append((extra, pl.BlockSpec((tm, tn), lambda m, n, k: (m, n))))
    out = (jax.ShapeDtypeStruct((M, N), out_dtype), pl.BlockSpec((tm, tn), lambda m, n, k: (m, n)))
    return _matmul(name, operands, [(0, 1, form, 0)], [out], (M // tm, N // tn, K // tk), (tm, tn), epilogue=epilogue,
                   comms=comms)[0]


def _ag_matmul(name, a, shard, form, out_dtype, blocked_out, comms=()):
    t, kdim = a.shape
    nl = shard.shape[1] if form == "nn" else shard.shape[0]
    tm = _tile(t, 1024)
    grid = (N_DEV, t // tm)
    n_steps = grid[0] * grid[1]
    per = N_DEV - 1
    lay = _CommLayout(comms)
    ident = lambda bx, by, bc: 4 * bx + 2 * by + bc

    def block_of(n):
        x, y, c = _position()
        far = jnp.where(n < 2, 0, jnp.where(n < 5, n - 1, n - 4))
        bx = jnp.where((far == 1) | (far == 3), 1 - x, x)
        by = jnp.where(far >= 2, 1 - y, y)
        bc = jnp.where((n == 0) | ((n >= 2) & (n < 5)), c, 1 - c)
        return ident(bx, by, bc)

    def body(*refs):
        a_ref, shard_ref = refs[:2]
        cin = refs[2:2 + lay.n_in]
        out_ref, land_ref = refs[2 + lay.n_in:4 + lay.n_in]
        cout = refs[4 + lay.n_in:4 + lay.n_in + lay.n_out]
        bbuf, send_sems, recv_sems, misc_sems = refs[4 + lay.n_in + lay.n_out:8 + lay.n_in + lay.n_out]
        csem = refs[8 + lay.n_in + lay.n_out:]
        n, m = pl.program_id(0), pl.program_id(1)
        step = n * grid[1] + m
        x, y, c = _position()
        me, sibling = (x, y, c), (x, y, 1 - c)
        chips = [(1 - x, y), (x, 1 - y), (1 - x, 1 - y)]

        def copy(k, block, to, from_shard=False):
            dst = land_ref.at[ident(*block)]
            return pltpu.make_async_remote_copy(
                src_ref=shard_ref if from_shard else dst, dst_ref=dst,
                send_sem=send_sems.at[k], recv_sem=recv_sems.at[k], device_id=to, device_id_type=MESH)

        local = pltpu.make_async_copy(shard_ref, land_ref.at[ident(*me)], misc_sems.at[0])

        def load(src):
            cp = pltpu.make_async_copy(src, bbuf, misc_sems.at[1])
            cp.start()
            cp.wait()

        @pl.when(step == 0)
        def _():
            local.start()
            copy(0, me, sibling, True).start()
            for j, chip in enumerate(chips):
                copy(1 + j, me, (*chip, c), True).start()

        lay.run(cin, cout, csem, step, n_steps, post=False)

        @pl.when(m == 0)
        def _():
            @pl.when(n == 0)
            def _():
                load(shard_ref)

            @pl.when(n == 1)
            def _():
                copy(0, sibling, me).wait_recv()
                load(land_ref.at[ident(*sibling)])

            for j, chip in enumerate(chips):
                @pl.when(n == 2 + j)
                def _(j=j, chip=chip):
                    copy(1 + j, (*chip, c), me).wait_recv()
                    copy(4 + j, (*chip, c), sibling).start()
                    load(land_ref.at[ident(*chip, c)])

                @pl.when(n == 5 + j)
                def _(j=j, chip=chip):
                    copy(4 + j, (*chip, 1 - c), me).wait_recv()
                    load(land_ref.at[ident(*chip, 1 - c)])

        out_ref[...] = lax.dot_general(a_ref[...], bbuf[...], _DIMS[form], preferred_element_type=F32).astype(out_ref.dtype)

        @pl.when(step == n_steps - 1)
        def _():
            copy(0, me, sibling, True).wait_send()
            for j, chip in enumerate(chips):
                copy(1 + j, me, (*chip, c), True).wait_send()
                copy(4 + j, (*chip, c), sibling).wait_send()
            local.wait()

        lay.run(cin, cout, csem, step, n_steps, post=True)

    if blocked_out:
        out_shape = jax.ShapeDtypeStruct((N_DEV, t, nl), out_dtype)
        out_spec = pl.BlockSpec((None, tm, nl), lambda n, m: (block_of(n), m, 0))
    else:
        out_shape = jax.ShapeDtypeStruct((t, N_DEV * nl), out_dtype)
        out_spec = pl.BlockSpec((tm, nl), lambda n, m: (m, block_of(n)))
    res = pl.pallas_call(
        body, name=name, grid=grid,
        in_specs=[pl.BlockSpec((tm, kdim), lambda n, m: (m, 0)), ANY] + lay.in_specs,
        out_specs=[out_spec, ANY] + lay.out_specs,
        scratch_shapes=[pltpu.VMEM(shard.shape, shard.dtype), pltpu.SemaphoreType.DMA((per,)),
                        pltpu.SemaphoreType.DMA((per,)), pltpu.SemaphoreType.DMA((2,))] + lay.sem_shapes,
        out_shape=[out_shape, jax.ShapeDtypeStruct((N_DEV,) + shard.shape, shard.dtype)] + lay.out_shapes,
        compiler_params=_cparams(2),
    )(a, shard, *lay.arrays)
    lay.deliver(res[2:])
    return res[0], res[1]


def _cast_bf16(name, w):
    r, c = w.shape
    tr = _rows(r, 6 * c)

    def body(w_ref, o_ref):
        o_ref[...] = w_ref[...].astype(BF16)

    return pl.pallas_call(
        body, name=name, grid=(r // tr,),
        in_specs=[pl.BlockSpec((tr, c), lambda i: (i, 0))],
        out_specs=pl.BlockSpec((tr, c), lambda i: (i, 0)),
        out_shape=jax.ShapeDtypeStruct((r, c), BF16),
        compiler_params=_cparams(1),
    )(w)


def _rms_stats(x):
    r = lax.rsqrt(jnp.mean(x * x, axis=-1, keepdims=True) + EPS)
    return x * r, r


def _rms_bwd(xhat, r, w, dy):
    dxh = dy * w
    return r * (dxh - xhat * jnp.mean(dxh * xhat, axis=-1, keepdims=True))


def _row_spec(tr, d):
    return pl.BlockSpec((tr, d), lambda i: (i, 0))


def _vec_spec(d):
    return pl.BlockSpec((1, d), lambda i: (0, 0))


def _rms_fwd(name, x, w):
    t, d = x.shape
    tr = _tile(t, ROW_TILE)

    def body(x_ref, w_ref, h_ref):
        xh, _ = _rms_stats(x_ref[...])
        h_ref[...] = (xh * w_ref[...]).astype(BF16)

    return pl.pallas_call(
        body, name=name, grid=(t // tr,),
        in_specs=[_row_spec(tr, d), _vec_spec(d)],
        out_specs=_row_spec(tr, d),
        out_shape=jax.ShapeDtypeStruct((t, d), BF16),
        compiler_params=_cparams(1),
    )(x, w)


def _resid_rms(name, xres, y, w_post, w_next):
    t, d = xres.shape
    tr = _tile(t, ROW_TILE)
    has_next = w_next is not None

    def body(*refs):
        if has_next:
            x_ref, y_ref, wp_ref, wn_ref, xo_ref, h_ref = refs
        else:
            x_ref, y_ref, wp_ref, xo_ref, h_ref = refs
        yh, _ = _rms_stats(y_ref[...])
        xn = x_ref[...] + yh * wp_ref[...]
        xo_ref[...] = xn
        if has_next:
            xh, _ = _rms_stats(xn)
            h_ref[...] = (xh * wn_ref[...]).astype(BF16)
        else:
            h_ref[...] = xn.astype(BF16)

    ins = [xres, y, w_post] + ([w_next] if has_next else [])
    in_specs = [_row_spec(tr, d), _row_spec(tr, d), _vec_spec(d)] + ([_vec_spec(d)] if has_next else [])
    return pl.pallas_call(
        body, name=name, grid=(t // tr,),
        in_specs=in_specs,
        out_specs=[_row_spec(tr, d), _row_spec(tr, d)],
        out_shape=[jax.ShapeDtypeStruct((t, d), F32), jax.ShapeDtypeStruct((t, d), BF16)],
        compiler_params=_cparams(1),
    )(*ins)


def _ple_loss(name, x2, pe, pgl, w_pp, tgt):
    t, d = x2.shape
    tr = _tile(t, ROW_TILE)

    def body(x2_ref, pe_ref, pgl_ref, w_ref, tgt_ref, loss_ref, d3_ref, dpe_ref, dpgl_ref, dw_ref):
        @pl.when(pl.program_id(0) == 0)
        def _():
            loss_ref[...] = jnp.zeros_like(loss_ref)
            dw_ref[...] = jnp.zeros_like(dw_ref)

        pe_v = pe_ref[...]
        s = _sigmoid(pgl_ref[...])
        y = pe_v * s
        yh, r = _rms_stats(y)
        w = w_ref[...]
        err = x2_ref[...] + yh * w - tgt_ref[...]
        loss_ref[...] += 0.5 * jnp.sum(jnp.mean(err * err, axis=-1, keepdims=True), axis=0, keepdims=True)
        d3 = err * (1.0 / d)
        d3_ref[...] = d3
        dw_ref[...] += jnp.sum(d3 * yh, axis=0, keepdims=True)
        dy = _rms_bwd(yh, r, w, d3)
        dpe_ref[...] = (dy * s).astype(BF16)
        dpgl_ref[...] = (dy * pe_v * s * (1.0 - s)).astype(BF16)

    return pl.pallas_call(
        body, name=name, grid=(t // tr,),
        in_specs=[_row_spec(tr, d), _row_spec(tr, d), _row_spec(tr, d), _vec_spec(d), _row_spec(tr, d)],
        out_specs=[pl.BlockSpec((1, 1), lambda i: (0, 0)), _row_spec(tr, d), _row_spec(tr, d), _row_spec(tr, d), _vec_spec(d)],
        out_shape=[jax.ShapeDtypeStruct((1, 1), F32), jax.ShapeDtypeStruct((t, d), F32),
                   jax.ShapeDtypeStruct((t, d), BF16), jax.ShapeDtypeStruct((t, d), BF16),
                   jax.ShapeDtypeStruct((1, d), F32)],
        compiler_params=_cparams(1),
    )(x2, pe, pgl, w_pp, tgt)


def _norm_bwd(name, dres, y, w_post):
    t, d = dres.shape
    tr = _tile(t, ROW_TILE)

    def body(d_ref, y_ref, w_ref, dy_ref, dw_ref):
        @pl.when(pl.program_id(0) == 0)
        def _():
            dw_ref[...] = jnp.zeros_like(dw_ref)

        dv = d_ref[...]
        yh, r = _rms_stats(y_ref[...])
        dw_ref[...] += jnp.sum(dv * yh, axis=0, keepdims=True)
        dy_ref[...] = _rms_bwd(yh, r, w_ref[...], dv).astype(BF16)

    return pl.pallas_call(
        body, name=name, grid=(t // tr,),
        in_specs=[_row_spec(tr, d), _row_spec(tr, d), _vec_spec(d)],
        out_specs=[_row_spec(tr, d), _vec_spec(d)],
        out_shape=[jax.ShapeDtypeStruct((t, d), BF16), jax.ShapeDtypeStruct((1, d), F32)],
        compiler_params=_cparams(1),
    )(dres, y, w_post)


def _prenorm_bwd(name, dres, dh, xin, w_pre, y=None, w_post=None):
    t, d = dres.shape
    tr = _tile(t, ROW_TILE)
    two = y is not None

    def body(*refs):
        if two:
            d_ref, dh_ref, x_ref, wpre_ref, y_ref, wpost_ref, do_ref, dwpre_ref, dy_ref, dwpost_ref = refs
        else:
            d_ref, dh_ref, x_ref, wpre_ref, do_ref, dwpre_ref = refs

        @pl.when(pl.program_id(0) == 0)
        def _():
            dwpre_ref[...] = jnp.zeros_like(dwpre_ref)
            if two:
                dwpost_ref[...] = jnp.zeros_like(dwpost_ref)

        dhv = dh_ref[...]
        xh, r = _rms_stats(x_ref[...])
        dwpre_ref[...] += jnp.sum(dhv * xh, axis=0, keepdims=True)
        dout = d_ref[...] + _rms_bwd(xh, r, wpre_ref[...], dhv)
        do_ref[...] = dout
        if two:
            yh, ry = _rms_stats(y_ref[...])
            dwpost_ref[...] += jnp.sum(dout * yh, axis=0, keepdims=True)
            dy_ref[...] = _rms_bwd(yh, ry, wpost_ref[...], dout).astype(BF16)

    ins = [dres, dh, xin, w_pre] + ([y, w_post] if two else [])
    in_specs = [_row_spec(tr, d)] * 3 + [_vec_spec(d)] + ([_row_spec(tr, d), _vec_spec(d)] if two else [])
    out_specs = [_row_spec(tr, d), _vec_spec(d)] + ([_row_spec(tr, d), _vec_spec(d)] if two else [])
    out_shape = [jax.ShapeDtypeStruct((t, d), F32), jax.ShapeDtypeStruct((1, d), F32)]
    if two:
        out_shape += [jax.ShapeDtypeStruct((t, d), BF16), jax.ShapeDtypeStruct((1, d), F32)]
    return pl.pallas_call(
        body, name=name, grid=(t // tr,),
        in_specs=in_specs, out_specs=out_specs, out_shape=out_shape,
        compiler_params=_cparams(1),
    )(*ins)


_LEVELS = (32, 16, 8, 4, 2, 1)
_N_CUM = 3 + 2 * len(_LEVELS)


def _hgrn_constants():
    c = GLA_CHUNK
    idx = np.arange(c)
    t, r = idx[:, None], idx[None, :]
    mats = [(r <= t), (r > t), np.ones((c, c), bool)]
    lq, lk, masks = [], [], []
    for h in _LEVELS:
        blk, pos = idx // (2 * h), idx % (2 * h)
        mid = blk * 2 * h + h - 1
        upper, lower = pos >= h, pos < h
        lq.append(upper[:, None] & (r > mid[:, None]) & (r <= t))
        lk.append(lower[:, None] & (r > t) & (r <= mid[:, None]))
        masks.append((blk[:, None] == blk[None, :]) & upper[:, None] & lower[None, :])
    cum = np.concatenate(mats + lq + lk, axis=0).astype(np.float32)
    rev = (r >= t).astype(np.float32)
    return (jnp.asarray(cum, BF16), jnp.asarray(rev, BF16), jnp.asarray(np.stack(masks).astype(np.float32)))


def _hgrn_gates(qp, fp, lb):
    sq = _sigmoid(qp)
    q = qp * sq
    sg = _sigmoid(fp)
    f = lb + (1.0 - lb) * sg
    k = 1.0 - f
    logf = jnp.log(jnp.maximum(f, 1e-30))
    return q, sq, sg, f, k, logf


def _hgrn_decays(cum_ref, logf):
    c = GLA_CHUNK
    e = jnp.exp(_dot_exact_l(cum_ref[...], logf))
    part = lambda i: e[i * c:(i + 1) * c]
    n = len(_LEVELS)
    return part(0), part(1), part(2), [part(3 + i) for i in range(n)], [part(3 + n + i) for i in range(n)]


def _hgrn_fwd(proj, lb, nw, n_heads, comms=()):
    t = proj.shape[0]
    aw = n_heads * HEAD
    rb = _tile(t, HGRN_ROWS)
    c = GLA_CHUNK
    n_sub = rb // c
    cum, _, masks = _hgrn_constants()
    lay = _CommLayout(comms)
    hp = HGRN_HEADS if n_heads % HGRN_HEADS == 0 else 1
    wd = hp * HEAD
    grid = (n_heads // hp, t // rb)

    def body(*refs):
        q_ref, f_ref, i_ref, g_ref, lb_ref, nw_ref, cum_ref, m_ref = refs[:8]
        cin = refs[8:8 + lay.n_in]
        a_ref, o_ref, s_ref, sc_ref = refs[8 + lay.n_in:12 + lay.n_in]
        cout = refs[12 + lay.n_in:12 + lay.n_in + lay.n_out]
        st = refs[12 + lay.n_in + lay.n_out]
        csem = refs[13 + lay.n_in + lay.n_out:]
        step = _linear_step(grid)
        lay.run(cin, cout, csem, step, grid[0] * grid[1], post=False)

        @pl.when(pl.program_id(1) == 0)
        def _():
            st[...] = jnp.zeros_like(st)

        lbv = lb_ref[...]
        nwv = nw_ref[...]
        eye = (lax.broadcasted_iota(jnp.int32, (c, c), 0) == lax.broadcasted_iota(jnp.int32, (c, c), 1)).astype(F32)
        heads = range(hp)
        hs = lambda a, h: a[:, h * HEAD:(h + 1) * HEAD]

        def chunk(j, carry):
            rows = pl.ds(pl.multiple_of(j * c, c), c)
            q, _, _, _, k, logf = _hgrn_gates(q_ref[rows, :], f_ref[rows, :], lbv)
            v = i_ref[rows, :]
            eb, ebe, eend, eq, ek = _hgrn_decays(cum_ref, logf)
            qt, kt, qk = q * eb, k * ebe, q * k
            s_in = [st[h] for h in heads]
            for h in heads:
                s_ref[h, j] = s_in[h]
            inter = [_dot(hs(qt, h), s_in[h], "nt") for h in heads]
            for h in heads:
                st[h] = s_in[h] * hs(eend, h)[0:1] + _dot(hs(v, h), hs(kt, h), "tn")
            scores = [eye * jnp.sum(hs(qk, h), axis=-1, keepdims=True) for h in heads]
            for lvl in range(len(_LEVELS)):
                ql, kl = q * eq[lvl], k * ek[lvl]
                for h in heads:
                    scores[h] = scores[h] + m_ref[lvl] * _dot(hs(ql, h), hs(kl, h), "nt")
            gv = g_ref[rows, :]
            gate = nwv * (gv * _sigmoid(gv))
            for h in heads:
                sc_ref[h, rows, :] = scores[h]
                o = inter[h] + _dot(scores[h], hs(v, h))
                o_ref[rows, h * HEAD:(h + 1) * HEAD] = o
                r = lax.rsqrt(jnp.mean(o * o, axis=-1, keepdims=True) + EPS)
                a_ref[rows, h * HEAD:(h + 1) * HEAD] = (o * r * hs(gate, h)).astype(BF16)
            return carry

        lax.fori_loop(0, n_sub, chunk, 0, unroll=HGRN_UNROLL)
        lay.run(cin, cout, csem, step, grid[0] * grid[1], post=True)

    n_hb = n_heads // hp
    col = lambda base: pl.BlockSpec((rb, wd), lambda h, r: (r, base * n_hb + h))
    vec = pl.BlockSpec((1, wd), lambda h, r: (0, h))
    res = pl.pallas_call(
        body, name="hgrn2_fwd", grid=grid,
        in_specs=[col(0), col(1), col(2), col(3), vec, vec,
                  pl.BlockSpec(cum.shape, lambda h, r: (0, 0)), pl.BlockSpec(masks.shape, lambda h, r: (0, 0, 0))] + lay.in_specs,
        out_specs=[pl.BlockSpec((rb, wd), lambda h, r: (r, h)), pl.BlockSpec((rb, wd), lambda h, r: (r, h)),
                   pl.BlockSpec((hp, n_sub, HEAD, HEAD), lambda h, r: (h, r, 0, 0)),
                   pl.BlockSpec((hp, rb, c), lambda h, r: (h, r, 0))] + lay.out_specs,
        out_shape=[jax.ShapeDtypeStruct((t, aw), BF16), jax.ShapeDtypeStruct((t, aw), F32),
                   jax.ShapeDtypeStruct((n_heads, t // c, HEAD, HEAD), F32),
                   jax.ShapeDtypeStruct((n_heads, t, c), F32)] + lay.out_shapes,
        scratch_shapes=[pltpu.VMEM((hp, HEAD, HEAD), F32)] + lay.sem_shapes,
        compiler_params=_cparams(2),
    )(proj, proj, proj, proj, lb, nw, cum, masks, *lay.arrays)
    lay.deliver(res[4:])
    return res[:4]


def _hgrn_bwd(proj, lb, nw, o_raw, states, scores, dab, n_heads, comms=()):
    t = proj.shape[0]
    aw = n_heads * HEAD
    rb = _tile(t, HGRN_ROWS)
    c = GLA_CHUNK
    n_sub = rb // c
    n_rb = t // rb
    cum, rev, masks = _hgrn_constants()
    lay = _CommLayout(comms)
    hp = HGRN_HEADS if n_heads % HGRN_HEADS == 0 else 1
    wd = hp * HEAD
    grid = (n_heads // hp, n_rb)

    def body(*refs):
        q_ref, f_ref, i_ref, g_ref, lb_ref, nw_ref, o_ref, s_ref, sc_ref, da_ref, cum_ref, rev_ref, m_ref = refs[:13]
        cin = refs[13:13 + lay.n_in]
        dq_ref, df_ref, di_ref, dg_ref, dlb_ref, dnw_ref = refs[13 + lay.n_in:19 + lay.n_in]
        cout = refs[19 + lay.n_in:19 + lay.n_in + lay.n_out]
        dst = refs[19 + lay.n_in + lay.n_out]
        csem = refs[20 + lay.n_in + lay.n_out:]
        step = _linear_step(grid)
        lay.run(cin, cout, csem, step, grid[0] * grid[1], post=False)

        @pl.when(pl.program_id(1) == 0)
        def _():
            dst[...] = jnp.zeros_like(dst)
            dlb_ref[...] = jnp.zeros_like(dlb_ref)
            dnw_ref[...] = jnp.zeros_like(dnw_ref)

        lbv = lb_ref[...]
        nwv = nw_ref[...]
        ri = lax.broadcasted_iota(jnp.int32, (c, c), 0)
        ci = lax.broadcasted_iota(jnp.int32, (c, c), 1)
        eye = (ri == ci).astype(F32)
        causal = (ci <= ri).astype(F32)
        last_row = (lax.broadcasted_iota(jnp.int32, (c, wd), 0) == c - 1).astype(F32)
        heads = range(hp)
        hs = lambda a, h: a[:, h * HEAD:(h + 1) * HEAD]
        wide = lambda parts: parts[0] if hp == 1 else jnp.concatenate(parts, axis=1)

        def head_mean(a):
            return wide([jnp.broadcast_to(jnp.mean(hs(a, h), axis=-1, keepdims=True), (c, HEAD)) for h in heads])

        def chunk(jj, carry):
            j = n_sub - 1 - jj
            rows = pl.ds(pl.multiple_of(j * c, c), c)
            qp = q_ref[rows, :]
            q, sq, sg, f, k, logf = _hgrn_gates(qp, f_ref[rows, :], lbv)
            v = i_ref[rows, :]
            gv = g_ref[rows, :]
            eb, ebe, eend, eq, ek = _hgrn_decays(cum_ref, logf)
            s_in = [s_ref[h, j] for h in heads]
            a_sc = [sc_ref[h, rows, :] for h in heads]
            dsn = [dst[h] for h in heads]
            o = o_ref[rows, :]
            r = lax.rsqrt(head_mean(o * o) + EPS)
            oh = o * r
            sgg = _sigmoid(gv)
            sil = gv * sgg
            da = da_ref[rows, :]
            dg_ref[rows, :] = (da * oh * nwv * (sgg * (1.0 + gv * (1.0 - sgg)))).astype(BF16)
            dnw_ref[...] += jnp.sum(da * oh * sil, axis=0, keepdims=True)
            doh = da * nwv * sil
            do = r * (doh - oh * head_mean(doh * oh))
            kt = k * ebe
            qt = q * eb
            d_sc = [_dot(hs(do, h), hs(v, h), "nt") * causal for h in heads]
            dqt = wide([_dot(hs(do, h), s_in[h]) for h in heads])
            dkt = wide([_dot(hs(v, h), dsn[h]) for h in heads])
            for h in heads:
                dst[h] = dsn[h] * hs(eend, h)[0:1] + _dot(hs(do, h), hs(qt, h), "tn")
            di_ref[rows, :] = wide([_dot(a_sc[h], hs(do, h), "tn") + _dot(hs(kt, h), dsn[h], "nt") for h in heads]).astype(BF16)
            diag = wide([jnp.broadcast_to(jnp.sum(d_sc[h] * eye, axis=-1, keepdims=True), (c, HEAD)) for h in heads])
            dq = dqt * eb
            dk = dkt * ebe
            db = q * dq - k * dk
            dq = dq + diag * k
            dk = dk + diag * q
            for lvl in range(len(_LEVELS)):
                ql = (q * eq[lvl]).astype(BF16)
                kl = (k * ek[lvl]).astype(BF16)
                dm = [(m_ref[lvl] * d_sc[h]).astype(BF16) for h in heads]
                gq = wide([_dot(dm[h], hs(kl, h)) for h in heads])
                gk = wide([_dot(dm[h], hs(ql, h), "tn") for h in heads])
                dq = dq + gq * eq[lvl]
                dk = dk + gk * ek[lvl]
                db = db + ql.astype(F32) * gq - kl.astype(F32) * gk
            state_term = wide([jnp.sum(s_in[h] * dsn[h], axis=0, keepdims=True) for h in heads])
            extra = jnp.sum(dkt * kt, axis=0, keepdims=True) + eend[0:1] * state_term
            db = db + last_row * extra
            dlogf = _dot_exact_l(rev_ref[...], db)
            dfv = jnp.where(f > 1e-30, dlogf / f, 0.0) - dk
            df_ref[rows, :] = (dfv * (1.0 - lbv) * sg * (1.0 - sg)).astype(BF16)
            dlb_ref[...] += jnp.sum(dfv * (1.0 - sg), axis=0, keepdims=True)
            dq_ref[rows, :] = (dq * (sq * (1.0 + qp * (1.0 - sq)))).astype(BF16)
            return carry

        lax.fori_loop(0, n_sub, chunk, 0, unroll=HGRN_UNROLL)
        lay.run(cin, cout, csem, step, grid[0] * grid[1], post=True)

    n_hb = n_heads // hp
    col = lambda base: pl.BlockSpec((rb, wd), lambda h, r: (n_rb - 1 - r, base * n_hb + h))
    blk = pl.BlockSpec((rb, wd), lambda h, r: (n_rb - 1 - r, h))
    vec = pl.BlockSpec((1, wd), lambda h, r: (0, h))
    const = lambda a: pl.BlockSpec(a.shape, lambda h, r: (0,) * a.ndim)
    res = pl.pallas_call(
        body, name="hgrn2_bwd", grid=grid,
        in_specs=[col(0), col(1), col(2), col(3), vec, vec, blk,
                  pl.BlockSpec((hp, n_sub, HEAD, HEAD), lambda h, r: (h, n_rb - 1 - r, 0, 0)),
                  pl.BlockSpec((hp, rb, c), lambda h, r: (h, n_rb - 1 - r, 0)),
                  blk, const(cum), const(rev), const(masks)] + lay.in_specs,
        out_specs=[blk, blk, blk, blk, vec, vec] + lay.out_specs,
        out_shape=[jax.ShapeDtypeStruct((t, aw), BF16)] * 4 + [jax.ShapeDtypeStruct((1, aw), F32)] * 2 + lay.out_shapes,
        scratch_shapes=[pltpu.VMEM((hp, HEAD, HEAD), F32)] + lay.sem_shapes,
        compiler_params=_cparams(2),
    )(proj, proj, proj, proj, lb, nw, o_raw, states, scores, dab, cum, rev, masks, *lay.arrays)
    lay.deliver(res[6:])
    return res[:6]


def _lb_fwd(lb_param):
    def body(p_ref, o_ref):
        p = p_ref[...]
        e = jnp.exp(p - jnp.max(p, axis=0, keepdims=True))
        o_ref[...] = e[0:1] / jnp.sum(e, axis=0, keepdims=True)

    return pl.pallas_call(body, name="lb_fwd", out_shape=jax.ShapeDtypeStruct((1, lb_param.shape[1]), F32))(lb_param)


def _lb_bwd(lb_param, dlb):
    def body(p_ref, d_ref, o_ref):
        p = p_ref[...]
        e = jnp.exp(p - jnp.max(p, axis=0, keepdims=True))
        s = e / jnp.sum(e, axis=0, keepdims=True)
        first = (lax.broadcasted_iota(jnp.int32, p.shape, 0) == 0).astype(F32)
        o_ref[...] = d_ref[...] * s[0:1] * (first - s)

    return pl.pallas_call(body, name="lb_bwd", out_shape=jax.ShapeDtypeStruct(lb_param.shape, F32))(lb_param, dlb)


def _gmlp_norm(v, lnw, lnb):
    vf = _gelu(v)
    mu = jnp.mean(vf, axis=-1, keepdims=True)
    cen = vf - mu
    rstd = lax.rsqrt(jnp.mean(cen * cen, axis=-1, keepdims=True) + EPS)
    xh = cen * rstd
    return xh, rstd, xh * lnw + lnb


def _tril(n):
    return (lax.broadcasted_iota(jnp.int32, (n, n), 1) <= lax.broadcasted_iota(jnp.int32, (n, n), 0)).astype(F32)


def _gmlp_fwd(proj, lnw, lnb, w_sp, bs_t, n_groups, col_base):
    t = proj.shape[0]
    bw = n_groups * HEAD
    c = GMLP_CHUNK

    def body(u_ref, v_ref, lnw_ref, lnb_ref, w_ref, bs_ref, o_ref):
        tri = _tril(c)
        uf = _gelu(u_ref[...])
        _, _, vn = _gmlp_norm(v_ref[...], lnw_ref[...], lnb_ref[...])
        for g in range(n_groups):
            cols = slice(g * HEAD, (g + 1) * HEAD)
            z = _dot(w_ref[g] * tri, vn[:, cols]) + bs_ref[:, g:g + 1]
            o_ref[:, cols] = (uf[:, cols] * z).astype(BF16)

    blk = lambda b: pl.BlockSpec((c, bw), lambda n: (n, b))
    const = lambda a: pl.BlockSpec(a.shape, lambda n: (0,) * a.ndim)
    return pl.pallas_call(
        body, name="gmlp_fwd", grid=(t // c,),
        in_specs=[blk(col_base), blk(col_base + 1), const(lnw), const(lnb), const(w_sp), const(bs_t)],
        out_specs=pl.BlockSpec((c, bw), lambda n: (n, 0)),
        out_shape=jax.ShapeDtypeStruct((t, bw), BF16),
        compiler_params=_cparams(1),
    )(proj, proj, lnw, lnb, w_sp, bs_t)


def _gmlp_bwd(proj, lnw, lnb, w_sp, bs_t, dab, n_groups, col_base):
    t = proj.shape[0]
    bw = n_groups * HEAD
    c = GMLP_CHUNK
    n_steps = t // c
    sel = jnp.asarray((np.arange(bw)[:, None] // HEAD == np.arange(n_groups)[None, :]).astype(np.float32), BF16)

    def body(u_ref, v_ref, lnw_ref, lnb_ref, w_ref, bs_ref, d_ref, sel_ref,
             du_ref, dv_ref, dlnw_ref, dlnb_ref, dw_ref, dbs_ref, dz_acc, dvn_scr):
        step = pl.program_id(0)

        @pl.when(step == 0)
        def _():
            dlnw_ref[...] = jnp.zeros_like(dlnw_ref)
            dlnb_ref[...] = jnp.zeros_like(dlnb_ref)
            dw_ref[...] = jnp.zeros_like(dw_ref)
            dz_acc[...] = jnp.zeros_like(dz_acc)

        tri = _tril(c)
        u = u_ref[...]
        v = v_ref[...]
        uf = _gelu(u)
        lnw_v = lnw_ref[...]
        xh, rstd, vn = _gmlp_norm(v, lnw_v, lnb_ref[...])
        dbo = d_ref[...]
        dz = dbo * uf
        dz_acc[...] += dz
        for g in range(n_groups):
            cols = slice(g * HEAD, (g + 1) * HEAD)
            wg = w_ref[g] * tri
            z = _dot(wg, vn[:, cols]) + bs_ref[:, g:g + 1]
            du_ref[:, cols] = (dbo[:, cols] * z * _gelu_grad(u[:, cols])).astype(BF16)
            dvn_scr[:, cols] = _dot(wg, dz[:, cols], "tn")
            dw_ref[g] += tri * _dot(dz[:, cols], vn[:, cols], "nt")
        dvn = dvn_scr[...]
        dlnw_ref[...] += jnp.sum(dvn * xh, axis=0, keepdims=True)
        dlnb_ref[...] += jnp.sum(dvn, axis=0, keepdims=True)
        dxh = dvn * lnw_v
        dvf = rstd * (dxh - jnp.mean(dxh, axis=-1, keepdims=True) - xh * jnp.mean(dxh * xh, axis=-1, keepdims=True))
        dv_ref[...] = (dvf * _gelu_grad(v)).astype(BF16)

        @pl.when(step == n_steps - 1)
        def _():
            dbs_ref[...] = _dot_exact_r(dz_acc[...], sel_ref[...])

    blk = lambda b: pl.BlockSpec((c, bw), lambda n: (n, b))
    const = lambda a: pl.BlockSpec(a.shape, lambda n: (0,) * a.ndim)
    row = pl.BlockSpec((c, bw), lambda n: (n, 0))
    vec = pl.BlockSpec((1, bw), lambda n: (0, 0))
    return pl.pallas_call(
        body, name="gmlp_bwd", grid=(n_steps,),
        in_specs=[blk(col_base), blk(col_base + 1), const(lnw), const(lnb), const(w_sp), const(bs_t), blk(1), const(sel)],
        out_specs=[row, row, vec, vec, const(w_sp), const(bs_t)],
        out_shape=[jax.ShapeDtypeStruct((t, bw), BF16), jax.ShapeDtypeStruct((t, bw), BF16),
                   jax.ShapeDtypeStruct((1, bw), F32), jax.ShapeDtypeStruct((1, bw), F32),
                   jax.ShapeDtypeStruct(w_sp.shape, F32), jax.ShapeDtypeStruct(bs_t.shape, F32)],
        scratch_shapes=[pltpu.VMEM((c, bw), F32), pltpu.VMEM((c, bw), F32)],
        compiler_params=_cparams(1),
    )(proj, proj, lnw, lnb, w_sp, bs_t, dab, sel)


def _pair_sum(name, grad, other, core):
    _, _, r, c = grad.shape
    tr = _rows(r, 6 * c)

    def body(core_ref, g_ref, o_ref, out_ref):
        out_ref[...] = (g_ref[...].astype(F32) + o_ref[...].astype(F32)).astype(BF16)

    return pl.pallas_call(
        body, name=name,
        grid_spec=pltpu.PrefetchScalarGridSpec(
            num_scalar_prefetch=1, grid=(N_CHIP, r // tr),
            in_specs=[pl.BlockSpec((None, None, tr, c), lambda k, i, core_ref: (k, core_ref[0], i, 0)),
                      pl.BlockSpec((None, tr, c), lambda k, i, core_ref: (k, i, 0))],
            out_specs=pl.BlockSpec((None, tr, c), lambda k, i, core_ref: (k, i, 0))),
        out_shape=jax.ShapeDtypeStruct((N_CHIP, r, c), BF16),
        compiler_params=_cparams(2),
    )(core, grad, other)


def _adamw_math(w, g, m, v):
    m = ADAM_B1 * m + (1.0 - ADAM_B1) * g
    v = ADAM_B2 * v + (1.0 - ADAM_B2) * (g * g)
    m_hat = m / (1.0 - ADAM_B1 ** ADAM_STEP)
    v_hat = v / (1.0 - ADAM_B2 ** ADAM_STEP)
    delta = -ADAM_LR * (m_hat / (jnp.sqrt(v_hat) + ADAM_EPS) + ADAM_WD * w)
    return delta, m, v


def _adamw(name, parts, w, m, v, comms=()):
    n_parts, r, c = parts.shape
    tr = _rows(r, c * (n_parts * parts.dtype.itemsize + 28), mult=8)
    lay = _CommLayout(comms)
    grid = (r // tr,)

    def body(*refs):
        p_ref, w_ref, m_ref, v_ref = refs[:4]
        cin = refs[4:4 + lay.n_in]
        g_ref, d_ref, mo_ref, vo_ref = refs[4 + lay.n_in:8 + lay.n_in]
        cout = refs[8 + lay.n_in:8 + lay.n_in + lay.n_out]
        csem = refs[8 + lay.n_in + lay.n_out:]
        step = pl.program_id(0)
        lay.run(cin, cout, csem, step, grid[0], post=False)
        g = p_ref[0].astype(F32)
        for i in range(1, n_parts):
            g = g + p_ref[i].astype(F32)
        g_ref[...] = g
        d_ref[...], mo_ref[...], vo_ref[...] = _adamw_math(w_ref[...], g, m_ref[...], v_ref[...])
        lay.run(cin, cout, csem, step, grid[0], post=True)

    row = pl.BlockSpec((tr, c), lambda i: (i, 0))
    res = pl.pallas_call(
        body, name=name, grid=grid,
        in_specs=[pl.BlockSpec((n_parts, tr, c), lambda i: (0, i, 0)), row, row, row] + lay.in_specs,
        out_specs=[row] * 4 + lay.out_specs,
        out_shape=[jax.ShapeDtypeStruct((r, c), F32)] * 4 + lay.out_shapes,
        scratch_shapes=lay.sem_shapes,
        compiler_params=_cparams(1),
    )(parts, w, m, v, *lay.arrays)
    lay.deliver(res[4:])
    return res[:4]


def kernel(x, p, pre_mix_w, w_in, lb_param, a_norm_w, gmlp_ln_w, gmlp_ln_b, w_spatial, b_spatial, w_out, post_mix_w, pre_ffn_w, w_gate, w_up, w_down, post_ffn_w, w_ple, w_ple_gate, post_ple_w, loss_target, m_pre_mix_w, m_w_in, m_lb_param, m_a_norm_w, m_gmlp_ln_w, m_gmlp_ln_b, m_w_spatial, m_b_spatial, m_w_out, m_post_mix_w, m_pre_ffn_w, m_w_gate, m_w_up, m_w_down, m_post_ffn_w, m_w_ple, m_w_ple_gate, m_post_ple_w, v_pre_mix_w, v_w_in, v_lb_param, v_a_norm_w, v_gmlp_ln_w, v_gmlp_ln_b, v_w_spatial, v_b_spatial, v_w_out, v_post_mix_w, v_pre_ffn_w, v_w_gate, v_w_up, v_w_down, v_post_ffn_w, v_w_ple, v_w_ple_gate, v_post_ple_w):
    big_names = ["w_in", "w_out", "w_gate", "w_up", "w_down", "w_ple", "w_ple_gate"]
    small_names = ["pre_mix_w", "lb_param", "a_norm_w", "gmlp_ln_w", "gmlp_ln_b", "w_spatial", "b_spatial",
                   "post_mix_w", "pre_ffn_w", "post_ffn_w", "post_ple_w"]
    all_names = ["pre_mix_w", "w_in", "lb_param", "a_norm_w", "gmlp_ln_w", "gmlp_ln_b", "w_spatial", "b_spatial",
                 "w_out", "post_mix_w", "pre_ffn_w", "w_gate", "w_up", "w_down", "post_ffn_w", "w_ple", "w_ple_gate",
                 "post_ple_w"]
    env = dict(locals())
    W = {n: env[n] for n in all_names}
    M = {n: env["m_" + n] for n in all_names}
    V = {n: env["v_" + n] for n in all_names}

    xs = x[0]
    ps = p[0, 0]
    tgt = loss_target[0]
    t, d = xs.shape
    aw = a_norm_w.shape[1]
    bw = gmlp_ln_w.shape[1]
    n_heads, n_groups = aw // HEAD, bw // HEAD
    core = lax.axis_index("c").astype(jnp.int32).reshape(1)

    transposed = ("w_gate", "w_up")
    local = lambda a, n: jnp.swapaxes(a, 1, 2)[0] if n in transposed else a[0]
    unlocal = lambda a, n: jnp.swapaxes(a[None], 1, 2) if n in transposed else a[None]
    shard = {n: local(W[n], n) for n in big_names}
    bf = {n: _cast_bf16("cast_" + n, shard[n]) for n in big_names}
    n_in = bf["w_in"].shape[1]
    ffl = bf["w_gate"].shape[0]
    n_ple = bf["w_ple"].shape[1]
    ple = ps.shape[1]

    TM, TK = 1024, 1024
    tm = _tile(t, TM)
    tkd = _tile(d, TK)
    tn1 = _tile(d, 1024)

    h1 = _rms_fwd("rms_pre_mix", xs, pre_mix_w)
    ag_a = _ag_comm([bf["w_out"]], start_frac=0.625, mid_frac=0.93)
    proj, win_g = _ag_matmul("mm_proj", h1, bf["w_in"], "nn", F32, False, comms=[ag_a])
    wout_f = ag_a.results[0].reshape(d, d)
    lb = _lb_fwd(lb_param)
    ag_b = _ag_comm([bf["w_up"]], mid_frac=0.85)
    a_out, o_raw, states, scores = _hgrn_fwd(proj, lb, a_norm_w, n_heads, comms=[ag_b])
    wup_g = ag_b.results[0]
    bs_t = b_spatial[0].T
    w_sp = w_spatial[0]
    col_u = (4 * aw) // bw
    b_out = _gmlp_fwd(proj, gmlp_ln_w, gmlp_ln_b, w_sp, bs_t, n_groups, col_u)
    ab = jnp.concatenate([a_out, b_out], axis=1)
    mix = _mm_plain("mm_mix", ab, wout_f, "nn", F32, TM, 1024, d)
    x1, h2 = _resid_rms("resid_mix", xs, mix, post_mix_w, pre_ffn_w)

    def swiglu(accs, gate_v):
        gf = gate_v.astype(F32)
        return accs[0], gf * _sigmoid(gf) * accs[0]

    once = pl.Buffered(1)
    tmf = _tile(t, 512)
    blk3 = lambda: pl.BlockSpec((None, tmf, ffl), lambda j, m, k: (j, m, 0))
    gate, wgate_g = _ag_matmul("mm_ffn_gate", h2, bf["w_gate"], "nt", BF16, True)
    ag_d = _ag_comm([bf["w_down"]], mid_frac=0.85)
    up, act = _matmul(
        "mm_ffn_up",
        [(h2, pl.BlockSpec((tmf, d), lambda j, m, k: (m, 0))),
         (wup_g, pl.BlockSpec((None, ffl, d), lambda j, m, k: (j, 0, 0), pipeline_mode=once)),
         (gate, blk3())],
        [(0, 1, "nt", 0)],
        [(jax.ShapeDtypeStruct((N_DEV, t, ffl), BF16), blk3()) for _ in range(2)],
        (N_DEV, t // tmf, 1), (tmf, ffl), epilogue=swiglu, comms=[ag_d])
    wdown_g = ag_d.results[0]
    ag_e = _ag_comm([bf["w_ple_gate"], bf["w_ple"]], mid_frac=0.6)
    tn_d = _tile(d, 2048)
    ff = _matmul(
        "mm_ffn_down",
        [(act, pl.BlockSpec((None, tm, ffl), lambda m, n, k: (k, m, 0))),
         (wdown_g, pl.BlockSpec((None, ffl, tn_d), lambda m, n, k: (k, 0, n)))],
        [(0, 1, "nn", 0)],
        [(jax.ShapeDtypeStruct((t, d), F32), pl.BlockSpec((tm, tn_d), lambda m, n, k: (m, n)))],
        (t // tm, d // tn_d, N_DEV), (tm, tn_d), comms=[ag_e])[0]
    wpg_f = ag_e.results[0].reshape(d, d)
    wple_g = ag_e.results[1]
    x2, x2b = _resid_rms("resid_ffn", x1, ff, post_ffn_w, None)

    pgl = _mm_plain("mm_ple_gate", x2b, wpg_f, "nn", F32, TM, 1024, d)
    pe = _matmul(
        "mm_ple",
        [(ps, pl.BlockSpec((tm, ple), lambda m, n, k: (m, 0))), (wple_g, pl.BlockSpec((None, ple, n_ple), lambda m, n, k: (n, 0, 0)))],
        [(0, 1, "nn", 0)],
        [(jax.ShapeDtypeStruct((t, N_DEV * n_ple), F32), pl.BlockSpec((tm, n_ple), lambda m, n, k: (m, n)))],
        (t // tm, N_DEV, 1), (tm, n_ple))[0]
    loss_part, d3, dpe, dpgl, g_post_ple = _ple_loss("ple_loss", x2, pe, pgl, post_ple_w, tgt)

    tkt = _tile(t, TK)
    g_wple = _matmul(
        "mm_dw_ple",
        [(ps, pl.BlockSpec((tkt, ple), lambda n, k: (k, 0))), (dpe, pl.BlockSpec((tkt, n_ple), lambda n, k: (k, n)))],
        [(0, 1, "tn", 0)],
        [(jax.ShapeDtypeStruct((N_DEV, ple, n_ple), BF16), pl.BlockSpec((None, ple, n_ple), lambda n, k: (n, 0, 0)))],
        (N_DEV, t // tkt), (ple, n_ple))[0]
    g_wpg = _mm_plain("mm_dw_ple_gate", x2b, dpgl, "tn", BF16, TM, 1024, t)

    def by_chip(g):
        return g.reshape((N_CHIP, 2) + g.shape[-2:])

    def pair_sums(names, comm):
        return [_pair_sum("pair_sum_" + n, g, o, core) for n, g, o in zip(names, comm.arrays, comm.results)]

    r1_p = _pair_comm([by_chip(g_wpg.reshape(N_DEV, d // N_DEV, d)), by_chip(g_wple)])
    d2 = _mm_plain("mm_d_x2", dpgl, wpg_f, "nt", F32, TM, 512, d, extra=d3, epilogue=lambda accs, e: [accs[0] + e],
                   comms=[r1_p])
    r2_p = _chip_comm(pair_sums(["w_ple_gate", "w_ple"], r1_p))

    dff, g_post_ffn = _norm_bwd("norm_bwd_ffn", d2, ff, post_ffn_w)
    g_wdown = _matmul(
        "mm_dw_down",
        [(act, pl.BlockSpec((None, t, ffl), lambda j, n, k: (j, 0, 0), pipeline_mode=once)),
         (dff, pl.BlockSpec((t, tn1), lambda j, n, k: (0, n)))],
        [(0, 1, "tn", 0)],
        [(jax.ShapeDtypeStruct((N_DEV, ffl, d), BF16), pl.BlockSpec((None, ffl, tn1), lambda j, n, k: (j, 0, n)))],
        (N_DEV, d // tn1, 1), (ffl, tn1), comms=[r2_p])[0]
    r1_d = _pair_comm([by_chip(g_wdown)])

    def swiglu_bwd(accs, gate_v, up_v):
        dact = accs[0]
        gf = gate_v.astype(F32)
        sg = _sigmoid(gf)
        return dact * up_v.astype(F32) * (sg * (1.0 + gf * (1.0 - sg))), dact * (gf * sg)

    dgate, dup = _matmul(
        "mm_d_act",
        [(dff, pl.BlockSpec((tmf, d), lambda j, m, k: (m, 0))),
         (wdown_g, pl.BlockSpec((None, ffl, d), lambda j, m, k: (j, 0, 0), pipeline_mode=once)),
         (gate, blk3()), (up, blk3())],
        [(0, 1, "nt", 0)],
        [(jax.ShapeDtypeStruct((N_DEV, t, ffl), BF16), blk3()) for _ in range(2)],
        (N_DEV, t // tmf, 1), (tmf, ffl), epilogue=swiglu_bwd, comms=[r1_d])
    r2_d = _chip_comm(pair_sums(["w_down"], r1_d))
    tmd = _tile(d, TM)
    def dw_ffn(name, dy, comms):
        return _matmul(
            name,
            [(dy, pl.BlockSpec((None, t, ffl), lambda j, n, k: (j, 0, 0), pipeline_mode=once)),
             (h2, pl.BlockSpec((t, tn1), lambda j, n, k: (0, n)))],
            [(0, 1, "tn", 0)],
            [(jax.ShapeDtypeStruct((N_DEV, ffl, d), BF16), pl.BlockSpec((None, ffl, tn1), lambda j, n, k: (j, 0, n)))],
            (N_DEV, d // tn1, 1), (ffl, tn1), comms=comms)[0]

    g_wgate = dw_ffn("mm_dw_gate", dgate, [r2_d])
    r1_g = _pair_comm([by_chip(g_wgate)])
    g_wup = dw_ffn("mm_dw_up", dup, [r1_g])
    r2_g = _chip_comm(pair_sums(["w_gate"], r1_g))
    r1_u = _pair_comm([by_chip(g_wup)])
    tn1 = _tile(d, 1024)
    dh2 = _matmul(
        "mm_d_h2",
        [(dgate, pl.BlockSpec((None, tm, ffl), lambda m, n, k: (k, m, 0))),
         (wgate_g, pl.BlockSpec((None, ffl, tn1), lambda m, n, k: (k, 0, n))),
         (dup, pl.BlockSpec((None, tm, ffl), lambda m, n, k: (k, m, 0))),
         (wup_g, pl.BlockSpec((None, ffl, tn1), lambda m, n, k: (k, 0, n)))],
        [(0, 1, "nn", 0), (2, 3, "nn", 0)],
        [(jax.ShapeDtypeStruct((t, d), F32), pl.BlockSpec((tm, tn1), lambda m, n, k: (m, n)))],
        (t // tm, d // tn1, N_DEV), (tm, tn1), comms=[r2_g, r1_u])[0]
    r2_u = _chip_comm(pair_sums(["w_up"], r1_u))
    d1, g_pre_ffn, dmix, g_post_mix = _prenorm_bwd("prenorm_bwd_ffn", d2, dh2, x1, pre_ffn_w, mix, post_mix_w)

    g_wout = _mm_plain("mm_dw_out", ab, dmix, "tn", BF16, TM, 1024, t)
    r1_o = _pair_comm([by_chip(g_wout.reshape(N_DEV, d // N_DEV, d))])
    dab = _mm_plain("mm_d_ab", dmix, wout_f, "nt", F32, TM, 1024, d, comms=[r1_o])
    r2_o = _chip_comm(pair_sums(["w_out"], r1_o))
    dq, df, di, dg, dlb, g_a_norm = _hgrn_bwd(proj, lb, a_norm_w, o_raw, states, scores, dab, n_heads, comms=[r2_u])
    du, dv, g_ln_w, g_ln_b, g_wsp, g_bs_t = _gmlp_bwd(proj, gmlp_ln_w, gmlp_ln_b, w_sp, bs_t, dab, n_groups, col_u)
    dproj = jnp.concatenate([dq, df, di, dg, du, dv], axis=1)
    g_win = _matmul(
        "mm_dw_in",
        [(h1, pl.BlockSpec((t, tmd), lambda j, m, k: (0, m))),
         (dproj, pl.BlockSpec((t, n_in), lambda j, m, k: (0, j), pipeline_mode=once))],
        [(0, 1, "tn", 0)],
        [(jax.ShapeDtypeStruct((N_DEV, d, n_in), BF16), pl.BlockSpec((None, tmd, n_in), lambda j, m, k: (j, m, 0)))],
        (N_DEV, d // tmd, 1), (tmd, n_in), comms=[r2_o])[0]
    grads, deltas, new_m, new_v = {}, {}, {}, {}

    def adam(n, parts, comms=()):
        res = _adamw("adamw_" + n, parts, shard[n], local(M[n], n), local(V[n], n), comms=comms)
        grads[n], deltas[n], new_m[n], new_v[n] = (unlocal(a, n) for a in res)

    r1_in = _pair_comm([by_chip(g_win)])
    adam("w_down", r2_d.results[0], [r1_in])
    r2_in = _chip_comm(pair_sums(["w_in"], r1_in))
    dh1 = _matmul(
        "mm_d_h1",
        [(dproj, pl.BlockSpec((tm, n_in), lambda m, n, k: (m, k))), (win_g, pl.BlockSpec((None, tn1, n_in), lambda m, n, k: (k, n, 0)))],
        [(0, 1, "nt", 0)],
        [(jax.ShapeDtypeStruct((t, d), F32), pl.BlockSpec((tm, tn1), lambda m, n, k: (m, n)))],
        (t // tm, d // tn1, N_DEV), (tm, tn1), comms=[r2_in])[0]
    grad_x, g_pre_mix = _prenorm_bwd("prenorm_bwd_mix", d1, dh1, xs, pre_mix_w)

    reduced = {
        "w_in": r2_in.results[0], "w_out": r2_o.results[0], "w_gate": r2_g.results[0], "w_up": r2_u.results[0],
        "w_ple": r2_p.results[1], "w_ple_gate": r2_p.results[0],
    }
    small_grad = {
        "pre_mix_w": g_pre_mix, "lb_param": _lb_bwd(lb_param, dlb), "a_norm_w": g_a_norm, "gmlp_ln_w": g_ln_w,
        "gmlp_ln_b": g_ln_b, "w_spatial": g_wsp, "b_spatial": g_bs_t.T, "post_mix_w": g_post_mix,
        "pre_ffn_w": g_pre_ffn, "post_ffn_w": g_post_ffn, "post_ple_w": g_post_ple,
    }
    pack = lambda get: jnp.concatenate([get(n).reshape(-1, LANE) for n in small_names], axis=0)
    ag_small = _ag_comm([pack(lambda n: small_grad[n])], mid_frac=0.5)
    for n in big_names:
        if n != "w_down":
            adam(n, reduced[n], [ag_small] if n == "w_in" else [])
    g_all = ag_small.results[0]
    sg, sd, sm, sv = _adamw("adamw_small", g_all, pack(lambda n: W[n]), pack(lambda n: M[n]), pack(lambda n: V[n]))
    off = 0
    for n in small_names:
        rows = W[n].size // LANE
        for src, dst in ((sg, grads), (sd, deltas), (sm, new_m), (sv, new_v)):
            dst[n] = src[off:off + rows].reshape(W[n].shape)
        off += rows

    loss = lax.psum(loss_part[0, 0], ("x", "y", "c"))
    return (loss, grad_x[None], *[grads[n] for n in all_names], *[deltas[n] for n in all_names],
            *[new_m[n] for n in all_names], *[new_v[n] for n in all_names])
```

```python
import functools

import numpy as np
import jax
import jax.numpy as jnp
from jax import lax
from jax.experimental import pallas as pl
from jax.experimental.pallas import tpu as pltpu

F32 = jnp.float32
BF16 = jnp.bfloat16

EPS = 1e-6
HEAD = 128
GLA_CHUNK = 64
GMLP_CHUNK = 128
N_DEV = 8
N_CHIP = 4
LANE = 128
VMEM_LIMIT = 56 * 1024 * 1024
HGRN_ROWS = 512
ROW_TILE = 128
EPILOGUE_ROWS = 256
HGRN_UNROLL = 1
HGRN_HEADS = 8

ADAM_LR = 0.001
ADAM_B1 = 0.9
ADAM_B2 = 0.999
ADAM_EPS = 1e-08
ADAM_WD = 0.01
ADAM_STEP = 10

MESH = pl.DeviceIdType.MESH
ANY = pl.BlockSpec(memory_space=pl.ANY)

_DIMS = {
    "nn": (((1,), (0,)), ((), ())),
    "nt": (((1,), (1,)), ((), ())),
    "tn": (((0,), (0,)), ((), ())),
}


def _tile(dim, pref):
    return pref if dim % pref == 0 else dim


def _rows(r, bytes_per_row, budget=18 * 1024 * 1024, mult=16):
    best = None
    for cand in range(mult, r + 1, mult):
        if r % cand == 0 and cand * bytes_per_row <= budget:
            best = cand
    return best if best is not None else r


def _cparams(n_axes):
    return pltpu.CompilerParams(dimension_semantics=("arbitrary",) * n_axes, vmem_limit_bytes=VMEM_LIMIT)


def _dot(a, b, form="nn"):
    return lax.dot_general(a.astype(BF16), b.astype(BF16), _DIMS[form], preferred_element_type=F32)


def _split3(x):
    hi = x.astype(BF16)
    r = x - hi.astype(F32)
    mid = r.astype(BF16)
    lo = (r - mid.astype(F32)).astype(BF16)
    return hi, mid, lo


def _dot_exact_l(c, x):
    hi, mid, lo = _split3(x)
    d = lambda y: lax.dot_general(c, y, _DIMS["nn"], preferred_element_type=F32)
    return d(hi) + d(mid) + d(lo)


def _dot_exact_r(x, c):
    hi, mid, lo = _split3(x)
    d = lambda y: lax.dot_general(y, c, _DIMS["nn"], preferred_element_type=F32)
    return d(hi) + d(mid) + d(lo)


def _sigmoid(x):
    return 1.0 / (1.0 + jnp.exp(-x))


def _gelu(x):
    return 0.5 * x * (1.0 + lax.erf(x * 0.7071067811865476))


def _gelu_grad(x):
    cdf = 0.5 * (1.0 + lax.erf(x * 0.7071067811865476))
    pdf = jnp.exp(-0.5 * x * x) * 0.3989422804014327
    return cdf + x * pdf


def _position():
    return lax.axis_index("x"), lax.axis_index("y"), lax.axis_index("c")


def _linear_step(grid):
    step = 0
    for ax, n in enumerate(grid):
        step = step * n + pl.program_id(ax)
    return step


class _Comm:
    def __init__(self, arrays, out_shapes, sem_shapes, phases):
        self.arrays, self.out_shapes, self.sem_shapes, self.phases = list(arrays), list(out_shapes), list(sem_shapes), phases
        self.results = None


class _CommLayout:
    def __init__(self, comms, space=pl.ANY):
        self.comms = list(comms)
        self.arrays = [a for c in self.comms for a in c.arrays]
        self.out_shapes = [s for c in self.comms for s in c.out_shapes]
        self.sem_shapes = [s for c in self.comms for s in c.sem_shapes]
        self.n_in, self.n_out = len(self.arrays), len(self.out_shapes)
        self.in_specs = [pl.BlockSpec(memory_space=space)] * self.n_in
        self.out_specs = [pl.BlockSpec(memory_space=space)] * self.n_out

    def run(self, cin, cout, csem, step, n_steps, post):
        i = o = s = 0
        for c in self.comms:
            ins, outs, sems = cin[i:i + len(c.arrays)], cout[o:o + len(c.out_shapes)], csem[s:s + len(c.sem_shapes)]
            i, o, s = i + len(c.arrays), o + len(c.out_shapes), s + len(c.sem_shapes)
            for frac, fn in c.phases:
                if (frac is None) != post:
                    continue
                due = n_steps - 1 if frac is None else max(0, min(int(frac * n_steps), n_steps - 2))
                if n_steps == 1:
                    fn(ins, outs, sems)
                else:
                    pl.when(step == due)(functools.partial(fn, ins, outs, sems))

    def deliver(self, results):
        o = 0
        for c in self.comms:
            c.results = list(results[o:o + len(c.out_shapes)])
            o += len(c.out_shapes)


def _comm_only(name, comms, in_vmem=False):
    lay = _CommLayout(comms, pltpu.VMEM if in_vmem else pl.ANY)

    def body(*refs):
        cin, cout, csem = refs[:lay.n_in], refs[lay.n_in:lay.n_in + lay.n_out], refs[lay.n_in + lay.n_out:]
        lay.run(cin, cout, csem, 0, 1, post=False)
        lay.run(cin, cout, csem, 0, 1, post=True)

    res = pl.pallas_call(
        body, name=name, in_specs=lay.in_specs, out_specs=lay.out_specs, out_shape=lay.out_shapes,
        scratch_shapes=lay.sem_shapes,
    )(*lay.arrays)
    lay.deliver(res)


def _ag_comm(shards, mid_frac=0.0, start_frac=0.0):
    n = len(shards)
    per = N_DEV - 1

    def tools(ins, outs, sems):
        send_sems, recv_sems, local_sems = sems
        x, y, c = _position()
        me, sibling = (x, y, c), (x, y, 1 - c)
        chips = [(1 - x, y), (x, 1 - y), (1 - x, 1 - y)]

        def copy(a, k, block, to, from_shard=False):
            dst = outs[a].at[4 * block[0] + 2 * block[1] + block[2]]
            return pltpu.make_async_remote_copy(
                src_ref=ins[a] if from_shard else dst, dst_ref=dst,
                send_sem=send_sems.at[a * per + k], recv_sem=recv_sems.at[a * per + k],
                device_id=to, device_id_type=MESH)

        def local(a):
            return pltpu.make_async_copy(ins[a], outs[a].at[4 * x + 2 * y + c], local_sems.at[a])

        return me, sibling, chips, c, copy, local

    def first(ins, outs, sems):
        me, sibling, chips, c, copy, local = tools(ins, outs, sems)
        for a in range(n):
            local(a).start()
            copy(a, 0, me, sibling, True).start()
            for j, chip in enumerate(chips):
                copy(a, 1 + j, me, (*chip, c), True).start()

    def middle(ins, outs, sems):
        me, sibling, chips, c, copy, local = tools(ins, outs, sems)
        for a in range(n):
            for j, chip in enumerate(chips):
                copy(a, 1 + j, (*chip, c), me).wait_recv()
                copy(a, 4 + j, (*chip, c), sibling).start()

    def last(ins, outs, sems):
        me, sibling, chips, c, copy, local = tools(ins, outs, sems)
        for a in range(n):
            copy(a, 0, sibling, me).wait_recv()
            copy(a, 0, me, sibling, True).wait_send()
            for j, chip in enumerate(chips):
                copy(a, 4 + j, (*chip, 1 - c), me).wait_recv()
                copy(a, 1 + j, me, (*chip, c), True).wait_send()
                copy(a, 4 + j, (*chip, c), sibling).wait_send()
            local(a).wait()

    return _Comm(
        shards, [jax.ShapeDtypeStruct((N_DEV,) + s.shape, s.dtype) for s in shards],
        [pltpu.SemaphoreType.DMA((n * per,)), pltpu.SemaphoreType.DMA((n * per,)), pltpu.SemaphoreType.DMA((n,))],
        [(start_frac, first), (max(mid_frac, start_frac), middle), (None, last)])


def _pair_comm(grads):
    n = len(grads)

    def copies(ins, outs, sems):
        send_sems, recv_sems = sems
        x, y, c = _position()
        return [pltpu.make_async_remote_copy(
            src_ref=ins[a].at[k, 1 - c], dst_ref=outs[a].at[k],
            send_sem=send_sems.at[a * N_CHIP + k], recv_sem=recv_sems.at[a * N_CHIP + k],
            device_id=(x, y, 1 - c), device_id_type=MESH) for a in range(n) for k in range(N_CHIP)]

    def first(ins, outs, sems):
        for cp in copies(ins, outs, sems):
            cp.start()

    def last(ins, outs, sems):
        for cp in copies(ins, outs, sems):
            cp.wait()

    return _Comm(
        grads, [jax.ShapeDtypeStruct((N_CHIP,) + g.shape[2:], g.dtype) for g in grads],
        [pltpu.SemaphoreType.DMA((n * N_CHIP,)), pltpu.SemaphoreType.DMA((n * N_CHIP,))],
        [(0.0, first), (None, last)])


def _chip_comm(sums):
    n = len(sums)
    per = N_CHIP - 1

    def copies(ins, outs, sems):
        send_sems, recv_sems, local_sems = sems
        x, y, c = _position()
        my_chip = 2 * x + y
        cps = []
        for a in range(n):
            cps.append(pltpu.make_async_copy(ins[a].at[my_chip], outs[a].at[my_chip], local_sems.at[a]))
            for j, (px, py) in enumerate([(1 - x, y), (x, 1 - y), (1 - x, 1 - y)]):
                cps.append(pltpu.make_async_remote_copy(
                    src_ref=ins[a].at[2 * px + py], dst_ref=outs[a].at[my_chip],
                    send_sem=send_sems.at[a * per + j], recv_sem=recv_sems.at[a * per + j],
                    device_id=(px, py, c), device_id_type=MESH))
        return cps

    def first(ins, outs, sems):
        for cp in copies(ins, outs, sems):
            cp.start()

    def last(ins, outs, sems):
        for cp in copies(ins, outs, sems):
            cp.wait()

    return _Comm(
        sums, [jax.ShapeDtypeStruct(s.shape, s.dtype) for s in sums],
        [pltpu.SemaphoreType.DMA((n * per,)), pltpu.SemaphoreType.DMA((n * per,)), pltpu.SemaphoreType.DMA((n,))],
        [(0.0, first), (None, last)])


def _matmul(name, operands, pairs, outs, grid, acc_shape, n_slots=1, epilogue=None, comms=()):
    used = sorted({i for p in pairs for i in p[:2]})
    n_op = len(operands)
    n_out = len(outs)
    k_axis = len(grid) - 1
    n_k = grid[-1]
    lay = _CommLayout(comms)

    direct = n_k == 1 and epilogue is None
    n_acc = 0 if direct else 1

    def body(*refs):
        ops = refs[:n_op]
        out_refs = refs[n_op + lay.n_in:n_op + lay.n_in + n_out]
        acc = None if direct else refs[n_op + lay.n_in + n_out + lay.n_out]
        k = pl.program_id(k_axis)
        step = _linear_step(grid)
        cin = refs[n_op:n_op + lay.n_in]
        cout = refs[n_op + lay.n_in + n_out:n_op + lay.n_in + n_out + lay.n_out]
        csem = refs[n_op + lay.n_in + n_out + lay.n_out + n_acc:]
        lay.run(cin, cout, csem, step, int(np.prod(grid)), post=False)

        if n_k > 1:
            @pl.when(k == 0)
            def _():
                acc[...] = jnp.zeros_like(acc)

        vals = {i: ops[i][...] for i in used}
        vals = {i: (v if v.dtype == BF16 else v.astype(BF16)) for i, v in vals.items()}
        for s in range(n_slots):
            tot = None
            for ia, ib, form, slot in pairs:
                if slot != s:
                    continue
                d = lax.dot_general(vals[ia], vals[ib], _DIMS[form], preferred_element_type=F32)
                tot = d if tot is None else tot + d
            if direct:
                out_refs[s][...] = tot.astype(out_refs[s].dtype)
            elif n_k == 1:
                acc[s] = tot
            else:
                acc[s] += tot

        def finish():
            rows = acc_shape[0]
            chunk = EPILOGUE_ROWS if (epilogue is not None and rows % EPILOGUE_ROWS == 0) else rows
            for r0 in range(0, rows, chunk):
                sl = slice(r0, r0 + chunk)
                accs = [acc[s, sl, :] for s in range(n_slots)]
                extra = [ops[i][sl, :] for i in range(n_op) if i not in used]
                res = epilogue(accs, *extra) if epilogue is not None else accs
                for o, v in zip(out_refs, res):
                    o[sl, :] = v.astype(o.dtype)

        if n_k > 1:
            pl.when(k == n_k - 1)(finish)
        elif not direct:
            finish()

        lay.run(cin, cout, csem, step, int(np.prod(grid)), post=True)

    res = pl.pallas_call(
        body,
        name=name,
        grid=grid,
        in_specs=[s for _, s in operands] + lay.in_specs,
        out_specs=[s for _, s in outs] + lay.out_specs,
        out_shape=[s for s, _ in outs] + lay.out_shapes,
        scratch_shapes=([] if direct else [pltpu.VMEM((n_slots,) + tuple(acc_shape), F32)]) + lay.sem_shapes,
        compiler_params=_cparams(len(grid)),
    )(*[a for a, _ in operands], *lay.arrays)
    lay.deliver(res[n_out:])
    return res[:n_out]


def _mm_plain(name, a, b, form, out_dtype, tm, tn, tk, extra=None, epilogue=None, comms=()):
    if form == "nn":
        (M, K), N = a.shape, b.shape[1]
    elif form == "nt":
        (M, K), N = a.shape, b.shape[0]
    else:
        (K, M), N = a.shape, b.shape[1]
    tm, tn, tk = _tile(M, tm), _tile(N, tn), _tile(K, tk)
    a_spec = pl.BlockSpec((tk, tm), lambda m, n, k: (k, m)) if form == "tn" else pl.BlockSpec((tm, tk), lambda m, n, k: (m, k))
    b_spec = pl.BlockSpec((tn, tk), lambda m, n, k: (n, k)) if form == "nt" else pl.BlockSpec((tk, tn), lambda m, n, k: (k, n))
    operands = [(a, a_spec), (b, b_spec)]
    if extra is not None:
        operands.append((extra, pl.BlockSpec((tm, tn), lambda m, n, k: (m, n))))
    out = (jax.ShapeDtypeStruct((M, N), out_dtype), pl.BlockSpec((tm, tn), lambda m, n, k: (m, n)))
    return _matmul(name, operands, [(0, 1, form, 0)], [out], (M // tm, N // tn, K // tk), (tm, tn), epilogue=epilogue,
                   comms=comms)[0]


def _ag_matmul(name, a, shard, form, out_dtype, blocked_out, comms=()):
    t, kdim = a.shape
    nl = shard.shape[1] if form == "nn" else shard.shape[0]
    tm = _tile(t, 1024)
    grid = (N_DEV, t // tm)
    n_steps = grid[0] * grid[1]
    per = N_DEV - 1
    lay = _CommLayout(comms)
    ident = lambda bx, by, bc: 4 * bx + 2 * by + bc

    def block_of(n):
        x, y, c = _position()
        far = jnp.where(n < 2, 0, jnp.where(n < 5, n - 1, n - 4))
        bx = jnp.where((far == 1) | (far == 3), 1 - x, x)
        by = jnp.where(far >= 2, 1 - y, y)
        bc = jnp.where((n == 0) | ((n >= 2) & (n < 5)), c, 1 - c)
        return ident(bx, by, bc)

    def body(*refs):
        a_ref, shard_ref = refs[:2]
        cin = refs[2:2 + lay.n_in]
        out_ref, land_ref = refs[2 + lay.n_in:4 + lay.n_in]
        cout = refs[4 + lay.n_in:4 + lay.n_in + lay.n_out]
        bbuf, send_sems, recv_sems, misc_sems = refs[4 + lay.n_in + lay.n_out:8 + lay.n_in + lay.n_out]
        csem = refs[8 + lay.n_in + lay.n_out:]
        n, m = pl.program_id(0), pl.program_id(1)
        step = n * grid[1] + m
        x, y, c = _position()
        me, sibling = (x, y, c), (x, y, 1 - c)
        chips = [(1 - x, y), (x, 1 - y), (1 - x, 1 - y)]

        def copy(k, block, to, from_shard=False):
            dst = land_ref.at[ident(*block)]
            return pltpu.make_async_remote_copy(
                src_ref=shard_ref if from_shard else dst, dst_ref=dst,
                send_sem=send_sems.at[k], recv_sem=recv_sems.at[k], device_id=to, device_id_type=MESH)

        local = pltpu.make_async_copy(shard_ref, land_ref.at[ident(*me)], misc_sems.at[0])

        def load(src):
            cp = pltpu.make_async_copy(src, bbuf, misc_sems.at[1])
            cp.start()
            cp.wait()

        @pl.when(step == 0)
        def _():
            local.start()
            copy(0, me, sibling, True).start()
            for j, chip in enumerate(chips):
                copy(1 + j, me, (*chip, c), True).start()

        lay.run(cin, cout, csem, step, n_steps, post=False)

        @pl.when(m == 0)
        def _():
            @pl.when(n == 0)
            def _():
                load(shard_ref)

            @pl.when(n == 1)
            def _():
                copy(0, sibling, me).wait_recv()
                load(land_ref.at[ident(*sibling)])

            for j, chip in enumerate(chips):
                @pl.when(n == 2 + j)
                def _(j=j, chip=chip):
                    copy(1 + j, (*chip, c), me).wait_recv()
                    copy(4 + j, (*chip, c), sibling).start()
                    load(land_ref.at[ident(*chip, c)])

                @pl.when(n == 5 + j)
                def _(j=j, chip=chip):
                    copy(4 + j, (*chip, 1 - c), me).wait_recv()
                    load(land_ref.at[ident(*chip, 1 - c)])

        out_ref[...] = lax.dot_general(a_ref[...], bbuf[...], _DIMS[form], preferred_element_type=F32).astype(out_ref.dtype)

        @pl.when(step == n_steps - 1)
        def _():
            copy(0, me, sibling, True).wait_send()
            for j, chip in enumerate(chips):
                copy(1 + j, me, (*chip, c), True).wait_send()
                copy(4 + j, (*chip, c), sibling).wait_send()
            local.wait()

        lay.run(cin, cout, csem, step, n_steps, post=True)

    if blocked_out:
        out_shape = jax.ShapeDtypeStruct((N_DEV, t, nl), out_dtype)
        out_spec = pl.BlockSpec((None, tm, nl), lambda n, m: (block_of(n), m, 0))
    else:
        out_shape = jax.ShapeDtypeStruct((t, N_DEV * nl), out_dtype)
        out_spec = pl.BlockSpec((tm, nl), lambda n, m: (m, block_of(n)))
    res = pl.pallas_call(
        body, name=name, grid=grid,
        in_specs=[pl.BlockSpec((tm, kdim), lambda n, m: (m, 0)), ANY] + lay.in_specs,
        out_specs=[out_spec, ANY] + lay.out_specs,
        scratch_shapes=[pltpu.VMEM(shard.shape, shard.dtype), pltpu.SemaphoreType.DMA((per,)),
                        pltpu.SemaphoreType.DMA((per,)), pltpu.SemaphoreType.DMA((2,))] + lay.sem_shapes,
        out_shape=[out_shape, jax.ShapeDtypeStruct((N_DEV,) + shard.shape, shard.dtype)] + lay.out_shapes,
        compiler_params=_cparams(2),
    )(a, shard, *lay.arrays)
    lay.deliver(res[2:])
    return res[0], res[1]


def _cast_bf16(name, w):
    r, c = w.shape
    tr = _rows(r, 6 * c)

    def body(w_ref, o_ref):
        o_ref[...] = w_ref[...].astype(BF16)

    return pl.pallas_call(
        body, name=name, grid=(r // tr,),
        in_specs=[pl.BlockSpec((tr, c), lambda i: (i, 0))],
        out_specs=pl.BlockSpec((tr, c), lambda i: (i, 0)),
        out_shape=jax.ShapeDtypeStruct((r, c), BF16),
        compiler_params=_cparams(1),
    )(w)


def _rms_stats(x):
    r = lax.rsqrt(jnp.mean(x * x, axis=-1, keepdims=True) + EPS)
    return x * r, r


def _rms_bwd(xhat, r, w, dy):
    dxh = dy * w
    return r * (dxh - xhat * jnp.mean(dxh * xhat, axis=-1, keepdims=True))


def _row_spec(tr, d):
    return pl.BlockSpec((tr, d), lambda i: (i, 0))


def _vec_spec(d):
    return pl.BlockSpec((1, d), lambda i: (0, 0))


def _rms_fwd(name, x, w):
    t, d = x.shape
    tr = _tile(t, ROW_TILE)

    def body(x_ref, w_ref, h_ref):
        xh, _ = _rms_stats(x_ref[...])
        h_ref[...] = (xh * w_ref[...]).astype(BF16)

    return pl.pallas_call(
        body, name=name, grid=(t // tr,),
        in_specs=[_row_spec(tr, d), _vec_spec(d)],
        out_specs=_row_spec(tr, d),
        out_shape=jax.ShapeDtypeStruct((t, d), BF16),
        compiler_params=_cparams(1),
    )(x, w)


def _resid_rms(name, xres, y, w_post, w_next):
    t, d = xres.shape
    tr = _tile(t, ROW_TILE)
    has_next = w_next is not None

    def body(*refs):
        if has_next:
            x_ref, y_ref, wp_ref, wn_ref, xo_ref, h_ref = refs
        else:
            x_ref, y_ref, wp_ref, xo_ref, h_ref = refs
        yh, _ = _rms_stats(y_ref[...])
        xn = x_ref[...] + yh * wp_ref[...]
        xo_ref[...] = xn
        if has_next:
            xh, _ = _rms_stats(xn)
            h_ref[...] = (xh * wn_ref[...]).astype(BF16)
        else:
            h_ref[...] = xn.astype(BF16)

    ins = [xres, y, w_post] + ([w_next] if has_next else [])
    in_specs = [_row_spec(tr, d), _row_spec(tr, d), _vec_spec(d)] + ([_vec_spec(d)] if has_next else [])
    return pl.pallas_call(
        body, name=name, grid=(t // tr,),
        in_specs=in_specs,
        out_specs=[_row_spec(tr, d), _row_spec(tr, d)],
        out_shape=[jax.ShapeDtypeStruct((t, d), F32), jax.ShapeDtypeStruct((t, d), BF16)],
        compiler_params=_cparams(1),
    )(*ins)


def _ple_loss(name, x2, pe, pgl, w_pp, tgt):
    t, d = x2.shape
    tr = _tile(t, ROW_TILE)

    def body(x2_ref, pe_ref, pgl_ref, w_ref, tgt_ref, loss_ref, d3_ref, dpe_ref, dpgl_ref, dw_ref):
        @pl.when(pl.program_id(0) == 0)
        def _():
            loss_ref[...] = jnp.zeros_like(loss_ref)
            dw_ref[...] = jnp.zeros_like(dw_ref)

        pe_v = pe_ref[...]
        s = _sigmoid(pgl_ref[...])
        y = pe_v * s
        yh, r = _rms_stats(y)
        w = w_ref[...]
        err = x2_ref[...] + yh * w - tgt_ref[...]
        loss_ref[...] += 0.5 * jnp.sum(jnp.mean(err * err, axis=-1, keepdims=True), axis=0, keepdims=True)
        d3 = err * (1.0 / d)
        d3_ref[...] = d3
        dw_ref[...] += jnp.sum(d3 * yh, axis=0, keepdims=True)
        dy = _rms_bwd(yh, r, w, d3)
        dpe_ref[...] = (dy * s).astype(BF16)
        dpgl_ref[...] = (dy * pe_v * s * (1.0 - s)).astype(BF16)

    return pl.pallas_call(
        body, name=name, grid=(t // tr,),
        in_specs=[_row_spec(tr, d), _row_spec(tr, d), _row_spec(tr, d), _vec_spec(d), _row_spec(tr, d)],
        out_specs=[pl.BlockSpec((1, 1), lambda i: (0, 0)), _row_spec(tr, d), _row_spec(tr, d), _row_spec(tr, d), _vec_spec(d)],
        out_shape=[jax.ShapeDtypeStruct((1, 1), F32), jax.ShapeDtypeStruct((t, d), F32),
                   jax.ShapeDtypeStruct((t, d), BF16), jax.ShapeDtypeStruct((t, d), BF16),
                   jax.ShapeDtypeStruct((1, d), F32)],
        compiler_params=_cparams(1),
    )(x2, pe, pgl, w_pp, tgt)


def _norm_bwd(name, dres, y, w_post):
    t, d = dres.shape
    tr = _tile(t, ROW_TILE)

    def body(d_ref, y_ref, w_ref, dy_ref, dw_ref):
        @pl.when(pl.program_id(0) == 0)
        def _():
            dw_ref[...] = jnp.zeros_like(dw_ref)

        dv = d_ref[...]
        yh, r = _rms_stats(y_ref[...])
        dw_ref[...] += jnp.sum(dv * yh, axis=0, keepdims=True)
        dy_ref[...] = _rms_bwd(yh, r, w_ref[...], dv).astype(BF16)

    return pl.pallas_call(
        body, name=name, grid=(t // tr,),
        in_specs=[_row_spec(tr, d), _row_spec(tr, d), _vec_spec(d)],
        out_specs=[_row_spec(tr, d), _vec_spec(d)],
        out_shape=[jax.ShapeDtypeStruct((t, d), BF16), jax.ShapeDtypeStruct((1, d), F32)],
        compiler_params=_cparams(1),
    )(dres, y, w_post)


def _prenorm_bwd(name, dres, dh, xin, w_pre, y=None, w_post=None):
    t, d = dres.shape
    tr = _tile(t, ROW_TILE)
    two = y is not None

    def body(*refs):
        if two:
            d_ref, dh_ref, x_ref, wpre_ref, y_ref, wpost_ref, do_ref, dwpre_ref, dy_ref, dwpost_ref = refs
        else:
            d_ref, dh_ref, x_ref, wpre_ref, do_ref, dwpre_ref = refs

        @pl.when(pl.program_id(0) == 0)
        def _():
            dwpre_ref[...] = jnp.zeros_like(dwpre_ref)
            if two:
                dwpost_ref[...] = jnp.zeros_like(dwpost_ref)

        dhv = dh_ref[...]
        xh, r = _rms_stats(x_ref[...])
        dwpre_ref[...] += jnp.sum(dhv * xh, axis=0, keepdims=True)
        dout = d_ref[...] + _rms_bwd(xh, r, wpre_ref[...], dhv)
        do_ref[...] = dout
        if two:
            yh, ry = _rms_stats(y_ref[...])
            dwpost_ref[...] += jnp.sum(dout * yh, axis=0, keepdims=True)
            dy_ref[...] = _rms_bwd(yh, ry, wpost_ref[...], dout).astype(BF16)

    ins = [dres, dh, xin, w_pre] + ([y, w_post] if two else [])
    in_specs = [_row_spec(tr, d)] * 3 + [_vec_spec(d)] + ([_row_spec(tr, d), _vec_spec(d)] if two else [])
    out_specs = [_row_spec(tr, d), _vec_spec(d)] + ([_row_spec(tr, d), _vec_spec(d)] if two else [])
    out_shape = [jax.ShapeDtypeStruct((t, d), F32), jax.ShapeDtypeStruct((1, d), F32)]
    if two:
        out_shape += [jax.ShapeDtypeStruct((t, d), BF16), jax.ShapeDtypeStruct((1, d), F32)]
    return pl.pallas_call(
        body, name=name, grid=(t // tr,),
        in_specs=in_specs, out_specs=out_specs, out_shape=out_shape,
        compiler_params=_cparams(1),
    )(*ins)


_LEVELS = (32, 16, 8, 4, 2, 1)
_N_CUM = 3 + 2 * len(_LEVELS)


def _hgrn_constants():
    c = GLA_CHUNK
    idx = np.arange(c)
    t, r = idx[:, None], idx[None, :]
    mats = [(r <= t), (r > t), np.ones((c, c), bool)]
    lq, lk, masks = [], [], []
    for h in _LEVELS:
        blk, pos = idx // (2 * h), idx % (2 * h)
        mid = blk * 2 * h + h - 1
        upper, lower = pos >= h, pos < h
        lq.append(upper[:, None] & (r > mid[:, None]) & (r <= t))
        lk.append(lower[:, None] & (r > t) & (r <= mid[:, None]))
        masks.append((blk[:, None] == blk[None, :]) & upper[:, None] & lower[None, :])
    cum = np.concatenate(mats + lq + lk, axis=0).astype(np.float32)
    rev = (r >= t).astype(np.float32)
    return (jnp.asarray(cum, BF16), jnp.asarray(rev, BF16), jnp.asarray(np.stack(masks).astype(np.float32)))


def _hgrn_gates(qp, fp, lb):
    sq = _sigmoid(qp)
    q = qp * sq
    sg = _sigmoid(fp)
    f = lb + (1.0 - lb) * sg
    k = 1.0 - f
    logf = jnp.log(jnp.maximum(f, 1e-30))
    return q, sq, sg, f, k, logf


def _hgrn_decays(cum_ref, logf):
    c = GLA_CHUNK
    e = jnp.exp(_dot_exact_l(cum_ref[...], logf))
    part = lambda i: e[i * c:(i + 1) * c]
    n = len(_LEVELS)
    return part(0), part(1), part(2), [part(3 + i) for i in range(n)], [part(3 + n + i) for i in range(n)]


def _hgrn_fwd(proj, lb, nw, n_heads, comms=()):
    t = proj.shape[0]
    aw = n_heads * HEAD
    rb = _tile(t, HGRN_ROWS)
    c = GLA_CHUNK
    n_sub = rb // c
    cum, _, masks = _hgrn_constants()
    lay = _CommLayout(comms)
    hp = HGRN_HEADS if n_heads % HGRN_HEADS == 0 else 1
    wd = hp * HEAD
    grid = (n_heads // hp, t // rb)

    def body(*refs):
        q_ref, f_ref, i_ref, g_ref, lb_ref, nw_ref, cum_ref, m_ref = refs[:8]
        cin = refs[8:8 + lay.n_in]
        a_ref, o_ref, s_ref, sc_ref = refs[8 + lay.n_in:12 + lay.n_in]
        cout = refs[12 + lay.n_in:12 + lay.n_in + lay.n_out]
        st = refs[12 + lay.n_in + lay.n_out]
        csem = refs[13 + lay.n_in + lay.n_out:]
        step = _linear_step(grid)
        lay.run(cin, cout, csem, step, grid[0] * grid[1], post=False)

        @pl.when(pl.program_id(1) == 0)
        def _():
            st[...] = jnp.zeros_like(st)

        lbv = lb_ref[...]
        nwv = nw_ref[...]
        eye = (lax.broadcasted_iota(jnp.int32, (c, c), 0) == lax.broadcasted_iota(jnp.int32, (c, c), 1)).astype(F32)
        heads = range(hp)
        hs = lambda a, h: a[:, h * HEAD:(h + 1) * HEAD]

        def chunk(j, carry):
            rows = pl.ds(pl.multiple_of(j * c, c), c)
            q, _, _, _, k, logf = _hgrn_gates(q_ref[rows, :], f_ref[rows, :], lbv)
            v = i_ref[rows, :]
            eb, ebe, eend, eq, ek = _hgrn_decays(cum_ref, logf)
            qt, kt, qk = q * eb, k * ebe, q * k
            s_in = [st[h] for h in heads]
            for h in heads:
                s_ref[h, j] = s_in[h]
            inter = [_dot(hs(qt, h), s_in[h], "nt") for h in heads]
            for h in heads:
                st[h] = s_in[h] * hs(eend, h)[0:1] + _dot(hs(v, h), hs(kt, h), "tn")
            scores = [eye * jnp.sum(hs(qk, h), axis=-1, keepdims=True) for h in heads]
            for lvl in range(len(_LEVELS)):
                ql, kl = q * eq[lvl], k * ek[lvl]
                for h in heads:
                    scores[h] = scores[h] + m_ref[lvl] * _dot(hs(ql, h), hs(kl, h), "nt")
            gv = g_ref[rows, :]
            gate = nwv * (gv * _sigmoid(gv))
            for h in heads:
                sc_ref[h, rows, :] = scores[h]
                o = inter[h] + _dot(scores[h], hs(v, h))
                o_ref[rows, h * HEAD:(h + 1) * HEAD] = o
                r = lax.rsqrt(jnp.mean(o * o, axis=-1, keepdims=True) + EPS)
                a_ref[rows, h * HEAD:(h + 1) * HEAD] = (o * r * hs(gate, h)).astype(BF16)
            return carry

        lax.fori_loop(0, n_sub, chunk, 0, unroll=HGRN_UNROLL)
        lay.run(cin, cout, csem, step, grid[0] * grid[1], post=True)

    n_hb = n_heads // hp
    col = lambda base: pl.BlockSpec((rb, wd), lambda h, r: (r, base * n_hb + h))
    vec = pl.BlockSpec((1, wd), lambda h, r: (0, h))
    res = pl.pallas_call(
        body, name="hgrn2_fwd", grid=grid,
        in_specs=[col(0), col(1), col(2), col(3), vec, vec,
                  pl.BlockSpec(cum.shape, lambda h, r: (0, 0)), pl.BlockSpec(masks.shape, lambda h, r: (0, 0, 0))] + lay.in_specs,
        out_specs=[pl.BlockSpec((rb, wd), lambda h, r: (r, h)), pl.BlockSpec((rb, wd), lambda h, r: (r, h)),
                   pl.BlockSpec((hp, n_sub, HEAD, HEAD), lambda h, r: (h, r, 0, 0)),
                   pl.BlockSpec((hp, rb, c), lambda h, r: (h, r, 0))] + lay.out_specs,
        out_shape=[jax.ShapeDtypeStruct((t, aw), BF16), jax.ShapeDtypeStruct((t, aw), F32),
                   jax.ShapeDtypeStruct((n_heads, t // c, HEAD, HEAD), F32),
                   jax.ShapeDtypeStruct((n_heads, t, c), F32)] + lay.out_shapes,
        scratch_shapes=[pltpu.VMEM((hp, HEAD, HEAD), F32)] + lay.sem_shapes,
        compiler_params=_cparams(2),
    )(proj, proj, proj, proj, lb, nw, cum, masks, *lay.arrays)
    lay.deliver(res[4:])
    return res[:4]


def _hgrn_bwd(proj, lb, nw, o_raw, states, scores, dab, n_heads, comms=()):
    t = proj.shape[0]
    aw = n_heads * HEAD
    rb = _tile(t, HGRN_ROWS)
    c = GLA_CHUNK
    n_sub = rb // c
    n_rb = t // rb
    cum, rev, masks = _hgrn_constants()
    lay = _CommLayout(comms)
    hp = HGRN_HEADS if n_heads % HGRN_HEADS == 0 else 1
    wd = hp * HEAD
    grid = (n_heads // hp, n_rb)

    def body(*refs):
        q_ref, f_ref, i_ref, g_ref, lb_ref, nw_ref, o_ref, s_ref, sc_ref, da_ref, cum_ref, rev_ref, m_ref = refs[:13]
        cin = refs[13:13 + lay.n_in]
        dq_ref, df_ref, di_ref, dg_ref, dlb_ref, dnw_ref = refs[13 + lay.n_in:19 + lay.n_in]
        cout = refs[19 + lay.n_in:19 + lay.n_in + lay.n_out]
        dst = refs[19 + lay.n_in + lay.n_out]
        csem = refs[20 + lay.n_in + lay.n_out:]
        step = _linear_step(grid)
        lay.run(cin, cout, csem, step, grid[0] * grid[1], post=False)

        @pl.when(pl.program_id(1) == 0)
        def _():
            dst[...] = jnp.zeros_like(dst)
            dlb_ref[...] = jnp.zeros_like(dlb_ref)
            dnw_ref[...] = jnp.zeros_like(dnw_ref)

        lbv = lb_ref[...]
        nwv = nw_ref[...]
        ri = lax.broadcasted_iota(jnp.int32, (c, c), 0)
        ci = lax.broadcasted_iota(jnp.int32, (c, c), 1)
        eye = (ri == ci).astype(F32)
        causal = (ci <= ri).astype(F32)
        last_row = (lax.broadcasted_iota(jnp.int32, (c, wd), 0) == c - 1).astype(F32)
        heads = range(hp)
        hs = lambda a, h: a[:, h * HEAD:(h + 1) * HEAD]
        wide = lambda parts: parts[0] if hp == 1 else jnp.concatenate(parts, axis=1)

        def head_mean(a):
            return wide([jnp.broadcast_to(jnp.mean(hs(a, h), axis=-1, keepdims=True), (c, HEAD)) for h in heads])

        def chunk(jj, carry):
            j = n_sub - 1 - jj
            rows = pl.ds(pl.multiple_of(j * c, c), c)
            qp = q_ref[rows, :]
            q, sq, sg, f, k, logf = _hgrn_gates(qp, f_ref[rows, :], lbv)
            v = i_ref[rows, :]
            gv = g_ref[rows, :]
            eb, ebe, eend, eq, ek = _hgrn_decays(cum_ref, logf)
            s_in = [s_ref[h, j] for h in heads]
            a_sc = [sc_ref[h, rows, :] for h in heads]
            dsn = [dst[h] for h in heads]
            o = o_ref[rows, :]
            r = lax.rsqrt(head_mean(o * o) + EPS)
            oh = o * r
            sgg = _sigmoid(gv)
            sil = gv * sgg
            da = da_ref[rows, :]
            dg_ref[rows, :] = (da * oh * nwv * (sgg * (1.0 + gv * (1.0 - sgg)))).astype(BF16)
            dnw_ref[...] += jnp.sum(da * oh * sil, axis=0, keepdims=True)
            doh = da * nwv * sil
            do = r * (doh - oh * head_mean(doh * oh))
            kt = k * ebe
            qt = q * eb
            d_sc = [_dot(hs(do, h), hs(v, h), "nt") * causal for h in heads]
            dqt = wide([_dot(hs(do, h), s_in[h]) for h in heads])
            dkt = wide([_dot(hs(v, h), dsn[h]) for h in heads])
            for h in heads:
                dst[h] = dsn[h] * hs(eend, h)[0:1] + _dot(hs(do, h), hs(qt, h), "tn")
            di_ref[rows, :] = wide([_dot(a_sc[h], hs(do, h), "tn") + _dot(hs(kt, h), dsn[h], "nt") for h in heads]).astype(BF16)
            diag = wide([jnp.broadcast_to(jnp.sum(d_sc[h] * eye, axis=-1, keepdims=True), (c, HEAD)) for h in heads])
            dq = dqt * eb
            dk = dkt * ebe
            db = q * dq - k * dk
            dq = dq + diag * k
            dk = dk + diag * q
            for lvl in range(len(_LEVELS)):
                ql = (q * eq[lvl]).astype(BF16)
                kl = (k * ek[lvl]).astype(BF16)
                dm = [(m_ref[lvl] * d_sc[h]).astype(BF16) for h in heads]
                gq = wide([_dot(dm[h], hs(kl, h)) for h in heads])
                gk = wide([_dot(dm[h], hs(ql, h), "tn") for h in heads])
                dq = dq + gq * eq[lvl]
                dk = dk + gk * ek[lvl]
                db = db + ql.astype(F32) * gq - kl.astype(F32) * gk
            state_term = wide([jnp.sum(s_in[h] * dsn[h], axis=0, keepdims=True) for h in heads])
            extra = jnp.sum(dkt * kt, axis=0, keepdims=True) + eend[0:1] * state_term
            db = db + last_row * extra
            dlogf = _dot_exact_l(rev_ref[...], db)
            dfv = jnp.where(f > 1e-30, dlogf / f, 0.0) - dk
            df_ref[rows, :] = (dfv * (1.0 - lbv) * sg * (1.0 - sg)).astype(BF16)
            dlb_ref[...] += jnp.sum(dfv * (1.0 - sg), axis=0, keepdims=True)
            dq_ref[rows, :] = (dq * (sq * (1.0 + qp * (1.0 - sq)))).astype(BF16)
            return carry

        lax.fori_loop(0, n_sub, chunk, 0, unroll=HGRN_UNROLL)
        lay.run(cin, cout, csem, step, grid[0] * grid[1], post=True)

    n_hb = n_heads // hp
    col = lambda base: pl.BlockSpec((rb, wd), lambda h, r: (n_rb - 1 - r, base * n_hb + h))
    blk = pl.BlockSpec((rb, wd), lambda h, r: (n_rb - 1 - r, h))
    vec = pl.BlockSpec((1, wd), lambda h, r: (0, h))
    const = lambda a: pl.BlockSpec(a.shape, lambda h, r: (0,) * a.ndim)
    res = pl.pallas_call(
        body, name="hgrn2_bwd", grid=grid,
        in_specs=[col(0), col(1), col(2), col(3), vec, vec, blk,
                  pl.BlockSpec((hp, n_sub, HEAD, HEAD), lambda h, r: (h, n_rb - 1 - r, 0, 0)),
                  pl.BlockSpec((hp, rb, c), lambda h, r: (h, n_rb - 1 - r, 0)),
                  blk, const(cum), const(rev), const(masks)] + lay.in_specs,
        out_specs=[blk, blk, blk, blk, vec, vec] + lay.out_specs,
        out_shape=[jax.ShapeDtypeStruct((t, aw), BF16)] * 4 + [jax.ShapeDtypeStruct((1, aw), F32)] * 2 + lay.out_shapes,
        scratch_shapes=[pltpu.VMEM((hp, HEAD, HEAD), F32)] + lay.sem_shapes,
        compiler_params=_cparams(2),
    )(proj, proj, proj, proj, lb, nw, o_raw, states, scores, dab, cum, rev, masks, *lay.arrays)
    lay.deliver(res[6:])
    return res[:6]


def _lb_fwd(lb_param):
    def body(p_ref, o_ref):
        p = p_ref[...]
        e = jnp.exp(p - jnp.max(p, axis=0, keepdims=True))
        o_ref[...] = e[0:1] / jnp.sum(e, axis=0, keepdims=True)

    return pl.pallas_call(body, name="lb_fwd", out_shape=jax.ShapeDtypeStruct((1, lb_param.shape[1]), F32))(lb_param)


def _lb_bwd(lb_param, dlb):
    def body(p_ref, d_ref, o_ref):
        p = p_ref[...]
        e = jnp.exp(p - jnp.max(p, axis=0, keepdims=True))
        s = e / jnp.sum(e, axis=0, keepdims=True)
        first = (lax.broadcasted_iota(jnp.int32, p.shape, 0) == 0).astype(F32)
        o_ref[...] = d_ref[...] * s[0:1] * (first - s)

    return pl.pallas_call(body, name="lb_bwd", out_shape=jax.ShapeDtypeStruct(lb_param.shape, F32))(lb_param, dlb)


def _gmlp_norm(v, lnw, lnb):
    vf = _gelu(v)
    mu = jnp.mean(vf, axis=-1, keepdims=True)
    cen = vf - mu
    rstd = lax.rsqrt(jnp.mean(cen * cen, axis=-1, keepdims=True) + EPS)
    xh = cen * rstd
    return xh, rstd, xh * lnw + lnb


def _tril(n):
    return (lax.broadcasted_iota(jnp.int32, (n, n), 1) <= lax.broadcasted_iota(jnp.int32, (n, n), 0)).astype(F32)


def _gmlp_fwd(proj, lnw, lnb, w_sp, bs_t, n_groups, col_base):
    t = proj.shape[0]
    bw = n_groups * HEAD
    c = GMLP_CHUNK

    def body(u_ref, v_ref, lnw_ref, lnb_ref, w_ref, bs_ref, o_ref):
        tri = _tril(c)
        uf = _gelu(u_ref[...])
        _, _, vn = _gmlp_norm(v_ref[...], lnw_ref[...], lnb_ref[...])
        for g in range(n_groups):
            cols = slice(g * HEAD, (g + 1) * HEAD)
            z = _dot(w_ref[g] * tri, vn[:, cols]) + bs_ref[:, g:g + 1]
            o_ref[:, cols] = (uf[:, cols] * z).astype(BF16)

    blk = lambda b: pl.BlockSpec((c, bw), lambda n: (n, b))
    const = lambda a: pl.BlockSpec(a.shape, lambda n: (0,) * a.ndim)
    return pl.pallas_call(
        body, name="gmlp_fwd", grid=(t // c,),
        in_specs=[blk(col_base), blk(col_base + 1), const(lnw), const(lnb), const(w_sp), const(bs_t)],
        out_specs=pl.BlockSpec((c, bw), lambda n: (n, 0)),
        out_shape=jax.ShapeDtypeStruct((t, bw), BF16),
        compiler_params=_cparams(1),
    )(proj, proj, lnw, lnb, w_sp, bs_t)


def _gmlp_bwd(proj, lnw, lnb, w_sp, bs_t, dab, n_groups, col_base):
    t = proj.shape[0]
    bw = n_groups * HEAD
    c = GMLP_CHUNK
    n_steps = t // c
    sel = jnp.asarray((np.arange(bw)[:, None] // HEAD == np.arange(n_groups)[None, :]).astype(np.float32), BF16)

    def body(u_ref, v_ref, lnw_ref, lnb_ref, w_ref, bs_ref, d_ref, sel_ref,
             du_ref, dv_ref, dlnw_ref, dlnb_ref, dw_ref, dbs_ref, dz_acc, dvn_scr):
        step = pl.program_id(0)

        @pl.when(step == 0)
        def _():
            dlnw_ref[...] = jnp.zeros_like(dlnw_ref)
            dlnb_ref[...] = jnp.zeros_like(dlnb_ref)
            dw_ref[...] = jnp.zeros_like(dw_ref)
            dz_acc[...] = jnp.zeros_like(dz_acc)

        tri = _tril(c)
        u = u_ref[...]
        v = v_ref[...]
        uf = _gelu(u)
        lnw_v = lnw_ref[...]
        xh, rstd, vn = _gmlp_norm(v, lnw_v, lnb_ref[...])
        dbo = d_ref[...]
        dz = dbo * uf
        dz_acc[...] += dz
        for g in range(n_groups):
            cols = slice(g * HEAD, (g + 1) * HEAD)
            wg = w_ref[g] * tri
            z = _dot(wg, vn[:, cols]) + bs_ref[:, g:g + 1]
            du_ref[:, cols] = (dbo[:, cols] * z * _gelu_grad(u[:, cols])).astype(BF16)
            dvn_scr[:, cols] = _dot(wg, dz[:, cols], "tn")
            dw_ref[g] += tri * _dot(dz[:, cols], vn[:, cols], "nt")
        dvn = dvn_scr[...]
        dlnw_ref[...] += jnp.sum(dvn * xh, axis=0, keepdims=True)
        dlnb_ref[...] += jnp.sum(dvn, axis=0, keepdims=True)
        dxh = dvn * lnw_v
        dvf = rstd * (dxh - jnp.mean(dxh, axis=-1, keepdims=True) - xh * jnp.mean(dxh * xh, axis=-1, keepdims=True))
        dv_ref[...] = (dvf * _gelu_grad(v)).astype(BF16)

        @pl.when(step == n_steps - 1)
        def _():
            dbs_ref[...] = _dot_exact_r(dz_acc[...], sel_ref[...])

    blk = lambda b: pl.BlockSpec((c, bw), lambda n: (n, b))
    const = lambda a: pl.BlockSpec(a.shape, lambda n: (0,) * a.ndim)
    row = pl.BlockSpec((c, bw), lambda n: (n, 0))
    vec = pl.BlockSpec((1, bw), lambda n: (0, 0))
    return pl.pallas_call(
        body, name="gmlp_bwd", grid=(n_steps,),
        in_specs=[blk(col_base), blk(col_base + 1), const(lnw), const(lnb), const(w_sp), const(bs_t), blk(1), const(sel)],
        out_specs=[row, row, vec, vec, const(w_sp), const(bs_t)],
        out_shape=[jax.ShapeDtypeStruct((t, bw), BF16), jax.ShapeDtypeStruct((t, bw), BF16),
                   jax.ShapeDtypeStruct((1, bw), F32), jax.ShapeDtypeStruct((1, bw), F32),
                   jax.ShapeDtypeStruct(w_sp.shape, F32), jax.ShapeDtypeStruct(bs_t.shape, F32)],
        scratch_shapes=[pltpu.VMEM((c, bw), F32), pltpu.VMEM((c, bw), F32)],
        compiler_params=_cparams(1),
    )(proj, proj, lnw, lnb, w_sp, bs_t, dab, sel)


def _pair_sum(name, grad, other, core):
    _, _, r, c = grad.shape
    tr = _rows(r, 6 * c)

    def body(core_ref, g_ref, o_ref, out_ref):
        out_ref[...] = (g_ref[...].astype(F32) + o_ref[...].astype(F32)).astype(BF16)

    return pl.pallas_call(
        body, name=name,
        grid_spec=pltpu.PrefetchScalarGridSpec(
            num_scalar_prefetch=1, grid=(N_CHIP, r // tr),
            in_specs=[pl.BlockSpec((None, None, tr, c), lambda k, i, core_ref: (k, core_ref[0], i, 0)),
                      pl.BlockSpec((None, tr, c), lambda k, i, core_ref: (k, i, 0))],
            out_specs=pl.BlockSpec((None, tr, c), lambda k, i, core_ref: (k, i, 0))),
        out_shape=jax.ShapeDtypeStruct((N_CHIP, r, c), BF16),
        compiler_params=_cparams(2),
    )(core, grad, other)


def _adamw_math(w, g, m, v):
    m = ADAM_B1 * m + (1.0 - ADAM_B1) * g
    v = ADAM_B2 * v + (1.0 - ADAM_B2) * (g * g)
    m_hat = m / (1.0 - ADAM_B1 ** ADAM_STEP)
    v_hat = v / (1.0 - ADAM_B2 ** ADAM_STEP)
    delta = -ADAM_LR * (m_hat / (jnp.sqrt(v_hat) + ADAM_EPS) + ADAM_WD * w)
    return delta, m, v


def _adamw(name, parts, w, m, v, comms=()):
    n_parts, r, c = parts.shape
    tr = _rows(r, c * (n_parts * parts.dtype.itemsize + 28), mult=8)
    lay = _CommLayout(comms)
    grid = (r // tr,)

    def body(*refs):
        p_ref, w_ref, m_ref, v_ref = refs[:4]
        cin = refs[4:4 + lay.n_in]
        g_ref, d_ref, mo_ref, vo_ref = refs[4 + lay.n_in:8 + lay.n_in]
        cout = refs[8 + lay.n_in:8 + lay.n_in + lay.n_out]
        csem = refs[8 + lay.n_in + lay.n_out:]
        step = pl.program_id(0)
        lay.run(cin, cout, csem, step, grid[0], post=False)
        g = p_ref[0].astype(F32)
        for i in range(1, n_parts):
            g = g + p_ref[i].astype(F32)
        g_ref[...] = g
        d_ref[...], mo_ref[...], vo_ref[...] = _adamw_math(w_ref[...], g, m_ref[...], v_ref[...])
        lay.run(cin, cout, csem, step, grid[0], post=True)

    row = pl.BlockSpec((tr, c), lambda i: (i, 0))
    res = pl.pallas_call(
        body, name=name, grid=grid,
        in_specs=[pl.BlockSpec((n_parts, tr, c), lambda i: (0, i, 0)), row, row, row] + lay.in_specs,
        out_specs=[row] * 4 + lay.out_specs,
        out_shape=[jax.ShapeDtypeStruct((r, c), F32)] * 4 + lay.out_shapes,
        scratch_shapes=lay.sem_shapes,
        compiler_params=_cparams(1),
    )(parts, w, m, v, *lay.arrays)
    lay.deliver(res[4:])
    return res[:4]


def kernel(x, p, pre_mix_w, w_in, lb_param, a_norm_w, gmlp_ln_w, gmlp_ln_b, w_spatial, b_spatial, w_out, post_mix_w, pre_ffn_w, w_gate, w_up, w_down, post_ffn_w, w_ple, w_ple_gate, post_ple_w, loss_target, m_pre_mix_w, m_w_in, m_lb_param, m_a_norm_w, m_gmlp_ln_w, m_gmlp_ln_b, m_w_spatial, m_b_spatial, m_w_out, m_post_mix_w, m_pre_ffn_w, m_w_gate, m_w_up, m_w_down, m_post_ffn_w, m_w_ple, m_w_ple_gate, m_post_ple_w, v_pre_mix_w, v_w_in, v_lb_param, v_a_norm_w, v_gmlp_ln_w, v_gmlp_ln_b, v_w_spatial, v_b_spatial, v_w_out, v_post_mix_w, v_pre_ffn_w, v_w_gate, v_w_up, v_w_down, v_post_ffn_w, v_w_ple, v_w_ple_gate, v_post_ple_w):
    big_names = ["w_in", "w_out", "w_gate", "w_up", "w_down", "w_ple", "w_ple_gate"]
    small_names = ["pre_mix_w", "lb_param", "a_norm_w", "gmlp_ln_w", "gmlp_ln_b", "w_spatial", "b_spatial",
                   "post_mix_w", "pre_ffn_w", "post_ffn_w", "post_ple_w"]
    all_names = ["pre_mix_w", "w_in", "lb_param", "a_norm_w", "gmlp_ln_w", "gmlp_ln_b", "w_spatial", "b_spatial",
                 "w_out", "post_mix_w", "pre_ffn_w", "w_gate", "w_up", "w_down", "post_ffn_w", "w_ple", "w_ple_gate",
                 "post_ple_w"]
    env = dict(locals())
    W = {n: env[n] for n in all_names}
    M = {n: env["m_" + n] for n in all_names}
    V = {n: env["v_" + n] for n in all_names}

    xs = x[0]
    ps = p[0, 0]
    tgt = loss_target[0]
    t, d = xs.shape
    aw = a_norm_w.shape[1]
    bw = gmlp_ln_w.shape[1]
    n_heads, n_groups = aw // HEAD, bw // HEAD
    core = lax.axis_index("c").astype(jnp.int32).reshape(1)

    transposed = ("w_gate", "w_up")
    local = lambda a, n: jnp.swapaxes(a, 1, 2)[0] if n in transposed else a[0]
    unlocal = lambda a, n: jnp.swapaxes(a[None], 1, 2) if n in transposed else a[None]
    shard = {n: local(W[n], n) for n in big_names}
    bf = {n: _cast_bf16("cast_" + n, shard[n]) for n in big_names}
    ag_in = _ag_comm([bf["w_in"]])
    _comm_only("ag_w_in", [ag_in])
    win_g = ag_in.results[0]
    n_in = bf["w_in"].shape[1]
    ffl = bf["w_gate"].shape[0]
    n_ple = bf["w_ple"].shape[1]
    ple = ps.shape[1]

    TM, TK = 1024, 1024
    tm = _tile(t, TM)
    tkd = _tile(d, TK)
    tn1 = _tile(d, 1024)

    h1 = _rms_fwd("rms_pre_mix", xs, pre_mix_w)
    once = pl.Buffered(1)
    ag_a = _ag_comm([bf["w_gate"]], mid_frac=0.85)
    proj = _matmul(
        "mm_proj",
        [(h1, pl.BlockSpec((tm, d), lambda n, m, k: (m, 0))),
         (win_g, pl.BlockSpec((None, d, n_in), lambda n, m, k: (n, 0, 0), pipeline_mode=once))],
        [(0, 1, "nn", 0)],
        [(jax.ShapeDtypeStruct((t, N_DEV * n_in), F32), pl.BlockSpec((tm, n_in), lambda n, m, k: (m, n)))],
        (N_DEV, t // tm, 1), (tm, n_in), comms=[ag_a])[0]
    wgate_g = ag_a.results[0]
    lb = _lb_fwd(lb_param)
    ag_b = _ag_comm([bf["w_out"]], mid_frac=0.6)
    a_out, o_raw, states, scores = _hgrn_fwd(proj, lb, a_norm_w, n_heads, comms=[ag_b])
    wout_f = ag_b.results[0].reshape(d, d)
    bs_t = b_spatial[0].T
    w_sp = w_spatial[0]
    col_u = (4 * aw) // bw
    b_out = _gmlp_fwd(proj, gmlp_ln_w, gmlp_ln_b, w_sp, bs_t, n_groups, col_u)
    ab = jnp.concatenate([a_out, b_out], axis=1)
    mix = _mm_plain("mm_mix", ab, wout_f, "nn", F32, TM, 1024, d)
    x1, h2 = _resid_rms("resid_mix", xs, mix, post_mix_w, pre_ffn_w)

    def swiglu(accs, gate_v):
        gf = gate_v.astype(F32)
        return accs[0], gf * _sigmoid(gf) * accs[0]

    tmf = _tile(t, 512)
    blk3 = lambda: pl.BlockSpec((None, tmf, ffl), lambda j, m, k: (j, m, 0))
    ag_c = _ag_comm([bf["w_up"]], mid_frac=0.85)
    gate = _matmul(
        "mm_ffn_gate",
        [(h2, pl.BlockSpec((tm, d), lambda j, m, k: (m, 0))),
         (wgate_g, pl.BlockSpec((None, ffl, d), lambda j, m, k: (j, 0, 0), pipeline_mode=once))],
        [(0, 1, "nt", 0)],
        [(jax.ShapeDtypeStruct((N_DEV, t, ffl), BF16), pl.BlockSpec((None, tm, ffl), lambda j, m, k: (j, m, 0)))],
        (N_DEV, t // tm, 1), (tm, ffl), comms=[ag_c])[0]
    wup_g = ag_c.results[0]
    ag_d = _ag_comm([bf["w_down"]], mid_frac=0.85)
    up, act = _matmul(
        "mm_ffn_up",
        [(h2, pl.BlockSpec((tmf, d), lambda j, m, k: (m, 0))),
         (wup_g, pl.BlockSpec((None, ffl, d), lambda j, m, k: (j, 0, 0), pipeline_mode=once)),
         (gate, blk3())],
        [(0, 1, "nt", 0)],
        [(jax.ShapeDtypeStruct((N_DEV, t, ffl), BF16), blk3()) for _ in range(2)],
        (N_DEV, t // tmf, 1), (tmf, ffl), epilogue=swiglu, comms=[ag_d])
    wdown_g = ag_d.results[0]
    ag_e = _ag_comm([bf["w_ple_gate"], bf["w_ple"]], mid_frac=0.6)
    tn_d = _tile(d, 2048)
    ff = _matmul(
        "mm_ffn_down",
        [(act, pl.BlockSpec((None, tm, ffl), lambda m, n, k: (k, m, 0))),
         (wdown_g, pl.BlockSpec((None, ffl, tn_d), lambda m, n, k: (k, 0, n)))],
        [(0, 1, "nn", 0)],
        [(jax.ShapeDtypeStruct((t, d), F32), pl.BlockSpec((tm, tn_d), lambda m, n, k: (m, n)))],
        (t // tm, d // tn_d, N_DEV), (tm, tn_d), comms=[ag_e])[0]
    wpg_f = ag_e.results[0].reshape(d, d)
    wple_g = ag_e.results[1]
    x2, x2b = _resid_rms("resid_ffn", x1, ff, post_ffn_w, None)

    pgl = _mm_plain("mm_ple_gate", x2b, wpg_f, "nn", F32, TM, 1024, d)
    pe = _matmul(
        "mm_ple",
        [(ps, pl.BlockSpec((tm, ple), lambda m, n, k: (m, 0))), (wple_g, pl.BlockSpec((None, ple, n_ple), lambda m, n, k: (n, 0, 0)))],
        [(0, 1, "nn", 0)],
        [(jax.ShapeDtypeStruct((t, N_DEV * n_ple), F32), pl.BlockSpec((tm, n_ple), lambda m, n, k: (m, n)))],
        (t // tm, N_DEV, 1), (tm, n_ple))[0]
    loss_part, d3, dpe, dpgl, g_post_ple = _ple_loss("ple_loss", x2, pe, pgl, post_ple_w, tgt)

    tkt = _tile(t, TK)
    g_wple = _matmul(
        "mm_dw_ple",
        [(ps, pl.BlockSpec((tkt, ple), lambda n, k: (k, 0))), (dpe, pl.BlockSpec((tkt, n_ple), lambda n, k: (k, n)))],
        [(0, 1, "tn", 0)],
        [(jax.ShapeDtypeStruct((N_DEV, ple, n_ple), BF16), pl.BlockSpec((None, ple, n_ple), lambda n, k: (n, 0, 0)))],
        (N_DEV, t // tkt), (ple, n_ple))[0]
    g_wpg = _mm_plain("mm_dw_ple_gate", x2b, dpgl, "tn", BF16, TM, 1024, t)

    def by_chip(g):
        return g.reshape((N_CHIP, 2) + g.shape[-2:])

    def pair_sums(names, comm):
        return [_pair_sum("pair_sum_" + n, g, o, core) for n, g, o in zip(names, comm.arrays, comm.results)]

    r1_p = _pair_comm([by_chip(g_wpg.reshape(N_DEV, d // N_DEV, d)), by_chip(g_wple)])
    d2 = _mm_plain("mm_d_x2", dpgl, wpg_f, "nt", F32, TM, 512, d, extra=d3, epilogue=lambda accs, e: [accs[0] + e],
                   comms=[r1_p])
    r2_p = _chip_comm(pair_sums(["w_ple_gate", "w_ple"], r1_p))

    dff, g_post_ffn = _norm_bwd("norm_bwd_ffn", d2, ff, post_ffn_w)
    g_wdown = _matmul(
        "mm_dw_down",
        [(act, pl.BlockSpec((None, t, ffl), lambda j, n, k: (j, 0, 0), pipeline_mode=once)),
         (dff, pl.BlockSpec((t, tn1), lambda j, n, k: (0, n)))],
        [(0, 1, "tn", 0)],
        [(jax.ShapeDtypeStruct((N_DEV, ffl, d), BF16), pl.BlockSpec((None, ffl, tn1), lambda j, n, k: (j, 0, n)))],
        (N_DEV, d // tn1, 1), (ffl, tn1), comms=[r2_p])[0]
    r1_d = _pair_comm([by_chip(g_wdown)])

    def swiglu_bwd(accs, gate_v, up_v):
        dact = accs[0]
        gf = gate_v.astype(F32)
        sg = _sigmoid(gf)
        return dact * up_v.astype(F32) * (sg * (1.0 + gf * (1.0 - sg))), dact * (gf * sg)

    dgate, dup = _matmul(
        "mm_d_act",
        [(dff, pl.BlockSpec((tmf, d), lambda j, m, k: (m, 0))),
         (wdown_g, pl.BlockSpec((None, ffl, d), lambda j, m, k: (j, 0, 0), pipeline_mode=once)),
         (gate, blk3()), (up, blk3())],
        [(0, 1, "nt", 0)],
        [(jax.ShapeDtypeStruct((N_DEV, t, ffl), BF16), blk3()) for _ in range(2)],
        (N_DEV, t // tmf, 1), (tmf, ffl), epilogue=swiglu_bwd, comms=[r1_d])
    r2_d = _chip_comm(pair_sums(["w_down"], r1_d))
    tmd = _tile(d, TM)
    def dw_ffn(name, dy, comms):
        return _matmul(
            name,
            [(dy, pl.BlockSpec((None, t, ffl), lambda j, n, k: (j, 0, 0), pipeline_mode=once)),
             (h2, pl.BlockSpec((t, tn1), lambda j, n, k: (0, n)))],
            [(0, 1, "tn", 0)],
            [(jax.ShapeDtypeStruct((N_DEV, ffl, d), BF16), pl.BlockSpec((None, ffl, tn1), lambda j, n, k: (j, 0, n)))],
            (N_DEV, d // tn1, 1), (ffl, tn1), comms=comms)[0]

    g_wgate = dw_ffn("mm_dw_gate", dgate, [r2_d])
    r1_g = _pair_comm([by_chip(g_wgate)])
    g_wup = dw_ffn("mm_dw_up", dup, [r1_g])
    r2_g = _chip_comm(pair_sums(["w_gate"], r1_g))
    r1_u = _pair_comm([by_chip(g_wup)])
    tn1 = _tile(d, 1024)
    dh2 = _matmul(
        "mm_d_h2",
        [(dgate, pl.BlockSpec((None, tm, ffl), lambda m, n, k: (k, m, 0))),
         (wgate_g, pl.BlockSpec((None, ffl, tn1), lambda m, n, k: (k, 0, n))),
         (dup, pl.BlockSpec((None, tm, ffl), lambda m, n, k: (k, m, 0))),
         (wup_g, pl.BlockSpec((None, ffl, tn1), lambda m, n, k: (k, 0, n)))],
        [(0, 1, "nn", 0), (2, 3, "nn", 0)],
        [(jax.ShapeDtypeStruct((t, d), F32), pl.BlockSpec((tm, tn1), lambda m, n, k: (m, n)))],
        (t // tm, d // tn1, N_DEV), (tm, tn1), comms=[r2_g, r1_u])[0]
    r2_u = _chip_comm(pair_sums(["w_up"], r1_u))
    d1, g_pre_ffn, dmix, g_post_mix = _prenorm_bwd("prenorm_bwd_ffn", d2, dh2, x1, pre_ffn_w, mix, post_mix_w)

    g_wout = _mm_plain("mm_dw_out", ab, dmix, "tn", BF16, TM, 1024, t)
    r1_o = _pair_comm([by_chip(g_wout.reshape(N_DEV, d // N_DEV, d))])
    dab = _mm_plain("mm_d_ab", dmix, wout_f, "nt", F32, TM, 1024, d, comms=[r1_o])
    r2_o = _chip_comm(pair_sums(["w_out"], r1_o))
    dq, df, di, dg, dlb, g_a_norm = _hgrn_bwd(proj, lb, a_norm_w, o_raw, states, scores, dab, n_heads, comms=[r2_u])
    du, dv, g_ln_w, g_ln_b, g_wsp, g_bs_t = _gmlp_bwd(proj, gmlp_ln_w, gmlp_ln_b, w_sp, bs_t, dab, n_groups, col_u)
    dproj = jnp.concatenate([dq, df, di, dg, du, dv], axis=1)
    g_win = _matmul(
        "mm_dw_in",
        [(h1, pl.BlockSpec((t, tmd), lambda j, m, k: (0, m))),
         (dproj, pl.BlockSpec((t, n_in), lambda j, m, k: (0, j), pipeline_mode=once))],
        [(0, 1, "tn", 0)],
        [(jax.ShapeDtypeStruct((N_DEV, d, n_in), BF16), pl.BlockSpec((None, tmd, n_in), lambda j, m, k: (j, m, 0)))],
        (N_DEV, d // tmd, 1), (tmd, n_in), comms=[r2_o])[0]
    grads, deltas, new_m, new_v = {}, {}, {}, {}

    def adam(n, parts, comms=()):
        res = _adamw("adamw_" + n, parts, shard[n], local(M[n], n), local(V[n], n), comms=comms)
        grads[n], deltas[n], new_m[n], new_v[n] = (unlocal(a, n) for a in res)

    r1_in = _pair_comm([by_chip(g_win)])
    adam("w_down", r2_d.results[0], [r1_in])
    r2_in = _chip_comm(pair_sums(["w_in"], r1_in))
    dh1 = _matmul(
        "mm_d_h1",
        [(dproj, pl.BlockSpec((tm, n_in), lambda m, n, k: (m, k))), (win_g, pl.BlockSpec((None, tn1, n_in), lambda m, n, k: (k, n, 0)))],
        [(0, 1, "nt", 0)],
        [(jax.ShapeDtypeStruct((t, d), F32), pl.BlockSpec((tm, tn1), lambda m, n, k: (m, n)))],
        (t // tm, d // tn1, N_DEV), (tm, tn1), comms=[r2_in])[0]
    grad_x, g_pre_mix = _prenorm_bwd("prenorm_bwd_mix", d1, dh1, xs, pre_mix_w)

    reduced = {
        "w_in": r2_in.results[0], "w_out": r2_o.results[0], "w_gate": r2_g.results[0], "w_up": r2_u.results[0],
        "w_ple": r2_p.results[1], "w_ple_gate": r2_p.results[0],
    }
    small_grad = {
        "pre_mix_w": g_pre_mix, "lb_param": _lb_bwd(lb_param, dlb), "a_norm_w": g_a_norm, "gmlp_ln_w": g_ln_w,
        "gmlp_ln_b": g_ln_b, "w_spatial": g_wsp, "b_spatial": g_bs_t.T, "post_mix_w": g_post_mix,
        "pre_ffn_w": g_pre_ffn, "post_ffn_w": g_post_ffn, "post_ple_w": g_post_ple,
    }
    pack = lambda get: jnp.concatenate([get(n).reshape(-1, LANE) for n in small_names], axis=0)
    ag_small = _ag_comm([pack(lambda n: small_grad[n])], mid_frac=0.5)
    for n in big_names:
        if n != "w_down":
            adam(n, reduced[n], [ag_small] if n == "w_in" else [])
    g_all = ag_small.results[0]
    sg, sd, sm, sv = _adamw("adamw_small", g_all, pack(lambda n: W[n]), pack(lambda n: M[n]), pack(lambda n: V[n]))
    off = 0
    for n in small_names:
        rows = W[n].size // LANE
        for src, dst in ((sg, grads), (sd, deltas), (sm, new_m), (sv, new_v)):
            dst[n] = src[off:off + rows].reshape(W[n].shape)
        off += rows

    loss = lax.psum(loss_part[0, 0], ("x", "y", "c"))
    return (loss, grad_x[None], *[grads[n] for n in all_names], *[deltas[n] for n in all_names],
            *[new_m[n] for n in all_names], *[new_v[n] for n in all_names])
```

```python
import functools

import numpy as np
import jax
import jax.numpy as jnp
from jax import lax
from jax.experimental import pallas as pl
from jax.experimental.pallas import tpu as pltpu

F32 = jnp.float32
BF16 = jnp.bfloat16

EPS = 1e-6
HEAD = 128
GLA_CHUNK = 64
GMLP_CHUNK = 128
N_DEV = 8
N_CHIP = 4
LANE = 128
VMEM_LIMIT = 56 * 1024 * 1024
HGRN_ROWS = 512
ROW_TILE = 128
EPILOGUE_ROWS = 256
HGRN_UNROLL = 1
HGRN_HEADS = 8

ADAM_LR = 0.001
ADAM_B1 = 0.9
ADAM_B2 = 0.999
ADAM_EPS = 1e-08
ADAM_WD = 0.01
ADAM_STEP = 10

MESH = pl.DeviceIdType.MESH
ANY = pl.BlockSpec(memory_space=pl.ANY)

_DIMS = {
    "nn": (((1,), (0,)), ((), ())),
    "nt": (((1,), (1,)), ((), ())),
    "tn": (((0,), (0,)), ((), ())),
}


def _tile(dim, pref):
    return pref if dim % pref == 0 else dim


def _rows(r, bytes_per_row, budget=18 * 1024 * 1024, mult=16):
    best = None
    for cand in range(mult, r + 1, mult):
        if r % cand == 0 and cand * bytes_per_row <= budget:
            best = cand
    return best if best is not None else r


def _cparams(n_axes):
    return pltpu.CompilerParams(dimension_semantics=("arbitrary",) * n_axes, vmem_limit_bytes=VMEM_LIMIT)


def _dot(a, b, form="nn"):
    return lax.dot_general(a.astype(BF16), b.astype(BF16), _DIMS[form], preferred_element_type=F32)


def _split3(x):
    hi = x.astype(BF16)
    r = x - hi.astype(F32)
    mid = r.astype(BF16)
    lo = (r - mid.astype(F32)).astype(BF16)
    return hi, mid, lo


def _dot_exact_l(c, x):
    hi, mid, lo = _split3(x)
    d = lambda y: lax.dot_general(c, y, _DIMS["nn"], preferred_element_type=F32)
    return d(hi) + d(mid) + d(lo)


def _dot_exact_r(x, c):
    hi, mid, lo = _split3(x)
    d = lambda y: lax.dot_general(y, c, _DIMS["nn"], preferred_element_type=F32)
    return d(hi) + d(mid) + d(lo)


def _sigmoid(x):
    return 1.0 / (1.0 + jnp.exp(-x))


def _gelu(x):
    return 0.5 * x * (1.0 + lax.erf(x * 0.7071067811865476))


def _gelu_grad(x):
    cdf = 0.5 * (1.0 + lax.erf(x * 0.7071067811865476))
    pdf = jnp.exp(-0.5 * x * x) * 0.3989422804014327
    return cdf + x * pdf


def _position():
    return lax.axis_index("x"), lax.axis_index("y"), lax.axis_index("c")


def _linear_step(grid):
    step = 0
    for ax, n in enumerate(grid):
        step = step * n + pl.program_id(ax)
    return step


class _Comm:
    def __init__(self, arrays, out_shapes, sem_shapes, phases):
        self.arrays, self.out_shapes, self.sem_shapes, self.phases = list(arrays), list(out_shapes), list(sem_shapes), phases
        self.results = None


class _CommLayout:
    def __init__(self, comms, space=pl.ANY):
        self.comms = list(comms)
        self.arrays = [a for c in self.comms for a in c.arrays]
        self.out_shapes = [s for c in self.comms for s in c.out_shapes]
        self.sem_shapes = [s for c in self.comms for s in c.sem_shapes]
        self.n_in, self.n_out = len(self.arrays), len(self.out_shapes)
        self.in_specs = [pl.BlockSpec(memory_space=space)] * self.n_in
        self.out_specs = [pl.BlockSpec(memory_space=space)] * self.n_out

    def run(self, cin, cout, csem, step, n_steps, post):
        i = o = s = 0
        for c in self.comms:
            ins, outs, sems = cin[i:i + len(c.arrays)], cout[o:o + len(c.out_shapes)], csem[s:s + len(c.sem_shapes)]
            i, o, s = i + len(c.arrays), o + len(c.out_shapes), s + len(c.sem_shapes)
            for frac, fn in c.phases:
                if (frac is None) != post:
                    continue
                due = n_steps - 1 if frac is None else max(0, min(int(frac * n_steps), n_steps - 2))
                if n_steps == 1:
                    fn(ins, outs, sems)
                else:
                    pl.when(step == due)(functools.partial(fn, ins, outs, sems))

    def deliver(self, results):
        o = 0
        for c in self.comms:
            c.results = list(results[o:o + len(c.out_shapes)])
            o += len(c.out_shapes)


def _comm_only(name, comms, in_vmem=False):
    lay = _CommLayout(comms, pltpu.VMEM if in_vmem else pl.ANY)

    def body(*refs):
        cin, cout, csem = refs[:lay.n_in], refs[lay.n_in:lay.n_in + lay.n_out], refs[lay.n_in + lay.n_out:]
        lay.run(cin, cout, csem, 0, 1, post=False)
        lay.run(cin, cout, csem, 0, 1, post=True)

    res = pl.pallas_call(
        body, name=name, in_specs=lay.in_specs, out_specs=lay.out_specs, out_shape=lay.out_shapes,
        scratch_shapes=lay.sem_shapes,
    )(*lay.arrays)
    lay.deliver(res)


def _ag_comm(shards, mid_frac=0.0, start_frac=0.0):
    n = len(shards)
    per = N_DEV - 1

    def tools(ins, outs, sems):
        send_sems, recv_sems, local_sems = sems
        x, y, c = _position()
        me, sibling = (x, y, c), (x, y, 1 - c)
        chips = [(1 - x, y), (x, 1 - y), (1 - x, 1 - y)]

        def copy(a, k, block, to, from_shard=False):
            dst = outs[a].at[4 * block[0] + 2 * block[1] + block[2]]
            return pltpu.make_async_remote_copy(
                src_ref=ins[a] if from_shard else dst, dst_ref=dst,
                send_sem=send_sems.at[a * per + k], recv_sem=recv_sems.at[a * per + k],
                device_id=to, device_id_type=MESH)

        def local(a):
            return pltpu.make_async_copy(ins[a], outs[a].at[4 * x + 2 * y + c], local_sems.at[a])

        return me, sibling, chips, c, copy, local

    def first(ins, outs, sems):
        me, sibling, chips, c, copy, local = tools(ins, outs, sems)
        for a in range(n):
            local(a).start()
            copy(a, 0, me, sibling, True).start()
            for j, chip in enumerate(chips):
                copy(a, 1 + j, me, (*chip, c), True).start()

    def middle(ins, outs, sems):
        me, sibling, chips, c, copy, local = tools(ins, outs, sems)
        for a in range(n):
            for j, chip in enumerate(chips):
                copy(a, 1 + j, (*chip, c), me).wait_recv()
                copy(a, 4 + j, (*chip, c), sibling).start()

    def last(ins, outs, sems):
        me, sibling, chips, c, copy, local = tools(ins, outs, sems)
        for a in range(n):
            copy(a, 0, sibling, me).wait_recv()
            copy(a, 0, me, sibling, True).wait_send()
            for j, chip in enumerate(chips):
                copy(a, 4 + j, (*chip, 1 - c), me).wait_recv()
                copy(a, 1 + j, me, (*chip, c), True).wait_send()
                copy(a, 4 + j, (*chip, c), sibling).wait_send()
            local(a).wait()

    return _Comm(
        shards, [jax.ShapeDtypeStruct((N_DEV,) + s.shape, s.dtype) for s in shards],
        [pltpu.SemaphoreType.DMA((n * per,)), pltpu.SemaphoreType.DMA((n * per,)), pltpu.SemaphoreType.DMA((n,))],
        [(start_frac, first), (max(mid_frac, start_frac), middle), (None, last)])


def _pair_comm(grads):
    n = len(grads)

    def copies(ins, outs, sems):
        send_sems, recv_sems = sems
        x, y, c = _position()
        return [pltpu.make_async_remote_copy(
            src_ref=ins[a].at[k, 1 - c], dst_ref=outs[a].at[k],
            send_sem=send_sems.at[a * N_CHIP + k], recv_sem=recv_sems.at[a * N_CHIP + k],
            device_id=(x, y, 1 - c), device_id_type=MESH) for a in range(n) for k in range(N_CHIP)]

    def first(ins, outs, sems):
        for cp in copies(ins, outs, sems):
            cp.start()

    def last(ins, outs, sems):
        for cp in copies(ins, outs, sems):
            cp.wait()

    return _Comm(
        grads, [jax.ShapeDtypeStruct((N_CHIP,) + g.shape[2:], g.dtype) for g in grads],
        [pltpu.SemaphoreType.DMA((n * N_CHIP,)), pltpu.SemaphoreType.DMA((n * N_CHIP,))],
        [(0.0, first), (None, last)])


def _chip_comm(sums):
    n = len(sums)
    per = N_CHIP - 1

    def copies(ins, outs, sems):
        send_sems, recv_sems, local_sems = sems
        x, y, c = _position()
        my_chip = 2 * x + y
        cps = []
        for a in range(n):
            cps.append(pltpu.make_async_copy(ins[a].at[my_chip], outs[a].at[my_chip], local_sems.at[a]))
            for j, (px, py) in enumerate([(1 - x, y), (x, 1 - y), (1 - x, 1 - y)]):
                cps.append(pltpu.make_async_remote_copy(
                    src_ref=ins[a].at[2 * px + py], dst_ref=outs[a].at[my_chip],
                    send_sem=send_sems.at[a * per + j], recv_sem=recv_sems.at[a * per + j],
                    device_id=(px, py, c), device_id_type=MESH))
        return cps

    def first(ins, outs, sems):
        for cp in copies(ins, outs, sems):
            cp.start()

    def last(ins, outs, sems):
        for cp in copies(ins, outs, sems):
            cp.wait()

    return _Comm(
        sums, [jax.ShapeDtypeStruct(s.shape, s.dtype) for s in sums],
        [pltpu.SemaphoreType.DMA((n * per,)), pltpu.SemaphoreType.DMA((n * per,)), pltpu.SemaphoreType.DMA((n,))],
        [(0.0, first), (None, last)])


def _matmul(name, operands, pairs, outs, grid, acc_shape, n_slots=1, epilogue=None, comms=()):
    used = sorted({i for p in pairs for i in p[:2]})
    n_op = len(operands)
    n_out = len(outs)
    k_axis = len(grid) - 1
    n_k = grid[-1]
    lay = _CommLayout(comms)

    direct = n_k == 1 and epilogue is None
    n_acc = 0 if direct else 1

    def body(*refs):
        ops = refs[:n_op]
        out_refs = refs[n_op + lay.n_in:n_op + lay.n_in + n_out]
        acc = None if direct else refs[n_op + lay.n_in + n_out + lay.n_out]
        k = pl.program_id(k_axis)
        step = _linear_step(grid)
        cin = refs[n_op:n_op + lay.n_in]
        cout = refs[n_op + lay.n_in + n_out:n_op + lay.n_in + n_out + lay.n_out]
        csem = refs[n_op + lay.n_in + n_out + lay.n_out + n_acc:]
        lay.run(cin, cout, csem, step, int(np.prod(grid)), post=False)

        if n_k > 1:
            @pl.when(k == 0)
            def _():
                acc[...] = jnp.zeros_like(acc)

        vals = {i: ops[i][...] for i in used}
        vals = {i: (v if v.dtype == BF16 else v.astype(BF16)) for i, v in vals.items()}
        for s in range(n_slots):
            tot = None
            for ia, ib, form, slot in pairs:
                if slot != s:
                    continue
                d = lax.dot_general(vals[ia], vals[ib], _DIMS[form], preferred_element_type=F32)
                tot = d if tot is None else tot + d
            if direct:
                out_refs[s][...] = tot.astype(out_refs[s].dtype)
            elif n_k == 1:
                acc[s] = tot
            else:
                acc[s] += tot

        def finish():
            rows = acc_shape[0]
            chunk = EPILOGUE_ROWS if (epilogue is not None and rows % EPILOGUE_ROWS == 0) else rows
            for r0 in range(0, rows, chunk):
                sl = slice(r0, r0 + chunk)
                accs = [acc[s, sl, :] for s in range(n_slots)]
                extra = [ops[i][sl, :] for i in range(n_op) if i not in used]
                res = epilogue(accs, *extra) if epilogue is not None else accs
                for o, v in zip(out_refs, res):
                    o[sl, :] = v.astype(o.dtype)

        if n_k > 1:
            pl.when(k == n_k - 1)(finish)
        elif not direct:
            finish()

        lay.run(cin, cout, csem, step, int(np.prod(grid)), post=True)

    res = pl.pallas_call(
        body,
        name=name,
        grid=grid,
        in_specs=[s for _, s in operands] + lay.in_specs,
        out_specs=[s for _, s in outs] + lay.out_specs,
        out_shape=[s for s, _ in outs] + lay.out_shapes,
        scratch_shapes=([] if direct else [pltpu.VMEM((n_slots,) + tuple(acc_shape), F32)]) + lay.sem_shapes,
        compiler_params=_cparams(len(grid)),
    )(*[a for a, _ in operands], *lay.arrays)
    lay.deliver(res[n_out:])
    return res[:n_out]


def _mm_plain(name, a, b, form, out_dtype, tm, tn, tk, extra=None, epilogue=None, comms=()):
    if form == "nn":
        (M, K), N = a.shape, b.shape[1]
    elif form == "nt":
        (M, K), N = a.shape, b.shape[0]
    else:
        (K, M), N = a.shape, b.shape[1]
    tm, tn, tk = _tile(M, tm), _tile(N, tn), _tile(K, tk)
    a_spec = pl.BlockSpec((tk, tm), lambda m, n, k: (k, m)) if form == "tn" else pl.BlockSpec((tm, tk), lambda m, n, k: (m, k))
    b_spec = pl.BlockSpec((tn, tk), lambda m, n, k: (n, k)) if form == "nt" else pl.BlockSpec((tk, tn), lambda m, n, k: (k, n))
    operands = [(a, a_spec), (b, b_spec)]
    if extra is not None:
        operands.append((extra, pl.BlockSpec((tm, tn), lambda m, n, k: (m, n))))
    out = (jax.ShapeDtypeStruct((M, N), out_dtype), pl.BlockSpec((tm, tn), lambda m, n, k: (m, n)))
    return _matmul(name, operands, [(0, 1, form, 0)], [out], (M // tm, N // tn, K // tk), (tm, tn), epilogue=epilogue,
                   comms=comms)[0]


def _ag_matmul(name, a, shard, form, out_dtype, blocked_out, comms=()):
    t, kdim = a.shape
    nl = shard.shape[1] if form == "nn" else shard.shape[0]
    tm = _tile(t, 1024)
    grid = (N_DEV, t // tm)
    n_steps = grid[0] * grid[1]
    per = N_DEV - 1
    lay = _CommLayout(comms)
    ident = lambda bx, by, bc: 4 * bx + 2 * by + bc

    def block_of(n):
        x, y, c = _position()
        far = jnp.where(n < 2, 0, jnp.where(n < 6, 1 + (n & 1), 3))
        bx = jnp.where((far == 1) | (far == 3), 1 - x, x)
        by = jnp.where(far >= 2, 1 - y, y)
        bc = jnp.where((n == 0) | (n == 2) | (n == 3) | (n == 6), c, 1 - c)
        return ident(bx, by, bc)

    def body(*refs):
        a_ref, shard_ref = refs[:2]
        cin = refs[2:2 + lay.n_in]
        out_ref, land_ref = refs[2 + lay.n_in:4 + lay.n_in]
        cout = refs[4 + lay.n_in:4 + lay.n_in + lay.n_out]
        bbuf, send_sems, recv_sems, misc_sems = refs[4 + lay.n_in + lay.n_out:8 + lay.n_in + lay.n_out]
        csem = refs[8 + lay.n_in + lay.n_out:]
        n, m = pl.program_id(0), pl.program_id(1)
        step = n * grid[1] + m
        x, y, c = _position()
        me, sibling = (x, y, c), (x, y, 1 - c)
        chips = [(1 - x, y), (x, 1 - y), (1 - x, 1 - y)]

        def copy(k, block, to, from_shard=False):
            dst = land_ref.at[ident(*block)]
            return pltpu.make_async_remote_copy(
                src_ref=shard_ref if from_shard else dst, dst_ref=dst,
                send_sem=send_sems.at[k], recv_sem=recv_sems.at[k], device_id=to, device_id_type=MESH)

        local = pltpu.make_async_copy(shard_ref, land_ref.at[ident(*me)], misc_sems.at[0])

        def load(src):
            cp = pltpu.make_async_copy(src, bbuf, misc_sems.at[1])
            cp.start()
            cp.wait()

        @pl.when(step == 0)
        def _():
            local.start()
            copy(0, me, sibling, True).start()
            for j, chip in enumerate(chips):
                copy(1 + j, me, (*chip, c), True).start()

        lay.run(cin, cout, csem, step, n_steps, post=False)

        @pl.when(m == 0)
        def _():
            @pl.when(n == 0)
            def _():
                load(shard_ref)

            @pl.when(n == 1)
            def _():
                copy(0, sibling, me).wait_recv()
                load(land_ref.at[ident(*sibling)])

            for j, chip in enumerate(chips):
                @pl.when(n == (2, 3, 6)[j])
                def _(j=j, chip=chip):
                    copy(1 + j, (*chip, c), me).wait_recv()
                    copy(4 + j, (*chip, c), sibling).start()
                    load(land_ref.at[ident(*chip, c)])

                @pl.when(n == (4, 5, 7)[j])
                def _(j=j, chip=chip):
                    copy(4 + j, (*chip, 1 - c), me).wait_recv()
                    load(land_ref.at[ident(*chip, 1 - c)])

        out_ref[...] = lax.dot_general(a_ref[...], bbuf[...], _DIMS[form], preferred_element_type=F32).astype(out_ref.dtype)

        @pl.when(step == n_steps - 1)
        def _():
            copy(0, me, sibling, True).wait_send()
            for j, chip in enumerate(chips):
                copy(1 + j, me, (*chip, c), True).wait_send()
                copy(4 + j, (*chip, c), sibling).wait_send()
            local.wait()

        lay.run(cin, cout, csem, step, n_steps, post=True)

    if blocked_out:
        out_shape = jax.ShapeDtypeStruct((N_DEV, t, nl), out_dtype)
        out_spec = pl.BlockSpec((None, tm, nl), lambda n, m: (block_of(n), m, 0))
    else:
        out_shape = jax.ShapeDtypeStruct((t, N_DEV * nl), out_dtype)
        out_spec = pl.BlockSpec((tm, nl), lambda n, m: (m, block_of(n)))
    res = pl.pallas_call(
        body, name=name, grid=grid,
        in_specs=[pl.BlockSpec((tm, kdim), lambda n, m: (m, 0)), ANY] + lay.in_specs,
        out_specs=[out_spec, ANY] + lay.out_specs,
        scratch_shapes=[pltpu.VMEM(shard.shape, shard.dtype), pltpu.SemaphoreType.DMA((per,)),
                        pltpu.SemaphoreType.DMA((per,)), pltpu.SemaphoreType.DMA((2,))] + lay.sem_shapes,
        out_shape=[out_shape, jax.ShapeDtypeStruct((N_DEV,) + shard.shape, shard.dtype)] + lay.out_shapes,
        compiler_params=_cparams(2),
    )(a, shard, *lay.arrays)
    lay.deliver(res[2:])
    return res[0], res[1]


def _cast_bf16(name, w):
    r, c = w.shape
    tr = _rows(r, 6 * c)

    def body(w_ref, o_ref):
        o_ref[...] = w_ref[...].astype(BF16)

    return pl.pallas_call(
        body, name=name, grid=(r // tr,),
        in_specs=[pl.BlockSpec((tr, c), lambda i: (i, 0))],
        out_specs=pl.BlockSpec((tr, c), lambda i: (i, 0)),
        out_shape=jax.ShapeDtypeStruct((r, c), BF16),
        compiler_params=_cparams(1),
    )(w)


def _rms_stats(x):
    r = lax.rsqrt(jnp.mean(x * x, axis=-1, keepdims=True) + EPS)
    return x * r, r


def _rms_bwd(xhat, r, w, dy):
    dxh = dy * w
    return r * (dxh - xhat * jnp.mean(dxh * xhat, axis=-1, keepdims=True))


def _row_spec(tr, d):
    return pl.BlockSpec((tr, d), lambda i: (i, 0))


def _vec_spec(d):
    return pl.BlockSpec((1, d), lambda i: (0, 0))


def _rms_fwd(name, x, w):
    t, d = x.shape
    tr = _tile(t, ROW_TILE)

    def body(x_ref, w_ref, h_ref):
        xh, _ = _rms_stats(x_ref[...])
        h_ref[...] = (xh * w_ref[...]).astype(BF16)

    return pl.pallas_call(
        body, name=name, grid=(t // tr,),
        in_specs=[_row_spec(tr, d), _vec_spec(d)],
        out_specs=_row_spec(tr, d),
        out_shape=jax.ShapeDtypeStruct((t, d), BF16),
        compiler_params=_cparams(1),
    )(x, w)


def _resid_rms(name, xres, y, w_post, w_next):
    t, d = xres.shape
    tr = _tile(t, ROW_TILE)
    has_next = w_next is not None

    def body(*refs):
        if has_next:
            x_ref, y_ref, wp_ref, wn_ref, xo_ref, h_ref = refs
        else:
            x_ref, y_ref, wp_ref, xo_ref, h_ref = refs
        yh, _ = _rms_stats(y_ref[...])
        xn = x_ref[...] + yh * wp_ref[...]
        xo_ref[...] = xn
        if has_next:
            xh, _ = _rms_stats(xn)
            h_ref[...] = (xh * wn_ref[...]).astype(BF16)
        else:
            h_ref[...] = xn.astype(BF16)

    ins = [xres, y, w_post] + ([w_next] if has_next else [])
    in_specs = [_row_spec(tr, d), _row_spec(tr, d), _vec_spec(d)] + ([_vec_spec(d)] if has_next else [])
    return pl.pallas_call(
        body, name=name, grid=(t // tr,),
        in_specs=in_specs,
        out_specs=[_row_spec(tr, d), _row_spec(tr, d)],
        out_shape=[jax.ShapeDtypeStruct((t, d), F32), jax.ShapeDtypeStruct((t, d), BF16)],
        compiler_params=_cparams(1),
    )(*ins)


def _ple_loss(name, x2, pe, pgl, w_pp, tgt):
    t, d = x2.shape
    tr = _tile(t, ROW_TILE)

    def body(x2_ref, pe_ref, pgl_ref, w_ref, tgt_ref, loss_ref, d3_ref, dpe_ref, dpgl_ref, dw_ref):
        @pl.when(pl.program_id(0) == 0)
        def _():
            loss_ref[...] = jnp.zeros_like(loss_ref)
            dw_ref[...] = jnp.zeros_like(dw_ref)

        pe_v = pe_ref[...]
        s = _sigmoid(pgl_ref[...])
        y = pe_v * s
        yh, r = _rms_stats(y)
        w = w_ref[...]
        err = x2_ref[...] + yh * w - tgt_ref[...]
        loss_ref[...] += 0.5 * jnp.sum(jnp.mean(err * err, axis=-1, keepdims=True), axis=0, keepdims=True)
        d3 = err * (1.0 / d)
        d3_ref[...] = d3
        dw_ref[...] += jnp.sum(d3 * yh, axis=0, keepdims=True)
        dy = _rms_bwd(yh, r, w, d3)
        dpe_ref[...] = (dy * s).astype(BF16)
        dpgl_ref[...] = (dy * pe_v * s * (1.0 - s)).astype(BF16)

    return pl.pallas_call(
        body, name=name, grid=(t // tr,),
        in_specs=[_row_spec(tr, d), _row_spec(tr, d), _row_spec(tr, d), _vec_spec(d), _row_spec(tr, d)],
        out_specs=[pl.BlockSpec((1, 1), lambda i: (0, 0)), _row_spec(tr, d), _row_spec(tr, d), _row_spec(tr, d), _vec_spec(d)],
        out_shape=[jax.ShapeDtypeStruct((1, 1), F32), jax.ShapeDtypeStruct((t, d), F32),
                   jax.ShapeDtypeStruct((t, d), BF16), jax.ShapeDtypeStruct((t, d), BF16),
                   jax.ShapeDtypeStruct((1, d), F32)],
        compiler_params=_cparams(1),
    )(x2, pe, pgl, w_pp, tgt)


def _norm_bwd(name, dres, y, w_post):
    t, d = dres.shape
    tr = _tile(t, ROW_TILE)

    def body(d_ref, y_ref, w_ref, dy_ref, dw_ref):
        @pl.when(pl.program_id(0) == 0)
        def _():
            dw_ref[...] = jnp.zeros_like(dw_ref)

        dv = d_ref[...]
        yh, r = _rms_stats(y_ref[...])
        dw_ref[...] += jnp.sum(dv * yh, axis=0, keepdims=True)
        dy_ref[...] = _rms_bwd(yh, r, w_ref[...], dv).astype(BF16)

    return pl.pallas_call(
        body, name=name, grid=(t // tr,),
        in_specs=[_row_spec(tr, d), _row_spec(tr, d), _vec_spec(d)],
        out_specs=[_row_spec(tr, d), _vec_spec(d)],
        out_shape=[jax.ShapeDtypeStruct((t, d), BF16), jax.ShapeDtypeStruct((1, d), F32)],
        compiler_params=_cparams(1),
    )(dres, y, w_post)


def _prenorm_bwd(name, dres, dh, xin, w_pre, y=None, w_post=None):
    t, d = dres.shape
    tr = _tile(t, ROW_TILE)
    two = y is not None

    def body(*refs):
        if two:
            d_ref, dh_ref, x_ref, wpre_ref, y_ref, wpost_ref, do_ref, dwpre_ref, dy_ref, dwpost_ref = refs
        else:
            d_ref, dh_ref, x_ref, wpre_ref, do_ref, dwpre_ref = refs

        @pl.when(pl.program_id(0) == 0)
        def _():
            dwpre_ref[...] = jnp.zeros_like(dwpre_ref)
            if two:
                dwpost_ref[...] = jnp.zeros_like(dwpost_ref)

        dhv = dh_ref[...]
        xh, r = _rms_stats(x_ref[...])
        dwpre_ref[...] += jnp.sum(dhv * xh, axis=0, keepdims=True)
        dout = d_ref[...] + _rms_bwd(xh, r, wpre_ref[...], dhv)
        do_ref[...] = dout
        if two:
            yh, ry = _rms_stats(y_ref[...])
            dwpost_ref[...] += jnp.sum(dout * yh, axis=0, keepdims=True)
            dy_ref[...] = _rms_bwd(yh, ry, wpost_ref[...], dout).astype(BF16)

    ins = [dres, dh, xin, w_pre] + ([y, w_post] if two else [])
    in_specs = [_row_spec(tr, d)] * 3 + [_vec_spec(d)] + ([_row_spec(tr, d), _vec_spec(d)] if two else [])
    out_specs = [_row_spec(tr, d), _vec_spec(d)] + ([_row_spec(tr, d), _vec_spec(d)] if two else [])
    out_shape = [jax.ShapeDtypeStruct((t, d), F32), jax.ShapeDtypeStruct((1, d), F32)]
    if two:
        out_shape += [jax.ShapeDtypeStruct((t, d), BF16), jax.ShapeDtypeStruct((1, d), F32)]
    return pl.pallas_call(
        body, name=name, grid=(t // tr,),
        in_specs=in_specs, out_specs=out_specs, out_shape=out_shape,
        compiler_params=_cparams(1),
    )(*ins)


_LEVELS = (32, 16, 8, 4, 2, 1)
_N_CUM = 3 + 2 * len(_LEVELS)


def _hgrn_constants():
    c = GLA_CHUNK
    idx = np.arange(c)
    t, r = idx[:, None], idx[None, :]
    mats = [(r <= t), (r > t), np.ones((c, c), bool)]
    lq, lk, masks = [], [], []
    for h in _LEVELS:
        blk, pos = idx // (2 * h), idx % (2 * h)
        mid = blk * 2 * h + h - 1
        upper, lower = pos >= h, pos < h
        lq.append(upper[:, None] & (r > mid[:, None]) & (r <= t))
        lk.append(lower[:, None] & (r > t) & (r <= mid[:, None]))
        masks.append((blk[:, None] == blk[None, :]) & upper[:, None] & lower[None, :])
    cum = np.concatenate(mats + lq + lk, axis=0).astype(np.float32)
    rev = (r >= t).astype(np.float32)
    return (jnp.asarray(cum, BF16), jnp.asarray(rev, BF16), jnp.asarray(np.stack(masks).astype(np.float32)))


def _hgrn_gates(qp, fp, lb):
    sq = _sigmoid(qp)
    q = qp * sq
    sg = _sigmoid(fp)
    f = lb + (1.0 - lb) * sg
    k = 1.0 - f
    logf = jnp.log(jnp.maximum(f, 1e-30))
    return q, sq, sg, f, k, logf


def _hgrn_decays(cum_ref, logf):
    c = GLA_CHUNK
    e = jnp.exp(_dot_exact_l(cum_ref[...], logf))
    part = lambda i: e[i * c:(i + 1) * c]
    n = len(_LEVELS)
    return part(0), part(1), part(2), [part(3 + i) for i in range(n)], [part(3 + n + i) for i in range(n)]


def _hgrn_fwd(proj, lb, nw, n_heads, comms=()):
    t = proj.shape[0]
    aw = n_heads * HEAD
    rb = _tile(t, HGRN_ROWS)
    c = GLA_CHUNK
    n_sub = rb // c
    cum, _, masks = _hgrn_constants()
    lay = _CommLayout(comms)
    hp = HGRN_HEADS if n_heads % HGRN_HEADS == 0 else 1
    wd = hp * HEAD
    grid = (n_heads // hp, t // rb)

    def body(*refs):
        q_ref, f_ref, i_ref, g_ref, lb_ref, nw_ref, cum_ref, m_ref = refs[:8]
        cin = refs[8:8 + lay.n_in]
        a_ref, o_ref, s_ref, sc_ref = refs[8 + lay.n_in:12 + lay.n_in]
        cout = refs[12 + lay.n_in:12 + lay.n_in + lay.n_out]
        st = refs[12 + lay.n_in + lay.n_out]
        csem = refs[13 + lay.n_in + lay.n_out:]
        step = _linear_step(grid)
        lay.run(cin, cout, csem, step, grid[0] * grid[1], post=False)

        @pl.when(pl.program_id(1) == 0)
        def _():
            st[...] = jnp.zeros_like(st)

        lbv = lb_ref[...]
        nwv = nw_ref[...]
        eye = (lax.broadcasted_iota(jnp.int32, (c, c), 0) == lax.broadcasted_iota(jnp.int32, (c, c), 1)).astype(F32)
        heads = range(hp)
        hs = lambda a, h: a[:, h * HEAD:(h + 1) * HEAD]

        def chunk(j, carry):
            rows = pl.ds(pl.multiple_of(j * c, c), c)
            q, _, _, _, k, logf = _hgrn_gates(q_ref[rows, :], f_ref[rows, :], lbv)
            v = i_ref[rows, :]
            eb, ebe, eend, eq, ek = _hgrn_decays(cum_ref, logf)
            qt, kt, qk = q * eb, k * ebe, q * k
            s_in = [st[h] for h in heads]
            for h in heads:
                s_ref[h, j] = s_in[h]
            inter = [_dot(hs(qt, h), s_in[h], "nt") for h in heads]
            for h in heads:
                st[h] = s_in[h] * hs(eend, h)[0:1] + _dot(hs(v, h), hs(kt, h), "tn")
            scores = [eye * jnp.sum(hs(qk, h), axis=-1, keepdims=True) for h in heads]
            for lvl in range(len(_LEVELS)):
                ql, kl = q * eq[lvl], k * ek[lvl]
                for h in heads:
                    scores[h] = scores[h] + m_ref[lvl] * _dot(hs(ql, h), hs(kl, h), "nt")
            gv = g_ref[rows, :]
            gate = nwv * (gv * _sigmoid(gv))
            for h in heads:
                sc_ref[h, rows, :] = scores[h]
                o = inter[h] + _dot(scores[h], hs(v, h))
                o_ref[rows, h * HEAD:(h + 1) * HEAD] = o
                r = lax.rsqrt(jnp.mean(o * o, axis=-1, keepdims=True) + EPS)
                a_ref[rows, h * HEAD:(h + 1) * HEAD] = (o * r * hs(gate, h)).astype(BF16)
            return carry

        lax.fori_loop(0, n_sub, chunk, 0, unroll=HGRN_UNROLL)
        lay.run(cin, cout, csem, step, grid[0] * grid[1], post=True)

    n_hb = n_heads // hp
    col = lambda base: pl.BlockSpec((rb, wd), lambda h, r: (r, base * n_hb + h))
    vec = pl.BlockSpec((1, wd), lambda h, r: (0, h))
    res = pl.pallas_call(
        body, name="hgrn2_fwd", grid=grid,
        in_specs=[col(0), col(1), col(2), col(3), vec, vec,
                  pl.BlockSpec(cum.shape, lambda h, r: (0, 0)), pl.BlockSpec(masks.shape, lambda h, r: (0, 0, 0))] + lay.in_specs,
        out_specs=[pl.BlockSpec((rb, wd), lambda h, r: (r, h)), pl.BlockSpec((rb, wd), lambda h, r: (r, h)),
                   pl.BlockSpec((hp, n_sub, HEAD, HEAD), lambda h, r: (h, r, 0, 0)),
                   pl.BlockSpec((hp, rb, c), lambda h, r: (h, r, 0))] + lay.out_specs,
        out_shape=[jax.ShapeDtypeStruct((t, aw), BF16), jax.ShapeDtypeStruct((t, aw), F32),
                   jax.ShapeDtypeStruct((n_heads, t // c, HEAD, HEAD), F32),
                   jax.ShapeDtypeStruct((n_heads, t, c), F32)] + lay.out_shapes,
        scratch_shapes=[pltpu.VMEM((hp, HEAD, HEAD), F32)] + lay.sem_shapes,
        compiler_params=_cparams(2),
    )(proj, proj, proj, proj, lb, nw, cum, masks, *lay.arrays)
    lay.deliver(res[4:])
    return res[:4]


def _hgrn_bwd(proj, lb, nw, o_raw, states, scores, dab, n_heads, comms=()):
    t = proj.shape[0]
    aw = n_heads * HEAD
    rb = _tile(t, HGRN_ROWS)
    c = GLA_CHUNK
    n_sub = rb // c
    n_rb = t // rb
    cum, rev, masks = _hgrn_constants()
    lay = _CommLayout(comms)
    hp = HGRN_HEADS if n_heads % HGRN_HEADS == 0 else 1
    wd = hp * HEAD
    grid = (n_heads // hp, n_rb)

    def body(*refs):
        q_ref, f_ref, i_ref, g_ref, lb_ref, nw_ref, o_ref, s_ref, sc_ref, da_ref, cum_ref, rev_ref, m_ref = refs[:13]
        cin = refs[13:13 + lay.n_in]
        dq_ref, df_ref, di_ref, dg_ref, dlb_ref, dnw_ref = refs[13 + lay.n_in:19 + lay.n_in]
        cout = refs[19 + lay.n_in:19 + lay.n_in + lay.n_out]
        dst = refs[19 + lay.n_in + lay.n_out]
        csem = refs[20 + lay.n_in + lay.n_out:]
        step = _linear_step(grid)
        lay.run(cin, cout, csem, step, grid[0] * grid[1], post=False)

        @pl.when(pl.program_id(1) == 0)
        def _():
            dst[...] = jnp.zeros_like(dst)
            dlb_ref[...] = jnp.zeros_like(dlb_ref)
            dnw_ref[...] = jnp.zeros_like(dnw_ref)

        lbv = lb_ref[...]
        nwv = nw_ref[...]
        ri = lax.broadcasted_iota(jnp.int32, (c, c), 0)
        ci = lax.broadcasted_iota(jnp.int32, (c, c), 1)
        eye = (ri == ci).astype(F32)
        causal = (ci <= ri).astype(F32)
        last_row = (lax.broadcasted_iota(jnp.int32, (c, wd), 0) == c - 1).astype(F32)
        heads = range(hp)
        hs = lambda a, h: a[:, h * HEAD:(h + 1) * HEAD]
        wide = lambda parts: parts[0] if hp == 1 else jnp.concatenate(parts, axis=1)

        def head_mean(a):
            return wide([jnp.broadcast_to(jnp.mean(hs(a, h), axis=-1, keepdims=True), (c, HEAD)) for h in heads])

        def chunk(jj, carry):
            j = n_sub - 1 - jj
            rows = pl.ds(pl.multiple_of(j * c, c), c)
            qp = q_ref[rows, :]
            q, sq, sg, f, k, logf = _hgrn_gates(qp, f_ref[rows, :], lbv)
            v = i_ref[rows, :]
            gv = g_ref[rows, :]
            eb, ebe, eend, eq, ek = _hgrn_decays(cum_ref, logf)
            s_in = [s_ref[h, j] for h in heads]
            a_sc = [sc_ref[h, rows, :] for h in heads]
            dsn = [dst[h] for h in heads]
            o = o_ref[rows, :]
            r = lax.rsqrt(head_mean(o * o) + EPS)
            oh = o * r
            sgg = _sigmoid(gv)
            sil = gv * sgg
            da = da_ref[rows, :]
            dg_ref[rows, :] = (da * oh * nwv * (sgg * (1.0 + gv * (1.0 - sgg)))).astype(BF16)
            dnw_ref[...] += jnp.sum(da * oh * sil, axis=0, keepdims=True)
            doh = da * nwv * sil
            do = r * (doh - oh * head_mean(doh * oh))
            kt = k * ebe
            qt = q * eb
            d_sc = [_dot(hs(do, h), hs(v, h), "nt") * causal for h in heads]
            dqt = wide([_dot(hs(do, h), s_in[h]) for h in heads])
            dkt = wide([_dot(hs(v, h), dsn[h]) for h in heads])
            for h in heads:
                dst[h] = dsn[h] * hs(eend, h)[0:1] + _dot(hs(do, h), hs(qt, h), "tn")
            di_ref[rows, :] = wide([_dot(a_sc[h], hs(do, h), "tn") + _dot(hs(kt, h), dsn[h], "nt") for h in heads]).astype(BF16)
            diag = wide([jnp.broadcast_to(jnp.sum(d_sc[h] * eye, axis=-1, keepdims=True), (c, HEAD)) for h in heads])
            dq = dqt * eb
            dk = dkt * ebe
            db = q * dq - k * dk
            dq = dq + diag * k
            dk = dk + diag * q
            for lvl in range(len(_LEVELS)):
                ql = (q * eq[lvl]).astype(BF16)
                kl = (k * ek[lvl]).astype(BF16)
                dm = [(m_ref[lvl] * d_sc[h]).astype(BF16) for h in heads]
                gq = wide([_dot(dm[h], hs(kl, h)) for h in heads])
                gk = wide([_dot(dm[h], hs(ql, h), "tn") for h in heads])
                dq = dq + gq * eq[lvl]
                dk = dk + gk * ek[lvl]
                db = db + ql.astype(F32) * gq - kl.astype(F32) * gk
            state_term = wide([jnp.sum(s_in[h] * dsn[h], axis=0, keepdims=True) for h in heads])
            extra = jnp.sum(dkt * kt, axis=0, keepdims=True) + eend[0:1] * state_term
            db = db + last_row * extra
            dlogf = _dot_exact_l(rev_ref[...], db)
            dfv = jnp.where(f > 1e-30, dlogf / f, 0.0) - dk
            df_ref[rows, :] = (dfv * (1.0 - lbv) * sg * (1.0 - sg)).astype(BF16)
            dlb_ref[...] += jnp.sum(dfv * (1.0 - sg), axis=0, keepdims=True)
            dq_ref[rows, :] = (dq * (sq * (1.0 + qp * (1.0 - sq)))).astype(BF16)
            return carry

        lax.fori_loop(0, n_sub, chunk, 0, unroll=HGRN_UNROLL)
        lay.run(cin, cout, csem, step, grid[0] * grid[1], post=True)

    n_hb = n_heads // hp
    col = lambda base: pl.BlockSpec((rb, wd), lambda h, r: (n_rb - 1 - r, base * n_hb + h))
    blk = pl.BlockSpec((rb, wd), lambda h, r: (n_rb - 1 - r, h))
    vec = pl.BlockSpec((1, wd), lambda h, r: (0, h))
    const = lambda a: pl.BlockSpec(a.shape, lambda h, r: (0,) * a.ndim)
    res = pl.pallas_call(
        body, name="hgrn2_bwd", grid=grid,
        in_specs=[col(0), col(1), col(2), col(3), vec, vec, blk,
                  pl.BlockSpec((hp, n_sub, HEAD, HEAD), lambda h, r: (h, n_rb - 1 - r, 0, 0)),
                  pl.BlockSpec((hp, rb, c), lambda h, r: (h, n_rb - 1 - r, 0)),
                  blk, const(cum), const(rev), const(masks)] + lay.in_specs,
        out_specs=[blk, blk, blk, blk, vec, vec] + lay.out_specs,
        out_shape=[jax.ShapeDtypeStruct((t, aw), BF16)] * 4 + [jax.ShapeDtypeStruct((1, aw), F32)] * 2 + lay.out_shapes,
        scratch_shapes=[pltpu.VMEM((hp, HEAD, HEAD), F32)] + lay.sem_shapes,
        compiler_params=_cparams(2),
    )(proj, proj, proj, proj, lb, nw, o_raw, states, scores, dab, cum, rev, masks, *lay.arrays)
    lay.deliver(res[6:])
    return res[:6]


def _lb_fwd(lb_param):
    def body(p_ref, o_ref):
        p = p_ref[...]
        e = jnp.exp(p - jnp.max(p, axis=0, keepdims=True))
        o_ref[...] = e[0:1] / jnp.sum(e, axis=0, keepdims=True)

    return pl.pallas_call(body, name="lb_fwd", out_shape=jax.ShapeDtypeStruct((1, lb_param.shape[1]), F32))(lb_param)


def _lb_bwd(lb_param, dlb):
    def body(p_ref, d_ref, o_ref):
        p = p_ref[...]
        e = jnp.exp(p - jnp.max(p, axis=0, keepdims=True))
        s = e / jnp.sum(e, axis=0, keepdims=True)
        first = (lax.broadcasted_iota(jnp.int32, p.shape, 0) == 0).astype(F32)
        o_ref[...] = d_ref[...] * s[0:1] * (first - s)

    return pl.pallas_call(body, name="lb_bwd", out_shape=jax.ShapeDtypeStruct(lb_param.shape, F32))(lb_param, dlb)


def _gmlp_norm(v, lnw, lnb):
    vf = _gelu(v)
    mu = jnp.mean(vf, axis=-1, keepdims=True)
    cen = vf - mu
    rstd = lax.rsqrt(jnp.mean(cen * cen, axis=-1, keepdims=True) + EPS)
    xh = cen * rstd
    return xh, rstd, xh * lnw + lnb


def _tril(n):
    return (lax.broadcasted_iota(jnp.int32, (n, n), 1) <= lax.broadcasted_iota(jnp.int32, (n, n), 0)).astype(F32)


def _gmlp_fwd(proj, lnw, lnb, w_sp, bs_t, n_groups, col_base):
    t = proj.shape[0]
    bw = n_groups * HEAD
    c = GMLP_CHUNK

    def body(u_ref, v_ref, lnw_ref, lnb_ref, w_ref, bs_ref, o_ref):
        tri = _tril(c)
        uf = _gelu(u_ref[...])
        _, _, vn = _gmlp_norm(v_ref[...], lnw_ref[...], lnb_ref[...])
        for g in range(n_groups):
            cols = slice(g * HEAD, (g + 1) * HEAD)
            z = _dot(w_ref[g] * tri, vn[:, cols]) + bs_ref[:, g:g + 1]
            o_ref[:, cols] = (uf[:, cols] * z).astype(BF16)

    blk = lambda b: pl.BlockSpec((c, bw), lambda n: (n, b))
    const = lambda a: pl.BlockSpec(a.shape, lambda n: (0,) * a.ndim)
    return pl.pallas_call(
        body, name="gmlp_fwd", grid=(t // c,),
        in_specs=[blk(col_base), blk(col_base + 1), const(lnw), const(lnb), const(w_sp), const(bs_t)],
        out_specs=pl.BlockSpec((c, bw), lambda n: (n, 0)),
        out_shape=jax.ShapeDtypeStruct((t, bw), BF16),
        compiler_params=_cparams(1),
    )(proj, proj, lnw, lnb, w_sp, bs_t)


def _gmlp_bwd(proj, lnw, lnb, w_sp, bs_t, dab, n_groups, col_base):
    t = proj.shape[0]
    bw = n_groups * HEAD
    c = GMLP_CHUNK
    n_steps = t // c
    sel = jnp.asarray((np.arange(bw)[:, None] // HEAD == np.arange(n_groups)[None, :]).astype(np.float32), BF16)

    def body(u_ref, v_ref, lnw_ref, lnb_ref, w_ref, bs_ref, d_ref, sel_ref,
             du_ref, dv_ref, dlnw_ref, dlnb_ref, dw_ref, dbs_ref, dz_acc, dvn_scr):
        step = pl.program_id(0)

        @pl.when(step == 0)
        def _():
            dlnw_ref[...] = jnp.zeros_like(dlnw_ref)
            dlnb_ref[...] = jnp.zeros_like(dlnb_ref)
            dw_ref[...] = jnp.zeros_like(dw_ref)
            dz_acc[...] = jnp.zeros_like(dz_acc)

        tri = _tril(c)
        u = u_ref[...]
        v = v_ref[...]
        uf = _gelu(u)
        lnw_v = lnw_ref[...]
        xh, rstd, vn = _gmlp_norm(v, lnw_v, lnb_ref[...])
        dbo = d_ref[...]
        dz = dbo * uf
        dz_acc[...] += dz
        for g in range(n_groups):
            cols = slice(g * HEAD, (g + 1) * HEAD)
            wg = w_ref[g] * tri
            z = _dot(wg, vn[:, cols]) + bs_ref[:, g:g + 1]
            du_ref[:, cols] = (dbo[:, cols] * z * _gelu_grad(u[:, cols])).astype(BF16)
            dvn_scr[:, cols] = _dot(wg, dz[:, cols], "tn")
            dw_ref[g] += tri * _dot(dz[:, cols], vn[:, cols], "nt")
        dvn = dvn_scr[...]
        dlnw_ref[...] += jnp.sum(dvn * xh, axis=0, keepdims=True)
        dlnb_ref[...] += jnp.sum(dvn, axis=0, keepdims=True)
        dxh = dvn * lnw_v
        dvf = rstd * (dxh - jnp.mean(dxh, axis=-1, keepdims=True) - xh * jnp.mean(dxh * xh, axis=-1, keepdims=True))
        dv_ref[...] = (dvf * _gelu_grad(v)).astype(BF16)

        @pl.when(step == n_steps - 1)
        def _():
            dbs_ref[...] = _dot_exact_r(dz_acc[...], sel_ref[...])

    blk = lambda b: pl.BlockSpec((c, bw), lambda n: (n, b))
    const = lambda a: pl.BlockSpec(a.shape, lambda n: (0,) * a.ndim)
    row = pl.BlockSpec((c, bw), lambda n: (n, 0))
    vec = pl.BlockSpec((1, bw), lambda n: (0, 0))
    return pl.pallas_call(
        body, name="gmlp_bwd", grid=(n_steps,),
        in_specs=[blk(col_base), blk(col_base + 1), const(lnw), const(lnb), const(w_sp), const(bs_t), blk(1), const(sel)],
        out_specs=[row, row, vec, vec, const(w_sp), const(bs_t)],
        out_shape=[jax.ShapeDtypeStruct((t, bw), BF16), jax.ShapeDtypeStruct((t, bw), BF16),
                   jax.ShapeDtypeStruct((1, bw), F32), jax.ShapeDtypeStruct((1, bw), F32),
                   jax.ShapeDtypeStruct(w_sp.shape, F32), jax.ShapeDtypeStruct(bs_t.shape, F32)],
        scratch_shapes=[pltpu.VMEM((c, bw), F32), pltpu.VMEM((c, bw), F32)],
        compiler_params=_cparams(1),
    )(proj, proj, lnw, lnb, w_sp, bs_t, dab, sel)


def _pair_sum(name, grad, other, core):
    _, _, r, c = grad.shape
    tr = _rows(r, 6 * c)

    def body(core_ref, g_ref, o_ref, out_ref):
        out_ref[...] = (g_ref[...].astype(F32) + o_ref[...].astype(F32)).astype(BF16)

    return pl.pallas_call(
        body, name=name,
        grid_spec=pltpu.PrefetchScalarGridSpec(
            num_scalar_prefetch=1, grid=(N_CHIP, r // tr),
            in_specs=[pl.BlockSpec((None, None, tr, c), lambda k, i, core_ref: (k, core_ref[0], i, 0)),
                      pl.BlockSpec((None, tr, c), lambda k, i, core_ref: (k, i, 0))],
            out_specs=pl.BlockSpec((None, tr, c), lambda k, i, core_ref: (k, i, 0))),
        out_shape=jax.ShapeDtypeStruct((N_CHIP, r, c), BF16),
        compiler_params=_cparams(2),
    )(core, grad, other)


def _adamw_math(w, g, m, v):
    m = ADAM_B1 * m + (1.0 - ADAM_B1) * g
    v = ADAM_B2 * v + (1.0 - ADAM_B2) * (g * g)
    m_hat = m / (1.0 - ADAM_B1 ** ADAM_STEP)
    v_hat = v / (1.0 - ADAM_B2 ** ADAM_STEP)
    delta = -ADAM_LR * (m_hat / (jnp.sqrt(v_hat) + ADAM_EPS) + ADAM_WD * w)
    return delta, m, v


def _adamw(name, parts, w, m, v, comms=()):
    n_parts, r, c = parts.shape
    tr = _rows(r, c * (n_parts * parts.dtype.itemsize + 28), mult=8)
    lay = _CommLayout(comms)
    grid = (r // tr,)

    def body(*refs):
        p_ref, w_ref, m_ref, v_ref = refs[:4]
        cin = refs[4:4 + lay.n_in]
        g_ref, d_ref, mo_ref, vo_ref = refs[4 + lay.n_in:8 + lay.n_in]
        cout = refs[8 + lay.n_in:8 + lay.n_in + lay.n_out]
        csem = refs[8 + lay.n_in + lay.n_out:]
        step = pl.program_id(0)
        lay.run(cin, cout, csem, step, grid[0], post=False)
        g = p_ref[0].astype(F32)
        for i in range(1, n_parts):
            g = g + p_ref[i].astype(F32)
        g_ref[...] = g
        d_ref[...], mo_ref[...], vo_ref[...] = _adamw_math(w_ref[...], g, m_ref[...], v_ref[...])
        lay.run(cin, cout, csem, step, grid[0], post=True)

    row = pl.BlockSpec((tr, c), lambda i: (i, 0))
    res = pl.pallas_call(
        body, name=name, grid=grid,
        in_specs=[pl.BlockSpec((n_parts, tr, c), lambda i: (0, i, 0)), row, row, row] + lay.in_specs,
        out_specs=[row] * 4 + lay.out_specs,
        out_shape=[jax.ShapeDtypeStruct((r, c), F32)] * 4 + lay.out_shapes,
        scratch_shapes=lay.sem_shapes,
        compiler_params=_cparams(1),
    )(parts, w, m, v, *lay.arrays)
    lay.deliver(res[4:])
    return res[:4]


def kernel(x, p, pre_mix_w, w_in, lb_param, a_norm_w, gmlp_ln_w, gmlp_ln_b, w_spatial, b_spatial, w_out, post_mix_w, pre_ffn_w, w_gate, w_up, w_down, post_ffn_w, w_ple, w_ple_gate, post_ple_w, loss_target, m_pre_mix_w, m_w_in, m_lb_param, m_a_norm_w, m_gmlp_ln_w, m_gmlp_ln_b, m_w_spatial, m_b_spatial, m_w_out, m_post_mix_w, m_pre_ffn_w, m_w_gate, m_w_up, m_w_down, m_post_ffn_w, m_w_ple, m_w_ple_gate, m_post_ple_w, v_pre_mix_w, v_w_in, v_lb_param, v_a_norm_w, v_gmlp_ln_w, v_gmlp_ln_b, v_w_spatial, v_b_spatial, v_w_out, v_post_mix_w, v_pre_ffn_w, v_w_gate, v_w_up, v_w_down, v_post_ffn_w, v_w_ple, v_w_ple_gate, v_post_ple_w):
    big_names = ["w_in", "w_out", "w_gate", "w_up", "w_down", "w_ple", "w_ple_gate"]
    small_names = ["pre_mix_w", "lb_param", "a_norm_w", "gmlp_ln_w", "gmlp_ln_b", "w_spatial", "b_spatial",
                   "post_mix_w", "pre_ffn_w", "post_ffn_w", "post_ple_w"]
    all_names = ["pre_mix_w", "w_in", "lb_param", "a_norm_w", "gmlp_ln_w", "gmlp_ln_b", "w_spatial", "b_spatial",
                 "w_out", "post_mix_w", "pre_ffn_w", "w_gate", "w_up", "w_down", "post_ffn_w", "w_ple", "w_ple_gate",
                 "post_ple_w"]
    env = dict(locals())
    W = {n: env[n] for n in all_names}
    M = {n: env["m_" + n] for n in all_names}
    V = {n: env["v_" + n] for n in all_names}

    xs = x[0]
    ps = p[0, 0]
    tgt = loss_target[0]
    t, d = xs.shape
    aw = a_norm_w.shape[1]
    bw = gmlp_ln_w.shape[1]
    n_heads, n_groups = aw // HEAD, bw // HEAD
    core = lax.axis_index("c").astype(jnp.int32).reshape(1)

    transposed = ("w_gate", "w_up")
    local = lambda a, n: jnp.swapaxes(a, 1, 2)[0] if n in transposed else a[0]
    unlocal = lambda a, n: jnp.swapaxes(a[None], 1, 2) if n in transposed else a[None]
    shard = {n: local(W[n], n) for n in big_names}
    bf = {n: _cast_bf16("cast_" + n, shard[n]) for n in big_names}
    n_in = bf["w_in"].shape[1]
    ffl = bf["w_gate"].shape[0]
    n_ple = bf["w_ple"].shape[1]
    ple = ps.shape[1]

    TM, TK = 1024, 1024
    tm = _tile(t, TM)
    tkd = _tile(d, TK)
    tn1 = _tile(d, 1024)

    h1 = _rms_fwd("rms_pre_mix", xs, pre_mix_w)
    once = pl.Buffered(1)
    ag_a = _ag_comm([bf["w_out"]], start_frac=0.75, mid_frac=0.95)
    proj, win_g = _ag_matmul("mm_proj", h1, bf["w_in"], "nn", F32, False, comms=[ag_a])
    wout_f = ag_a.results[0].reshape(d, d)
    lb = _lb_fwd(lb_param)
    ag_b = _ag_comm([bf["w_gate"]], mid_frac=0.85)
    a_out, o_raw, states, scores = _hgrn_fwd(proj, lb, a_norm_w, n_heads, comms=[ag_b])
    wgate_g = ag_b.results[0]
    bs_t = b_spatial[0].T
    w_sp = w_spatial[0]
    col_u = (4 * aw) // bw
    b_out = _gmlp_fwd(proj, gmlp_ln_w, gmlp_ln_b, w_sp, bs_t, n_groups, col_u)
    ab = jnp.concatenate([a_out, b_out], axis=1)
    mix = _mm_plain("mm_mix", ab, wout_f, "nn", F32, TM, 1024, d)
    x1, h2 = _resid_rms("resid_mix", xs, mix, post_mix_w, pre_ffn_w)

    def swiglu(accs, gate_v):
        gf = gate_v.astype(F32)
        return accs[0], gf * _sigmoid(gf) * accs[0]

    tmf = _tile(t, 512)
    blk3 = lambda: pl.BlockSpec((None, tmf, ffl), lambda j, m, k: (j, m, 0))
    ag_c = _ag_comm([bf["w_up"]], mid_frac=0.85)
    gate = _matmul(
        "mm_ffn_gate",
        [(h2, pl.BlockSpec((tm, d), lambda j, m, k: (m, 0))),
         (wgate_g, pl.BlockSpec((None, ffl, d), lambda j, m, k: (j, 0, 0), pipeline_mode=once))],
        [(0, 1, "nt", 0)],
        [(jax.ShapeDtypeStruct((N_DEV, t, ffl), BF16), pl.BlockSpec((None, tm, ffl), lambda j, m, k: (j, m, 0)))],
        (N_DEV, t // tm, 1), (tm, ffl), comms=[ag_c])[0]
    wup_g = ag_c.results[0]
    ag_d = _ag_comm([bf["w_down"]], mid_frac=0.85)
    up, act = _matmul(
        "mm_ffn_up",
        [(h2, pl.BlockSpec((tmf, d), lambda j, m, k: (m, 0))),
         (wup_g, pl.BlockSpec((None, ffl, d), lambda j, m, k: (j, 0, 0), pipeline_mode=once)),
         (gate, blk3())],
        [(0, 1, "nt", 0)],
        [(jax.ShapeDtypeStruct((N_DEV, t, ffl), BF16), blk3()) for _ in range(2)],
        (N_DEV, t // tmf, 1), (tmf, ffl), epilogue=swiglu, comms=[ag_d])
    wdown_g = ag_d.results[0]
    ag_e = _ag_comm([bf["w_ple_gate"], bf["w_ple"]], mid_frac=0.6)
    tn_d = _tile(d, 2048)
    ff = _matmul(
        "mm_ffn_down",
        [(act, pl.BlockSpec((None, tm, ffl), lambda m, n, k: (k, m, 0))),
         (wdown_g, pl.BlockSpec((None, ffl, tn_d), lambda m, n, k: (k, 0, n)))],
        [(0, 1, "nn", 0)],
        [(jax.ShapeDtypeStruct((t, d), F32), pl.BlockSpec((tm, tn_d), lambda m, n, k: (m, n)))],
        (t // tm, d // tn_d, N_DEV), (tm, tn_d), comms=[ag_e])[0]
    wpg_f = ag_e.results[0].reshape(d, d)
    wple_g = ag_e.results[1]
    x2, x2b = _resid_rms("resid_ffn", x1, ff, post_ffn_w, None)

    pgl = _mm_plain("mm_ple_gate", x2b, wpg_f, "nn", F32, TM, 1024, d)
    pe = _matmul(
        "mm_ple",
        [(ps, pl.BlockSpec((tm, ple), lambda m, n, k: (m, 0))), (wple_g, pl.BlockSpec((None, ple, n_ple), lambda m, n, k: (n, 0, 0)))],
        [(0, 1, "nn", 0)],
        [(jax.ShapeDtypeStruct((t, N_DEV * n_ple), F32), pl.BlockSpec((tm, n_ple), lambda m, n, k: (m, n)))],
        (t // tm, N_DEV, 1), (tm, n_ple))[0]
    loss_part, d3, dpe, dpgl, g_post_ple = _ple_loss("ple_loss", x2, pe, pgl, post_ple_w, tgt)

    tkt = _tile(t, TK)
    g_wple = _matmul(
        "mm_dw_ple",
        [(ps, pl.BlockSpec((tkt, ple), lambda n, k: (k, 0))), (dpe, pl.BlockSpec((tkt, n_ple), lambda n, k: (k, n)))],
        [(0, 1, "tn", 0)],
        [(jax.ShapeDtypeStruct((N_DEV, ple, n_ple), BF16), pl.BlockSpec((None, ple, n_ple), lambda n, k: (n, 0, 0)))],
        (N_DEV, t // tkt), (ple, n_ple))[0]
    g_wpg = _mm_plain("mm_dw_ple_gate", x2b, dpgl, "tn", BF16, TM, 1024, t)

    def by_chip(g):
        return g.reshape((N_CHIP, 2) + g.shape[-2:])

    def pair_sums(names, comm):
        return [_pair_sum("pair_sum_" + n, g, o, core) for n, g, o in zip(names, comm.arrays, comm.results)]

    r1_p = _pair_comm([by_chip(g_wpg.reshape(N_DEV, d // N_DEV, d)), by_chip(g_wple)])
    d2 = _mm_plain("mm_d_x2", dpgl, wpg_f, "nt", F32, TM, 512, d, extra=d3, epilogue=lambda accs, e: [accs[0] + e],
                   comms=[r1_p])
    r2_p = _chip_comm(pair_sums(["w_ple_gate", "w_ple"], r1_p))

    dff, g_post_ffn = _norm_bwd("norm_bwd_ffn", d2, ff, post_ffn_w)
    g_wdown = _matmul(
        "mm_dw_down",
        [(act, pl.BlockSpec((None, t, ffl), lambda j, n, k: (j, 0, 0), pipeline_mode=once)),
         (dff, pl.BlockSpec((t, tn1), lambda j, n, k: (0, n)))],
        [(0, 1, "tn", 0)],
        [(jax.ShapeDtypeStruct((N_DEV, ffl, d), BF16), pl.BlockSpec((None, ffl, tn1), lambda j, n, k: (j, 0, n)))],
        (N_DEV, d // tn1, 1), (ffl, tn1), comms=[r2_p])[0]
    r1_d = _pair_comm([by_chip(g_wdown)])

    def swiglu_bwd(accs, gate_v, up_v):
        dact = accs[0]
        gf = gate_v.astype(F32)
        sg = _sigmoid(gf)
        return dact * up_v.astype(F32) * (sg * (1.0 + gf * (1.0 - sg))), dact * (gf * sg)

    dgate, dup = _matmul(
        "mm_d_act",
        [(dff, pl.BlockSpec((tmf, d), lambda j, m, k: (m, 0))),
         (wdown_g, pl.BlockSpec((None, ffl, d), lambda j, m, k: (j, 0, 0), pipeline_mode=once)),
         (gate, blk3()), (up, blk3())],
        [(0, 1, "nt", 0)],
        [(jax.ShapeDtypeStruct((N_DEV, t, ffl), BF16), blk3()) for _ in range(2)],
        (N_DEV, t // tmf, 1), (tmf, ffl), epilogue=swiglu_bwd, comms=[r1_d])
    r2_d = _chip_comm(pair_sums(["w_down"], r1_d))
    tmd = _tile(d, TM)
    def dw_ffn(name, dy, comms):
        return _matmul(
            name,
            [(dy, pl.BlockSpec((None, t, ffl), lambda j, n, k: (j, 0, 0), pipeline_mode=once)),
             (h2, pl.BlockSpec((t, tn1), lambda j, n, k: (0, n)))],
            [(0, 1, "tn", 0)],
            [(jax.ShapeDtypeStruct((N_DEV, ffl, d), BF16), pl.BlockSpec((None, ffl, tn1), lambda j, n, k: (j, 0, n)))],
            (N_DEV, d // tn1, 1), (ffl, tn1), comms=comms)[0]

    g_wgate = dw_ffn("mm_dw_gate", dgate, [r2_d])
    r1_g = _pair_comm([by_chip(g_wgate)])
    g_wup = dw_ffn("mm_dw_up", dup, [r1_g])
    r2_g = _chip_comm(pair_sums(["w_gate"], r1_g))
    r1_u = _pair_comm([by_chip(g_wup)])
    tn1 = _tile(d, 1024)
    dh2 = _matmul(
        "mm_d_h2",
        [(dgate, pl.BlockSpec((None, tm, ffl), lambda m, n, k: (k, m, 0))),
         (wgate_g, pl.BlockSpec((None, ffl, tn1), lambda m, n, k: (k, 0, n))),
         (dup, pl.BlockSpec((None, tm, ffl), lambda m, n, k: (k, m, 0))),
         (wup_g, pl.BlockSpec((None, ffl, tn1), lambda m, n, k: (k, 0, n)))],
        [(0, 1, "nn", 0), (2, 3, "nn", 0)],
        [(jax.ShapeDtypeStruct((t, d), F32), pl.BlockSpec((tm, tn1), lambda m, n, k: (m, n)))],
        (t // tm, d // tn1, N_DEV), (tm, tn1), comms=[r2_g, r1_u])[0]
    r2_u = _chip_comm(pair_sums(["w_up"], r1_u))
    d1, g_pre_ffn, dmix, g_post_mix = _prenorm_bwd("prenorm_bwd_ffn", d2, dh2, x1, pre_ffn_w, mix, post_mix_w)

    g_wout = _mm_plain("mm_dw_out", ab, dmix, "tn", BF16, TM, 1024, t)
    r1_o = _pair_comm([by_chip(g_wout.reshape(N_DEV, d // N_DEV, d))])
    dab = _mm_plain("mm_d_ab", dmix, wout_f, "nt", F32, TM, 1024, d, comms=[r1_o])
    r2_o = _chip_comm(pair_sums(["w_out"], r1_o))
    dq, df, di, dg, dlb, g_a_norm = _hgrn_bwd(proj, lb, a_norm_w, o_raw, states, scores, dab, n_heads, comms=[r2_u])
    du, dv, g_ln_w, g_ln_b, g_wsp, g_bs_t = _gmlp_bwd(proj, gmlp_ln_w, gmlp_ln_b, w_sp, bs_t, dab, n_groups, col_u)
    dproj = jnp.concatenate([dq, df, di, dg, du, dv], axis=1)
    g_win = _matmul(
        "mm_dw_in",
        [(h1, pl.BlockSpec((t, tmd), lambda j, m, k: (0, m))),
         (dproj, pl.BlockSpec((t, n_in), lambda j, m, k: (0, j), pipeline_mode=once))],
        [(0, 1, "tn", 0)],
        [(jax.ShapeDtypeStruct((N_DEV, d, n_in), BF16), pl.BlockSpec((None, tmd, n_in), lambda j, m, k: (j, m, 0)))],
        (N_DEV, d // tmd, 1), (tmd, n_in), comms=[r2_o])[0]
    grads, deltas, new_m, new_v = {}, {}, {}, {}

    def adam(n, parts, comms=()):
        res = _adamw("adamw_" + n, parts, shard[n], local(M[n], n), local(V[n], n), comms=comms)
        grads[n], deltas[n], new_m[n], new_v[n] = (unlocal(a, n) for a in res)

    r1_in = _pair_comm([by_chip(g_win)])
    adam("w_down", r2_d.results[0], [r1_in])
    r2_in = _chip_comm(pair_sums(["w_in"], r1_in))
    dh1 = _matmul(
        "mm_d_h1",
        [(dproj, pl.BlockSpec((tm, n_in), lambda m, n, k: (m, k))), (win_g, pl.BlockSpec((None, tn1, n_in), lambda m, n, k: (k, n, 0)))],
        [(0, 1, "nt", 0)],
        [(jax.ShapeDtypeStruct((t, d), F32), pl.BlockSpec((tm, tn1), lambda m, n, k: (m, n)))],
        (t // tm, d // tn1, N_DEV), (tm, tn1), comms=[r2_in])[0]
    grad_x, g_pre_mix = _prenorm_bwd("prenorm_bwd_mix", d1, dh1, xs, pre_mix_w)

    reduced = {
        "w_in": r2_in.results[0], "w_out": r2_o.results[0], "w_gate": r2_g.results[0], "w_up": r2_u.results[0],
        "w_ple": r2_p.results[1], "w_ple_gate": r2_p.results[0],
    }
    small_grad = {
        "pre_mix_w": g_pre_mix, "lb_param": _lb_bwd(lb_param, dlb), "a_norm_w": g_a_norm, "gmlp_ln_w": g_ln_w,
        "gmlp_ln_b": g_ln_b, "w_spatial": g_wsp, "b_spatial": g_bs_t.T, "post_mix_w": g_post_mix,
        "pre_ffn_w": g_pre_ffn, "post_ffn_w": g_post_ffn, "post_ple_w": g_post_ple,
    }
    pack = lambda get: jnp.concatenate([get(n).reshape(-1, LANE) for n in small_names], axis=0)
    ag_small = _ag_comm([pack(lambda n: small_grad[n])], mid_frac=0.5)
    for n in big_names:
        if n != "w_down":
            adam(n, reduced[n], [ag_small] if n == "w_in" else [])
    g_all = ag_small.results[0]
    sg, sd, sm, sv = _adamw("adamw_small", g_all, pack(lambda n: W[n]), pack(lambda n: M[n]), pack(lambda n: V[n]))
    off = 0
    for n in small_names:
        rows = W[n].size // LANE
        for src, dst in ((sg, grads), (sd, deltas), (sm, new_m), (sv, new_v)):
            dst[n] = src[off:off + rows].reshape(W[n].shape)
        off += rows

    loss = lax.psum(loss_part[0, 0], ("x", "y", "c"))
    return (loss, grad_x[None], *[grads[n] for n in all_names], *[deltas[n] for n in all_names],
            *[new_m[n] for n in all_names], *[new_v[n] for n in all_names])
```

```python
import functools

import numpy as np
import jax
import jax.numpy as jnp
from jax import lax
from jax.experimental import pallas as pl
from jax.experimental.pallas import tpu as pltpu

F32 = jnp.float32
BF16 = jnp.bfloat16

EPS = 1e-6
HEAD = 128
GLA_CHUNK = 64
GMLP_CHUNK = 128
N_DEV = 8
N_CHIP = 4
LANE = 128
VMEM_LIMIT = 56 * 1024 * 1024
HGRN_ROWS = 512
ROW_TILE = 128
EPILOGUE_ROWS = 256
HGRN_UNROLL = 1
HGRN_HEADS = 8

ADAM_LR = 0.001
ADAM_B1 = 0.9
ADAM_B2 = 0.999
ADAM_EPS = 1e-08
ADAM_WD = 0.01
ADAM_STEP = 10

MESH = pl.DeviceIdType.MESH

_DIMS = {
    "nn": (((1,), (0,)), ((), ())),
    "nt": (((1,), (1,)), ((), ())),
    "tn": (((0,), (0,)), ((), ())),
}


def _tile(dim, pref):
    return pref if dim % pref == 0 else dim


def _rows(r, bytes_per_row, budget=18 * 1024 * 1024, mult=16):
    best = None
    for cand in range(mult, r + 1, mult):
        if r % cand == 0 and cand * bytes_per_row <= budget:
            best = cand
    return best if best is not None else r


def _cparams(n_axes):
    return pltpu.CompilerParams(dimension_semantics=("arbitrary",) * n_axes, vmem_limit_bytes=VMEM_LIMIT)


def _dot(a, b, form="nn"):
    return lax.dot_general(a.astype(BF16), b.astype(BF16), _DIMS[form], preferred_element_type=F32)


def _split3(x):
    hi = x.astype(BF16)
    r = x - hi.astype(F32)
    mid = r.astype(BF16)
    lo = (r - mid.astype(F32)).astype(BF16)
    return hi, mid, lo


def _dot_exact_l(c, x):
    hi, mid, lo = _split3(x)
    d = lambda y: lax.dot_general(c, y, _DIMS["nn"], preferred_element_type=F32)
    return d(hi) + d(mid) + d(lo)


def _dot_exact_r(x, c):
    hi, mid, lo = _split3(x)
    d = lambda y: lax.dot_general(y, c, _DIMS["nn"], preferred_element_type=F32)
    return d(hi) + d(mid) + d(lo)


def _sigmoid(x):
    return 1.0 / (1.0 + jnp.exp(-x))


def _gelu(x):
    return 0.5 * x * (1.0 + lax.erf(x * 0.7071067811865476))


def _gelu_grad(x):
    cdf = 0.5 * (1.0 + lax.erf(x * 0.7071067811865476))
    pdf = jnp.exp(-0.5 * x * x) * 0.3989422804014327
    return cdf + x * pdf


def _position():
    return lax.axis_index("x"), lax.axis_index("y"), lax.axis_index("c")


def _linear_step(grid):
    step = 0
    for ax, n in enumerate(grid):
        step = step * n + pl.program_id(ax)
    return step


class _Comm:
    def __init__(self, arrays, out_shapes, sem_shapes, phases):
        self.arrays, self.out_shapes, self.sem_shapes, self.phases = list(arrays), list(out_shapes), list(sem_shapes), phases
        self.results = None


class _CommLayout:
    def __init__(self, comms, space=pl.ANY):
        self.comms = list(comms)
        self.arrays = [a for c in self.comms for a in c.arrays]
        self.out_shapes = [s for c in self.comms for s in c.out_shapes]
        self.sem_shapes = [s for c in self.comms for s in c.sem_shapes]
        self.n_in, self.n_out = len(self.arrays), len(self.out_shapes)
        self.in_specs = [pl.BlockSpec(memory_space=space)] * self.n_in
        self.out_specs = [pl.BlockSpec(memory_space=space)] * self.n_out

    def run(self, cin, cout, csem, step, n_steps, post):
        i = o = s = 0
        for c in self.comms:
            ins, outs, sems = cin[i:i + len(c.arrays)], cout[o:o + len(c.out_shapes)], csem[s:s + len(c.sem_shapes)]
            i, o, s = i + len(c.arrays), o + len(c.out_shapes), s + len(c.sem_shapes)
            for frac, fn in c.phases:
                if (frac is None) != post:
                    continue
                due = n_steps - 1 if frac is None else max(0, min(int(frac * n_steps), n_steps - 2))
                if n_steps == 1:
                    fn(ins, outs, sems)
                else:
                    pl.when(step == due)(functools.partial(fn, ins, outs, sems))

    def deliver(self, results):
        o = 0
        for c in self.comms:
            c.results = list(results[o:o + len(c.out_shapes)])
            o += len(c.out_shapes)


def _comm_only(name, comms, in_vmem=False):
    lay = _CommLayout(comms, pltpu.VMEM if in_vmem else pl.ANY)

    def body(*refs):
        cin, cout, csem = refs[:lay.n_in], refs[lay.n_in:lay.n_in + lay.n_out], refs[lay.n_in + lay.n_out:]
        lay.run(cin, cout, csem, 0, 1, post=False)
        lay.run(cin, cout, csem, 0, 1, post=True)

    res = pl.pallas_call(
        body, name=name, in_specs=lay.in_specs, out_specs=lay.out_specs, out_shape=lay.out_shapes,
        scratch_shapes=lay.sem_shapes,
    )(*lay.arrays)
    lay.deliver(res)


def _ag_comm(shards, mid_frac=0.0):
    n = len(shards)
    per = N_DEV - 1

    def tools(ins, outs, sems):
        send_sems, recv_sems, local_sems = sems
        x, y, c = _position()
        me, sibling = (x, y, c), (x, y, 1 - c)
        chips = [(1 - x, y), (x, 1 - y), (1 - x, 1 - y)]

        def copy(a, k, block, to, from_shard=False):
            dst = outs[a].at[4 * block[0] + 2 * block[1] + block[2]]
            return pltpu.make_async_remote_copy(
                src_ref=ins[a] if from_shard else dst, dst_ref=dst,
                send_sem=send_sems.at[a * per + k], recv_sem=recv_sems.at[a * per + k],
                device_id=to, device_id_type=MESH)

        def local(a):
            return pltpu.make_async_copy(ins[a], outs[a].at[4 * x + 2 * y + c], local_sems.at[a])

        return me, sibling, chips, c, copy, local

    def first(ins, outs, sems):
        me, sibling, chips, c, copy, local = tools(ins, outs, sems)
        for a in range(n):
            local(a).start()
            copy(a, 0, me, sibling, True).start()
            for j, chip in enumerate(chips):
                copy(a, 1 + j, me, (*chip, c), True).start()

    def middle(ins, outs, sems):
        me, sibling, chips, c, copy, local = tools(ins, outs, sems)
        for a in range(n):
            for j, chip in enumerate(chips):
                copy(a, 1 + j, (*chip, c), me).wait_recv()
                copy(a, 4 + j, (*chip, c), sibling).start()

    def last(ins, outs, sems):
        me, sibling, chips, c, copy, local = tools(ins, outs, sems)
        for a in range(n):
            copy(a, 0, sibling, me).wait_recv()
            copy(a, 0, me, sibling, True).wait_send()
            for j, chip in enumerate(chips):
                copy(a, 4 + j, (*chip, 1 - c), me).wait_recv()
                copy(a, 1 + j, me, (*chip, c), True).wait_send()
                copy(a, 4 + j, (*chip, c), sibling).wait_send()
            local(a).wait()

    return _Comm(
        shards, [jax.ShapeDtypeStruct((N_DEV,) + s.shape, s.dtype) for s in shards],
        [pltpu.SemaphoreType.DMA((n * per,)), pltpu.SemaphoreType.DMA((n * per,)), pltpu.SemaphoreType.DMA((n,))],
        [(0.0, first), (mid_frac, middle), (None, last)])


def _pair_comm(grads):
    n = len(grads)

    def copies(ins, outs, sems):
        send_sems, recv_sems = sems
        x, y, c = _position()
        return [pltpu.make_async_remote_copy(
            src_ref=ins[a].at[k, 1 - c], dst_ref=outs[a].at[k],
            send_sem=send_sems.at[a * N_CHIP + k], recv_sem=recv_sems.at[a * N_CHIP + k],
            device_id=(x, y, 1 - c), device_id_type=MESH) for a in range(n) for k in range(N_CHIP)]

    def first(ins, outs, sems):
        for cp in copies(ins, outs, sems):
            cp.start()

    def last(ins, outs, sems):
        for cp in copies(ins, outs, sems):
            cp.wait()

    return _Comm(
        grads, [jax.ShapeDtypeStruct((N_CHIP,) + g.shape[2:], g.dtype) for g in grads],
        [pltpu.SemaphoreType.DMA((n * N_CHIP,)), pltpu.SemaphoreType.DMA((n * N_CHIP,))],
        [(0.0, first), (None, last)])


def _chip_comm(sums):
    n = len(sums)
    per = N_CHIP - 1

    def copies(ins, outs, sems):
        send_sems, recv_sems, local_sems = sems
        x, y, c = _position()
        my_chip = 2 * x + y
        cps = []
        for a in range(n):
            cps.append(pltpu.make_async_copy(ins[a].at[my_chip], outs[a].at[my_chip], local_sems.at[a]))
            for j, (px, py) in enumerate([(1 - x, y), (x, 1 - y), (1 - x, 1 - y)]):
                cps.append(pltpu.make_async_remote_copy(
                    src_ref=ins[a].at[2 * px + py], dst_ref=outs[a].at[my_chip],
                    send_sem=send_sems.at[a * per + j], recv_sem=recv_sems.at[a * per + j],
                    device_id=(px, py, c), device_id_type=MESH))
        return cps

    def first(ins, outs, sems):
        for cp in copies(ins, outs, sems):
            cp.start()

    def last(ins, outs, sems):
        for cp in copies(ins, outs, sems):
            cp.wait()

    return _Comm(
        sums, [jax.ShapeDtypeStruct(s.shape, s.dtype) for s in sums],
        [pltpu.SemaphoreType.DMA((n * per,)), pltpu.SemaphoreType.DMA((n * per,)), pltpu.SemaphoreType.DMA((n,))],
        [(0.0, first), (None, last)])


def _matmul(name, operands, pairs, outs, grid, acc_shape, n_slots=1, epilogue=None, comms=()):
    used = sorted({i for p in pairs for i in p[:2]})
    n_op = len(operands)
    n_out = len(outs)
    k_axis = len(grid) - 1
    n_k = grid[-1]
    lay = _CommLayout(comms)

    direct = n_k == 1 and epilogue is None
    n_acc = 0 if direct else 1

    def body(*refs):
        ops = refs[:n_op]
        out_refs = refs[n_op + lay.n_in:n_op + lay.n_in + n_out]
        acc = None if direct else refs[n_op + lay.n_in + n_out + lay.n_out]
        k = pl.program_id(k_axis)
        step = _linear_step(grid)
        cin = refs[n_op:n_op + lay.n_in]
        cout = refs[n_op + lay.n_in + n_out:n_op + lay.n_in + n_out + lay.n_out]
        csem = refs[n_op + lay.n_in + n_out + lay.n_out + n_acc:]
        lay.run(cin, cout, csem, step, int(np.prod(grid)), post=False)

        if n_k > 1:
            @pl.when(k == 0)
            def _():
                acc[...] = jnp.zeros_like(acc)

        vals = {i: ops[i][...] for i in used}
        vals = {i: (v if v.dtype == BF16 else v.astype(BF16)) for i, v in vals.items()}
        for s in range(n_slots):
            tot = None
            for ia, ib, form, slot in pairs:
                if slot != s:
                    continue
                d = lax.dot_general(vals[ia], vals[ib], _DIMS[form], preferred_element_type=F32)
                tot = d if tot is None else tot + d
            if direct:
                out_refs[s][...] = tot.astype(out_refs[s].dtype)
            elif n_k == 1:
                acc[s] = tot
            else:
                acc[s] += tot

        def finish():
            rows = acc_shape[0]
            chunk = EPILOGUE_ROWS if (epilogue is not None and rows % EPILOGUE_ROWS == 0) else rows
            for r0 in range(0, rows, chunk):
                sl = slice(r0, r0 + chunk)
                accs = [acc[s, sl, :] for s in range(n_slots)]
                extra = [ops[i][sl, :] for i in range(n_op) if i not in used]
                res = epilogue(accs, *extra) if epilogue is not None else accs
                for o, v in zip(out_refs, res):
                    o[sl, :] = v.astype(o.dtype)

        if n_k > 1:
            pl.when(k == n_k - 1)(finish)
        elif not direct:
            finish()

        lay.run(cin, cout, csem, step, int(np.prod(grid)), post=True)

    res = pl.pallas_call(
        body,
        name=name,
        grid=grid,
        in_specs=[s for _, s in operands] + lay.in_specs,
        out_specs=[s for _, s in outs] + lay.out_specs,
        out_shape=[s for s, _ in outs] + lay.out_shapes,
        scratch_shapes=([] if direct else [pltpu.VMEM((n_slots,) + tuple(acc_shape), F32)]) + lay.sem_shapes,
        compiler_params=_cparams(len(grid)),
    )(*[a for a, _ in operands], *lay.arrays)
    lay.deliver(res[n_out:])
    return res[:n_out]


def _mm_plain(name, a, b, form, out_dtype, tm, tn, tk, extra=None, epilogue=None, comms=()):
    if form == "nn":
        (M, K), N = a.shape, b.shape[1]
    elif form == "nt":
        (M, K), N = a.shape, b.shape[0]
    else:
        (K, M), N = a.shape, b.shape[1]
    tm, tn, tk = _tile(M, tm), _tile(N, tn), _tile(K, tk)
    a_spec = pl.BlockSpec((tk, tm), lambda m, n, k: (k, m)) if form == "tn" else pl.BlockSpec((tm, tk), lambda m, n, k: (m, k))
    b_spec = pl.BlockSpec((tn, tk), lambda m, n, k: (n, k)) if form == "nt" else pl.BlockSpec((tk, tn), lambda m, n, k: (k, n))
    operands = [(a, a_spec), (b, b_spec)]
    if extra is not None:
        operands.append((extra, pl.BlockSpec((tm, tn), lambda m, n, k: (m, n))))
    out = (jax.ShapeDtypeStruct((M, N), out_dtype), pl.BlockSpec((tm, tn), lambda m, n, k: (m, n)))
    return _matmul(name, operands, [(0, 1, form, 0)], [out], (M // tm, N // tn, K // tk), (tm, tn), epilogue=epilogue,
                   comms=comms)[0]


def _cast_bf16(name, w):
    r, c = w.shape
    tr = _rows(r, 6 * c)

    def body(w_ref, o_ref):
        o_ref[...] = w_ref[...].astype(BF16)

    return pl.pallas_call(
        body, name=name, grid=(r // tr,),
        in_specs=[pl.BlockSpec((tr, c), lambda i: (i, 0))],
        out_specs=pl.BlockSpec((tr, c), lambda i: (i, 0)),
        out_shape=jax.ShapeDtypeStruct((r, c), BF16),
        compiler_params=_cparams(1),
    )(w)


def _rms_stats(x):
    r = lax.rsqrt(jnp.mean(x * x, axis=-1, keepdims=True) + EPS)
    return x * r, r


def _rms_bwd(xhat, r, w, dy):
    dxh = dy * w
    return r * (dxh - xhat * jnp.mean(dxh * xhat, axis=-1, keepdims=True))


def _row_spec(tr, d):
    return pl.BlockSpec((tr, d), lambda i: (i, 0))


def _vec_spec(d):
    return pl.BlockSpec((1, d), lambda i: (0, 0))


def _rms_fwd(name, x, w):
    t, d = x.shape
    tr = _tile(t, ROW_TILE)

    def body(x_ref, w_ref, h_ref):
        xh, _ = _rms_stats(x_ref[...])
        h_ref[...] = (xh * w_ref[...]).astype(BF16)

    return pl.pallas_call(
        body, name=name, grid=(t // tr,),
        in_specs=[_row_spec(tr, d), _vec_spec(d)],
        out_specs=_row_spec(tr, d),
        out_shape=jax.ShapeDtypeStruct((t, d), BF16),
        compiler_params=_cparams(1),
    )(x, w)


def _resid_rms(name, xres, y, w_post, w_next):
    t, d = xres.shape
    tr = _tile(t, ROW_TILE)
    has_next = w_next is not None

    def body(*refs):
        if has_next:
            x_ref, y_ref, wp_ref, wn_ref, xo_ref, h_ref = refs
        else:
            x_ref, y_ref, wp_ref, xo_ref, h_ref = refs
        yh, _ = _rms_stats(y_ref[...])
        xn = x_ref[...] + yh * wp_ref[...]
        xo_ref[...] = xn
        if has_next:
            xh, _ = _rms_stats(xn)
            h_ref[...] = (xh * wn_ref[...]).astype(BF16)
        else:
            h_ref[...] = xn.astype(BF16)

    ins = [xres, y, w_post] + ([w_next] if has_next else [])
    in_specs = [_row_spec(tr, d), _row_spec(tr, d), _vec_spec(d)] + ([_vec_spec(d)] if has_next else [])
    return pl.pallas_call(
        body, name=name, grid=(t // tr,),
        in_specs=in_specs,
        out_specs=[_row_spec(tr, d), _row_spec(tr, d)],
        out_shape=[jax.ShapeDtypeStruct((t, d), F32), jax.ShapeDtypeStruct((t, d), BF16)],
        compiler_params=_cparams(1),
    )(*ins)


def _ple_loss(name, x2, pe, pgl, w_pp, tgt):
    t, d = x2.shape
    tr = _tile(t, ROW_TILE)

    def body(x2_ref, pe_ref, pgl_ref, w_ref, tgt_ref, loss_ref, d3_ref, dpe_ref, dpgl_ref, dw_ref):
        @pl.when(pl.program_id(0) == 0)
        def _():
            loss_ref[...] = jnp.zeros_like(loss_ref)
            dw_ref[...] = jnp.zeros_like(dw_ref)

        pe_v = pe_ref[...]
        s = _sigmoid(pgl_ref[...])
        y = pe_v * s
        yh, r = _rms_stats(y)
        w = w_ref[...]
        err = x2_ref[...] + yh * w - tgt_ref[...]
        loss_ref[...] += 0.5 * jnp.sum(jnp.mean(err * err, axis=-1, keepdims=True), axis=0, keepdims=True)
        d3 = err * (1.0 / d)
        d3_ref[...] = d3
        dw_ref[...] += jnp.sum(d3 * yh, axis=0, keepdims=True)
        dy = _rms_bwd(yh, r, w, d3)
        dpe_ref[...] = (dy * s).astype(BF16)
        dpgl_ref[...] = (dy * pe_v * s * (1.0 - s)).astype(BF16)

    return pl.pallas_call(
        body, name=name, grid=(t // tr,),
        in_specs=[_row_spec(tr, d), _row_spec(tr, d), _row_spec(tr, d), _vec_spec(d), _row_spec(tr, d)],
        out_specs=[pl.BlockSpec((1, 1), lambda i: (0, 0)), _row_spec(tr, d), _row_spec(tr, d), _row_spec(tr, d), _vec_spec(d)],
        out_shape=[jax.ShapeDtypeStruct((1, 1), F32), jax.ShapeDtypeStruct((t, d), F32),
                   jax.ShapeDtypeStruct((t, d), BF16), jax.ShapeDtypeStruct((t, d), BF16),
                   jax.ShapeDtypeStruct((1, d), F32)],
        compiler_params=_cparams(1),
    )(x2, pe, pgl, w_pp, tgt)


def _norm_bwd(name, dres, y, w_post):
    t, d = dres.shape
    tr = _tile(t, ROW_TILE)

    def body(d_ref, y_ref, w_ref, dy_ref, dw_ref):
        @pl.when(pl.program_id(0) == 0)
        def _():
            dw_ref[...] = jnp.zeros_like(dw_ref)

        dv = d_ref[...]
        yh, r = _rms_stats(y_ref[...])
        dw_ref[...] += jnp.sum(dv * yh, axis=0, keepdims=True)
        dy_ref[...] = _rms_bwd(yh, r, w_ref[...], dv).astype(BF16)

    return pl.pallas_call(
        body, name=name, grid=(t // tr,),
        in_specs=[_row_spec(tr, d), _row_spec(tr, d), _vec_spec(d)],
        out_specs=[_row_spec(tr, d), _vec_spec(d)],
        out_shape=[jax.ShapeDtypeStruct((t, d), BF16), jax.ShapeDtypeStruct((1, d), F32)],
        compiler_params=_cparams(1),
    )(dres, y, w_post)


def _prenorm_bwd(name, dres, dh, xin, w_pre, y=None, w_post=None):
    t, d = dres.shape
    tr = _tile(t, ROW_TILE)
    two = y is not None

    def body(*refs):
        if two:
            d_ref, dh_ref, x_ref, wpre_ref, y_ref, wpost_ref, do_ref, dwpre_ref, dy_ref, dwpost_ref = refs
        else:
            d_ref, dh_ref, x_ref, wpre_ref, do_ref, dwpre_ref = refs

        @pl.when(pl.program_id(0) == 0)
        def _():
            dwpre_ref[...] = jnp.zeros_like(dwpre_ref)
            if two:
                dwpost_ref[...] = jnp.zeros_like(dwpost_ref)

        dhv = dh_ref[...]
        xh, r = _rms_stats(x_ref[...])
        dwpre_ref[...] += jnp.sum(dhv * xh, axis=0, keepdims=True)
        dout = d_ref[...] + _rms_bwd(xh, r, wpre_ref[...], dhv)
        do_ref[...] = dout
        if two:
            yh, ry = _rms_stats(y_ref[...])
            dwpost_ref[...] += jnp.sum(dout * yh, axis=0, keepdims=True)
            dy_ref[...] = _rms_bwd(yh, ry, wpost_ref[...], dout).astype(BF16)

    ins = [dres, dh, xin, w_pre] + ([y, w_post] if two else [])
    in_specs = [_row_spec(tr, d)] * 3 + [_vec_spec(d)] + ([_row_spec(tr, d), _vec_spec(d)] if two else [])
    out_specs = [_row_spec(tr, d), _vec_spec(d)] + ([_row_spec(tr, d), _vec_spec(d)] if two else [])
    out_shape = [jax.ShapeDtypeStruct((t, d), F32), jax.ShapeDtypeStruct((1, d), F32)]
    if two:
        out_shape += [jax.ShapeDtypeStruct((t, d), BF16), jax.ShapeDtypeStruct((1, d), F32)]
    return pl.pallas_call(
        body, name=name, grid=(t // tr,),
        in_specs=in_specs, out_specs=out_specs, out_shape=out_shape,
        compiler_params=_cparams(1),
    )(*ins)


_LEVELS = (32, 16, 8, 4, 2, 1)
_N_CUM = 3 + 2 * len(_LEVELS)


def _hgrn_constants():
    c = GLA_CHUNK
    idx = np.arange(c)
    t, r = idx[:, None], idx[None, :]
    mats = [(r <= t), (r > t), np.ones((c, c), bool)]
    lq, lk, masks = [], [], []
    for h in _LEVELS:
        blk, pos = idx // (2 * h), idx % (2 * h)
        mid = blk * 2 * h + h - 1
        upper, lower = pos >= h, pos < h
        lq.append(upper[:, None] & (r > mid[:, None]) & (r <= t))
        lk.append(lower[:, None] & (r > t) & (r <= mid[:, None]))
        masks.append((blk[:, None] == blk[None, :]) & upper[:, None] & lower[None, :])
    cum = np.concatenate(mats + lq + lk, axis=0).astype(np.float32)
    rev = (r >= t).astype(np.float32)
    return (jnp.asarray(cum, BF16), jnp.asarray(rev, BF16), jnp.asarray(np.stack(masks).astype(np.float32)))


def _hgrn_gates(qp, fp, lb):
    sq = _sigmoid(qp)
    q = qp * sq
    sg = _sigmoid(fp)
    f = lb + (1.0 - lb) * sg
    k = 1.0 - f
    logf = jnp.log(jnp.maximum(f, 1e-30))
    return q, sq, sg, f, k, logf


def _hgrn_decays(cum_ref, logf):
    c = GLA_CHUNK
    e = jnp.exp(_dot_exact_l(cum_ref[...], logf))
    part = lambda i: e[i * c:(i + 1) * c]
    n = len(_LEVELS)
    return part(0), part(1), part(2), [part(3 + i) for i in range(n)], [part(3 + n + i) for i in range(n)]


def _hgrn_fwd(proj, lb, nw, n_heads, comms=()):
    t = proj.shape[0]
    aw = n_heads * HEAD
    rb = _tile(t, HGRN_ROWS)
    c = GLA_CHUNK
    n_sub = rb // c
    cum, _, masks = _hgrn_constants()
    lay = _CommLayout(comms)
    hp = HGRN_HEADS if n_heads % HGRN_HEADS == 0 else 1
    wd = hp * HEAD
    grid = (n_heads // hp, t // rb)

    def body(*refs):
        q_ref, f_ref, i_ref, g_ref, lb_ref, nw_ref, cum_ref, m_ref = refs[:8]
        cin = refs[8:8 + lay.n_in]
        a_ref, o_ref, s_ref, sc_ref = refs[8 + lay.n_in:12 + lay.n_in]
        cout = refs[12 + lay.n_in:12 + lay.n_in + lay.n_out]
        st = refs[12 + lay.n_in + lay.n_out]
        csem = refs[13 + lay.n_in + lay.n_out:]
        step = _linear_step(grid)
        lay.run(cin, cout, csem, step, grid[0] * grid[1], post=False)

        @pl.when(pl.program_id(1) == 0)
        def _():
            st[...] = jnp.zeros_like(st)

        lbv = lb_ref[...]
        nwv = nw_ref[...]
        eye = (lax.broadcasted_iota(jnp.int32, (c, c), 0) == lax.broadcasted_iota(jnp.int32, (c, c), 1)).astype(F32)
        heads = range(hp)
        hs = lambda a, h: a[:, h * HEAD:(h + 1) * HEAD]

        def chunk(j, carry):
            rows = pl.ds(pl.multiple_of(j * c, c), c)
            q, _, _, _, k, logf = _hgrn_gates(q_ref[rows, :], f_ref[rows, :], lbv)
            v = i_ref[rows, :]
            eb, ebe, eend, eq, ek = _hgrn_decays(cum_ref, logf)
            qt, kt, qk = q * eb, k * ebe, q * k
            s_in = [st[h] for h in heads]
            for h in heads:
                s_ref[h, j] = s_in[h]
            inter = [_dot(hs(qt, h), s_in[h], "nt") for h in heads]
            for h in heads:
                st[h] = s_in[h] * hs(eend, h)[0:1] + _dot(hs(v, h), hs(kt, h), "tn")
            scores = [eye * jnp.sum(hs(qk, h), axis=-1, keepdims=True) for h in heads]
            for lvl in range(len(_LEVELS)):
                ql, kl = q * eq[lvl], k * ek[lvl]
                for h in heads:
                    scores[h] = scores[h] + m_ref[lvl] * _dot(hs(ql, h), hs(kl, h), "nt")
            gv = g_ref[rows, :]
            gate = nwv * (gv * _sigmoid(gv))
            for h in heads:
                sc_ref[h, rows, :] = scores[h]
                o = inter[h] + _dot(scores[h], hs(v, h))
                o_ref[rows, h * HEAD:(h + 1) * HEAD] = o
                r = lax.rsqrt(jnp.mean(o * o, axis=-1, keepdims=True) + EPS)
                a_ref[rows, h * HEAD:(h + 1) * HEAD] = (o * r * hs(gate, h)).astype(BF16)
            return carry

        lax.fori_loop(0, n_sub, chunk, 0, unroll=HGRN_UNROLL)
        lay.run(cin, cout, csem, step, grid[0] * grid[1], post=True)

    n_hb = n_heads // hp
    col = lambda base: pl.BlockSpec((rb, wd), lambda h, r: (r, base * n_hb + h))
    vec = pl.BlockSpec((1, wd), lambda h, r: (0, h))
    res = pl.pallas_call(
        body, name="hgrn2_fwd", grid=grid,
        in_specs=[col(0), col(1), col(2), col(3), vec, vec,
                  pl.BlockSpec(cum.shape, lambda h, r: (0, 0)), pl.BlockSpec(masks.shape, lambda h, r: (0, 0, 0))] + lay.in_specs,
        out_specs=[pl.BlockSpec((rb, wd), lambda h, r: (r, h)), pl.BlockSpec((rb, wd), lambda h, r: (r, h)),
                   pl.BlockSpec((hp, n_sub, HEAD, HEAD), lambda h, r: (h, r, 0, 0)),
                   pl.BlockSpec((hp, rb, c), lambda h, r: (h, r, 0))] + lay.out_specs,
        out_shape=[jax.ShapeDtypeStruct((t, aw), BF16), jax.ShapeDtypeStruct((t, aw), F32),
                   jax.ShapeDtypeStruct((n_heads, t // c, HEAD, HEAD), F32),
                   jax.ShapeDtypeStruct((n_heads, t, c), F32)] + lay.out_shapes,
        scratch_shapes=[pltpu.VMEM((hp, HEAD, HEAD), F32)] + lay.sem_shapes,
        compiler_params=_cparams(2),
    )(proj, proj, proj, proj, lb, nw, cum, masks, *lay.arrays)
    lay.deliver(res[4:])
    return res[:4]


def _hgrn_bwd(proj, lb, nw, o_raw, states, scores, dab, n_heads, comms=()):
    t = proj.shape[0]
    aw = n_heads * HEAD
    rb = _tile(t, HGRN_ROWS)
    c = GLA_CHUNK
    n_sub = rb // c
    n_rb = t // rb
    cum, rev, masks = _hgrn_constants()
    lay = _CommLayout(comms)
    hp = HGRN_HEADS if n_heads % HGRN_HEADS == 0 else 1
    wd = hp * HEAD
    grid = (n_heads // hp, n_rb)

    def body(*refs):
        q_ref, f_ref, i_ref, g_ref, lb_ref, nw_ref, o_ref, s_ref, sc_ref, da_ref, cum_ref, rev_ref, m_ref = refs[:13]
        cin = refs[13:13 + lay.n_in]
        dq_ref, df_ref, di_ref, dg_ref, dlb_ref, dnw_ref = refs[13 + lay.n_in:19 + lay.n_in]
        cout = refs[19 + lay.n_in:19 + lay.n_in + lay.n_out]
        dst = refs[19 + lay.n_in + lay.n_out]
        csem = refs[20 + lay.n_in + lay.n_out:]
        step = _linear_step(grid)
        lay.run(cin, cout, csem, step, grid[0] * grid[1], post=False)

        @pl.when(pl.program_id(1) == 0)
        def _():
            dst[...] = jnp.zeros_like(dst)
            dlb_ref[...] = jnp.zeros_like(dlb_ref)
            dnw_ref[...] = jnp.zeros_like(dnw_ref)

        lbv = lb_ref[...]
        nwv = nw_ref[...]
        ri = lax.broadcasted_iota(jnp.int32, (c, c), 0)
        ci = lax.broadcasted_iota(jnp.int32, (c, c), 1)
        eye = (ri == ci).astype(F32)
        causal = (ci <= ri).astype(F32)
        last_row = (lax.broadcasted_iota(jnp.int32, (c, wd), 0) == c - 1).astype(F32)
        heads = range(hp)
        hs = lambda a, h: a[:, h * HEAD:(h + 1) * HEAD]
        wide = lambda parts: parts[0] if hp == 1 else jnp.concatenate(parts, axis=1)

        def head_mean(a):
            return wide([jnp.broadcast_to(jnp.mean(hs(a, h), axis=-1, keepdims=True), (c, HEAD)) for h in heads])

        def chunk(jj, carry):
            j = n_sub - 1 - jj
            rows = pl.ds(pl.multiple_of(j * c, c), c)
            qp = q_ref[rows, :]
            q, sq, sg, f, k, logf = _hgrn_gates(qp, f_ref[rows, :], lbv)
            v = i_ref[rows, :]
            gv = g_ref[rows, :]
            eb, ebe, eend, eq, ek = _hgrn_decays(cum_ref, logf)
            s_in = [s_ref[h, j] for h in heads]
            a_sc = [sc_ref[h, rows, :] for h in heads]
            dsn = [dst[h] for h in heads]
            o = o_ref[rows, :]
            r = lax.rsqrt(head_mean(o * o) + EPS)
            oh = o * r
            sgg = _sigmoid(gv)
            sil = gv * sgg
            da = da_ref[rows, :]
            dg_ref[rows, :] = (da * oh * nwv * (sgg * (1.0 + gv * (1.0 - sgg)))).astype(BF16)
            dnw_ref[...] += jnp.sum(da * oh * sil, axis=0, keepdims=True)
            doh = da * nwv * sil
            do = r * (doh - oh * head_mean(doh * oh))
            kt = k * ebe
            qt = q * eb
            d_sc = [_dot(hs(do, h), hs(v, h), "nt") * causal for h in heads]
            dqt = wide([_dot(hs(do, h), s_in[h]) for h in heads])
            dkt = wide([_dot(hs(v, h), dsn[h]) for h in heads])
            for h in heads:
                dst[h] = dsn[h] * hs(eend, h)[0:1] + _dot(hs(do, h), hs(qt, h), "tn")
            di_ref[rows, :] = wide([_dot(a_sc[h], hs(do, h), "tn") + _dot(hs(kt, h), dsn[h], "nt") for h in heads]).astype(BF16)
            diag = wide([jnp.broadcast_to(jnp.sum(d_sc[h] * eye, axis=-1, keepdims=True), (c, HEAD)) for h in heads])
            dq = dqt * eb
            dk = dkt * ebe
            db = q * dq - k * dk
            dq = dq + diag * k
            dk = dk + diag * q
            for lvl in range(len(_LEVELS)):
                ql = (q * eq[lvl]).astype(BF16)
                kl = (k * ek[lvl]).astype(BF16)
                dm = [(m_ref[lvl] * d_sc[h]).astype(BF16) for h in heads]
                gq = wide([_dot(dm[h], hs(kl, h)) for h in heads])
                gk = wide([_dot(dm[h], hs(ql, h), "tn") for h in heads])
                dq = dq + gq * eq[lvl]
                dk = dk + gk * ek[lvl]
                db = db + ql.astype(F32) * gq - kl.astype(F32) * gk
            state_term = wide([jnp.sum(s_in[h] * dsn[h], axis=0, keepdims=True) for h in heads])
            extra = jnp.sum(dkt * kt, axis=0, keepdims=True) + eend[0:1] * state_term
            db = db + last_row * extra
            dlogf = _dot_exact_l(rev_ref[...], db)
            dfv = jnp.where(f > 1e-30, dlogf / f, 0.0) - dk
            df_ref[rows, :] = (dfv * (1.0 - lbv) * sg * (1.0 - sg)).astype(BF16)
            dlb_ref[...] += jnp.sum(dfv * (1.0 - sg), axis=0, keepdims=True)
            dq_ref[rows, :] = (dq * (sq * (1.0 + qp * (1.0 - sq)))).astype(BF16)
            return carry

        lax.fori_loop(0, n_sub, chunk, 0, unroll=HGRN_UNROLL)
        lay.run(cin, cout, csem, step, grid[0] * grid[1], post=True)

    n_hb = n_heads // hp
    col = lambda base: pl.BlockSpec((rb, wd), lambda h, r: (n_rb - 1 - r, base * n_hb + h))
    blk = pl.BlockSpec((rb, wd), lambda h, r: (n_rb - 1 - r, h))
    vec = pl.BlockSpec((1, wd), lambda h, r: (0, h))
    const = lambda a: pl.BlockSpec(a.shape, lambda h, r: (0,) * a.ndim)
    res = pl.pallas_call(
        body, name="hgrn2_bwd", grid=grid,
        in_specs=[col(0), col(1), col(2), col(3), vec, vec, blk,
                  pl.BlockSpec((hp, n_sub, HEAD, HEAD), lambda h, r: (h, n_rb - 1 - r, 0, 0)),
                  pl.BlockSpec((hp, rb, c), lambda h, r: (h, n_rb - 1 - r, 0)),
                  blk, const(cum), const(rev), const(masks)] + lay.in_specs,
        out_specs=[blk, blk, blk, blk, vec, vec] + lay.out_specs,
        out_shape=[jax.ShapeDtypeStruct((t, aw), BF16)] * 4 + [jax.ShapeDtypeStruct((1, aw), F32)] * 2 + lay.out_shapes,
        scratch_shapes=[pltpu.VMEM((hp, HEAD, HEAD), F32)] + lay.sem_shapes,
        compiler_params=_cparams(2),
    )(proj, proj, proj, proj, lb, nw, o_raw, states, scores, dab, cum, rev, masks, *lay.arrays)
    lay.deliver(res[6:])
    return res[:6]


def _lb_fwd(lb_param):
    def body(p_ref, o_ref):
        p = p_ref[...]
        e = jnp.exp(p - jnp.max(p, axis=0, keepdims=True))
        o_ref[...] = e[0:1] / jnp.sum(e, axis=0, keepdims=True)

    return pl.pallas_call(body, name="lb_fwd", out_shape=jax.ShapeDtypeStruct((1, lb_param.shape[1]), F32))(lb_param)


def _lb_bwd(lb_param, dlb):
    def body(p_ref, d_ref, o_ref):
        p = p_ref[...]
        e = jnp.exp(p - jnp.max(p, axis=0, keepdims=True))
        s = e / jnp.sum(e, axis=0, keepdims=True)
        first = (lax.broadcasted_iota(jnp.int32, p.shape, 0) == 0).astype(F32)
        o_ref[...] = d_ref[...] * s[0:1] * (first - s)

    return pl.pallas_call(body, name="lb_bwd", out_shape=jax.ShapeDtypeStruct(lb_param.shape, F32))(lb_param, dlb)


def _gmlp_norm(v, lnw, lnb):
    vf = _gelu(v)
    mu = jnp.mean(vf, axis=-1, keepdims=True)
    cen = vf - mu
    rstd = lax.rsqrt(jnp.mean(cen * cen, axis=-1, keepdims=True) + EPS)
    xh = cen * rstd
    return xh, rstd, xh * lnw + lnb


def _tril(n):
    return (lax.broadcasted_iota(jnp.int32, (n, n), 1) <= lax.broadcasted_iota(jnp.int32, (n, n), 0)).astype(F32)


def _gmlp_fwd(proj, lnw, lnb, w_sp, bs_t, n_groups, col_base):
    t = proj.shape[0]
    bw = n_groups * HEAD
    c = GMLP_CHUNK

    def body(u_ref, v_ref, lnw_ref, lnb_ref, w_ref, bs_ref, o_ref):
        tri = _tril(c)
        uf = _gelu(u_ref[...])
        _, _, vn = _gmlp_norm(v_ref[...], lnw_ref[...], lnb_ref[...])
        for g in range(n_groups):
            cols = slice(g * HEAD, (g + 1) * HEAD)
            z = _dot(w_ref[g] * tri, vn[:, cols]) + bs_ref[:, g:g + 1]
            o_ref[:, cols] = (uf[:, cols] * z).astype(BF16)

    blk = lambda b: pl.BlockSpec((c, bw), lambda n: (n, b))
    const = lambda a: pl.BlockSpec(a.shape, lambda n: (0,) * a.ndim)
    return pl.pallas_call(
        body, name="gmlp_fwd", grid=(t // c,),
        in_specs=[blk(col_base), blk(col_base + 1), const(lnw), const(lnb), const(w_sp), const(bs_t)],
        out_specs=pl.BlockSpec((c, bw), lambda n: (n, 0)),
        out_shape=jax.ShapeDtypeStruct((t, bw), BF16),
        compiler_params=_cparams(1),
    )(proj, proj, lnw, lnb, w_sp, bs_t)


def _gmlp_bwd(proj, lnw, lnb, w_sp, bs_t, dab, n_groups, col_base):
    t = proj.shape[0]
    bw = n_groups * HEAD
    c = GMLP_CHUNK
    n_steps = t // c
    sel = jnp.asarray((np.arange(bw)[:, None] // HEAD == np.arange(n_groups)[None, :]).astype(np.float32), BF16)

    def body(u_ref, v_ref, lnw_ref, lnb_ref, w_ref, bs_ref, d_ref, sel_ref,
             du_ref, dv_ref, dlnw_ref, dlnb_ref, dw_ref, dbs_ref, dz_acc, dvn_scr):
        step = pl.program_id(0)

        @pl.when(step == 0)
        def _():
            dlnw_ref[...] = jnp.zeros_like(dlnw_ref)
            dlnb_ref[...] = jnp.zeros_like(dlnb_ref)
            dw_ref[...] = jnp.zeros_like(dw_ref)
            dz_acc[...] = jnp.zeros_like(dz_acc)

        tri = _tril(c)
        u = u_ref[...]
        v = v_ref[...]
        uf = _gelu(u)
        lnw_v = lnw_ref[...]
        xh, rstd, vn = _gmlp_norm(v, lnw_v, lnb_ref[...])
        dbo = d_ref[...]
        dz = dbo * uf
        dz_acc[...] += dz
        for g in range(n_groups):
            cols = slice(g * HEAD, (g + 1) * HEAD)
            wg = w_ref[g] * tri
            z = _dot(wg, vn[:, cols]) + bs_ref[:, g:g + 1]
            du_ref[:, cols] = (dbo[:, cols] * z * _gelu_grad(u[:, cols])).astype(BF16)
            dvn_scr[:, cols] = _dot(wg, dz[:, cols], "tn")
            dw_ref[g] += tri * _dot(dz[:, cols], vn[:, cols], "nt")
        dvn = dvn_scr[...]
        dlnw_ref[...] += jnp.sum(dvn * xh, axis=0, keepdims=True)
        dlnb_ref[...] += jnp.sum(dvn, axis=0, keepdims=True)
        dxh = dvn * lnw_v
        dvf = rstd * (dxh - jnp.mean(dxh, axis=-1, keepdims=True) - xh * jnp.mean(dxh * xh, axis=-1, keepdims=True))
        dv_ref[...] = (dvf * _gelu_grad(v)).astype(BF16)

        @pl.when(step == n_steps - 1)
        def _():
            dbs_ref[...] = _dot_exact_r(dz_acc[...], sel_ref[...])

    blk = lambda b: pl.BlockSpec((c, bw), lambda n: (n, b))
    const = lambda a: pl.BlockSpec(a.shape, lambda n: (0,) * a.ndim)
    row = pl.BlockSpec((c, bw), lambda n: (n, 0))
    vec = pl.BlockSpec((1, bw), lambda n: (0, 0))
    return pl.pallas_call(
        body, name="gmlp_bwd", grid=(n_steps,),
        in_specs=[blk(col_base), blk(col_base + 1), const(lnw), const(lnb), const(w_sp), const(bs_t), blk(1), const(sel)],
        out_specs=[row, row, vec, vec, const(w_sp), const(bs_t)],
        out_shape=[jax.ShapeDtypeStruct((t, bw), BF16), jax.ShapeDtypeStruct((t, bw), BF16),
                   jax.ShapeDtypeStruct((1, bw), F32), jax.ShapeDtypeStruct((1, bw), F32),
                   jax.ShapeDtypeStruct(w_sp.shape, F32), jax.ShapeDtypeStruct(bs_t.shape, F32)],
        scratch_shapes=[pltpu.VMEM((c, bw), F32), pltpu.VMEM((c, bw), F32)],
        compiler_params=_cparams(1),
    )(proj, proj, lnw, lnb, w_sp, bs_t, dab, sel)


def _pair_sum(name, grad, other, core):
    _, _, r, c = grad.shape
    tr = _rows(r, 6 * c)

    def body(core_ref, g_ref, o_ref, out_ref):
        out_ref[...] = (g_ref[...].astype(F32) + o_ref[...].astype(F32)).astype(BF16)

    return pl.pallas_call(
        body, name=name,
        grid_spec=pltpu.PrefetchScalarGridSpec(
            num_scalar_prefetch=1, grid=(N_CHIP, r // tr),
            in_specs=[pl.BlockSpec((None, None, tr, c), lambda k, i, core_ref: (k, core_ref[0], i, 0)),
                      pl.BlockSpec((None, tr, c), lambda k, i, core_ref: (k, i, 0))],
            out_specs=pl.BlockSpec((None, tr, c), lambda k, i, core_ref: (k, i, 0))),
        out_shape=jax.ShapeDtypeStruct((N_CHIP, r, c), BF16),
        compiler_params=_cparams(2),
    )(core, grad, other)


def _adamw_math(w, g, m, v):
    m = ADAM_B1 * m + (1.0 - ADAM_B1) * g
    v = ADAM_B2 * v + (1.0 - ADAM_B2) * (g * g)
    m_hat = m / (1.0 - ADAM_B1 ** ADAM_STEP)
    v_hat = v / (1.0 - ADAM_B2 ** ADAM_STEP)
    delta = -ADAM_LR * (m_hat / (jnp.sqrt(v_hat) + ADAM_EPS) + ADAM_WD * w)
    return delta, m, v


def _adamw(name, parts, w, m, v):
    n_parts, r, c = parts.shape
    tr = _rows(r, c * (n_parts * parts.dtype.itemsize + 28), mult=8)

    def body(p_ref, w_ref, m_ref, v_ref, g_ref, d_ref, mo_ref, vo_ref):
        g = p_ref[0].astype(F32)
        for i in range(1, n_parts):
            g = g + p_ref[i].astype(F32)
        g_ref[...] = g
        d_ref[...], mo_ref[...], vo_ref[...] = _adamw_math(w_ref[...], g, m_ref[...], v_ref[...])

    row = pl.BlockSpec((tr, c), lambda i: (i, 0))
    return pl.pallas_call(
        body, name=name, grid=(r // tr,),
        in_specs=[pl.BlockSpec((n_parts, tr, c), lambda i: (0, i, 0)), row, row, row],
        out_specs=[row] * 4,
        out_shape=[jax.ShapeDtypeStruct((r, c), F32)] * 4,
        compiler_params=_cparams(1),
    )(parts, w, m, v)


def kernel(x, p, pre_mix_w, w_in, lb_param, a_norm_w, gmlp_ln_w, gmlp_ln_b, w_spatial, b_spatial, w_out, post_mix_w, pre_ffn_w, w_gate, w_up, w_down, post_ffn_w, w_ple, w_ple_gate, post_ple_w, loss_target, m_pre_mix_w, m_w_in, m_lb_param, m_a_norm_w, m_gmlp_ln_w, m_gmlp_ln_b, m_w_spatial, m_b_spatial, m_w_out, m_post_mix_w, m_pre_ffn_w, m_w_gate, m_w_up, m_w_down, m_post_ffn_w, m_w_ple, m_w_ple_gate, m_post_ple_w, v_pre_mix_w, v_w_in, v_lb_param, v_a_norm_w, v_gmlp_ln_w, v_gmlp_ln_b, v_w_spatial, v_b_spatial, v_w_out, v_post_mix_w, v_pre_ffn_w, v_w_gate, v_w_up, v_w_down, v_post_ffn_w, v_w_ple, v_w_ple_gate, v_post_ple_w):
    big_names = ["w_in", "w_out", "w_gate", "w_up", "w_down", "w_ple", "w_ple_gate"]
    small_names = ["pre_mix_w", "lb_param", "a_norm_w", "gmlp_ln_w", "gmlp_ln_b", "w_spatial", "b_spatial",
                   "post_mix_w", "pre_ffn_w", "post_ffn_w", "post_ple_w"]
    all_names = ["pre_mix_w", "w_in", "lb_param", "a_norm_w", "gmlp_ln_w", "gmlp_ln_b", "w_spatial", "b_spatial",
                 "w_out", "post_mix_w", "pre_ffn_w", "w_gate", "w_up", "w_down", "post_ffn_w", "w_ple", "w_ple_gate",
                 "post_ple_w"]
    env = dict(locals())
    W = {n: env[n] for n in all_names}
    M = {n: env["m_" + n] for n in all_names}
    V = {n: env["v_" + n] for n in all_names}

    xs = x[0]
    ps = p[0, 0]
    tgt = loss_target[0]
    t, d = xs.shape
    aw = a_norm_w.shape[1]
    bw = gmlp_ln_w.shape[1]
    n_heads, n_groups = aw // HEAD, bw // HEAD
    core = lax.axis_index("c").astype(jnp.int32).reshape(1)

    transposed = ("w_gate", "w_up")
    local = lambda a, n: jnp.swapaxes(a, 1, 2)[0] if n in transposed else a[0]
    unlocal = lambda a, n: jnp.swapaxes(a[None], 1, 2) if n in transposed else a[None]
    shard = {n: local(W[n], n) for n in big_names}
    bf = {n: _cast_bf16("cast_" + n, shard[n]) for n in big_names}
    ag_in = _ag_comm([bf["w_in"]])
    _comm_only("ag_w_in", [ag_in])
    win_g = ag_in.results[0]
    n_in = bf["w_in"].shape[1]
    ffl = bf["w_gate"].shape[0]
    n_ple = bf["w_ple"].shape[1]
    ple = ps.shape[1]

    TM, TK = 1024, 1024
    tm = _tile(t, TM)
    tn1 = _tile(d, 1024)

    h1 = _rms_fwd("rms_pre_mix", xs, pre_mix_w)
    once = pl.Buffered(1)
    ag_a = _ag_comm([bf["w_gate"]], mid_frac=0.85)
    proj = _matmul(
        "mm_proj",
        [(h1, pl.BlockSpec((tm, d), lambda n, m, k: (m, 0))),
         (win_g, pl.BlockSpec((None, d, n_in), lambda n, m, k: (n, 0, 0), pipeline_mode=once))],
        [(0, 1, "nn", 0)],
        [(jax.ShapeDtypeStruct((t, N_DEV * n_in), F32), pl.BlockSpec((tm, n_in), lambda n, m, k: (m, n)))],
        (N_DEV, t // tm, 1), (tm, n_in), comms=[ag_a])[0]
    wgate_g = ag_a.results[0]
    lb = _lb_fwd(lb_param)
    ag_b = _ag_comm([bf["w_out"]], mid_frac=0.6)
    a_out, o_raw, states, scores = _hgrn_fwd(proj, lb, a_norm_w, n_heads, comms=[ag_b])
    wout_f = ag_b.results[0].reshape(d, d)
    bs_t = b_spatial[0].T
    w_sp = w_spatial[0]
    col_u = (4 * aw) // bw
    b_out = _gmlp_fwd(proj, gmlp_ln_w, gmlp_ln_b, w_sp, bs_t, n_groups, col_u)
    ab = jnp.concatenate([a_out, b_out], axis=1)
    mix = _mm_plain("mm_mix", ab, wout_f, "nn", F32, TM, 1024, d)
    x1, h2 = _resid_rms("resid_mix", xs, mix, post_mix_w, pre_ffn_w)

    def swiglu(accs, gate_v):
        gf = gate_v.astype(F32)
        return accs[0], gf * _sigmoid(gf) * accs[0]

    tmf = _tile(t, 512)
    blk3 = lambda: pl.BlockSpec((None, tmf, ffl), lambda j, m, k: (j, m, 0))
    ag_c = _ag_comm([bf["w_up"]], mid_frac=0.85)
    gate = _matmul(
        "mm_ffn_gate",
        [(h2, pl.BlockSpec((tm, d), lambda j, m, k: (m, 0))),
         (wgate_g, pl.BlockSpec((None, ffl, d), lambda j, m, k: (j, 0, 0), pipeline_mode=once))],
        [(0, 1, "nt", 0)],
        [(jax.ShapeDtypeStruct((N_DEV, t, ffl), BF16), pl.BlockSpec((None, tm, ffl), lambda j, m, k: (j, m, 0)))],
        (N_DEV, t // tm, 1), (tm, ffl), comms=[ag_c])[0]
    wup_g = ag_c.results[0]
    ag_d = _ag_comm([bf["w_down"]], mid_frac=0.85)
    up, act = _matmul(
        "mm_ffn_up",
        [(h2, pl.BlockSpec((tmf, d), lambda j, m, k: (m, 0))),
         (wup_g, pl.BlockSpec((None, ffl, d), lambda j, m, k: (j, 0, 0), pipeline_mode=once)),
         (gate, blk3())],
        [(0, 1, "nt", 0)],
        [(jax.ShapeDtypeStruct((N_DEV, t, ffl), BF16), blk3()) for _ in range(2)],
        (N_DEV, t // tmf, 1), (tmf, ffl), epilogue=swiglu, comms=[ag_d])
    wdown_g = ag_d.results[0]
    ag_e = _ag_comm([bf["w_ple_gate"], bf["w_ple"]], mid_frac=0.6)
    tn_d = _tile(d, 2048)
    ff = _matmul(
        "mm_ffn_down",
        [(act, pl.BlockSpec((None, tm, ffl), lambda m, n, k: (k, m, 0))),
         (wdown_g, pl.BlockSpec((None, ffl, tn_d), lambda m, n, k: (k, 0, n)))],
        [(0, 1, "nn", 0)],
        [(jax.ShapeDtypeStruct((t, d), F32), pl.BlockSpec((tm, tn_d), lambda m, n, k: (m, n)))],
        (t // tm, d // tn_d, N_DEV), (tm, tn_d), comms=[ag_e])[0]
    wpg_f = ag_e.results[0].reshape(d, d)
    wple_g = ag_e.results[1]
    x2, x2b = _resid_rms("resid_ffn", x1, ff, post_ffn_w, None)

    pgl = _mm_plain("mm_ple_gate", x2b, wpg_f, "nn", F32, TM, 1024, d)
    pe = _matmul(
        "mm_ple",
        [(ps, pl.BlockSpec((tm, ple), lambda m, n, k: (m, 0))), (wple_g, pl.BlockSpec((None, ple, n_ple), lambda m, n, k: (n, 0, 0)))],
        [(0, 1, "nn", 0)],
        [(jax.ShapeDtypeStruct((t, N_DEV * n_ple), F32), pl.BlockSpec((tm, n_ple), lambda m, n, k: (m, n)))],
        (t // tm, N_DEV, 1), (tm, n_ple))[0]
    loss_part, d3, dpe, dpgl, g_post_ple = _ple_loss("ple_loss", x2, pe, pgl, post_ple_w, tgt)

    tkt = _tile(t, TK)
    g_wple = _matmul(
        "mm_dw_ple",
        [(ps, pl.BlockSpec((tkt, ple), lambda n, k: (k, 0))), (dpe, pl.BlockSpec((tkt, n_ple), lambda n, k: (k, n)))],
        [(0, 1, "tn", 0)],
        [(jax.ShapeDtypeStruct((N_DEV, ple, n_ple), BF16), pl.BlockSpec((None, ple, n_ple), lambda n, k: (n, 0, 0)))],
        (N_DEV, t // tkt), (ple, n_ple))[0]
    g_wpg = _mm_plain("mm_dw_ple_gate", x2b, dpgl, "tn", BF16, TM, 1024, t)

    def by_chip(g):
        return g.reshape((N_CHIP, 2) + g.shape[-2:])

    def pair_sums(names, comm):
        return [_pair_sum("pair_sum_" + n, g, o, core) for n, g, o in zip(names, comm.arrays, comm.results)]

    r1_p = _pair_comm([by_chip(g_wpg.reshape(N_DEV, d // N_DEV, d)), by_chip(g_wple)])
    d2 = _mm_plain("mm_d_x2", dpgl, wpg_f, "nt", F32, TM, 512, d, extra=d3, epilogue=lambda accs, e: [accs[0] + e],
                   comms=[r1_p])
    r2_p = _chip_comm(pair_sums(["w_ple_gate", "w_ple"], r1_p))

    dff, g_post_ffn = _norm_bwd("norm_bwd_ffn", d2, ff, post_ffn_w)
    g_wdown = _matmul(
        "mm_dw_down",
        [(act, pl.BlockSpec((None, t, ffl), lambda j, n, k: (j, 0, 0), pipeline_mode=once)),
         (dff, pl.BlockSpec((t, tn1), lambda j, n, k: (0, n)))],
        [(0, 1, "tn", 0)],
        [(jax.ShapeDtypeStruct((N_DEV, ffl, d), BF16), pl.BlockSpec((None, ffl, tn1), lambda j, n, k: (j, 0, n)))],
        (N_DEV, d // tn1, 1), (ffl, tn1), comms=[r2_p])[0]
    r1_d = _pair_comm([by_chip(g_wdown)])

    def swiglu_bwd(accs, gate_v, up_v):
        dact = accs[0]
        gf = gate_v.astype(F32)
        sg = _sigmoid(gf)
        return dact * up_v.astype(F32) * (sg * (1.0 + gf * (1.0 - sg))), dact * (gf * sg)

    dgate, dup = _matmul(
        "mm_d_act",
        [(dff, pl.BlockSpec((tmf, d), lambda j, m, k: (m, 0))),
         (wdown_g, pl.BlockSpec((None, ffl, d), lambda j, m, k: (j, 0, 0), pipeline_mode=once)),
         (gate, blk3()), (up, blk3())],
        [(0, 1, "nt", 0)],
        [(jax.ShapeDtypeStruct((N_DEV, t, ffl), BF16), blk3()) for _ in range(2)],
        (N_DEV, t // tmf, 1), (tmf, ffl), epilogue=swiglu_bwd, comms=[r1_d])
    r2_d = _chip_comm(pair_sums(["w_down"], r1_d))
    tmd = _tile(d, TM)
    def dw_ffn(name, dy, comms):
        return _matmul(
            name,
            [(dy, pl.BlockSpec((None, t, ffl), lambda j, n, k: (j, 0, 0), pipeline_mode=once)),
             (h2, pl.BlockSpec((t, tn1), lambda j, n, k: (0, n)))],
            [(0, 1, "tn", 0)],
            [(jax.ShapeDtypeStruct((N_DEV, ffl, d), BF16), pl.BlockSpec((None, ffl, tn1), lambda j, n, k: (j, 0, n)))],
            (N_DEV, d // tn1, 1), (ffl, tn1), comms=comms)[0]

    g_wgate = dw_ffn("mm_dw_gate", dgate, [r2_d])
    r1_g = _pair_comm([by_chip(g_wgate)])
    g_wup = dw_ffn("mm_dw_up", dup, [r1_g])
    r2_g = _chip_comm(pair_sums(["w_gate"], r1_g))
    r1_u = _pair_comm([by_chip(g_wup)])
    tn1 = _tile(d, 1024)
    dh2 = _matmul(
        "mm_d_h2",
        [(dgate, pl.BlockSpec((None, tm, ffl), lambda m, n, k: (k, m, 0))),
         (wgate_g, pl.BlockSpec((None, ffl, tn1), lambda m, n, k: (k, 0, n))),
         (dup, pl.BlockSpec((None, tm, ffl), lambda m, n, k: (k, m, 0))),
         (wup_g, pl.BlockSpec((None, ffl, tn1), lambda m, n, k: (k, 0, n)))],
        [(0, 1, "nn", 0), (2, 3, "nn", 0)],
        [(jax.ShapeDtypeStruct((t, d), F32), pl.BlockSpec((tm, tn1), lambda m, n, k: (m, n)))],
        (t // tm, d // tn1, N_DEV), (tm, tn1), comms=[r2_g, r1_u])[0]
    r2_u = _chip_comm(pair_sums(["w_up"], r1_u))
    d1, g_pre_ffn, dmix, g_post_mix = _prenorm_bwd("prenorm_bwd_ffn", d2, dh2, x1, pre_ffn_w, mix, post_mix_w)

    g_wout = _mm_plain("mm_dw_out", ab, dmix, "tn", BF16, TM, 1024, t)
    r1_o = _pair_comm([by_chip(g_wout.reshape(N_DEV, d // N_DEV, d))])
    dab = _mm_plain("mm_d_ab", dmix, wout_f, "nt", F32, TM, 1024, d, comms=[r1_o])
    r2_o = _chip_comm(pair_sums(["w_out"], r1_o))
    dq, df, di, dg, dlb, g_a_norm = _hgrn_bwd(proj, lb, a_norm_w, o_raw, states, scores, dab, n_heads, comms=[r2_u])
    du, dv, g_ln_w, g_ln_b, g_wsp, g_bs_t = _gmlp_bwd(proj, gmlp_ln_w, gmlp_ln_b, w_sp, bs_t, dab, n_groups, col_u)
    dproj = jnp.concatenate([dq, df, di, dg, du, dv], axis=1)
    g_win = _matmul(
        "mm_dw_in",
        [(h1, pl.BlockSpec((t, tmd), lambda j, m, k: (0, m))),
         (dproj, pl.BlockSpec((t, n_in), lambda j, m, k: (0, j), pipeline_mode=once))],
        [(0, 1, "tn", 0)],
        [(jax.ShapeDtypeStruct((N_DEV, d, n_in), BF16), pl.BlockSpec((None, tmd, n_in), lambda j, m, k: (j, m, 0)))],
        (N_DEV, d // tmd, 1), (tmd, n_in), comms=[r2_o])[0]
    r1_in = _pair_comm([by_chip(g_win)])
    _comm_only("rs_pair_w_in", [r1_in])
    r2_in = _chip_comm(pair_sums(["w_in"], r1_in))
    small_grad = {
        "lb_param": _lb_bwd(lb_param, dlb), "a_norm_w": g_a_norm, "gmlp_ln_w": g_ln_w,
        "gmlp_ln_b": g_ln_b, "w_spatial": g_wsp, "b_spatial": g_bs_t.T, "post_mix_w": g_post_mix,
        "pre_ffn_w": g_pre_ffn, "post_ffn_w": g_post_ffn, "post_ple_w": g_post_ple,
    }
    assert small_names[0] == "pre_mix_w"
    pack = lambda get, names: jnp.concatenate([get(n).reshape(-1, LANE) for n in names], axis=0)
    ag_main = _ag_comm([pack(lambda n: small_grad[n], small_names[1:])], mid_frac=0.5)
    dh1 = _matmul(
        "mm_d_h1",
        [(dproj, pl.BlockSpec((tm, n_in), lambda m, n, k: (m, k))), (win_g, pl.BlockSpec((None, tn1, n_in), lambda m, n, k: (k, n, 0)))],
        [(0, 1, "nt", 0)],
        [(jax.ShapeDtypeStruct((t, d), F32), pl.BlockSpec((tm, tn1), lambda m, n, k: (m, n)))],
        (t // tm, d // tn1, N_DEV), (tm, tn1), comms=[r2_in, ag_main])[0]
    grad_x, g_pre_mix = _prenorm_bwd("prenorm_bwd_mix", d1, dh1, xs, pre_mix_w)
    ag_pre = _ag_comm([g_pre_mix.reshape(-1, LANE)])
    _comm_only("ag_small_pre", [ag_pre], in_vmem=True)
    g_all = jnp.concatenate([ag_pre.results[0], ag_main.results[0]], axis=1)

    reduced = {
        "w_in": r2_in.results[0], "w_out": r2_o.results[0], "w_gate": r2_g.results[0], "w_up": r2_u.results[0],
        "w_down": r2_d.results[0], "w_ple": r2_p.results[1], "w_ple_gate": r2_p.results[0],
    }
    grads, deltas, new_m, new_v = {}, {}, {}, {}
    for n in big_names:
        res = _adamw("adamw_" + n, reduced[n], shard[n], local(M[n], n), local(V[n], n))
        grads[n], deltas[n], new_m[n], new_v[n] = (unlocal(a, n) for a in res)
    sg, sd, sm, sv = _adamw("adamw_small", g_all, pack(lambda n: W[n], small_names), pack(lambda n: M[n], small_names),
                            pack(lambda n: V[n], small_names))
    off = 0
    for n in small_names:
        rows = W[n].size // LANE
        for src, dst in ((sg, grads), (sd, deltas), (sm, new_m), (sv, new_v)):
            dst[n] = src[off:off + rows].reshape(W[n].shape)
        off += rows

    loss = lax.psum(loss_part[0, 0], ("x", "y", "c"))
    return (loss, grad_x[None], *[grads[n] for n in all_names], *[deltas[n] for n in all_names],
            *[new_m[n] for n in all_names], *[new_v[n] for n in all_names])
```

```python
import functools

import numpy as np
import jax
import jax.numpy as jnp
from jax import lax
from jax.experimental import pallas as pl
from jax.experimental.pallas import tpu as pltpu

F32 = jnp.float32
BF16 = jnp.bfloat16

EPS = 1e-6
HEAD = 128
GLA_CHUNK = 64
GMLP_CHUNK = 128
N_DEV = 8
N_CHIP = 4
LANE = 128
VMEM_LIMIT = 56 * 1024 * 1024
HGRN_ROWS = 512
ROW_TILE = 128
EPILOGUE_ROWS = 256
HGRN_UNROLL = 1
HGRN_HEADS = 8

ADAM_LR = 0.001
ADAM_B1 = 0.9
ADAM_B2 = 0.999
ADAM_EPS = 1e-08
ADAM_WD = 0.01
ADAM_STEP = 10

MESH = pl.DeviceIdType.MESH

_DIMS = {
    "nn": (((1,), (0,)), ((), ())),
    "nt": (((1,), (1,)), ((), ())),
    "tn": (((0,), (0,)), ((), ())),
}


def _tile(dim, pref):
    return pref if dim % pref == 0 else dim


def _rows(r, bytes_per_row, budget=18 * 1024 * 1024, mult=16):
    best = None
    for cand in range(mult, r + 1, mult):
        if r % cand == 0 and cand * bytes_per_row <= budget:
            best = cand
    return best if best is not None else r


def _cparams(n_axes):
    return pltpu.CompilerParams(dimension_semantics=("arbitrary",) * n_axes, vmem_limit_bytes=VMEM_LIMIT)


def _dot(a, b, form="nn"):
    return lax.dot_general(a.astype(BF16), b.astype(BF16), _DIMS[form], preferred_element_type=F32)


def _split3(x):
    hi = x.astype(BF16)
    r = x - hi.astype(F32)
    mid = r.astype(BF16)
    lo = (r - mid.astype(F32)).astype(BF16)
    return hi, mid, lo


def _dot_exact_l(c, x):
    hi, mid, lo = _split3(x)
    d = lambda y: lax.dot_general(c, y, _DIMS["nn"], preferred_element_type=F32)
    return d(hi) + d(mid) + d(lo)


def _dot_exact_r(x, c):
    hi, mid, lo = _split3(x)
    d = lambda y: lax.dot_general(y, c, _DIMS["nn"], preferred_element_type=F32)
    return d(hi) + d(mid) + d(lo)


def _sigmoid(x):
    return 1.0 / (1.0 + jnp.exp(-x))


def _gelu(x):
    return 0.5 * x * (1.0 + lax.erf(x * 0.7071067811865476))


def _gelu_grad(x):
    cdf = 0.5 * (1.0 + lax.erf(x * 0.7071067811865476))
    pdf = jnp.exp(-0.5 * x * x) * 0.3989422804014327
    return cdf + x * pdf


def _position():
    return lax.axis_index("x"), lax.axis_index("y"), lax.axis_index("c")


def _linear_step(grid):
    step = 0
    for ax, n in enumerate(grid):
        step = step * n + pl.program_id(ax)
    return step


class _Comm:
    def __init__(self, arrays, out_shapes, sem_shapes, phases):
        self.arrays, self.out_shapes, self.sem_shapes, self.phases = list(arrays), list(out_shapes), list(sem_shapes), phases
        self.results = None


class _CommLayout:
    def __init__(self, comms, space=pl.ANY):
        self.comms = list(comms)
        self.arrays = [a for c in self.comms for a in c.arrays]
        self.out_shapes = [s for c in self.comms for s in c.out_shapes]
        self.sem_shapes = [s for c in self.comms for s in c.sem_shapes]
        self.n_in, self.n_out = len(self.arrays), len(self.out_shapes)
        self.in_specs = [pl.BlockSpec(memory_space=space)] * self.n_in
        self.out_specs = [pl.BlockSpec(memory_space=space)] * self.n_out

    def run(self, cin, cout, csem, step, n_steps, post):
        i = o = s = 0
        for c in self.comms:
            ins, outs, sems = cin[i:i + len(c.arrays)], cout[o:o + len(c.out_shapes)], csem[s:s + len(c.sem_shapes)]
            i, o, s = i + len(c.arrays), o + len(c.out_shapes), s + len(c.sem_shapes)
            for frac, fn in c.phases:
                if (frac is None) != post:
                    continue
                due = n_steps - 1 if frac is None else max(0, min(int(frac * n_steps), n_steps - 2))
                if n_steps == 1:
                    fn(ins, outs, sems)
                else:
                    pl.when(step == due)(functools.partial(fn, ins, outs, sems))

    def deliver(self, results):
        o = 0
        for c in self.comms:
            c.results = list(results[o:o + len(c.out_shapes)])
            o += len(c.out_shapes)


def _comm_only(name, comms, in_vmem=False):
    lay = _CommLayout(comms, pltpu.VMEM if in_vmem else pl.ANY)

    def body(*refs):
        cin, cout, csem = refs[:lay.n_in], refs[lay.n_in:lay.n_in + lay.n_out], refs[lay.n_in + lay.n_out:]
        lay.run(cin, cout, csem, 0, 1, post=False)
        lay.run(cin, cout, csem, 0, 1, post=True)

    res = pl.pallas_call(
        body, name=name, in_specs=lay.in_specs, out_specs=lay.out_specs, out_shape=lay.out_shapes,
        scratch_shapes=lay.sem_shapes,
    )(*lay.arrays)
    lay.deliver(res)


def _ag_comm(shards, mid_frac=0.0):
    n = len(shards)
    per = N_DEV - 1

    def tools(ins, outs, sems):
        send_sems, recv_sems, local_sems = sems
        x, y, c = _position()
        me, sibling = (x, y, c), (x, y, 1 - c)
        chips = [(1 - x, y), (x, 1 - y), (1 - x, 1 - y)]

        def copy(a, k, block, to, from_shard=False):
            dst = outs[a].at[4 * block[0] + 2 * block[1] + block[2]]
            return pltpu.make_async_remote_copy(
                src_ref=ins[a] if from_shard else dst, dst_ref=dst,
                send_sem=send_sems.at[a * per + k], recv_sem=recv_sems.at[a * per + k],
                device_id=to, device_id_type=MESH)

        def local(a):
            return pltpu.make_async_copy(ins[a], outs[a].at[4 * x + 2 * y + c], local_sems.at[a])

        return me, sibling, chips, c, copy, local

    def first(ins, outs, sems):
        me, sibling, chips, c, copy, local = tools(ins, outs, sems)
        for a in range(n):
            local(a).start()
            copy(a, 0, me, sibling, True).start()
            for j, chip in enumerate(chips):
                copy(a, 1 + j, me, (*chip, c), True).start()

    def middle(ins, outs, sems):
        me, sibling, chips, c, copy, local = tools(ins, outs, sems)
        for a in range(n):
            for j, chip in enumerate(chips):
                copy(a, 1 + j, (*chip, c), me).wait_recv()
                copy(a, 4 + j, (*chip, c), sibling).start()

    def last(ins, outs, sems):
        me, sibling, chips, c, copy, local = tools(ins, outs, sems)
        for a in range(n):
            copy(a, 0, sibling, me).wait_recv()
            copy(a, 0, me, sibling, True).wait_send()
            for j, chip in enumerate(chips):
                copy(a, 4 + j, (*chip, 1 - c), me).wait_recv()
                copy(a, 1 + j, me, (*chip, c), True).wait_send()
                copy(a, 4 + j, (*chip, c), sibling).wait_send()
            local(a).wait()

    return _Comm(
        shards, [jax.ShapeDtypeStruct((N_DEV,) + s.shape, s.dtype) for s in shards],
        [pltpu.SemaphoreType.DMA((n * per,)), pltpu.SemaphoreType.DMA((n * per,)), pltpu.SemaphoreType.DMA((n,))],
        [(0.0, first), (mid_frac, middle), (None, last)])


def _pair_comm(grads):
    n = len(grads)

    def copies(ins, outs, sems):
        send_sems, recv_sems = sems
        x, y, c = _position()
        return [pltpu.make_async_remote_copy(
            src_ref=ins[a].at[k, 1 - c], dst_ref=outs[a].at[k],
            send_sem=send_sems.at[a * N_CHIP + k], recv_sem=recv_sems.at[a * N_CHIP + k],
            device_id=(x, y, 1 - c), device_id_type=MESH) for a in range(n) for k in range(N_CHIP)]

    def first(ins, outs, sems):
        for cp in copies(ins, outs, sems):
            cp.start()

    def last(ins, outs, sems):
        for cp in copies(ins, outs, sems):
            cp.wait()

    return _Comm(
        grads, [jax.ShapeDtypeStruct((N_CHIP,) + g.shape[2:], g.dtype) for g in grads],
        [pltpu.SemaphoreType.DMA((n * N_CHIP,)), pltpu.SemaphoreType.DMA((n * N_CHIP,))],
        [(0.0, first), (None, last)])


def _chip_comm(sums):
    n = len(sums)
    per = N_CHIP - 1

    def copies(ins, outs, sems):
        send_sems, recv_sems, local_sems = sems
        x, y, c = _position()
        my_chip = 2 * x + y
        cps = []
        for a in range(n):
            cps.append(pltpu.make_async_copy(ins[a].at[my_chip], outs[a].at[my_chip], local_sems.at[a]))
            for j, (px, py) in enumerate([(1 - x, y), (x, 1 - y), (1 - x, 1 - y)]):
                cps.append(pltpu.make_async_remote_copy(
                    src_ref=ins[a].at[2 * px + py], dst_ref=outs[a].at[my_chip],
                    send_sem=send_sems.at[a * per + j], recv_sem=recv_sems.at[a * per + j],
                    device_id=(px, py, c), device_id_type=MESH))
        return cps

    def first(ins, outs, sems):
        for cp in copies(ins, outs, sems):
            cp.start()

    def last(ins, outs, sems):
        for cp in copies(ins, outs, sems):
            cp.wait()

    return _Comm(
        sums, [jax.ShapeDtypeStruct(s.shape, s.dtype) for s in sums],
        [pltpu.SemaphoreType.DMA((n * per,)), pltpu.SemaphoreType.DMA((n * per,)), pltpu.SemaphoreType.DMA((n,))],
        [(0.0, first), (None, last)])


def _matmul(name, operands, pairs, outs, grid, acc_shape, n_slots=1, epilogue=None, comms=()):
    used = sorted({i for p in pairs for i in p[:2]})
    n_op = len(operands)
    n_out = len(outs)
    k_axis = len(grid) - 1
    n_k = grid[-1]
    lay = _CommLayout(comms)

    direct = n_k == 1 and epilogue is None
    n_acc = 0 if direct else 1

    def body(*refs):
        ops = refs[:n_op]
        out_refs = refs[n_op + lay.n_in:n_op + lay.n_in + n_out]
        acc = None if direct else refs[n_op + lay.n_in + n_out + lay.n_out]
        k = pl.program_id(k_axis)
        step = _linear_step(grid)
        cin = refs[n_op:n_op + lay.n_in]
        cout = refs[n_op + lay.n_in + n_out:n_op + lay.n_in + n_out + lay.n_out]
        csem = refs[n_op + lay.n_in + n_out + lay.n_out + n_acc:]
        lay.run(cin, cout, csem, step, int(np.prod(grid)), post=False)

        if n_k > 1:
            @pl.when(k == 0)
            def _():
                acc[...] = jnp.zeros_like(acc)

        vals = {i: ops[i][...] for i in used}
        vals = {i: (v if v.dtype == BF16 else v.astype(BF16)) for i, v in vals.items()}
        for s in range(n_slots):
            tot = None
            for ia, ib, form, slot in pairs:
                if slot != s:
                    continue
                d = lax.dot_general(vals[ia], vals[ib], _DIMS[form], preferred_element_type=F32)
                tot = d if tot is None else tot + d
            if direct:
                out_refs[s][...] = tot.astype(out_refs[s].dtype)
            elif n_k == 1:
                acc[s] = tot
            else:
                acc[s] += tot

        def finish():
            rows = acc_shape[0]
            chunk = EPILOGUE_ROWS if (epilogue is not None and rows % EPILOGUE_ROWS == 0) else rows
            for r0 in range(0, rows, chunk):
                sl = slice(r0, r0 + chunk)
                accs = [acc[s, sl, :] for s in range(n_slots)]
                extra = [ops[i][sl, :] for i in range(n_op) if i not in used]
                res = epilogue(accs, *extra) if epilogue is not None else accs
                for o, v in zip(out_refs, res):
                    o[sl, :] = v.astype(o.dtype)

        if n_k > 1:
            pl.when(k == n_k - 1)(finish)
        elif not direct:
            finish()

        lay.run(cin, cout, csem, step, int(np.prod(grid)), post=True)

    res = pl.pallas_call(
        body,
        name=name,
        grid=grid,
        in_specs=[s for _, s in operands] + lay.in_specs,
        out_specs=[s for _, s in outs] + lay.out_specs,
        out_shape=[s for s, _ in outs] + lay.out_shapes,
        scratch_shapes=([] if direct else [pltpu.VMEM((n_slots,) + tuple(acc_shape), F32)]) + lay.sem_shapes,
        compiler_params=_cparams(len(grid)),
    )(*[a for a, _ in operands], *lay.arrays)
    lay.deliver(res[n_out:])
    return res[:n_out]


def _mm_plain(name, a, b, form, out_dtype, tm, tn, tk, extra=None, epilogue=None, comms=()):
    if form == "nn":
        (M, K), N = a.shape, b.shape[1]
    elif form == "nt":
        (M, K), N = a.shape, b.shape[0]
    else:
        (K, M), N = a.shape, b.shape[1]
    tm, tn, tk = _tile(M, tm), _tile(N, tn), _tile(K, tk)
    a_spec = pl.BlockSpec((tk, tm), lambda m, n, k: (k, m)) if form == "tn" else pl.BlockSpec((tm, tk), lambda m, n, k: (m, k))
    b_spec = pl.BlockSpec((tn, tk), lambda m, n, k: (n, k)) if form == "nt" else pl.BlockSpec((tk, tn), lambda m, n, k: (k, n))
    operands = [(a, a_spec), (b, b_spec)]
    if extra is not None:
        operands.append((extra, pl.BlockSpec((tm, tn), lambda m, n, k: (m, n))))
    out = (jax.ShapeDtypeStruct((M, N), out_dtype), pl.BlockSpec((tm, tn), lambda m, n, k: (m, n)))
    return _matmul(name, operands, [(0, 1, form, 0)], [out], (M // tm, N // tn, K // tk), (tm, tn), epilogue=epilogue,
                   comms=comms)[0]


def _cast_bf16(name, w):
    r, c = w.shape
    tr = _rows(r, 6 * c)

    def body(w_ref, o_ref):
        o_ref[...] = w_ref[...].astype(BF16)

    return pl.pallas_call(
        body, name=name, grid=(r // tr,),
        in_specs=[pl.BlockSpec((tr, c), lambda i: (i, 0))],
        out_specs=pl.BlockSpec((tr, c), lambda i: (i, 0)),
        out_shape=jax.ShapeDtypeStruct((r, c), BF16),
        compiler_params=_cparams(1),
    )(w)


def _rms_stats(x):
    r = lax.rsqrt(jnp.mean(x * x, axis=-1, keepdims=True) + EPS)
    return x * r, r


def _rms_bwd(xhat, r, w, dy):
    dxh = dy * w
    return r * (dxh - xhat * jnp.mean(dxh * xhat, axis=-1, keepdims=True))


def _row_spec(tr, d):
    return pl.BlockSpec((tr, d), lambda i: (i, 0))


def _vec_spec(d):
    return pl.BlockSpec((1, d), lambda i: (0, 0))


def _rms_fwd(name, x, w):
    t, d = x.shape
    tr = _tile(t, ROW_TILE)

    def body(x_ref, w_ref, h_ref):
        xh, _ = _rms_stats(x_ref[...])
        h_ref[...] = (xh * w_ref[...]).astype(BF16)

    return pl.pallas_call(
        body, name=name, grid=(t // tr,),
        in_specs=[_row_spec(tr, d), _vec_spec(d)],
        out_specs=_row_spec(tr, d),
        out_shape=jax.ShapeDtypeStruct((t, d), BF16),
        compiler_params=_cparams(1),
    )(x, w)


def _resid_rms(name, xres, y, w_post, w_next):
    t, d = xres.shape
    tr = _tile(t, ROW_TILE)
    has_next = w_next is not None

    def body(*refs):
        if has_next:
            x_ref, y_ref, wp_ref, wn_ref, xo_ref, h_ref = refs
        else:
            x_ref, y_ref, wp_ref, xo_ref, h_ref = refs
        yh, _ = _rms_stats(y_ref[...])
        xn = x_ref[...] + yh * wp_ref[...]
        xo_ref[...] = xn
        if has_next:
            xh, _ = _rms_stats(xn)
            h_ref[...] = (xh * wn_ref[...]).astype(BF16)
        else:
            h_ref[...] = xn.astype(BF16)

    ins = [xres, y, w_post] + ([w_next] if has_next else [])
    in_specs = [_row_spec(tr, d), _row_spec(tr, d), _vec_spec(d)] + ([_vec_spec(d)] if has_next else [])
    return pl.pallas_call(
        body, name=name, grid=(t // tr,),
        in_specs=in_specs,
        out_specs=[_row_spec(tr, d), _row_spec(tr, d)],
        out_shape=[jax.ShapeDtypeStruct((t, d), F32), jax.ShapeDtypeStruct((t, d), BF16)],
        compiler_params=_cparams(1),
    )(*ins)


def _ple_loss(name, x2, pe, pgl, w_pp, tgt):
    t, d = x2.shape
    tr = _tile(t, ROW_TILE)

    def body(x2_ref, pe_ref, pgl_ref, w_ref, tgt_ref, loss_ref, d3_ref, dpe_ref, dpgl_ref, dw_ref):
        @pl.when(pl.program_id(0) == 0)
        def _():
            loss_ref[...] = jnp.zeros_like(loss_ref)
            dw_ref[...] = jnp.zeros_like(dw_ref)

        pe_v = pe_ref[...]
        s = _sigmoid(pgl_ref[...])
        y = pe_v * s
        yh, r = _rms_stats(y)
        w = w_ref[...]
        err = x2_ref[...] + yh * w - tgt_ref[...]
        loss_ref[...] += 0.5 * jnp.sum(jnp.mean(err * err, axis=-1, keepdims=True), axis=0, keepdims=True)
        d3 = err * (1.0 / d)
        d3_ref[...] = d3
        dw_ref[...] += jnp.sum(d3 * yh, axis=0, keepdims=True)
        dy = _rms_bwd(yh, r, w, d3)
        dpe_ref[...] = (dy * s).astype(BF16)
        dpgl_ref[...] = (dy * pe_v * s * (1.0 - s)).astype(BF16)

    return pl.pallas_call(
        body, name=name, grid=(t // tr,),
        in_specs=[_row_spec(tr, d), _row_spec(tr, d), _row_spec(tr, d), _vec_spec(d), _row_spec(tr, d)],
        out_specs=[pl.BlockSpec((1, 1), lambda i: (0, 0)), _row_spec(tr, d), _row_spec(tr, d), _row_spec(tr, d), _vec_spec(d)],
        out_shape=[jax.ShapeDtypeStruct((1, 1), F32), jax.ShapeDtypeStruct((t, d), F32),
                   jax.ShapeDtypeStruct((t, d), BF16), jax.ShapeDtypeStruct((t, d), BF16),
                   jax.ShapeDtypeStruct((1, d), F32)],
        compiler_params=_cparams(1),
    )(x2, pe, pgl, w_pp, tgt)


def _norm_bwd(name, dres, y, w_post):
    t, d = dres.shape
    tr = _tile(t, ROW_TILE)

    def body(d_ref, y_ref, w_ref, dy_ref, dw_ref):
        @pl.when(pl.program_id(0) == 0)
        def _():
            dw_ref[...] = jnp.zeros_like(dw_ref)

        dv = d_ref[...]
        yh, r = _rms_stats(y_ref[...])
        dw_ref[...] += jnp.sum(dv * yh, axis=0, keepdims=True)
        dy_ref[...] = _rms_bwd(yh, r, w_ref[...], dv).astype(BF16)

    return pl.pallas_call(
        body, name=name, grid=(t // tr,),
        in_specs=[_row_spec(tr, d), _row_spec(tr, d), _vec_spec(d)],
        out_specs=[_row_spec(tr, d), _vec_spec(d)],
        out_shape=[jax.ShapeDtypeStruct((t, d), BF16), jax.ShapeDtypeStruct((1, d), F32)],
        compiler_params=_cparams(1),
    )(dres, y, w_post)


def _prenorm_bwd(name, dres, dh, xin, w_pre, y=None, w_post=None):
    t, d = dres.shape
    tr = _tile(t, ROW_TILE)
    two = y is not None

    def body(*refs):
        if two:
            d_ref, dh_ref, x_ref, wpre_ref, y_ref, wpost_ref, do_ref, dwpre_ref, dy_ref, dwpost_ref = refs
        else:
            d_ref, dh_ref, x_ref, wpre_ref, do_ref, dwpre_ref = refs

        @pl.when(pl.program_id(0) == 0)
        def _():
            dwpre_ref[...] = jnp.zeros_like(dwpre_ref)
            if two:
                dwpost_ref[...] = jnp.zeros_like(dwpost_ref)

        dhv = dh_ref[...]
        xh, r = _rms_stats(x_ref[...])
        dwpre_ref[...] += jnp.sum(dhv * xh, axis=0, keepdims=True)
        dout = d_ref[...] + _rms_bwd(xh, r, wpre_ref[...], dhv)
        do_ref[...] = dout
        if two:
            yh, ry = _rms_stats(y_ref[...])
            dwpost_ref[...] += jnp.sum(dout * yh, axis=0, keepdims=True)
            dy_ref[...] = _rms_bwd(yh, ry, wpost_ref[...], dout).astype(BF16)

    ins = [dres, dh, xin, w_pre] + ([y, w_post] if two else [])
    in_specs = [_row_spec(tr, d)] * 3 + [_vec_spec(d)] + ([_row_spec(tr, d), _vec_spec(d)] if two else [])
    out_specs = [_row_spec(tr, d), _vec_spec(d)] + ([_row_spec(tr, d), _vec_spec(d)] if two else [])
    out_shape = [jax.ShapeDtypeStruct((t, d), F32), jax.ShapeDtypeStruct((1, d), F32)]
    if two:
        out_shape += [jax.ShapeDtypeStruct((t, d), BF16), jax.ShapeDtypeStruct((1, d), F32)]
    return pl.pallas_call(
        body, name=name, grid=(t // tr,),
        in_specs=in_specs, out_specs=out_specs, out_shape=out_shape,
        compiler_params=_cparams(1),
    )(*ins)


_LEVELS = (32, 16, 8, 4, 2, 1)
_N_CUM = 3 + 2 * len(_LEVELS)


def _hgrn_constants():
    c = GLA_CHUNK
    idx = np.arange(c)
    t, r = idx[:, None], idx[None, :]
    mats = [(r <= t), (r > t), np.ones((c, c), bool)]
    lq, lk, masks = [], [], []
    for h in _LEVELS:
        blk, pos = idx // (2 * h), idx % (2 * h)
        mid = blk * 2 * h + h - 1
        upper, lower = pos >= h, pos < h
        lq.append(upper[:, None] & (r > mid[:, None]) & (r <= t))
        lk.append(lower[:, None] & (r > t) & (r <= mid[:, None]))
        masks.append((blk[:, None] == blk[None, :]) & upper[:, None] & lower[None, :])
    cum = np.concatenate(mats + lq + lk, axis=0).astype(np.float32)
    rev = (r >= t).astype(np.float32)
    return (jnp.asarray(cum, BF16), jnp.asarray(rev, BF16), jnp.asarray(np.stack(masks).astype(np.float32)))


def _hgrn_gates(qp, fp, lb):
    sq = _sigmoid(qp)
    q = qp * sq
    sg = _sigmoid(fp)
    f = lb + (1.0 - lb) * sg
    k = 1.0 - f
    logf = jnp.log(jnp.maximum(f, 1e-30))
    return q, sq, sg, f, k, logf


def _hgrn_decays(cum_ref, logf):
    c = GLA_CHUNK
    e = jnp.exp(_dot_exact_l(cum_ref[...], logf))
    part = lambda i: e[i * c:(i + 1) * c]
    n = len(_LEVELS)
    return part(0), part(1), part(2), [part(3 + i) for i in range(n)], [part(3 + n + i) for i in range(n)]


def _hgrn_fwd(proj, lb, nw, n_heads, comms=()):
    t = proj.shape[0]
    aw = n_heads * HEAD
    rb = _tile(t, HGRN_ROWS)
    c = GLA_CHUNK
    n_sub = rb // c
    cum, _, masks = _hgrn_constants()
    lay = _CommLayout(comms)
    hp = HGRN_HEADS if n_heads % HGRN_HEADS == 0 else 1
    wd = hp * HEAD
    grid = (n_heads // hp, t // rb)

    def body(*refs):
        q_ref, f_ref, i_ref, g_ref, lb_ref, nw_ref, cum_ref, m_ref = refs[:8]
        cin = refs[8:8 + lay.n_in]
        a_ref, o_ref, s_ref, sc_ref = refs[8 + lay.n_in:12 + lay.n_in]
        cout = refs[12 + lay.n_in:12 + lay.n_in + lay.n_out]
        st = refs[12 + lay.n_in + lay.n_out]
        csem = refs[13 + lay.n_in + lay.n_out:]
        step = _linear_step(grid)
        lay.run(cin, cout, csem, step, grid[0] * grid[1], post=False)

        @pl.when(pl.program_id(1) == 0)
        def _():
            st[...] = jnp.zeros_like(st)

        lbv = lb_ref[...]
        nwv = nw_ref[...]
        eye = (lax.broadcasted_iota(jnp.int32, (c, c), 0) == lax.broadcasted_iota(jnp.int32, (c, c), 1)).astype(F32)
        heads = range(hp)
        hs = lambda a, h: a[:, h * HEAD:(h + 1) * HEAD]

        def chunk(j, carry):
            rows = pl.ds(pl.multiple_of(j * c, c), c)
            q, _, _, _, k, logf = _hgrn_gates(q_ref[rows, :], f_ref[rows, :], lbv)
            v = i_ref[rows, :]
            eb, ebe, eend, eq, ek = _hgrn_decays(cum_ref, logf)
            qt, kt, qk = q * eb, k * ebe, q * k
            s_in = [st[h] for h in heads]
            for h in heads:
                s_ref[h, j] = s_in[h]
            inter = [_dot(hs(qt, h), s_in[h], "nt") for h in heads]
            for h in heads:
                st[h] = s_in[h] * hs(eend, h)[0:1] + _dot(hs(v, h), hs(kt, h), "tn")
            scores = [eye * jnp.sum(hs(qk, h), axis=-1, keepdims=True) for h in heads]
            for lvl in range(len(_LEVELS)):
                ql, kl = q * eq[lvl], k * ek[lvl]
                for h in heads:
                    scores[h] = scores[h] + m_ref[lvl] * _dot(hs(ql, h), hs(kl, h), "nt")
            gv = g_ref[rows, :]
            gate = nwv * (gv * _sigmoid(gv))
            for h in heads:
                sc_ref[h, rows, :] = scores[h]
                o = inter[h] + _dot(scores[h], hs(v, h))
                o_ref[rows, h * HEAD:(h + 1) * HEAD] = o
                r = lax.rsqrt(jnp.mean(o * o, axis=-1, keepdims=True) + EPS)
                a_ref[rows, h * HEAD:(h + 1) * HEAD] = (o * r * hs(gate, h)).astype(BF16)
            return carry

        lax.fori_loop(0, n_sub, chunk, 0, unroll=HGRN_UNROLL)
        lay.run(cin, cout, csem, step, grid[0] * grid[1], post=True)

    n_hb = n_heads // hp
    col = lambda base: pl.BlockSpec((rb, wd), lambda h, r: (r, base * n_hb + h))
    vec = pl.BlockSpec((1, wd), lambda h, r: (0, h))
    res = pl.pallas_call(
        body, name="hgrn2_fwd", grid=grid,
        in_specs=[col(0), col(1), col(2), col(3), vec, vec,
                  pl.BlockSpec(cum.shape, lambda h, r: (0, 0)), pl.BlockSpec(masks.shape, lambda h, r: (0, 0, 0))] + lay.in_specs,
        out_specs=[pl.BlockSpec((rb, wd), lambda h, r: (r, h)), pl.BlockSpec((rb, wd), lambda h, r: (r, h)),
                   pl.BlockSpec((hp, n_sub, HEAD, HEAD), lambda h, r: (h, r, 0, 0)),
                   pl.BlockSpec((hp, rb, c), lambda h, r: (h, r, 0))] + lay.out_specs,
        out_shape=[jax.ShapeDtypeStruct((t, aw), BF16), jax.ShapeDtypeStruct((t, aw), F32),
                   jax.ShapeDtypeStruct((n_heads, t // c, HEAD, HEAD), F32),
                   jax.ShapeDtypeStruct((n_heads, t, c), F32)] + lay.out_shapes,
        scratch_shapes=[pltpu.VMEM((hp, HEAD, HEAD), F32)] + lay.sem_shapes,
        compiler_params=_cparams(2),
    )(proj, proj, proj, proj, lb, nw, cum, masks, *lay.arrays)
    lay.deliver(res[4:])
    return res[:4]


def _hgrn_bwd(proj, lb, nw, o_raw, states, scores, dab, n_heads, comms=()):
    t = proj.shape[0]
    aw = n_heads * HEAD
    rb = _tile(t, HGRN_ROWS)
    c = GLA_CHUNK
    n_sub = rb // c
    n_rb = t // rb
    cum, rev, masks = _hgrn_constants()
    lay = _CommLayout(comms)
    hp = HGRN_HEADS if n_heads % HGRN_HEADS == 0 else 1
    wd = hp * HEAD
    grid = (n_heads // hp, n_rb)

    def body(*refs):
        q_ref, f_ref, i_ref, g_ref, lb_ref, nw_ref, o_ref, s_ref, sc_ref, da_ref, cum_ref, rev_ref, m_ref = refs[:13]
        cin = refs[13:13 + lay.n_in]
        dq_ref, df_ref, di_ref, dg_ref, dlb_ref, dnw_ref = refs[13 + lay.n_in:19 + lay.n_in]
        cout = refs[19 + lay.n_in:19 + lay.n_in + lay.n_out]
        dst = refs[19 + lay.n_in + lay.n_out]
        csem = refs[20 + lay.n_in + lay.n_out:]
        step = _linear_step(grid)
        lay.run(cin, cout, csem, step, grid[0] * grid[1], post=False)

        @pl.when(pl.program_id(1) == 0)
        def _():
            dst[...] = jnp.zeros_like(dst)
            dlb_ref[...] = jnp.zeros_like(dlb_ref)
            dnw_ref[...] = jnp.zeros_like(dnw_ref)

        lbv = lb_ref[...]
        nwv = nw_ref[...]
        ri = lax.broadcasted_iota(jnp.int32, (c, c), 0)
        ci = lax.broadcasted_iota(jnp.int32, (c, c), 1)
        eye = (ri == ci).astype(F32)
        causal = (ci <= ri).astype(F32)
        last_row = (lax.broadcasted_iota(jnp.int32, (c, wd), 0) == c - 1).astype(F32)
        heads = range(hp)
        hs = lambda a, h: a[:, h * HEAD:(h + 1) * HEAD]
        wide = lambda parts: parts[0] if hp == 1 else jnp.concatenate(parts, axis=1)

        def head_mean(a):
            return wide([jnp.broadcast_to(jnp.mean(hs(a, h), axis=-1, keepdims=True), (c, HEAD)) for h in heads])

        def chunk(jj, carry):
            j = n_sub - 1 - jj
            rows = pl.ds(pl.multiple_of(j * c, c), c)
            qp = q_ref[rows, :]
            q, sq, sg, f, k, logf = _hgrn_gates(qp, f_ref[rows, :], lbv)
            v = i_ref[rows, :]
            gv = g_ref[rows, :]
            eb, ebe, eend, eq, ek = _hgrn_decays(cum_ref, logf)
            s_in = [s_ref[h, j] for h in heads]
            a_sc = [sc_ref[h, rows, :] for h in heads]
            dsn = [dst[h] for h in heads]
            o = o_ref[rows, :]
            r = lax.rsqrt(head_mean(o * o) + EPS)
            oh = o * r
            sgg = _sigmoid(gv)
            sil = gv * sgg
            da = da_ref[rows, :]
            dg_ref[rows, :] = (da * oh * nwv * (sgg * (1.0 + gv * (1.0 - sgg)))).astype(BF16)
            dnw_ref[...] += jnp.sum(da * oh * sil, axis=0, keepdims=True)
            doh = da * nwv * sil
            do = r * (doh - oh * head_mean(doh * oh))
            kt = k * ebe
            qt = q * eb
            d_sc = [_dot(hs(do, h), hs(v, h), "nt") * causal for h in heads]
            dqt = wide([_dot(hs(do, h), s_in[h]) for h in heads])
            dkt = wide([_dot(hs(v, h), dsn[h]) for h in heads])
            for h in heads:
                dst[h] = dsn[h] * hs(eend, h)[0:1] + _dot(hs(do, h), hs(qt, h), "tn")
            di_ref[rows, :] = wide([_dot(a_sc[h], hs(do, h), "tn") + _dot(hs(kt, h), dsn[h], "nt") for h in heads]).astype(BF16)
            diag = wide([jnp.broadcast_to(jnp.sum(d_sc[h] * eye, axis=-1, keepdims=True), (c, HEAD)) for h in heads])
            dq = dqt * eb
            dk = dkt * ebe
            db = q * dq - k * dk
            dq = dq + diag * k
            dk = dk + diag * q
            for lvl in range(len(_LEVELS)):
                ql = (q * eq[lvl]).astype(BF16)
                kl = (k * ek[lvl]).astype(BF16)
                dm = [(m_ref[lvl] * d_sc[h]).astype(BF16) for h in heads]
                gq = wide([_dot(dm[h], hs(kl, h)) for h in heads])
                gk = wide([_dot(dm[h], hs(ql, h), "tn") for h in heads])
                dq = dq + gq * eq[lvl]
                dk = dk + gk * ek[lvl]
                db = db + ql.astype(F32) * gq - kl.astype(F32) * gk
            state_term = wide([jnp.sum(s_in[h] * dsn[h], axis=0, keepdims=True) for h in heads])
            extra = jnp.sum(dkt * kt, axis=0, keepdims=True) + eend[0:1] * state_term
            db = db + last_row * extra
            dlogf = _dot_exact_l(rev_ref[...], db)
            dfv = jnp.where(f > 1e-30, dlogf / f, 0.0) - dk
            df_ref[rows, :] = (dfv * (1.0 - lbv) * sg * (1.0 - sg)).astype(BF16)
            dlb_ref[...] += jnp.sum(dfv * (1.0 - sg), axis=0, keepdims=True)
            dq_ref[rows, :] = (dq * (sq * (1.0 + qp * (1.0 - sq)))).astype(BF16)
            return carry

        lax.fori_loop(0, n_sub, chunk, 0, unroll=HGRN_UNROLL)
        lay.run(cin, cout, csem, step, grid[0] * grid[1], post=True)

    n_hb = n_heads // hp
    col = lambda base: pl.BlockSpec((rb, wd), lambda h, r: (n_rb - 1 - r, base * n_hb + h))
    blk = pl.BlockSpec((rb, wd), lambda h, r: (n_rb - 1 - r, h))
    vec = pl.BlockSpec((1, wd), lambda h, r: (0, h))
    const = lambda a: pl.BlockSpec(a.shape, lambda h, r: (0,) * a.ndim)
    res = pl.pallas_call(
        body, name="hgrn2_bwd", grid=grid,
        in_specs=[col(0), col(1), col(2), col(3), vec, vec, blk,
                  pl.BlockSpec((hp, n_sub, HEAD, HEAD), lambda h, r: (h, n_rb - 1 - r, 0, 0)),
                  pl.BlockSpec((hp, rb, c), lambda h, r: (h, n_rb - 1 - r, 0)),
                  blk, const(cum), const(rev), const(masks)] + lay.in_specs,
        out_specs=[blk, blk, blk, blk, vec, vec] + lay.out_specs,
        out_shape=[jax.ShapeDtypeStruct((t, aw), BF16)] * 4 + [jax.ShapeDtypeStruct((1, aw), F32)] * 2 + lay.out_shapes,
        scratch_shapes=[pltpu.VMEM((hp, HEAD, HEAD), F32)] + lay.sem_shapes,
        compiler_params=_cparams(2),
    )(proj, proj, proj, proj, lb, nw, o_raw, states, scores, dab, cum, rev, masks, *lay.arrays)
    lay.deliver(res[6:])
    return res[:6]


def _lb_fwd(lb_param):
    def body(p_ref, o_ref):
        p = p_ref[...]
        e = jnp.exp(p - jnp.max(p, axis=0, keepdims=True))
        o_ref[...] = e[0:1] / jnp.sum(e, axis=0, keepdims=True)

    return pl.pallas_call(body, name="lb_fwd", out_shape=jax.ShapeDtypeStruct((1, lb_param.shape[1]), F32))(lb_param)


def _lb_bwd(lb_param, dlb):
    def body(p_ref, d_ref, o_ref):
        p = p_ref[...]
        e = jnp.exp(p - jnp.max(p, axis=0, keepdims=True))
        s = e / jnp.sum(e, axis=0, keepdims=True)
        first = (lax.broadcasted_iota(jnp.int32, p.shape, 0) == 0).astype(F32)
        o_ref[...] = d_ref[...] * s[0:1] * (first - s)

    return pl.pallas_call(body, name="lb_bwd", out_shape=jax.ShapeDtypeStruct(lb_param.shape, F32))(lb_param, dlb)


def _gmlp_norm(v, lnw, lnb):
    vf = _gelu(v)
    mu = jnp.mean(vf, axis=-1, keepdims=True)
    cen = vf - mu
    rstd = lax.rsqrt(jnp.mean(cen * cen, axis=-1, keepdims=True) + EPS)
    xh = cen * rstd
    return xh, rstd, xh * lnw + lnb


def _tril(n):
    return (lax.broadcasted_iota(jnp.int32, (n, n), 1) <= lax.broadcasted_iota(jnp.int32, (n, n), 0)).astype(F32)


def _gmlp_fwd(proj, lnw, lnb, w_sp, bs_t, n_groups, col_base):
    t = proj.shape[0]
    bw = n_groups * HEAD
    c = GMLP_CHUNK

    def body(u_ref, v_ref, lnw_ref, lnb_ref, w_ref, bs_ref, o_ref):
        tri = _tril(c)
        uf = _gelu(u_ref[...])
        _, _, vn = _gmlp_norm(v_ref[...], lnw_ref[...], lnb_ref[...])
        for g in range(n_groups):
            cols = slice(g * HEAD, (g + 1) * HEAD)
            z = _dot(w_ref[g] * tri, vn[:, cols]) + bs_ref[:, g:g + 1]
            o_ref[:, cols] = (uf[:, cols] * z).astype(BF16)

    blk = lambda b: pl.BlockSpec((c, bw), lambda n: (n, b))
    const = lambda a: pl.BlockSpec(a.shape, lambda n: (0,) * a.ndim)
    return pl.pallas_call(
        body, name="gmlp_fwd", grid=(t // c,),
        in_specs=[blk(col_base), blk(col_base + 1), const(lnw), const(lnb), const(w_sp), const(bs_t)],
        out_specs=pl.BlockSpec((c, bw), lambda n: (n, 0)),
        out_shape=jax.ShapeDtypeStruct((t, bw), BF16),
        compiler_params=_cparams(1),
    )(proj, proj, lnw, lnb, w_sp, bs_t)


def _gmlp_bwd(proj, lnw, lnb, w_sp, bs_t, dab, n_groups, col_base):
    t = proj.shape[0]
    bw = n_groups * HEAD
    c = GMLP_CHUNK
    n_steps = t // c
    sel = jnp.asarray((np.arange(bw)[:, None] // HEAD == np.arange(n_groups)[None, :]).astype(np.float32), BF16)

    def body(u_ref, v_ref, lnw_ref, lnb_ref, w_ref, bs_ref, d_ref, sel_ref,
             du_ref, dv_ref, dlnw_ref, dlnb_ref, dw_ref, dbs_ref, dz_acc, dvn_scr):
        step = pl.program_id(0)

        @pl.when(step == 0)
        def _():
            dlnw_ref[...] = jnp.zeros_like(dlnw_ref)
            dlnb_ref[...] = jnp.zeros_like(dlnb_ref)
            dw_ref[...] = jnp.zeros_like(dw_ref)
            dz_acc[...] = jnp.zeros_like(dz_acc)

        tri = _tril(c)
        u = u_ref[...]
        v = v_ref[...]
        uf = _gelu(u)
        lnw_v = lnw_ref[...]
        xh, rstd, vn = _gmlp_norm(v, lnw_v, lnb_ref[...])
        dbo = d_ref[...]
        dz = dbo * uf
        dz_acc[...] += dz
        for g in range(n_groups):
            cols = slice(g * HEAD, (g + 1) * HEAD)
            wg = w_ref[g] * tri
            z = _dot(wg, vn[:, cols]) + bs_ref[:, g:g + 1]
            du_ref[:, cols] = (dbo[:, cols] * z * _gelu_grad(u[:, cols])).astype(BF16)
            dvn_scr[:, cols] = _dot(wg, dz[:, cols], "tn")
            dw_ref[g] += tri * _dot(dz[:, cols], vn[:, cols], "nt")
        dvn = dvn_scr[...]
        dlnw_ref[...] += jnp.sum(dvn * xh, axis=0, keepdims=True)
        dlnb_ref[...] += jnp.sum(dvn, axis=0, keepdims=True)
        dxh = dvn * lnw_v
        dvf = rstd * (dxh - jnp.mean(dxh, axis=-1, keepdims=True) - xh * jnp.mean(dxh * xh, axis=-1, keepdims=True))
        dv_ref[...] = (dvf * _gelu_grad(v)).astype(BF16)

        @pl.when(step == n_steps - 1)
        def _():
            dbs_ref[...] = _dot_exact_r(dz_acc[...], sel_ref[...])

    blk = lambda b: pl.BlockSpec((c, bw), lambda n: (n, b))
    const = lambda a: pl.BlockSpec(a.shape, lambda n: (0,) * a.ndim)
    row = pl.BlockSpec((c, bw), lambda n: (n, 0))
    vec = pl.BlockSpec((1, bw), lambda n: (0, 0))
    return pl.pallas_call(
        body, name="gmlp_bwd", grid=(n_steps,),
        in_specs=[blk(col_base), blk(col_base + 1), const(lnw), const(lnb), const(w_sp), const(bs_t), blk(1), const(sel)],
        out_specs=[row, row, vec, vec, const(w_sp), const(bs_t)],
        out_shape=[jax.ShapeDtypeStruct((t, bw), BF16), jax.ShapeDtypeStruct((t, bw), BF16),
                   jax.ShapeDtypeStruct((1, bw), F32), jax.ShapeDtypeStruct((1, bw), F32),
                   jax.ShapeDtypeStruct(w_sp.shape, F32), jax.ShapeDtypeStruct(bs_t.shape, F32)],
        scratch_shapes=[pltpu.VMEM((c, bw), F32), pltpu.VMEM((c, bw), F32)],
        compiler_params=_cparams(1),
    )(proj, proj, lnw, lnb, w_sp, bs_t, dab, sel)


def _pair_sum(name, grad, other, core):
    _, _, r, c = grad.shape
    tr = _rows(r, 6 * c)

    def body(core_ref, g_ref, o_ref, out_ref):
        out_ref[...] = (g_ref[...].astype(F32) + o_ref[...].astype(F32)).astype(BF16)

    return pl.pallas_call(
        body, name=name,
        grid_spec=pltpu.PrefetchScalarGridSpec(
            num_scalar_prefetch=1, grid=(N_CHIP, r // tr),
            in_specs=[pl.BlockSpec((None, None, tr, c), lambda k, i, core_ref: (k, core_ref[0], i, 0)),
                      pl.BlockSpec((None, tr, c), lambda k, i, core_ref: (k, i, 0))],
            out_specs=pl.BlockSpec((None, tr, c), lambda k, i, core_ref: (k, i, 0))),
        out_shape=jax.ShapeDtypeStruct((N_CHIP, r, c), BF16),
        compiler_params=_cparams(2),
    )(core, grad, other)


def _adamw_math(w, g, m, v):
    m = ADAM_B1 * m + (1.0 - ADAM_B1) * g
    v = ADAM_B2 * v + (1.0 - ADAM_B2) * (g * g)
    m_hat = m / (1.0 - ADAM_B1 ** ADAM_STEP)
    v_hat = v / (1.0 - ADAM_B2 ** ADAM_STEP)
    delta = -ADAM_LR * (m_hat / (jnp.sqrt(v_hat) + ADAM_EPS) + ADAM_WD * w)
    return delta, m, v


def _adamw(name, parts, w, m, v):
    n_parts, r, c = parts.shape
    tr = _rows(r, c * (n_parts * parts.dtype.itemsize + 28), mult=8)

    def body(p_ref, w_ref, m_ref, v_ref, g_ref, d_ref, mo_ref, vo_ref):
        g = p_ref[0].astype(F32)
        for i in range(1, n_parts):
            g = g + p_ref[i].astype(F32)
        g_ref[...] = g
        d_ref[...], mo_ref[...], vo_ref[...] = _adamw_math(w_ref[...], g, m_ref[...], v_ref[...])

    row = pl.BlockSpec((tr, c), lambda i: (i, 0))
    return pl.pallas_call(
        body, name=name, grid=(r // tr,),
        in_specs=[pl.BlockSpec((n_parts, tr, c), lambda i: (0, i, 0)), row, row, row],
        out_specs=[row] * 4,
        out_shape=[jax.ShapeDtypeStruct((r, c), F32)] * 4,
        compiler_params=_cparams(1),
    )(parts, w, m, v)


def kernel(x, p, pre_mix_w, w_in, lb_param, a_norm_w, gmlp_ln_w, gmlp_ln_b, w_spatial, b_spatial, w_out, post_mix_w, pre_ffn_w, w_gate, w_up, w_down, post_ffn_w, w_ple, w_ple_gate, post_ple_w, loss_target, m_pre_mix_w, m_w_in, m_lb_param, m_a_norm_w, m_gmlp_ln_w, m_gmlp_ln_b, m_w_spatial, m_b_spatial, m_w_out, m_post_mix_w, m_pre_ffn_w, m_w_gate, m_w_up, m_w_down, m_post_ffn_w, m_w_ple, m_w_ple_gate, m_post_ple_w, v_pre_mix_w, v_w_in, v_lb_param, v_a_norm_w, v_gmlp_ln_w, v_gmlp_ln_b, v_w_spatial, v_b_spatial, v_w_out, v_post_mix_w, v_pre_ffn_w, v_w_gate, v_w_up, v_w_down, v_post_ffn_w, v_w_ple, v_w_ple_gate, v_post_ple_w):
    big_names = ["w_in", "w_out", "w_gate", "w_up", "w_down", "w_ple", "w_ple_gate"]
    small_names = ["pre_mix_w", "lb_param", "a_norm_w", "gmlp_ln_w", "gmlp_ln_b", "w_spatial", "b_spatial",
                   "post_mix_w", "pre_ffn_w", "post_ffn_w", "post_ple_w"]
    all_names = ["pre_mix_w", "w_in", "lb_param", "a_norm_w", "gmlp_ln_w", "gmlp_ln_b", "w_spatial", "b_spatial",
                 "w_out", "post_mix_w", "pre_ffn_w", "w_gate", "w_up", "w_down", "post_ffn_w", "w_ple", "w_ple_gate",
                 "post_ple_w"]
    env = dict(locals())
    W = {n: env[n] for n in all_names}
    M = {n: env["m_" + n] for n in all_names}
    V = {n: env["v_" + n] for n in all_names}

    xs = x[0]
    ps = p[0, 0]
    tgt = loss_target[0]
    t, d = xs.shape
    aw = a_norm_w.shape[1]
    bw = gmlp_ln_w.shape[1]
    n_heads, n_groups = aw // HEAD, bw // HEAD
    core = lax.axis_index("c").astype(jnp.int32).reshape(1)

    transposed = ("w_gate", "w_up")
    local = lambda a, n: jnp.swapaxes(a, 1, 2)[0] if n in transposed else a[0]
    unlocal = lambda a, n: jnp.swapaxes(a[None], 1, 2) if n in transposed else a[None]
    shard = {n: local(W[n], n) for n in big_names}
    bf = {n: _cast_bf16("cast_" + n, shard[n]) for n in big_names}
    ag_in = _ag_comm([bf["w_in"]])
    _comm_only("ag_w_in", [ag_in])
    win_g = ag_in.results[0]
    n_in = bf["w_in"].shape[1]
    ffl = bf["w_gate"].shape[0]
    n_ple = bf["w_ple"].shape[1]
    ple = ps.shape[1]

    TM, TK = 1024, 1024
    tm = _tile(t, TM)
    tn1 = _tile(d, 1024)

    h1 = _rms_fwd("rms_pre_mix", xs, pre_mix_w)
    once = pl.Buffered(1)
    ag_a = _ag_comm([bf["w_gate"]], mid_frac=0.85)
    proj = _matmul(
        "mm_proj",
        [(h1, pl.BlockSpec((tm, d), lambda n, m, k: (m, 0))),
         (win_g, pl.BlockSpec((None, d, n_in), lambda n, m, k: (n, 0, 0), pipeline_mode=once))],
        [(0, 1, "nn", 0)],
        [(jax.ShapeDtypeStruct((t, N_DEV * n_in), F32), pl.BlockSpec((tm, n_in), lambda n, m, k: (m, n)))],
        (N_DEV, t // tm, 1), (tm, n_in), comms=[ag_a])[0]
    wgate_g = ag_a.results[0]
    lb = _lb_fwd(lb_param)
    ag_b = _ag_comm([bf["w_out"]], mid_frac=0.6)
    a_out, o_raw, states, scores = _hgrn_fwd(proj, lb, a_norm_w, n_heads, comms=[ag_b])
    wout_f = ag_b.results[0].reshape(d, d)
    bs_t = b_spatial[0].T
    w_sp = w_spatial[0]
    col_u = (4 * aw) // bw
    b_out = _gmlp_fwd(proj, gmlp_ln_w, gmlp_ln_b, w_sp, bs_t, n_groups, col_u)
    ab = jnp.concatenate([a_out, b_out], axis=1)
    mix = _mm_plain("mm_mix", ab, wout_f, "nn", F32, TM, 1024, d)
    x1, h2 = _resid_rms("resid_mix", xs, mix, post_mix_w, pre_ffn_w)

    def swiglu(accs, gate_v):
        gf = gate_v.astype(F32)
        return accs[0], gf * _sigmoid(gf) * accs[0]

    tmf = _tile(t, 512)
    blk3 = lambda: pl.BlockSpec((None, tmf, ffl), lambda j, m, k: (j, m, 0))
    ag_c = _ag_comm([bf["w_up"]], mid_frac=0.85)
    gate = _matmul(
        "mm_ffn_gate",
        [(h2, pl.BlockSpec((tm, d), lambda j, m, k: (m, 0))),
         (wgate_g, pl.BlockSpec((None, ffl, d), lambda j, m, k: (j, 0, 0), pipeline_mode=once))],
        [(0, 1, "nt", 0)],
        [(jax.ShapeDtypeStruct((N_DEV, t, ffl), BF16), pl.BlockSpec((None, tm, ffl), lambda j, m, k: (j, m, 0)))],
        (N_DEV, t // tm, 1), (tm, ffl), comms=[ag_c])[0]
    wup_g = ag_c.results[0]
    ag_d = _ag_comm([bf["w_down"]], mid_frac=0.85)
    up, act = _matmul(
        "mm_ffn_up",
        [(h2, pl.BlockSpec((tmf, d), lambda j, m, k: (m, 0))),
         (wup_g, pl.BlockSpec((None, ffl, d), lambda j, m, k: (j, 0, 0), pipeline_mode=once)),
         (gate, blk3())],
        [(0, 1, "nt", 0)],
        [(jax.ShapeDtypeStruct((N_DEV, t, ffl), BF16), blk3()) for _ in range(2)],
        (N_DEV, t // tmf, 1), (tmf, ffl), epilogue=swiglu, comms=[ag_d])
    wdown_g = ag_d.results[0]
    ag_e = _ag_comm([bf["w_ple_gate"], bf["w_ple"]], mid_frac=0.6)
    tn_d = _tile(d, 2048)
    ff = _matmul(
        "mm_ffn_down",
        [(act, pl.BlockSpec((None, tm, ffl), lambda m, n, k: (k, m, 0))),
         (wdown_g, pl.BlockSpec((None, ffl, tn_d), lambda m, n, k: (k, 0, n)))],
        [(0, 1, "nn", 0)],
        [(jax.ShapeDtypeStruct((t, d), F32), pl.BlockSpec((tm, tn_d), lambda m, n, k: (m, n)))],
        (t // tm, d // tn_d, N_DEV), (tm, tn_d), comms=[ag_e])[0]
    wpg_f = ag_e.results[0].reshape(d, d)
    wple_g = ag_e.results[1]
    x2, x2b = _resid_rms("resid_ffn", x1, ff, post_ffn_w, None)

    pgl = _mm_plain("mm_ple_gate", x2b, wpg_f, "nn", F32, TM, 1024, d)
    pe = _matmul(
        "mm_ple",
        [(ps, pl.BlockSpec((tm, ple), lambda m, n, k: (m, 0))), (wple_g, pl.BlockSpec((None, ple, n_ple), lambda m, n, k: (n, 0, 0)))],
        [(0, 1, "nn", 0)],
        [(jax.ShapeDtypeStruct((t, N_DEV * n_ple), F32), pl.BlockSpec((tm, n_ple), lambda m, n, k: (m, n)))],
        (t // tm, N_DEV, 1), (tm, n_ple))[0]
    loss_part, d3, dpe, dpgl, g_post_ple = _ple_loss("ple_loss", x2, pe, pgl, post_ple_w, tgt)

    tkt = _tile(t, TK)
    g_wple = _matmul(
        "mm_dw_ple",
        [(ps, pl.BlockSpec((tkt, ple), lambda n, k: (k, 0))), (dpe, pl.BlockSpec((tkt, n_ple), lambda n, k: (k, n)))],
        [(0, 1, "tn", 0)],
        [(jax.ShapeDtypeStruct((N_DEV, ple, n_ple), BF16), pl.BlockSpec((None, ple, n_ple), lambda n, k: (n, 0, 0)))],
        (N_DEV, t // tkt), (ple, n_ple))[0]
    g_wpg = _mm_plain("mm_dw_ple_gate", x2b, dpgl, "tn", BF16, TM, 1024, t)

    def by_chip(g):
        return g.reshape((N_CHIP, 2) + g.shape[-2:])

    def pair_sums(names, comm):
        return [_pair_sum("pair_sum_" + n, g, o, core) for n, g, o in zip(names, comm.arrays, comm.results)]

    r1_p = _pair_comm([by_chip(g_wpg.reshape(N_DEV, d // N_DEV, d)), by_chip(g_wple)])
    d2 = _mm_plain("mm_d_x2", dpgl, wpg_f, "nt", F32, TM, 512, d, extra=d3, epilogue=lambda accs, e: [accs[0] + e],
                   comms=[r1_p])
    r2_p = _chip_comm(pair_sums(["w_ple_gate", "w_ple"], r1_p))

    dff, g_post_ffn = _norm_bwd("norm_bwd_ffn", d2, ff, post_ffn_w)
    g_wdown = _matmul(
        "mm_dw_down",
        [(act, pl.BlockSpec((None, t, ffl), lambda j, n, k: (j, 0, 0), pipeline_mode=once)),
         (dff, pl.BlockSpec((t, tn1), lambda j, n, k: (0, n)))],
        [(0, 1, "tn", 0)],
        [(jax.ShapeDtypeStruct((N_DEV, ffl, d), BF16), pl.BlockSpec((None, ffl, tn1), lambda j, n, k: (j, 0, n)))],
        (N_DEV, d // tn1, 1), (ffl, tn1), comms=[r2_p])[0]
    r1_d = _pair_comm([by_chip(g_wdown)])

    def swiglu_bwd(accs, gate_v, up_v):
        dact = accs[0]
        gf = gate_v.astype(F32)
        sg = _sigmoid(gf)
        return dact * up_v.astype(F32) * (sg * (1.0 + gf * (1.0 - sg))), dact * (gf * sg)

    dgate, dup = _matmul(
        "mm_d_act",
        [(dff, pl.BlockSpec((tmf, d), lambda j, m, k: (m, 0))),
         (wdown_g, pl.BlockSpec((None, ffl, d), lambda j, m, k: (j, 0, 0), pipeline_mode=once)),
         (gate, blk3()), (up, blk3())],
        [(0, 1, "nt", 0)],
        [(jax.ShapeDtypeStruct((N_DEV, t, ffl), BF16), blk3()) for _ in range(2)],
        (N_DEV, t // tmf, 1), (tmf, ffl), epilogue=swiglu_bwd, comms=[r1_d])
    r2_d = _chip_comm(pair_sums(["w_down"], r1_d))
    tmd = _tile(d, TM)
    def dw_ffn(name, dy, comms):
        return _matmul(
            name,
            [(dy, pl.BlockSpec((None, t, ffl), lambda j, n, k: (j, 0, 0), pipeline_mode=once)),
             (h2, pl.BlockSpec((t, tn1), lambda j, n, k: (0, n)))],
            [(0, 1, "tn", 0)],
            [(jax.ShapeDtypeStruct((N_DEV, ffl, d), BF16), pl.BlockSpec((None, ffl, tn1), lambda j, n, k: (j, 0, n)))],
            (N_DEV, d // tn1, 1), (ffl, tn1), comms=comms)[0]

    g_wgate = dw_ffn("mm_dw_gate", dgate, [r2_d])
    r1_g = _pair_comm([by_chip(g_wgate)])
    g_wup = dw_ffn("mm_dw_up", dup, [r1_g])
    r2_g = _chip_comm(pair_sums(["w_gate"], r1_g))
    r1_u = _pair_comm([by_chip(g_wup)])
    tn1 = _tile(d, 1024)
    dh2 = _matmul(
        "mm_d_h2",
        [(dgate, pl.BlockSpec((None, tm, ffl), lambda m, n, k: (k, m, 0))),
         (wgate_g, pl.BlockSpec((None, ffl, tn1), lambda m, n, k: (k, 0, n))),
         (dup, pl.BlockSpec((None, tm, ffl), lambda m, n, k: (k, m, 0))),
         (wup_g, pl.BlockSpec((None, ffl, tn1), lambda m, n, k: (k, 0, n)))],
        [(0, 1, "nn", 0), (2, 3, "nn", 0)],
        [(jax.ShapeDtypeStruct((t, d), F32), pl.BlockSpec((tm, tn1), lambda m, n, k: (m, n)))],
        (t // tm, d // tn1, N_DEV), (tm, tn1), comms=[r2_g, r1_u])[0]
    r2_u = _chip_comm(pair_sums(["w_up"], r1_u))
    d1, g_pre_ffn, dmix, g_post_mix = _prenorm_bwd("prenorm_bwd_ffn", d2, dh2, x1, pre_ffn_w, mix, post_mix_w)

    g_wout = _mm_plain("mm_dw_out", ab, dmix, "tn", BF16, TM, 1024, t)
    r1_o = _pair_comm([by_chip(g_wout.reshape(N_DEV, d // N_DEV, d))])
    dab = _mm_plain("mm_d_ab", dmix, wout_f, "nt", F32, TM, 1024, d, comms=[r1_o])
    r2_o = _chip_comm(pair_sums(["w_out"], r1_o))
    dq, df, di, dg, dlb, g_a_norm = _hgrn_bwd(proj, lb, a_norm_w, o_raw, states, scores, dab, n_heads, comms=[r2_u])
    du, dv, g_ln_w, g_ln_b, g_wsp, g_bs_t = _gmlp_bwd(proj, gmlp_ln_w, gmlp_ln_b, w_sp, bs_t, dab, n_groups, col_u)
    dproj = jnp.concatenate([dq, df, di, dg, du, dv], axis=1)
    g_win = _matmul(
        "mm_dw_in",
        [(h1, pl.BlockSpec((t, tmd), lambda j, m, k: (0, m))),
         (dproj, pl.BlockSpec((t, n_in), lambda j, m, k: (0, j), pipeline_mode=once))],
        [(0, 1, "tn", 0)],
        [(jax.ShapeDtypeStruct((N_DEV, d, n_in), BF16), pl.BlockSpec((None, tmd, n_in), lambda j, m, k: (j, m, 0)))],
        (N_DEV, d // tmd, 1), (tmd, n_in), comms=[r2_o])[0]
    r1_in = _pair_comm([by_chip(g_win)])
    _comm_only("rs_pair_w_in", [r1_in])
    r2_in = _chip_comm(pair_sums(["w_in"], r1_in))
    small_grad = {
        "lb_param": _lb_bwd(lb_param, dlb), "a_norm_w": g_a_norm, "gmlp_ln_w": g_ln_w,
        "gmlp_ln_b": g_ln_b, "w_spatial": g_wsp, "b_spatial": g_bs_t.T, "post_mix_w": g_post_mix,
        "pre_ffn_w": g_pre_ffn, "post_ffn_w": g_post_ffn, "post_ple_w": g_post_ple,
    }
    assert small_names[0] == "pre_mix_w"
    pack = lambda get, names: jnp.concatenate([get(n).reshape(-1, LANE) for n in names], axis=0)
    ag_main = _ag_comm([pack(lambda n: small_grad[n], small_names[1:])], mid_frac=0.9)
    dh1 = _matmul(
        "mm_d_h1",
        [(dproj, pl.BlockSpec((tm, n_in), lambda m, n, k: (m, k))), (win_g, pl.BlockSpec((None, tn1, n_in), lambda m, n, k: (k, n, 0)))],
        [(0, 1, "nt", 0)],
        [(jax.ShapeDtypeStruct((t, d), F32), pl.BlockSpec((tm, tn1), lambda m, n, k: (m, n)))],
        (t // tm, d // tn1, N_DEV), (tm, tn1), comms=[ag_main, r2_in])[0]
    grad_x, g_pre_mix = _prenorm_bwd("prenorm_bwd_mix", d1, dh1, xs, pre_mix_w)
    ag_pre = _ag_comm([g_pre_mix.reshape(-1, LANE)])
    _comm_only("ag_small_pre", [ag_pre], in_vmem=True)
    g_all = jnp.concatenate([ag_pre.results[0], ag_main.results[0]], axis=1)

    reduced = {
        "w_in": r2_in.results[0], "w_out": r2_o.results[0], "w_gate": r2_g.results[0], "w_up": r2_u.results[0],
        "w_down": r2_d.results[0], "w_ple": r2_p.results[1], "w_ple_gate": r2_p.results[0],
    }
    grads, deltas, new_m, new_v = {}, {}, {}, {}
    for n in big_names:
        res = _adamw("adamw_" + n, reduced[n], shard[n], local(M[n], n), local(V[n], n))
        grads[n], deltas[n], new_m[n], new_v[n] = (unlocal(a, n) for a in res)
    sg, sd, sm, sv = _adamw("adamw_small", g_all, pack(lambda n: W[n], small_names), pack(lambda n: M[n], small_names),
                            pack(lambda n: V[n], small_names))
    off = 0
    for n in small_names:
        rows = W[n].size // LANE
        for src, dst in ((sg, grads), (sd, deltas), (sm, new_m), (sv, new_v)):
            dst[n] = src[off:off + rows].reshape(W[n].shape)
        off += rows

    loss = lax.psum(loss_part[0, 0], ("x", "y", "c"))
    return (loss, grad_x[None], *[grads[n] for n in all_names], *[deltas[n] for n in all_names],
            *[new_m[n] for n in all_names], *[new_v[n] for n in all_names])
```

```python
import functools

import numpy as np
import jax
import jax.numpy as jnp
from jax import lax
from jax.experimental import pallas as pl
from jax.experimental.pallas import tpu as pltpu

F32 = jnp.float32
BF16 = jnp.bfloat16

EPS = 1e-6
HEAD = 128
GLA_CHUNK = 64
GMLP_CHUNK = 128
N_DEV = 8
N_CHIP = 4
LANE = 128
VMEM_LIMIT = 56 * 1024 * 1024
HGRN_ROWS = 512
ROW_TILE = 128
EPILOGUE_ROWS = 256
HGRN_UNROLL = 1
HGRN_HEADS = 8

ADAM_LR = 0.001
ADAM_B1 = 0.9
ADAM_B2 = 0.999
ADAM_EPS = 1e-08
ADAM_WD = 0.01
ADAM_STEP = 10

MESH = pl.DeviceIdType.MESH

_DIMS = {
    "nn": (((1,), (0,)), ((), ())),
    "nt": (((1,), (1,)), ((), ())),
    "tn": (((0,), (0,)), ((), ())),
}


def _tile(dim, pref):
    return pref if dim % pref == 0 else dim


def _rows(r, bytes_per_row, budget=18 * 1024 * 1024, mult=16):
    best = None
    for cand in range(mult, r + 1, mult):
        if r % cand == 0 and cand * bytes_per_row <= budget:
            best = cand
    return best if best is not None else r


def _cparams(n_axes):
    return pltpu.CompilerParams(dimension_semantics=("arbitrary",) * n_axes, vmem_limit_bytes=VMEM_LIMIT)


def _dot(a, b, form="nn"):
    return lax.dot_general(a.astype(BF16), b.astype(BF16), _DIMS[form], preferred_element_type=F32)


def _split3(x):
    hi = x.astype(BF16)
    r = x - hi.astype(F32)
    mid = r.astype(BF16)
    lo = (r - mid.astype(F32)).astype(BF16)
    return hi, mid, lo


def _dot_exact_l(c, x):
    hi, mid, lo = _split3(x)
    d = lambda y: lax.dot_general(c, y, _DIMS["nn"], preferred_element_type=F32)
    return d(hi) + d(mid) + d(lo)


def _dot_exact_r(x, c):
    hi, mid, lo = _split3(x)
    d = lambda y: lax.dot_general(y, c, _DIMS["nn"], preferred_element_type=F32)
    return d(hi) + d(mid) + d(lo)


def _sigmoid(x):
    return 1.0 / (1.0 + jnp.exp(-x))


def _gelu(x):
    return 0.5 * x * (1.0 + lax.erf(x * 0.7071067811865476))


def _gelu_grad(x):
    cdf = 0.5 * (1.0 + lax.erf(x * 0.7071067811865476))
    pdf = jnp.exp(-0.5 * x * x) * 0.3989422804014327
    return cdf + x * pdf


def _position():
    return lax.axis_index("x"), lax.axis_index("y"), lax.axis_index("c")


def _linear_step(grid):
    step = 0
    for ax, n in enumerate(grid):
        step = step * n + pl.program_id(ax)
    return step


class _Comm:
    def __init__(self, arrays, out_shapes, sem_shapes, phases):
        self.arrays, self.out_shapes, self.sem_shapes, self.phases = list(arrays), list(out_shapes), list(sem_shapes), phases
        self.results = None


class _CommLayout:
    def __init__(self, comms, space=pl.ANY):
        self.comms = list(comms)
        self.arrays = [a for c in self.comms for a in c.arrays]
        self.out_shapes = [s for c in self.comms for s in c.out_shapes]
        self.sem_shapes = [s for c in self.comms for s in c.sem_shapes]
        self.n_in, self.n_out = len(self.arrays), len(self.out_shapes)
        self.in_specs = [pl.BlockSpec(memory_space=space)] * self.n_in
        self.out_specs = [pl.BlockSpec(memory_space=space)] * self.n_out

    def run(self, cin, cout, csem, step, n_steps, post):
        i = o = s = 0
        for c in self.comms:
            ins, outs, sems = cin[i:i + len(c.arrays)], cout[o:o + len(c.out_shapes)], csem[s:s + len(c.sem_shapes)]
            i, o, s = i + len(c.arrays), o + len(c.out_shapes), s + len(c.sem_shapes)
            for frac, fn in c.phases:
                if (frac is None) != post:
                    continue
                due = n_steps - 1 if frac is None else max(0, min(int(frac * n_steps), n_steps - 2))
                if n_steps == 1:
                    fn(ins, outs, sems)
                else:
                    pl.when(step == due)(functools.partial(fn, ins, outs, sems))

    def deliver(self, results):
        o = 0
        for c in self.comms:
            c.results = list(results[o:o + len(c.out_shapes)])
            o += len(c.out_shapes)


def _comm_only(name, comms, in_vmem=False):
    lay = _CommLayout(comms, pltpu.VMEM if in_vmem else pl.ANY)

    def body(*refs):
        cin, cout, csem = refs[:lay.n_in], refs[lay.n_in:lay.n_in + lay.n_out], refs[lay.n_in + lay.n_out:]
        lay.run(cin, cout, csem, 0, 1, post=False)
        lay.run(cin, cout, csem, 0, 1, post=True)

    res = pl.pallas_call(
        body, name=name, in_specs=lay.in_specs, out_specs=lay.out_specs, out_shape=lay.out_shapes,
        scratch_shapes=lay.sem_shapes,
    )(*lay.arrays)
    lay.deliver(res)


def _ag_comm(shards, mid_frac=0.0):
    n = len(shards)
    per = N_DEV - 1

    def tools(ins, outs, sems):
        send_sems, recv_sems, local_sems = sems
        x, y, c = _position()
        me, sibling = (x, y, c), (x, y, 1 - c)
        chips = [(1 - x, y), (x, 1 - y), (1 - x, 1 - y)]

        def copy(a, k, block, to, from_shard=False):
            dst = outs[a].at[4 * block[0] + 2 * block[1] + block[2]]
            return pltpu.make_async_remote_copy(
                src_ref=ins[a] if from_shard else dst, dst_ref=dst,
                send_sem=send_sems.at[a * per + k], recv_sem=recv_sems.at[a * per + k],
                device_id=to, device_id_type=MESH)

        def local(a):
            return pltpu.make_async_copy(ins[a], outs[a].at[4 * x + 2 * y + c], local_sems.at[a])

        return me, sibling, chips, c, copy, local

    def first(ins, outs, sems):
        me, sibling, chips, c, copy, local = tools(ins, outs, sems)
        for a in range(n):
            local(a).start()
            copy(a, 0, me, sibling, True).start()
            for j, chip in enumerate(chips):
                copy(a, 1 + j, me, (*chip, c), True).start()

    def middle(ins, outs, sems):
        me, sibling, chips, c, copy, local = tools(ins, outs, sems)
        for a in range(n):
            for j, chip in enumerate(chips):
                copy(a, 1 + j, (*chip, c), me).wait_recv()
                copy(a, 4 + j, (*chip, c), sibling).start()

    def last(ins, outs, sems):
        me, sibling, chips, c, copy, local = tools(ins, outs, sems)
        for a in range(n):
            copy(a, 0, sibling, me).wait_recv()
            copy(a, 0, me, sibling, True).wait_send()
            for j, chip in enumerate(chips):
                copy(a, 4 + j, (*chip, 1 - c), me).wait_recv()
                copy(a, 1 + j, me, (*chip, c), True).wait_send()
                copy(a, 4 + j, (*chip, c), sibling).wait_send()
            local(a).wait()

    return _Comm(
        shards, [jax.ShapeDtypeStruct((N_DEV,) + s.shape, s.dtype) for s in shards],
        [pltpu.SemaphoreType.DMA((n * per,)), pltpu.SemaphoreType.DMA((n * per,)), pltpu.SemaphoreType.DMA((n,))],
        [(0.0, first), (mid_frac, middle), (None, last)])


def _pair_comm(grads):
    n = len(grads)

    def copies(ins, outs, sems):
        send_sems, recv_sems = sems
        x, y, c = _position()
        return [pltpu.make_async_remote_copy(
            src_ref=ins[a].at[k, 1 - c], dst_ref=outs[a].at[k],
            send_sem=send_sems.at[a * N_CHIP + k], recv_sem=recv_sems.at[a * N_CHIP + k],
            device_id=(x, y, 1 - c), device_id_type=MESH) for a in range(n) for k in range(N_CHIP)]

    def first(ins, outs, sems):
        for cp in copies(ins, outs, sems):
            cp.start()

    def last(ins, outs, sems):
        for cp in copies(ins, outs, sems):
            cp.wait()

    return _Comm(
        grads, [jax.ShapeDtypeStruct((N_CHIP,) + g.shape[2:], g.dtype) for g in grads],
        [pltpu.SemaphoreType.DMA((n * N_CHIP,)), pltpu.SemaphoreType.DMA((n * N_CHIP,))],
        [(0.0, first), (None, last)])


def _chip_comm(sums):
    n = len(sums)
    per = N_CHIP - 1

    def copies(ins, outs, sems):
        send_sems, recv_sems, local_sems = sems
        x, y, c = _position()
        my_chip = 2 * x + y
        cps = []
        for a in range(n):
            cps.append(pltpu.make_async_copy(ins[a].at[my_chip], outs[a].at[my_chip], local_sems.at[a]))
            for j, (px, py) in enumerate([(1 - x, y), (x, 1 - y), (1 - x, 1 - y)]):
                cps.append(pltpu.make_async_remote_copy(
                    src_ref=ins[a].at[2 * px + py], dst_ref=outs[a].at[my_chip],
                    send_sem=send_sems.at[a * per + j], recv_sem=recv_sems.at[a * per + j],
                    device_id=(px, py, c), device_id_type=MESH))
        return cps

    def first(ins, outs, sems):
        for cp in copies(ins, outs, sems):
            cp.start()

    def last(ins, outs, sems):
        for cp in copies(ins, outs, sems):
            cp.wait()

    return _Comm(
        sums, [jax.ShapeDtypeStruct(s.shape, s.dtype) for s in sums],
        [pltpu.SemaphoreType.DMA((n * per,)), pltpu.SemaphoreType.DMA((n * per,)), pltpu.SemaphoreType.DMA((n,))],
        [(0.0, first), (None, last)])


def _matmul(name, operands, pairs, outs, grid, acc_shape, n_slots=1, epilogue=None, comms=()):
    used = sorted({i for p in pairs for i in p[:2]})
    n_op = len(operands)
    n_out = len(outs)
    k_axis = len(grid) - 1
    n_k = grid[-1]
    lay = _CommLayout(comms)

    direct = n_k == 1 and epilogue is None
    n_acc = 0 if direct else 1

    def body(*refs):
        ops = refs[:n_op]
        out_refs = refs[n_op + lay.n_in:n_op + lay.n_in + n_out]
        acc = None if direct else refs[n_op + lay.n_in + n_out + lay.n_out]
        k = pl.program_id(k_axis)
        step = _linear_step(grid)
        cin = refs[n_op:n_op + lay.n_in]
        cout = refs[n_op + lay.n_in + n_out:n_op + lay.n_in + n_out + lay.n_out]
        csem = refs[n_op + lay.n_in + n_out + lay.n_out + n_acc:]
        lay.run(cin, cout, csem, step, int(np.prod(grid)), post=False)

        if n_k > 1:
            @pl.when(k == 0)
            def _():
                acc[...] = jnp.zeros_like(acc)

        vals = {i: ops[i][...] for i in used}
        vals = {i: (v if v.dtype == BF16 else v.astype(BF16)) for i, v in vals.items()}
        for s in range(n_slots):
            tot = None
            for ia, ib, form, slot in pairs:
                if slot != s:
                    continue
                d = lax.dot_general(vals[ia], vals[ib], _DIMS[form], preferred_element_type=F32)
                tot = d if tot is None else tot + d
            if direct:
                out_refs[s][...] = tot.astype(out_refs[s].dtype)
            elif n_k == 1:
                acc[s] = tot
            else:
                acc[s] += tot

        def finish():
            rows = acc_shape[0]
            chunk = EPILOGUE_ROWS if (epilogue is not None and rows % EPILOGUE_ROWS == 0) else rows
            for r0 in range(0, rows, chunk):
                sl = slice(r0, r0 + chunk)
                accs = [acc[s, sl, :] for s in range(n_slots)]
                extra = [ops[i][sl, :] for i in range(n_op) if i not in used]
                res = epilogue(accs, *extra) if epilogue is not None else accs
                for o, v in zip(out_refs, res):
                    o[sl, :] = v.astype(o.dtype)

        if n_k > 1:
            pl.when(k == n_k - 1)(finish)
        elif not direct:
            finish()

        lay.run(cin, cout, csem, step, int(np.prod(grid)), post=True)

    res = pl.pallas_call(
        body,
        name=name,
        grid=grid,
        in_specs=[s for _, s in operands] + lay.in_specs,
        out_specs=[s for _, s in outs] + lay.out_specs,
        out_shape=[s for s, _ in outs] + lay.out_shapes,
        scratch_shapes=([] if direct else [pltpu.VMEM((n_slots,) + tuple(acc_shape), F32)]) + lay.sem_shapes,
        compiler_params=_cparams(len(grid)),
    )(*[a for a, _ in operands], *lay.arrays)
    lay.deliver(res[n_out:])
    return res[:n_out]


def _mm_plain(name, a, b, form, out_dtype, tm, tn, tk, extra=None, epilogue=None, comms=()):
    if form == "nn":
        (M, K), N = a.shape, b.shape[1]
    elif form == "nt":
        (M, K), N = a.shape, b.shape[0]
    else:
        (K, M), N = a.shape, b.shape[1]
    tm, tn, tk = _tile(M, tm), _tile(N, tn), _tile(K, tk)
    a_spec = pl.BlockSpec((tk, tm), lambda m, n, k: (k, m)) if form == "tn" else pl.BlockSpec((tm, tk), lambda m, n, k: (m, k))
    b_spec = pl.BlockSpec((tn, tk), lambda m, n, k: (n, k)) if form == "nt" else pl.BlockSpec((tk, tn), lambda m, n, k: (k, n))
    operands = [(a, a_spec), (b, b_spec)]
    if extra is not None:
        operands.append((extra, pl.BlockSpec((tm, tn), lambda m, n, k: (m, n))))
    out = (jax.ShapeDtypeStruct((M, N), out_dtype), pl.BlockSpec((tm, tn), lambda m, n, k: (m, n)))
    return _matmul(name, operands, [(0, 1, form, 0)], [out], (M // tm, N // tn, K // tk), (tm, tn), epilogue=epilogue,
                   comms=comms)[0]


def _cast_bf16(name, w):
    r, c = w.shape
    tr = _rows(r, 6 * c)

    def body(w_ref, o_ref):
        o_ref[...] = w_ref[...].astype(BF16)

    return pl.pallas_call(
        body, name=name, grid=(r // tr,),
        in_specs=[pl.BlockSpec((tr, c), lambda i: (i, 0))],
        out_specs=pl.BlockSpec((tr, c), lambda i: (i, 0)),
        out_shape=jax.ShapeDtypeStruct((r, c), BF16),
        compiler_params=_cparams(1),
    )(w)


def _rms_stats(x):
    r = lax.rsqrt(jnp.mean(x * x, axis=-1, keepdims=True) + EPS)
    return x * r, r


def _rms_bwd(xhat, r, w, dy):
    dxh = dy * w
    return r * (dxh - xhat * jnp.mean(dxh * xhat, axis=-1, keepdims=True))


def _row_spec(tr, d):
    return pl.BlockSpec((tr, d), lambda i: (i, 0))


def _vec_spec(d):
    return pl.BlockSpec((1, d), lambda i: (0, 0))


def _rms_fwd(name, x, w):
    t, d = x.shape
    tr = _tile(t, ROW_TILE)

    def body(x_ref, w_ref, h_ref):
        xh, _ = _rms_stats(x_ref[...])
        h_ref[...] = (xh * w_ref[...]).astype(BF16)

    return pl.pallas_call(
        body, name=name, grid=(t // tr,),
        in_specs=[_row_spec(tr, d), _vec_spec(d)],
        out_specs=_row_spec(tr, d),
        out_shape=jax.ShapeDtypeStruct((t, d), BF16),
        compiler_params=_cparams(1),
    )(x, w)


def _resid_rms(name, xres, y, w_post, w_next):
    t, d = xres.shape
    tr = _tile(t, ROW_TILE)
    has_next = w_next is not None

    def body(*refs):
        if has_next:
            x_ref, y_ref, wp_ref, wn_ref, xo_ref, h_ref = refs
        else:
            x_ref, y_ref, wp_ref, xo_ref, h_ref = refs
        yh, _ = _rms_stats(y_ref[...])
        xn = x_ref[...] + yh * wp_ref[...]
        xo_ref[...] = xn
        if has_next:
            xh, _ = _rms_stats(xn)
            h_ref[...] = (xh * wn_ref[...]).astype(BF16)
        else:
            h_ref[...] = xn.astype(BF16)

    ins = [xres, y, w_post] + ([w_next] if has_next else [])
    in_specs = [_row_spec(tr, d), _row_spec(tr, d), _vec_spec(d)] + ([_vec_spec(d)] if has_next else [])
    return pl.pallas_call(
        body, name=name, grid=(t // tr,),
        in_specs=in_specs,
        out_specs=[_row_spec(tr, d), _row_spec(tr, d)],
        out_shape=[jax.ShapeDtypeStruct((t, d), F32), jax.ShapeDtypeStruct((t, d), BF16)],
        compiler_params=_cparams(1),
    )(*ins)


def _ple_loss(name, x2, pe, pgl, w_pp, tgt):
    t, d = x2.shape
    tr = _tile(t, ROW_TILE)

    def body(x2_ref, pe_ref, pgl_ref, w_ref, tgt_ref, loss_ref, d3_ref, dpe_ref, dpgl_ref, dw_ref):
        @pl.when(pl.program_id(0) == 0)
        def _():
            loss_ref[...] = jnp.zeros_like(loss_ref)
            dw_ref[...] = jnp.zeros_like(dw_ref)

        pe_v = pe_ref[...]
        s = _sigmoid(pgl_ref[...])
        y = pe_v * s
        yh, r = _rms_stats(y)
        w = w_ref[...]
        err = x2_ref[...] + yh * w - tgt_ref[...]
        loss_ref[...] += 0.5 * jnp.sum(jnp.mean(err * err, axis=-1, keepdims=True), axis=0, keepdims=True)
        d3 = err * (1.0 / d)
        d3_ref[...] = d3
        dw_ref[...] += jnp.sum(d3 * yh, axis=0, keepdims=True)
        dy = _rms_bwd(yh, r, w, d3)
        dpe_ref[...] = (dy * s).astype(BF16)
        dpgl_ref[...] = (dy * pe_v * s * (1.0 - s)).astype(BF16)

    return pl.pallas_call(
        body, name=name, grid=(t // tr,),
        in_specs=[_row_spec(tr, d), _row_spec(tr, d), _row_spec(tr, d), _vec_spec(d), _row_spec(tr, d)],
        out_specs=[pl.BlockSpec((1, 1), lambda i: (0, 0)), _row_spec(tr, d), _row_spec(tr, d), _row_spec(tr, d), _vec_spec(d)],
        out_shape=[jax.ShapeDtypeStruct((1, 1), F32), jax.ShapeDtypeStruct((t, d), F32),
                   jax.ShapeDtypeStruct((t, d), BF16), jax.ShapeDtypeStruct((t, d), BF16),
                   jax.ShapeDtypeStruct((1, d), F32)],
        compiler_params=_cparams(1),
    )(x2, pe, pgl, w_pp, tgt)


def _norm_bwd(name, dres, y, w_post):
    t, d = dres.shape
    tr = _tile(t, ROW_TILE)

    def body(d_ref, y_ref, w_ref, dy_ref, dw_ref):
        @pl.when(pl.program_id(0) == 0)
        def _():
            dw_ref[...] = jnp.zeros_like(dw_ref)

        dv = d_ref[...]
        yh, r = _rms_stats(y_ref[...])
        dw_ref[...] += jnp.sum(dv * yh, axis=0, keepdims=True)
        dy_ref[...] = _rms_bwd(yh, r, w_ref[...], dv).astype(BF16)

    return pl.pallas_call(
        body, name=name, grid=(t // tr,),
        in_specs=[_row_spec(tr, d), _row_spec(tr, d), _vec_spec(d)],
        out_specs=[_row_spec(tr, d), _vec_spec(d)],
        out_shape=[jax.ShapeDtypeStruct((t, d), BF16), jax.ShapeDtypeStruct((1, d), F32)],
        compiler_params=_cparams(1),
    )(dres, y, w_post)


def _prenorm_bwd(name, dres, dh, xin, w_pre, y=None, w_post=None):
    t, d = dres.shape
    tr = _tile(t, ROW_TILE)
    two = y is not None

    def body(*refs):
        if two:
            d_ref, dh_ref, x_ref, wpre_ref, y_ref, wpost_ref, do_ref, dwpre_ref, dy_ref, dwpost_ref = refs
        else:
            d_ref, dh_ref, x_ref, wpre_ref, do_ref, dwpre_ref = refs

        @pl.when(pl.program_id(0) == 0)
        def _():
            dwpre_ref[...] = jnp.zeros_like(dwpre_ref)
            if two:
                dwpost_ref[...] = jnp.zeros_like(dwpost_ref)

        dhv = dh_ref[...]
        xh, r = _rms_stats(x_ref[...])
        dwpre_ref[...] += jnp.sum(dhv * xh, axis=0, keepdims=True)
        dout = d_ref[...] + _rms_bwd(xh, r, wpre_ref[...], dhv)
        do_ref[...] = dout
        if two:
            yh, ry = _rms_stats(y_ref[...])
            dwpost_ref[...] += jnp.sum(dout * yh, axis=0, keepdims=True)
            dy_ref[...] = _rms_bwd(yh, ry, wpost_ref[...], dout).astype(BF16)

    ins = [dres, dh, xin, w_pre] + ([y, w_post] if two else [])
    in_specs = [_row_spec(tr, d)] * 3 + [_vec_spec(d)] + ([_row_spec(tr, d), _vec_spec(d)] if two else [])
    out_specs = [_row_spec(tr, d), _vec_spec(d)] + ([_row_spec(tr, d), _vec_spec(d)] if two else [])
    out_shape = [jax.ShapeDtypeStruct((t, d), F32), jax.ShapeDtypeStruct((1, d), F32)]
    if two:
        out_shape += [jax.ShapeDtypeStruct((t, d), BF16), jax.ShapeDtypeStruct((1, d), F32)]
    return pl.pallas_call(
        body, name=name, grid=(t // tr,),
        in_specs=in_specs, out_specs=out_specs, out_shape=out_shape,
        compiler_params=_cparams(1),
    )(*ins)


_LEVELS = (32, 16, 8, 4, 2, 1)
_N_CUM = 3 + 2 * len(_LEVELS)


def _hgrn_constants():
    c = GLA_CHUNK
    idx = np.arange(c)
    t, r = idx[:, None], idx[None, :]
    mats = [(r <= t), (r > t), np.ones((c, c), bool)]
    lq, lk, masks = [], [], []
    for h in _LEVELS:
        blk, pos = idx // (2 * h), idx % (2 * h)
        mid = blk * 2 * h + h - 1
        upper, lower = pos >= h, pos < h
        lq.append(upper[:, None] & (r > mid[:, None]) & (r <= t))
        lk.append(lower[:, None] & (r > t) & (r <= mid[:, None]))
        masks.append((blk[:, None] == blk[None, :]) & upper[:, None] & lower[None, :])
    cum = np.concatenate(mats + lq + lk, axis=0).astype(np.float32)
    rev = (r >= t).astype(np.float32)
    return (jnp.asarray(cum, BF16), jnp.asarray(rev, BF16), jnp.asarray(np.stack(masks).astype(np.float32)))


def _hgrn_gates(qp, fp, lb):
    sq = _sigmoid(qp)
    q = qp * sq
    sg = _sigmoid(fp)
    f = lb + (1.0 - lb) * sg
    k = 1.0 - f
    logf = jnp.log(jnp.maximum(f, 1e-30))
    return q, sq, sg, f, k, logf


def _hgrn_decays(cum_ref, logf):
    c = GLA_CHUNK
    e = jnp.exp(_dot_exact_l(cum_ref[...], logf))
    part = lambda i: e[i * c:(i + 1) * c]
    n = len(_LEVELS)
    return part(0), part(1), part(2), [part(3 + i) for i in range(n)], [part(3 + n + i) for i in range(n)]


def _hgrn_fwd(proj, lb, nw, n_heads, comms=()):
    t = proj.shape[0]
    aw = n_heads * HEAD
    rb = _tile(t, HGRN_ROWS)
    c = GLA_CHUNK
    n_sub = rb // c
    cum, _, masks = _hgrn_constants()
    lay = _CommLayout(comms)
    hp = HGRN_HEADS if n_heads % HGRN_HEADS == 0 else 1
    wd = hp * HEAD
    grid = (n_heads // hp, t // rb)

    def body(*refs):
        q_ref, f_ref, i_ref, g_ref, lb_ref, nw_ref, cum_ref, m_ref = refs[:8]
        cin = refs[8:8 + lay.n_in]
        a_ref, o_ref, s_ref, sc_ref = refs[8 + lay.n_in:12 + lay.n_in]
        cout = refs[12 + lay.n_in:12 + lay.n_in + lay.n_out]
        st = refs[12 + lay.n_in + lay.n_out]
        csem = refs[13 + lay.n_in + lay.n_out:]
        step = _linear_step(grid)
        lay.run(cin, cout, csem, step, grid[0] * grid[1], post=False)

        @pl.when(pl.program_id(1) == 0)
        def _():
            st[...] = jnp.zeros_like(st)

        lbv = lb_ref[...]
        nwv = nw_ref[...]
        eye = (lax.broadcasted_iota(jnp.int32, (c, c), 0) == lax.broadcasted_iota(jnp.int32, (c, c), 1)).astype(F32)
        heads = range(hp)
        hs = lambda a, h: a[:, h * HEAD:(h + 1) * HEAD]

        def chunk(j, carry):
            rows = pl.ds(pl.multiple_of(j * c, c), c)
            q, _, _, _, k, logf = _hgrn_gates(q_ref[rows, :], f_ref[rows, :], lbv)
            v = i_ref[rows, :]
            eb, ebe, eend, eq, ek = _hgrn_decays(cum_ref, logf)
            qt, kt, qk = q * eb, k * ebe, q * k
            s_in = [st[h] for h in heads]
            for h in heads:
                s_ref[h, j] = s_in[h]
            inter = [_dot(hs(qt, h), s_in[h], "nt") for h in heads]
            for h in heads:
                st[h] = s_in[h] * hs(eend, h)[0:1] + _dot(hs(v, h), hs(kt, h), "tn")
            scores = [eye * jnp.sum(hs(qk, h), axis=-1, keepdims=True) for h in heads]
            for lvl in range(len(_LEVELS)):
                ql, kl = q * eq[lvl], k * ek[lvl]
                for h in heads:
                    scores[h] = scores[h] + m_ref[lvl] * _dot(hs(ql, h), hs(kl, h), "nt")
            gv = g_ref[rows, :]
            gate = nwv * (gv * _sigmoid(gv))
            for h in heads:
                sc_ref[h, rows, :] = scores[h]
                o = inter[h] + _dot(scores[h], hs(v, h))
                o_ref[rows, h * HEAD:(h + 1) * HEAD] = o
                r = lax.rsqrt(jnp.mean(o * o, axis=-1, keepdims=True) + EPS)
                a_ref[rows, h * HEAD:(h + 1) * HEAD] = (o * r * hs(gate, h)).astype(BF16)
            return carry

        lax.fori_loop(0, n_sub, chunk, 0, unroll=HGRN_UNROLL)
        lay.run(cin, cout, csem, step, grid[0] * grid[1], post=True)

    n_hb = n_heads // hp
    col = lambda base: pl.BlockSpec((rb, wd), lambda h, r: (r, base * n_hb + h))
    vec = pl.BlockSpec((1, wd), lambda h, r: (0, h))
    res = pl.pallas_call(
        body, name="hgrn2_fwd", grid=grid,
        in_specs=[col(0), col(1), col(2), col(3), vec, vec,
                  pl.BlockSpec(cum.shape, lambda h, r: (0, 0)), pl.BlockSpec(masks.shape, lambda h, r: (0, 0, 0))] + lay.in_specs,
        out_specs=[pl.BlockSpec((rb, wd), lambda h, r: (r, h)), pl.BlockSpec((rb, wd), lambda h, r: (r, h)),
                   pl.BlockSpec((hp, n_sub, HEAD, HEAD), lambda h, r: (h, r, 0, 0)),
                   pl.BlockSpec((hp, rb, c), lambda h, r: (h, r, 0))] + lay.out_specs,
        out_shape=[jax.ShapeDtypeStruct((t, aw), BF16), jax.ShapeDtypeStruct((t, aw), F32),
                   jax.ShapeDtypeStruct((n_heads, t // c, HEAD, HEAD), F32),
                   jax.ShapeDtypeStruct((n_heads, t, c), F32)] + lay.out_shapes,
        scratch_shapes=[pltpu.VMEM((hp, HEAD, HEAD), F32)] + lay.sem_shapes,
        compiler_params=_cparams(2),
    )(proj, proj, proj, proj, lb, nw, cum, masks, *lay.arrays)
    lay.deliver(res[4:])
    return res[:4]


def _hgrn_bwd(proj, lb, nw, o_raw, states, scores, dab, n_heads, comms=()):
    t = proj.shape[0]
    aw = n_heads * HEAD
    rb = _tile(t, HGRN_ROWS)
    c = GLA_CHUNK
    n_sub = rb // c
    n_rb = t // rb
    cum, rev, masks = _hgrn_constants()
    lay = _CommLayout(comms)
    hp = HGRN_HEADS if n_heads % HGRN_HEADS == 0 else 1
    wd = hp * HEAD
    grid = (n_heads // hp, n_rb)

    def body(*refs):
        q_ref, f_ref, i_ref, g_ref, lb_ref, nw_ref, o_ref, s_ref, sc_ref, da_ref, cum_ref, rev_ref, m_ref = refs[:13]
        cin = refs[13:13 + lay.n_in]
        dq_ref, df_ref, di_ref, dg_ref, dlb_ref, dnw_ref = refs[13 + lay.n_in:19 + lay.n_in]
        cout = refs[19 + lay.n_in:19 + lay.n_in + lay.n_out]
        dst = refs[19 + lay.n_in + lay.n_out]
        csem = refs[20 + lay.n_in + lay.n_out:]
        step = _linear_step(grid)
        lay.run(cin, cout, csem, step, grid[0] * grid[1], post=False)

        @pl.when(pl.program_id(1) == 0)
        def _():
            dst[...] = jnp.zeros_like(dst)
            dlb_ref[...] = jnp.zeros_like(dlb_ref)
            dnw_ref[...] = jnp.zeros_like(dnw_ref)

        lbv = lb_ref[...]
        nwv = nw_ref[...]
        ri = lax.broadcasted_iota(jnp.int32, (c, c), 0)
        ci = lax.broadcasted_iota(jnp.int32, (c, c), 1)
        eye = (ri == ci).astype(F32)
        causal = (ci <= ri).astype(F32)
        last_row = (lax.broadcasted_iota(jnp.int32, (c, wd), 0) == c - 1).astype(F32)
        heads = range(hp)
        hs = lambda a, h: a[:, h * HEAD:(h + 1) * HEAD]
        wide = lambda parts: parts[0] if hp == 1 else jnp.concatenate(parts, axis=1)

        def head_mean(a):
            return wide([jnp.broadcast_to(jnp.mean(hs(a, h), axis=-1, keepdims=True), (c, HEAD)) for h in heads])

        def chunk(jj, carry):
            j = n_sub - 1 - jj
            rows = pl.ds(pl.multiple_of(j * c, c), c)
            qp = q_ref[rows, :]
            q, sq, sg, f, k, logf = _hgrn_gates(qp, f_ref[rows, :], lbv)
            v = i_ref[rows, :]
            gv = g_ref[rows, :]
            eb, ebe, eend, eq, ek = _hgrn_decays(cum_ref, logf)
            s_in = [s_ref[h, j] for h in heads]
            a_sc = [sc_ref[h, rows, :] for h in heads]
            dsn = [dst[h] for h in heads]
            o = o_ref[rows, :]
            r = lax.rsqrt(head_mean(o * o) + EPS)
            oh = o * r
            sgg = _sigmoid(gv)
            sil = gv * sgg
            da = da_ref[rows, :]
            dg_ref[rows, :] = (da * oh * nwv * (sgg * (1.0 + gv * (1.0 - sgg)))).astype(BF16)
            dnw_ref[...] += jnp.sum(da * oh * sil, axis=0, keepdims=True)
            doh = da * nwv * sil
            do = r * (doh - oh * head_mean(doh * oh))
            kt = k * ebe
            qt = q * eb
            d_sc = [_dot(hs(do, h), hs(v, h), "nt") * causal for h in heads]
            dqt = wide([_dot(hs(do, h), s_in[h]) for h in heads])
            dkt = wide([_dot(hs(v, h), dsn[h]) for h in heads])
            for h in heads:
                dst[h] = dsn[h] * hs(eend, h)[0:1] + _dot(hs(do, h), hs(qt, h), "tn")
            di_ref[rows, :] = wide([_dot(a_sc[h], hs(do, h), "tn") + _dot(hs(kt, h), dsn[h], "nt") for h in heads]).astype(BF16)
            diag = wide([jnp.broadcast_to(jnp.sum(d_sc[h] * eye, axis=-1, keepdims=True), (c, HEAD)) for h in heads])
            dq = dqt * eb
            dk = dkt * ebe
            db = q * dq - k * dk
            dq = dq + diag * k
            dk = dk + diag * q
            for lvl in range(len(_LEVELS)):
                ql = (q * eq[lvl]).astype(BF16)
                kl = (k * ek[lvl]).astype(BF16)
                dm = [(m_ref[lvl] * d_sc[h]).astype(BF16) for h in heads]
                gq = wide([_dot(dm[h], hs(kl, h)) for h in heads])
                gk = wide([_dot(dm[h], hs(ql, h), "tn") for h in heads])
                dq = dq + gq * eq[lvl]
                dk = dk + gk * ek[lvl]
                db = db + ql.astype(F32) * gq - kl.astype(F32) * gk
            state_term = wide([jnp.sum(s_in[h] * dsn[h], axis=0, keepdims=True) for h in heads])
            extra = jnp.sum(dkt * kt, axis=0, keepdims=True) + eend[0:1] * state_term
            db = db + last_row * extra
            dlogf = _dot_exact_l(rev_ref[...], db)
            dfv = jnp.where(f > 1e-30, dlogf / f, 0.0) - dk
            df_ref[rows, :] = (dfv * (1.0 - lbv) * sg * (1.0 - sg)).astype(BF16)
            dlb_ref[...] += jnp.sum(dfv * (1.0 - sg), axis=0, keepdims=True)
            dq_ref[rows, :] = (dq * (sq * (1.0 + qp * (1.0 - sq)))).astype(BF16)
            return carry

        lax.fori_loop(0, n_sub, chunk, 0, unroll=HGRN_UNROLL)
        lay.run(cin, cout, csem, step, grid[0] * grid[1], post=True)

    n_hb = n_heads // hp
    col = lambda base: pl.BlockSpec((rb, wd), lambda h, r: (n_rb - 1 - r, base * n_hb + h))
    blk = pl.BlockSpec((rb, wd), lambda h, r: (n_rb - 1 - r, h))
    vec = pl.BlockSpec((1, wd), lambda h, r: (0, h))
    const = lambda a: pl.BlockSpec(a.shape, lambda h, r: (0,) * a.ndim)
    res = pl.pallas_call(
        body, name="hgrn2_bwd", grid=grid,
        in_specs=[col(0), col(1), col(2), col(3), vec, vec, blk,
                  pl.BlockSpec((hp, n_sub, HEAD, HEAD), lambda h, r: (h, n_rb - 1 - r, 0, 0)),
                  pl.BlockSpec((hp, rb, c), lambda h, r: (h, n_rb - 1 - r, 0)),
                  blk, const(cum), const(rev), const(masks)] + lay.in_specs,
        out_specs=[blk, blk, blk, blk, vec, vec] + lay.out_specs,
        out_shape=[jax.ShapeDtypeStruct((t, aw), BF16)] * 4 + [jax.ShapeDtypeStruct((1, aw), F32)] * 2 + lay.out_shapes,
        scratch_shapes=[pltpu.VMEM((hp, HEAD, HEAD), F32)] + lay.sem_shapes,
        compiler_params=_cparams(2),
    )(proj, proj, proj, proj, lb, nw, o_raw, states, scores, dab, cum, rev, masks, *lay.arrays)
    lay.deliver(res[6:])
    return res[:6]


def _lb_fwd(lb_param):
    def body(p_ref, o_ref):
        p = p_ref[...]
        e = jnp.exp(p - jnp.max(p, axis=0, keepdims=True))
        o_ref[...] = e[0:1] / jnp.sum(e, axis=0, keepdims=True)

    return pl.pallas_call(body, name="lb_fwd", out_shape=jax.ShapeDtypeStruct((1, lb_param.shape[1]), F32))(lb_param)


def _lb_bwd(lb_param, dlb):
    def body(p_ref, d_ref, o_ref):
        p = p_ref[...]
        e = jnp.exp(p - jnp.max(p, axis=0, keepdims=True))
        s = e / jnp.sum(e, axis=0, keepdims=True)
        first = (lax.broadcasted_iota(jnp.int32, p.shape, 0) == 0).astype(F32)
        o_ref[...] = d_ref[...] * s[0:1] * (first - s)

    return pl.pallas_call(body, name="lb_bwd", out_shape=jax.ShapeDtypeStruct(lb_param.shape, F32))(lb_param, dlb)


def _gmlp_norm(v, lnw, lnb):
    vf = _gelu(v)
    mu = jnp.mean(vf, axis=-1, keepdims=True)
    cen = vf - mu
    rstd = lax.rsqrt(jnp.mean(cen * cen, axis=-1, keepdims=True) + EPS)
    xh = cen * rstd
    return xh, rstd, xh * lnw + lnb


def _tril(n):
    return (lax.broadcasted_iota(jnp.int32, (n, n), 1) <= lax.broadcasted_iota(jnp.int32, (n, n), 0)).astype(F32)


def _gmlp_fwd(proj, lnw, lnb, w_sp, bs_t, n_groups, col_base):
    t = proj.shape[0]
    bw = n_groups * HEAD
    c = GMLP_CHUNK

    def body(u_ref, v_ref, lnw_ref, lnb_ref, w_ref, bs_ref, o_ref):
        tri = _tril(c)
        uf = _gelu(u_ref[...])
        _, _, vn = _gmlp_norm(v_ref[...], lnw_ref[...], lnb_ref[...])
        for g in range(n_groups):
            cols = slice(g * HEAD, (g + 1) * HEAD)
            z = _dot(w_ref[g] * tri, vn[:, cols]) + bs_ref[:, g:g + 1]
            o_ref[:, cols] = (uf[:, cols] * z).astype(BF16)

    blk = lambda b: pl.BlockSpec((c, bw), lambda n: (n, b))
    const = lambda a: pl.BlockSpec(a.shape, lambda n: (0,) * a.ndim)
    return pl.pallas_call(
        body, name="gmlp_fwd", grid=(t // c,),
        in_specs=[blk(col_base), blk(col_base + 1), const(lnw), const(lnb), const(w_sp), const(bs_t)],
        out_specs=pl.BlockSpec((c, bw), lambda n: (n, 0)),
        out_shape=jax.ShapeDtypeStruct((t, bw), BF16),
        compiler_params=_cparams(1),
    )(proj, proj, lnw, lnb, w_sp, bs_t)


def _gmlp_bwd(proj, lnw, lnb, w_sp, bs_t, dab, n_groups, col_base):
    t = proj.shape[0]
    bw = n_groups * HEAD
    c = GMLP_CHUNK
    n_steps = t // c
    sel = jnp.asarray((np.arange(bw)[:, None] // HEAD == np.arange(n_groups)[None, :]).astype(np.float32), BF16)

    def body(u_ref, v_ref, lnw_ref, lnb_ref, w_ref, bs_ref, d_ref, sel_ref,
             du_ref, dv_ref, dlnw_ref, dlnb_ref, dw_ref, dbs_ref, dz_acc, dvn_scr):
        step = pl.program_id(0)

        @pl.when(step == 0)
        def _():
            dlnw_ref[...] = jnp.zeros_like(dlnw_ref)
            dlnb_ref[...] = jnp.zeros_like(dlnb_ref)
            dw_ref[...] = jnp.zeros_like(dw_ref)
            dz_acc[...] = jnp.zeros_like(dz_acc)

        tri = _tril(c)
        u = u_ref[...]
        v = v_ref[...]
        uf = _gelu(u)
        lnw_v = lnw_ref[...]
        xh, rstd, vn = _gmlp_norm(v, lnw_v, lnb_ref[...])
        dbo = d_ref[...]
        dz = dbo * uf
        dz_acc[...] += dz
        for g in range(n_groups):
            cols = slice(g * HEAD, (g + 1) * HEAD)
            wg = w_ref[g] * tri
            z = _dot(wg, vn[:, cols]) + bs_ref[:, g:g + 1]
            du_ref[:, cols] = (dbo[:, cols] * z * _gelu_grad(u[:, cols])).astype(BF16)
            dvn_scr[:, cols] = _dot(wg, dz[:, cols], "tn")
            dw_ref[g] += tri * _dot(dz[:, cols], vn[:, cols], "nt")
        dvn = dvn_scr[...]
        dlnw_ref[...] += jnp.sum(dvn * xh, axis=0, keepdims=True)
        dlnb_ref[...] += jnp.sum(dvn, axis=0, keepdims=True)
        dxh = dvn * lnw_v
        dvf = rstd * (dxh - jnp.mean(dxh, axis=-1, keepdims=True) - xh * jnp.mean(dxh * xh, axis=-1, keepdims=True))
        dv_ref[...] = (dvf * _gelu_grad(v)).astype(BF16)

        @pl.when(step == n_steps - 1)
        def _():
            dbs_ref[...] = _dot_exact_r(dz_acc[...], sel_ref[...])

    blk = lambda b: pl.BlockSpec((c, bw), lambda n: (n, b))
    const = lambda a: pl.BlockSpec(a.shape, lambda n: (0,) * a.ndim)
    row = pl.BlockSpec((c, bw), lambda n: (n, 0))
    vec = pl.BlockSpec((1, bw), lambda n: (0, 0))
    return pl.pallas_call(
        body, name="gmlp_bwd", grid=(n_steps,),
        in_specs=[blk(col_base), blk(col_base + 1), const(lnw), const(lnb), const(w_sp), const(bs_t), blk(1), const(sel)],
        out_specs=[row, row, vec, vec, const(w_sp), const(bs_t)],
        out_shape=[jax.ShapeDtypeStruct((t, bw), BF16), jax.ShapeDtypeStruct((t, bw), BF16),
                   jax.ShapeDtypeStruct((1, bw), F32), jax.ShapeDtypeStruct((1, bw), F32),
                   jax.ShapeDtypeStruct(w_sp.shape, F32), jax.ShapeDtypeStruct(bs_t.shape, F32)],
        scratch_shapes=[pltpu.VMEM((c, bw), F32), pltpu.VMEM((c, bw), F32)],
        compiler_params=_cparams(1),
    )(proj, proj, lnw, lnb, w_sp, bs_t, dab, sel)


def _pair_sum(name, grad, other, core):
    _, _, r, c = grad.shape
    tr = _rows(r, 6 * c)

    def body(core_ref, g_ref, o_ref, out_ref):
        out_ref[...] = (g_ref[...].astype(F32) + o_ref[...].astype(F32)).astype(BF16)

    return pl.pallas_call(
        body, name=name,
        grid_spec=pltpu.PrefetchScalarGridSpec(
            num_scalar_prefetch=1, grid=(N_CHIP, r // tr),
            in_specs=[pl.BlockSpec((None, None, tr, c), lambda k, i, core_ref: (k, core_ref[0], i, 0)),
                      pl.BlockSpec((None, tr, c), lambda k, i, core_ref: (k, i, 0))],
            out_specs=pl.BlockSpec((None, tr, c), lambda k, i, core_ref: (k, i, 0))),
        out_shape=jax.ShapeDtypeStruct((N_CHIP, r, c), BF16),
        compiler_params=_cparams(2),
    )(core, grad, other)


def _adamw_math(w, g, m, v):
    m = ADAM_B1 * m + (1.0 - ADAM_B1) * g
    v = ADAM_B2 * v + (1.0 - ADAM_B2) * (g * g)
    m_hat = m / (1.0 - ADAM_B1 ** ADAM_STEP)
    v_hat = v / (1.0 - ADAM_B2 ** ADAM_STEP)
    delta = -ADAM_LR * (m_hat / (jnp.sqrt(v_hat) + ADAM_EPS) + ADAM_WD * w)
    return delta, m, v


def _adamw(name, parts, w, m, v):
    n_parts, r, c = parts.shape
    tr = _rows(r, c * (n_parts * parts.dtype.itemsize + 28), mult=8)

    def body(p_ref, w_ref, m_ref, v_ref, g_ref, d_ref, mo_ref, vo_ref):
        g = p_ref[0].astype(F32)
        for i in range(1, n_parts):
            g = g + p_ref[i].astype(F32)
        g_ref[...] = g
        d_ref[...], mo_ref[...], vo_ref[...] = _adamw_math(w_ref[...], g, m_ref[...], v_ref[...])

    row = pl.BlockSpec((tr, c), lambda i: (i, 0))
    return pl.pallas_call(
        body, name=name, grid=(r // tr,),
        in_specs=[pl.BlockSpec((n_parts, tr, c), lambda i: (0, i, 0)), row, row, row],
        out_specs=[row] * 4,
        out_shape=[jax.ShapeDtypeStruct((r, c), F32)] * 4,
        compiler_params=_cparams(1),
    )(parts, w, m, v)


def kernel(x, p, pre_mix_w, w_in, lb_param, a_norm_w, gmlp_ln_w, gmlp_ln_b, w_spatial, b_spatial, w_out, post_mix_w, pre_ffn_w, w_gate, w_up, w_down, post_ffn_w, w_ple, w_ple_gate, post_ple_w, loss_target, m_pre_mix_w, m_w_in, m_lb_param, m_a_norm_w, m_gmlp_ln_w, m_gmlp_ln_b, m_w_spatial, m_b_spatial, m_w_out, m_post_mix_w, m_pre_ffn_w, m_w_gate, m_w_up, m_w_down, m_post_ffn_w, m_w_ple, m_w_ple_gate, m_post_ple_w, v_pre_mix_w, v_w_in, v_lb_param, v_a_norm_w, v_gmlp_ln_w, v_gmlp_ln_b, v_w_spatial, v_b_spatial, v_w_out, v_post_mix_w, v_pre_ffn_w, v_w_gate, v_w_up, v_w_down, v_post_ffn_w, v_w_ple, v_w_ple_gate, v_post_ple_w):
    big_names = ["w_in", "w_out", "w_gate", "w_up", "w_down", "w_ple", "w_ple_gate"]
    small_names = ["pre_mix_w", "lb_param", "a_norm_w", "gmlp_ln_w", "gmlp_ln_b", "w_spatial", "b_spatial",
                   "post_mix_w", "pre_ffn_w", "post_ffn_w", "post_ple_w"]
    all_names = ["pre_mix_w", "w_in", "lb_param", "a_norm_w", "gmlp_ln_w", "gmlp_ln_b", "w_spatial", "b_spatial",
                 "w_out", "post_mix_w", "pre_ffn_w", "w_gate", "w_up", "w_down", "post_ffn_w", "w_ple", "w_ple_gate",
                 "post_ple_w"]
    env = dict(locals())
    W = {n: env[n] for n in all_names}
    M = {n: env["m_" + n] for n in all_names}
    V = {n: env["v_" + n] for n in all_names}

    xs = x[0]
    ps = p[0, 0]
    tgt = loss_target[0]
    t, d = xs.shape
    aw = a_norm_w.shape[1]
    bw = gmlp_ln_w.shape[1]
    n_heads, n_groups = aw // HEAD, bw // HEAD
    core = lax.axis_index("c").astype(jnp.int32).reshape(1)

    transposed = ("w_gate", "w_up")
    local = lambda a, n: jnp.swapaxes(a, 1, 2)[0] if n in transposed else a[0]
    unlocal = lambda a, n: jnp.swapaxes(a[None], 1, 2) if n in transposed else a[None]
    shard = {n: local(W[n], n) for n in big_names}
    bf = {n: _cast_bf16("cast_" + n, shard[n]) for n in big_names}
    ag_in = _ag_comm([bf["w_in"]])
    _comm_only("ag_w_in", [ag_in])
    win_g = ag_in.results[0]
    n_in = bf["w_in"].shape[1]
    ffl = bf["w_gate"].shape[0]
    n_ple = bf["w_ple"].shape[1]
    ple = ps.shape[1]

    TM, TK = 1024, 1024
    tm = _tile(t, TM)
    tn1 = _tile(d, 1024)

    h1 = _rms_fwd("rms_pre_mix", xs, pre_mix_w)
    once = pl.Buffered(1)
    ag_a = _ag_comm([bf["w_gate"]], mid_frac=0.97)
    proj = _matmul(
        "mm_proj",
        [(h1, pl.BlockSpec((tm, d), lambda n, m, k: (m, 0))),
         (win_g, pl.BlockSpec((None, d, n_in), lambda n, m, k: (n, 0, 0), pipeline_mode=once))],
        [(0, 1, "nn", 0)],
        [(jax.ShapeDtypeStruct((t, N_DEV * n_in), F32), pl.BlockSpec((tm, n_in), lambda n, m, k: (m, n)))],
        (N_DEV, t // tm, 1), (tm, n_in), comms=[ag_a])[0]
    wgate_g = ag_a.results[0]
    lb = _lb_fwd(lb_param)
    ag_b = _ag_comm([bf["w_out"]], mid_frac=0.6)
    a_out, o_raw, states, scores = _hgrn_fwd(proj, lb, a_norm_w, n_heads, comms=[ag_b])
    wout_f = ag_b.results[0].reshape(d, d)
    bs_t = b_spatial[0].T
    w_sp = w_spatial[0]
    col_u = (4 * aw) // bw
    b_out = _gmlp_fwd(proj, gmlp_ln_w, gmlp_ln_b, w_sp, bs_t, n_groups, col_u)
    ab = jnp.concatenate([a_out, b_out], axis=1)
    mix = _mm_plain("mm_mix", ab, wout_f, "nn", F32, TM, 1024, d)
    x1, h2 = _resid_rms("resid_mix", xs, mix, post_mix_w, pre_ffn_w)

    def swiglu(accs, gate_v):
        gf = gate_v.astype(F32)
        return accs[0], gf * _sigmoid(gf) * accs[0]

    tmf = _tile(t, 512)
    blk3 = lambda: pl.BlockSpec((None, tmf, ffl), lambda j, m, k: (j, m, 0))
    ag_c = _ag_comm([bf["w_up"]], mid_frac=0.97)
    gate = _matmul(
        "mm_ffn_gate",
        [(h2, pl.BlockSpec((tm, d), lambda j, m, k: (m, 0))),
         (wgate_g, pl.BlockSpec((None, ffl, d), lambda j, m, k: (j, 0, 0), pipeline_mode=once))],
        [(0, 1, "nt", 0)],
        [(jax.ShapeDtypeStruct((N_DEV, t, ffl), BF16), pl.BlockSpec((None, tm, ffl), lambda j, m, k: (j, m, 0)))],
        (N_DEV, t // tm, 1), (tm, ffl), comms=[ag_c])[0]
    wup_g = ag_c.results[0]
    ag_d = _ag_comm([bf["w_down"]], mid_frac=0.97)
    up, act = _matmul(
        "mm_ffn_up",
        [(h2, pl.BlockSpec((tmf, d), lambda j, m, k: (m, 0))),
         (wup_g, pl.BlockSpec((None, ffl, d), lambda j, m, k: (j, 0, 0), pipeline_mode=once)),
         (gate, blk3())],
        [(0, 1, "nt", 0)],
        [(jax.ShapeDtypeStruct((N_DEV, t, ffl), BF16), blk3()) for _ in range(2)],
        (N_DEV, t // tmf, 1), (tmf, ffl), epilogue=swiglu, comms=[ag_d])
    wdown_g = ag_d.results[0]
    ag_e = _ag_comm([bf["w_ple_gate"], bf["w_ple"]], mid_frac=0.6)
    tn_d = _tile(d, 2048)
    ff = _matmul(
        "mm_ffn_down",
        [(act, pl.BlockSpec((None, tm, ffl), lambda m, n, k: (k, m, 0))),
         (wdown_g, pl.BlockSpec((None, ffl, tn_d), lambda m, n, k: (k, 0, n)))],
        [(0, 1, "nn", 0)],
        [(jax.ShapeDtypeStruct((t, d), F32), pl.BlockSpec((tm, tn_d), lambda m, n, k: (m, n)))],
        (t // tm, d // tn_d, N_DEV), (tm, tn_d), comms=[ag_e])[0]
    wpg_f = ag_e.results[0].reshape(d, d)
    wple_g = ag_e.results[1]
    x2, x2b = _resid_rms("resid_ffn", x1, ff, post_ffn_w, None)

    pgl = _mm_plain("mm_ple_gate", x2b, wpg_f, "nn", F32, TM, 1024, d)
    pe = _matmul(
        "mm_ple",
        [(ps, pl.BlockSpec((tm, ple), lambda m, n, k: (m, 0))), (wple_g, pl.BlockSpec((None, ple, n_ple), lambda m, n, k: (n, 0, 0)))],
        [(0, 1, "nn", 0)],
        [(jax.ShapeDtypeStruct((t, N_DEV * n_ple), F32), pl.BlockSpec((tm, n_ple), lambda m, n, k: (m, n)))],
        (t // tm, N_DEV, 1), (tm, n_ple))[0]
    loss_part, d3, dpe, dpgl, g_post_ple = _ple_loss("ple_loss", x2, pe, pgl, post_ple_w, tgt)

    tkt = _tile(t, TK)
    g_wple = _matmul(
        "mm_dw_ple",
        [(ps, pl.BlockSpec((tkt, ple), lambda n, k: (k, 0))), (dpe, pl.BlockSpec((tkt, n_ple), lambda n, k: (k, n)))],
        [(0, 1, "tn", 0)],
        [(jax.ShapeDtypeStruct((N_DEV, ple, n_ple), BF16), pl.BlockSpec((None, ple, n_ple), lambda n, k: (n, 0, 0)))],
        (N_DEV, t // tkt), (ple, n_ple))[0]
    g_wpg = _mm_plain("mm_dw_ple_gate", x2b, dpgl, "tn", BF16, TM, 1024, t)

    def by_chip(g):
        return g.reshape((N_CHIP, 2) + g.shape[-2:])

    def pair_sums(names, comm):
        return [_pair_sum("pair_sum_" + n, g, o, core) for n, g, o in zip(names, comm.arrays, comm.results)]

    r1_p = _pair_comm([by_chip(g_wpg.reshape(N_DEV, d // N_DEV, d)), by_chip(g_wple)])
    d2 = _mm_plain("mm_d_x2", dpgl, wpg_f, "nt", F32, TM, 512, d, extra=d3, epilogue=lambda accs, e: [accs[0] + e],
                   comms=[r1_p])
    r2_p = _chip_comm(pair_sums(["w_ple_gate", "w_ple"], r1_p))

    dff, g_post_ffn = _norm_bwd("norm_bwd_ffn", d2, ff, post_ffn_w)
    g_wdown = _matmul(
        "mm_dw_down",
        [(act, pl.BlockSpec((None, t, ffl), lambda j, n, k: (j, 0, 0), pipeline_mode=once)),
         (dff, pl.BlockSpec((t, tn1), lambda j, n, k: (0, n)))],
        [(0, 1, "tn", 0)],
        [(jax.ShapeDtypeStruct((N_DEV, ffl, d), BF16), pl.BlockSpec((None, ffl, tn1), lambda j, n, k: (j, 0, n)))],
        (N_DEV, d // tn1, 1), (ffl, tn1), comms=[r2_p])[0]
    r1_d = _pair_comm([by_chip(g_wdown)])

    def swiglu_bwd(accs, gate_v, up_v):
        dact = accs[0]
        gf = gate_v.astype(F32)
        sg = _sigmoid(gf)
        return dact * up_v.astype(F32) * (sg * (1.0 + gf * (1.0 - sg))), dact * (gf * sg)

    dgate, dup = _matmul(
        "mm_d_act",
        [(dff, pl.BlockSpec((tmf, d), lambda j, m, k: (m, 0))),
         (wdown_g, pl.BlockSpec((None, ffl, d), lambda j, m, k: (j, 0, 0), pipeline_mode=once)),
         (gate, blk3()), (up, blk3())],
        [(0, 1, "nt", 0)],
        [(jax.ShapeDtypeStruct((N_DEV, t, ffl), BF16), blk3()) for _ in range(2)],
        (N_DEV, t // tmf, 1), (tmf, ffl), epilogue=swiglu_bwd, comms=[r1_d])
    r2_d = _chip_comm(pair_sums(["w_down"], r1_d))
    tmd = _tile(d, TM)
    def dw_ffn(name, dy, comms):
        return _matmul(
            name,
            [(dy, pl.BlockSpec((None, t, ffl), lambda j, n, k: (j, 0, 0), pipeline_mode=once)),
             (h2, pl.BlockSpec((t, tn1), lambda j, n, k: (0, n)))],
            [(0, 1, "tn", 0)],
            [(jax.ShapeDtypeStruct((N_DEV, ffl, d), BF16), pl.BlockSpec((None, ffl, tn1), lambda j, n, k: (j, 0, n)))],
            (N_DEV, d // tn1, 1), (ffl, tn1), comms=comms)[0]

    g_wgate = dw_ffn("mm_dw_gate", dgate, [r2_d])
    r1_g = _pair_comm([by_chip(g_wgate)])
    g_wup = dw_ffn("mm_dw_up", dup, [r1_g])
    r2_g = _chip_comm(pair_sums(["w_gate"], r1_g))
    r1_u = _pair_comm([by_chip(g_wup)])
    tn1 = _tile(d, 1024)
    dh2 = _matmul(
        "mm_d_h2",
        [(dgate, pl.BlockSpec((None, tm, ffl), lambda m, n, k: (k, m, 0))),
         (wgate_g, pl.BlockSpec((None, ffl, tn1), lambda m, n, k: (k, 0, n))),
         (dup, pl.BlockSpec((None, tm, ffl), lambda m, n, k: (k, m, 0))),
         (wup_g, pl.BlockSpec((None, ffl, tn1), lambda m, n, k: (k, 0, n)))],
        [(0, 1, "nn", 0), (2, 3, "nn", 0)],
        [(jax.ShapeDtypeStruct((t, d), F32), pl.BlockSpec((tm, tn1), lambda m, n, k: (m, n)))],
        (t // tm, d // tn1, N_DEV), (tm, tn1), comms=[r2_g, r1_u])[0]
    r2_u = _chip_comm(pair_sums(["w_up"], r1_u))
    d1, g_pre_ffn, dmix, g_post_mix = _prenorm_bwd("prenorm_bwd_ffn", d2, dh2, x1, pre_ffn_w, mix, post_mix_w)

    g_wout = _mm_plain("mm_dw_out", ab, dmix, "tn", BF16, TM, 1024, t)
    r1_o = _pair_comm([by_chip(g_wout.reshape(N_DEV, d // N_DEV, d))])
    dab = _mm_plain("mm_d_ab", dmix, wout_f, "nt", F32, TM, 1024, d, comms=[r1_o])
    r2_o = _chip_comm(pair_sums(["w_out"], r1_o))
    dq, df, di, dg, dlb, g_a_norm = _hgrn_bwd(proj, lb, a_norm_w, o_raw, states, scores, dab, n_heads, comms=[r2_u])
    du, dv, g_ln_w, g_ln_b, g_wsp, g_bs_t = _gmlp_bwd(proj, gmlp_ln_w, gmlp_ln_b, w_sp, bs_t, dab, n_groups, col_u)
    dproj = jnp.concatenate([dq, df, di, dg, du, dv], axis=1)
    small_grad = {
        "lb_param": _lb_bwd(lb_param, dlb), "a_norm_w": g_a_norm, "gmlp_ln_w": g_ln_w,
        "gmlp_ln_b": g_ln_b, "w_spatial": g_wsp, "b_spatial": g_bs_t.T, "post_mix_w": g_post_mix,
        "pre_ffn_w": g_pre_ffn, "post_ffn_w": g_post_ffn, "post_ple_w": g_post_ple,
    }
    assert small_names[0] == "pre_mix_w"
    pack = lambda get, names: jnp.concatenate([get(n).reshape(-1, LANE) for n in names], axis=0)
    ag_main = _ag_comm([pack(lambda n: small_grad[n], small_names[1:])], mid_frac=0.5)
    g_win = _matmul(
        "mm_dw_in",
        [(h1, pl.BlockSpec((t, tmd), lambda j, m, k: (0, m))),
         (dproj, pl.BlockSpec((t, n_in), lambda j, m, k: (0, j), pipeline_mode=once))],
        [(0, 1, "tn", 0)],
        [(jax.ShapeDtypeStruct((N_DEV, d, n_in), BF16), pl.BlockSpec((None, tmd, n_in), lambda j, m, k: (j, m, 0)))],
        (N_DEV, d // tmd, 1), (tmd, n_in), comms=[ag_main, r2_o])[0]
    r1_in = _pair_comm([by_chip(g_win)])
    _comm_only("rs_pair_w_in", [r1_in])
    r2_in = _chip_comm(pair_sums(["w_in"], r1_in))
    dh1 = _matmul(
        "mm_d_h1",
        [(dproj, pl.BlockSpec((tm, n_in), lambda m, n, k: (m, k))), (win_g, pl.BlockSpec((None, tn1, n_in), lambda m, n, k: (k, n, 0)))],
        [(0, 1, "nt", 0)],
        [(jax.ShapeDtypeStruct((t, d), F32), pl.BlockSpec((tm, tn1), lambda m, n, k: (m, n)))],
        (t // tm, d // tn1, N_DEV), (tm, tn1), comms=[r2_in])[0]
    grad_x, g_pre_mix = _prenorm_bwd("prenorm_bwd_mix", d1, dh1, xs, pre_mix_w)
    ag_pre = _ag_comm([g_pre_mix.reshape(-1, LANE)])
    _comm_only("ag_small_pre", [ag_pre], in_vmem=True)
    g_all = jnp.concatenate([ag_pre.results[0], ag_main.results[0]], axis=1)

    reduced = {
        "w_in": r2_in.results[0], "w_out": r2_o.results[0], "w_gate": r2_g.results[0], "w_up": r2_u.results[0],
        "w_down": r2_d.results[0], "w_ple": r2_p.results[1], "w_ple_gate": r2_p.results[0],
    }
    grads, deltas, new_m, new_v = {}, {}, {}, {}
    for n in big_names:
        res = _adamw("adamw_" + n, reduced[n], shard[n], local(M[n], n), local(V[n], n))
        grads[n], deltas[n], new_m[n], new_v[n] = (unlocal(a, n) for a in res)
    sg, sd, sm, sv = _adamw("adamw_small", g_all, pack(lambda n: W[n], small_names), pack(lambda n: M[n], small_names),
                            pack(lambda n: V[n], small_names))
    off = 0
    for n in small_names:
        rows = W[n].size // LANE
        for src, dst in ((sg, grads), (sd, deltas), (sm, new_m), (sv, new_v)):
            dst[n] = src[off:off + rows].reshape(W[n].shape)
        off += rows

    loss = lax.psum(loss_part[0, 0], ("x", "y", "c"))
    return (loss, grad_x[None], *[grads[n] for n in all_names], *[deltas[n] for n in all_names],
            *[new_m[n] for n in all_names], *[new_v[n] for n in all_names])
```

```python
import functools

import numpy as np
import jax
import jax.numpy as jnp
from jax import lax
from jax.experimental import pallas as pl
from jax.experimental.pallas import tpu as pltpu

F32 = jnp.float32
BF16 = jnp.bfloat16

EPS = 1e-6
HEAD = 128
GLA_CHUNK = 64
GMLP_CHUNK = 128
N_DEV = 8
N_CHIP = 4
LANE = 128
VMEM_LIMIT = 56 * 1024 * 1024
HGRN_ROWS = 512
ROW_TILE = 128
EPILOGUE_ROWS = 256
HGRN_UNROLL = 1
HGRN_HEADS = 8

ADAM_LR = 0.001
ADAM_B1 = 0.9
ADAM_B2 = 0.999
ADAM_EPS = 1e-08
ADAM_WD = 0.01
ADAM_STEP = 10

MESH = pl.DeviceIdType.MESH

_DIMS = {
    "nn": (((1,), (0,)), ((), ())),
    "nt": (((1,), (1,)), ((), ())),
    "tn": (((0,), (0,)), ((), ())),
}


def _tile(dim, pref):
    return pref if dim % pref == 0 else dim


def _rows(r, bytes_per_row, budget=18 * 1024 * 1024, mult=16):
    best = None
    for cand in range(mult, r + 1, mult):
        if r % cand == 0 and cand * bytes_per_row <= budget:
            best = cand
    return best if best is not None else r


def _cparams(n_axes):
    return pltpu.CompilerParams(dimension_semantics=("arbitrary",) * n_axes, vmem_limit_bytes=VMEM_LIMIT)


def _dot(a, b, form="nn"):
    return lax.dot_general(a.astype(BF16), b.astype(BF16), _DIMS[form], preferred_element_type=F32)


def _split3(x):
    hi = x.astype(BF16)
    r = x - hi.astype(F32)
    mid = r.astype(BF16)
    lo = (r - mid.astype(F32)).astype(BF16)
    return hi, mid, lo


def _dot_exact_l(c, x):
    hi, mid, lo = _split3(x)
    d = lambda y: lax.dot_general(c, y, _DIMS["nn"], preferred_element_type=F32)
    return d(hi) + d(mid) + d(lo)


def _dot_exact_r(x, c):
    hi, mid, lo = _split3(x)
    d = lambda y: lax.dot_general(y, c, _DIMS["nn"], preferred_element_type=F32)
    return d(hi) + d(mid) + d(lo)


def _sigmoid(x):
    return 1.0 / (1.0 + jnp.exp(-x))


def _gelu(x):
    return 0.5 * x * (1.0 + lax.erf(x * 0.7071067811865476))


def _gelu_grad(x):
    cdf = 0.5 * (1.0 + lax.erf(x * 0.7071067811865476))
    pdf = jnp.exp(-0.5 * x * x) * 0.3989422804014327
    return cdf + x * pdf


def _position():
    return lax.axis_index("x"), lax.axis_index("y"), lax.axis_index("c")


def _linear_step(grid):
    step = 0
    for ax, n in enumerate(grid):
        step = step * n + pl.program_id(ax)
    return step


class _Comm:
    def __init__(self, arrays, out_shapes, sem_shapes, phases):
        self.arrays, self.out_shapes, self.sem_shapes, self.phases = list(arrays), list(out_shapes), list(sem_shapes), phases
        self.results = None


class _CommLayout:
    def __init__(self, comms, space=pl.ANY):
        self.comms = list(comms)
        self.arrays = [a for c in self.comms for a in c.arrays]
        self.out_shapes = [s for c in self.comms for s in c.out_shapes]
        self.sem_shapes = [s for c in self.comms for s in c.sem_shapes]
        self.n_in, self.n_out = len(self.arrays), len(self.out_shapes)
        self.in_specs = [pl.BlockSpec(memory_space=space)] * self.n_in
        self.out_specs = [pl.BlockSpec(memory_space=space)] * self.n_out

    def run(self, cin, cout, csem, step, n_steps, post):
        i = o = s = 0
        for c in self.comms:
            ins, outs, sems = cin[i:i + len(c.arrays)], cout[o:o + len(c.out_shapes)], csem[s:s + len(c.sem_shapes)]
            i, o, s = i + len(c.arrays), o + len(c.out_shapes), s + len(c.sem_shapes)
            for frac, fn in c.phases:
                if (frac is None) != post:
                    continue
                due = n_steps - 1 if frac is None else max(0, min(int(frac * n_steps), n_steps - 2))
                if n_steps == 1:
                    fn(ins, outs, sems)
                else:
                    pl.when(step == due)(functools.partial(fn, ins, outs, sems))

    def deliver(self, results):
        o = 0
        for c in self.comms:
            c.results = list(results[o:o + len(c.out_shapes)])
            o += len(c.out_shapes)


def _comm_only(name, comms, in_vmem=False):
    lay = _CommLayout(comms, pltpu.VMEM if in_vmem else pl.ANY)

    def body(*refs):
        cin, cout, csem = refs[:lay.n_in], refs[lay.n_in:lay.n_in + lay.n_out], refs[lay.n_in + lay.n_out:]
        lay.run(cin, cout, csem, 0, 1, post=False)
        lay.run(cin, cout, csem, 0, 1, post=True)

    res = pl.pallas_call(
        body, name=name, in_specs=lay.in_specs, out_specs=lay.out_specs, out_shape=lay.out_shapes,
        scratch_shapes=lay.sem_shapes,
    )(*lay.arrays)
    lay.deliver(res)


def _ag_comm(shards, mid_frac=0.0):
    n = len(shards)
    per = N_DEV - 1

    def tools(ins, outs, sems):
        send_sems, recv_sems, local_sems = sems
        x, y, c = _position()
        me, sibling = (x, y, c), (x, y, 1 - c)
        chips = [(1 - x, y), (x, 1 - y), (1 - x, 1 - y)]

        def copy(a, k, block, to, from_shard=False):
            dst = outs[a].at[4 * block[0] + 2 * block[1] + block[2]]
            return pltpu.make_async_remote_copy(
                src_ref=ins[a] if from_shard else dst, dst_ref=dst,
                send_sem=send_sems.at[a * per + k], recv_sem=recv_sems.at[a * per + k],
                device_id=to, device_id_type=MESH)

        def local(a):
            return pltpu.make_async_copy(ins[a], outs[a].at[4 * x + 2 * y + c], local_sems.at[a])

        return me, sibling, chips, c, copy, local

    def first(ins, outs, sems):
        me, sibling, chips, c, copy, local = tools(ins, outs, sems)
        for a in range(n):
            local(a).start()
            copy(a, 0, me, sibling, True).start()
            for j, chip in enumerate(chips):
                copy(a, 1 + j, me, (*chip, c), True).start()

    def middle(ins, outs, sems):
        me, sibling, chips, c, copy, local = tools(ins, outs, sems)
        for a in range(n):
            for j, chip in enumerate(chips):
                copy(a, 1 + j, (*chip, c), me).wait_recv()
                copy(a, 4 + j, (*chip, c), sibling).start()

    def last(ins, outs, sems):
        me, sibling, chips, c, copy, local = tools(ins, outs, sems)
        for a in range(n):
            copy(a, 0, sibling, me).wait_recv()
            copy(a, 0, me, sibling, True).wait_send()
            for j, chip in enumerate(chips):
                copy(a, 4 + j, (*chip, 1 - c), me).wait_recv()
                copy(a, 1 + j, me, (*chip, c), True).wait_send()
                copy(a, 4 + j, (*chip, c), sibling).wait_send()
            local(a).wait()

    return _Comm(
        shards, [jax.ShapeDtypeStruct((N_DEV,) + s.shape, s.dtype) for s in shards],
        [pltpu.SemaphoreType.DMA((n * per,)), pltpu.SemaphoreType.DMA((n * per,)), pltpu.SemaphoreType.DMA((n,))],
        [(0.0, first), (mid_frac, middle), (None, last)])


def _pair_comm(grads):
    n = len(grads)

    def copies(ins, outs, sems):
        send_sems, recv_sems = sems
        x, y, c = _position()
        return [pltpu.make_async_remote_copy(
            src_ref=ins[a].at[k, 1 - c], dst_ref=outs[a].at[k],
            send_sem=send_sems.at[a * N_CHIP + k], recv_sem=recv_sems.at[a * N_CHIP + k],
            device_id=(x, y, 1 - c), device_id_type=MESH) for a in range(n) for k in range(N_CHIP)]

    def first(ins, outs, sems):
        for cp in copies(ins, outs, sems):
            cp.start()

    def last(ins, outs, sems):
        for cp in copies(ins, outs, sems):
            cp.wait()

    return _Comm(
        grads, [jax.ShapeDtypeStruct((N_CHIP,) + g.shape[2:], g.dtype) for g in grads],
        [pltpu.SemaphoreType.DMA((n * N_CHIP,)), pltpu.SemaphoreType.DMA((n * N_CHIP,))],
        [(0.0, first), (None, last)])


def _chip_comm(sums):
    n = len(sums)
    per = N_CHIP - 1

    def copies(ins, outs, sems):
        send_sems, recv_sems, local_sems = sems
        x, y, c = _position()
        my_chip = 2 * x + y
        cps = []
        for a in range(n):
            cps.append(pltpu.make_async_copy(ins[a].at[my_chip], outs[a].at[my_chip], local_sems.at[a]))
            for j, (px, py) in enumerate([(1 - x, y), (x, 1 - y), (1 - x, 1 - y)]):
                cps.append(pltpu.make_async_remote_copy(
                    src_ref=ins[a].at[2 * px + py], dst_ref=outs[a].at[my_chip],
                    send_sem=send_sems.at[a * per + j], recv_sem=recv_sems.at[a * per + j],
                    device_id=(px, py, c), device_id_type=MESH))
        return cps

    def first(ins, outs, sems):
        for cp in copies(ins, outs, sems):
            cp.start()

    def last(ins, outs, sems):
        for cp in copies(ins, outs, sems):
            cp.wait()

    return _Comm(
        sums, [jax.ShapeDtypeStruct(s.shape, s.dtype) for s in sums],
        [pltpu.SemaphoreType.DMA((n * per,)), pltpu.SemaphoreType.DMA((n * per,)), pltpu.SemaphoreType.DMA((n,))],
        [(0.0, first), (None, last)])


def _matmul(name, operands, pairs, outs, grid, acc_shape, n_slots=1, epilogue=None, comms=()):
    used = sorted({i for p in pairs for i in p[:2]})
    n_op = len(operands)
    n_out = len(outs)
    k_axis = len(grid) - 1
    n_k = grid[-1]
    lay = _CommLayout(comms)

    direct = n_k == 1 and epilogue is None
    n_acc = 0 if direct else 1

    def body(*refs):
        ops = refs[:n_op]
        out_refs = refs[n_op + lay.n_in:n_op + lay.n_in + n_out]
        acc = None if direct else refs[n_op + lay.n_in + n_out + lay.n_out]
        k = pl.program_id(k_axis)
        step = _linear_step(grid)
        cin = refs[n_op:n_op + lay.n_in]
        cout = refs[n_op + lay.n_in + n_out:n_op + lay.n_in + n_out + lay.n_out]
        csem = refs[n_op + lay.n_in + n_out + lay.n_out + n_acc:]
        lay.run(cin, cout, csem, step, int(np.prod(grid)), post=False)

        if n_k > 1:
            @pl.when(k == 0)
            def _():
                acc[...] = jnp.zeros_like(acc)

        vals = {i: ops[i][...] for i in used}
        vals = {i: (v if v.dtype == BF16 else v.astype(BF16)) for i, v in vals.items()}
        for s in range(n_slots):
            tot = None
            for ia, ib, form, slot in pairs:
                if slot != s:
                    continue
                d = lax.dot_general(vals[ia], vals[ib], _DIMS[form], preferred_element_type=F32)
                tot = d if tot is None else tot + d
            if direct:
                out_refs[s][...] = tot.astype(out_refs[s].dtype)
            elif n_k == 1:
                acc[s] = tot
            else:
                acc[s] += tot

        def finish():
            rows = acc_shape[0]
            chunk = EPILOGUE_ROWS if (epilogue is not None and rows % EPILOGUE_ROWS == 0) else rows
            for r0 in range(0, rows, chunk):
                sl = slice(r0, r0 + chunk)
                accs = [acc[s, sl, :] for s in range(n_slots)]
                extra = [ops[i][sl, :] for i in range(n_op) if i not in used]
                res = epilogue(accs, *extra) if epilogue is not None else accs
                for o, v in zip(out_refs, res):
                    o[sl, :] = v.astype(o.dtype)

        if n_k > 1:
            pl.when(k == n_k - 1)(finish)
        elif not direct:
            finish()

        lay.run(cin, cout, csem, step, int(np.prod(grid)), post=True)

    res = pl.pallas_call(
        body,
        name=name,
        grid=grid,
        in_specs=[s for _, s in operands] + lay.in_specs,
        out_specs=[s for _, s in outs] + lay.out_specs,
        out_shape=[s for s, _ in outs] + lay.out_shapes,
        scratch_shapes=([] if direct else [pltpu.VMEM((n_slots,) + tuple(acc_shape), F32)]) + lay.sem_shapes,
        compiler_params=_cparams(len(grid)),
    )(*[a for a, _ in operands], *lay.arrays)
    lay.deliver(res[n_out:])
    return res[:n_out]


def _mm_plain(name, a, b, form, out_dtype, tm, tn, tk, extra=None, epilogue=None, comms=()):
    if form == "nn":
        (M, K), N = a.shape, b.shape[1]
    elif form == "nt":
        (M, K), N = a.shape, b.shape[0]
    else:
        (K, M), N = a.shape, b.shape[1]
    tm, tn, tk = _tile(M, tm), _tile(N, tn), _tile(K, tk)
    a_spec = pl.BlockSpec((tk, tm), lambda m, n, k: (k, m)) if form == "tn" else pl.BlockSpec((tm, tk), lambda m, n, k: (m, k))
    b_spec = pl.BlockSpec((tn, tk), lambda m, n, k: (n, k)) if form == "nt" else pl.BlockSpec((tk, tn), lambda m, n, k: (k, n))
    operands = [(a, a_spec), (b, b_spec)]
    if extra is not None:
        operands.append((extra, pl.BlockSpec((tm, tn), lambda m, n, k: (m, n))))
    out = (jax.ShapeDtypeStruct((M, N), out_dtype), pl.BlockSpec((tm, tn), lambda m, n, k: (m, n)))
    return _matmul(name, operands, [(0, 1, form, 0)], [out], (M // tm, N // tn, K // tk), (tm, tn), epilogue=epilogue,
                   comms=comms)[0]


def _cast_bf16(name, w):
    r, c = w.shape
    tr = _rows(r, 6 * c)

    def body(w_ref, o_ref):
        o_ref[...] = w_ref[...].astype(BF16)

    return pl.pallas_call(
        body, name=name, grid=(r // tr,),
        in_specs=[pl.BlockSpec((tr, c), lambda i: (i, 0))],
        out_specs=pl.BlockSpec((tr, c), lambda i: (i, 0)),
        out_shape=jax.ShapeDtypeStruct((r, c), BF16),
        compiler_params=_cparams(1),
    )(w)


def _rms_stats(x):
    r = lax.rsqrt(jnp.mean(x * x, axis=-1, keepdims=True) + EPS)
    return x * r, r


def _rms_bwd(xhat, r, w, dy):
    dxh = dy * w
    return r * (dxh - xhat * jnp.mean(dxh * xhat, axis=-1, keepdims=True))


def _row_spec(tr, d):
    return pl.BlockSpec((tr, d), lambda i: (i, 0))


def _vec_spec(d):
    return pl.BlockSpec((1, d), lambda i: (0, 0))


def _rms_fwd(name, x, w):
    t, d = x.shape
    tr = _tile(t, ROW_TILE)

    def body(x_ref, w_ref, h_ref):
        xh, _ = _rms_stats(x_ref[...])
        h_ref[...] = (xh * w_ref[...]).astype(BF16)

    return pl.pallas_call(
        body, name=name, grid=(t // tr,),
        in_specs=[_row_spec(tr, d), _vec_spec(d)],
        out_specs=_row_spec(tr, d),
        out_shape=jax.ShapeDtypeStruct((t, d), BF16),
        compiler_params=_cparams(1),
    )(x, w)


def _resid_rms(name, xres, y, w_post, w_next):
    t, d = xres.shape
    tr = _tile(t, ROW_TILE)
    has_next = w_next is not None

    def body(*refs):
        if has_next:
            x_ref, y_ref, wp_ref, wn_ref, xo_ref, h_ref = refs
        else:
            x_ref, y_ref, wp_ref, xo_ref, h_ref = refs
        yh, _ = _rms_stats(y_ref[...])
        xn = x_ref[...] + yh * wp_ref[...]
        xo_ref[...] = xn
        if has_next:
            xh, _ = _rms_stats(xn)
            h_ref[...] = (xh * wn_ref[...]).astype(BF16)
        else:
            h_ref[...] = xn.astype(BF16)

    ins = [xres, y, w_post] + ([w_next] if has_next else [])
    in_specs = [_row_spec(tr, d), _row_spec(tr, d), _vec_spec(d)] + ([_vec_spec(d)] if has_next else [])
    return pl.pallas_call(
        body, name=name, grid=(t // tr,),
        in_specs=in_specs,
        out_specs=[_row_spec(tr, d), _row_spec(tr, d)],
        out_shape=[jax.ShapeDtypeStruct((t, d), F32), jax.ShapeDtypeStruct((t, d), BF16)],
        compiler_params=_cparams(1),
    )(*ins)


def _ple_loss(name, x2, pe, pgl, w_pp, tgt):
    t, d = x2.shape
    tr = _tile(t, ROW_TILE)

    def body(x2_ref, pe_ref, pgl_ref, w_ref, tgt_ref, loss_ref, d3_ref, dpe_ref, dpgl_ref, dw_ref):
        @pl.when(pl.program_id(0) == 0)
        def _():
            loss_ref[...] = jnp.zeros_like(loss_ref)
            dw_ref[...] = jnp.zeros_like(dw_ref)

        pe_v = pe_ref[...]
        s = _sigmoid(pgl_ref[...])
        y = pe_v * s
        yh, r = _rms_stats(y)
        w = w_ref[...]
        err = x2_ref[...] + yh * w - tgt_ref[...]
        loss_ref[...] += 0.5 * jnp.sum(jnp.mean(err * err, axis=-1, keepdims=True), axis=0, keepdims=True)
        d3 = err * (1.0 / d)
        d3_ref[...] = d3
        dw_ref[...] += jnp.sum(d3 * yh, axis=0, keepdims=True)
        dy = _rms_bwd(yh, r, w, d3)
        dpe_ref[...] = (dy * s).astype(BF16)
        dpgl_ref[...] = (dy * pe_v * s * (1.0 - s)).astype(BF16)

    return pl.pallas_call(
        body, name=name, grid=(t // tr,),
        in_specs=[_row_spec(tr, d), _row_spec(tr, d), _row_spec(tr, d), _vec_spec(d), _row_spec(tr, d)],
        out_specs=[pl.BlockSpec((1, 1), lambda i: (0, 0)), _row_spec(tr, d), _row_spec(tr, d), _row_spec(tr, d), _vec_spec(d)],
        out_shape=[jax.ShapeDtypeStruct((1, 1), F32), jax.ShapeDtypeStruct((t, d), F32),
                   jax.ShapeDtypeStruct((t, d), BF16), jax.ShapeDtypeStruct((t, d), BF16),
                   jax.ShapeDtypeStruct((1, d), F32)],
        compiler_params=_cparams(1),
    )(x2, pe, pgl, w_pp, tgt)


def _norm_bwd(name, dres, y, w_post):
    t, d = dres.shape
    tr = _tile(t, ROW_TILE)

    def body(d_ref, y_ref, w_ref, dy_ref, dw_ref):
        @pl.when(pl.program_id(0) == 0)
        def _():
            dw_ref[...] = jnp.zeros_like(dw_ref)

        dv = d_ref[...]
        yh, r = _rms_stats(y_ref[...])
        dw_ref[...] += jnp.sum(dv * yh, axis=0, keepdims=True)
        dy_ref[...] = _rms_bwd(yh, r, w_ref[...], dv).astype(BF16)

    return pl.pallas_call(
        body, name=name, grid=(t // tr,),
        in_specs=[_row_spec(tr, d), _row_spec(tr, d), _vec_spec(d)],
        out_specs=[_row_spec(tr, d), _vec_spec(d)],
        out_shape=[jax.ShapeDtypeStruct((t, d), BF16), jax.ShapeDtypeStruct((1, d), F32)],
        compiler_params=_cparams(1),
    )(dres, y, w_post)


def _prenorm_bwd(name, dres, dh, xin, w_pre, y=None, w_post=None):
    t, d = dres.shape
    tr = _tile(t, ROW_TILE)
    two = y is not None

    def body(*refs):
        if two:
            d_ref, dh_ref, x_ref, wpre_ref, y_ref, wpost_ref, do_ref, dwpre_ref, dy_ref, dwpost_ref = refs
        else:
            d_ref, dh_ref, x_ref, wpre_ref, do_ref, dwpre_ref = refs

        @pl.when(pl.program_id(0) == 0)
        def _():
            dwpre_ref[...] = jnp.zeros_like(dwpre_ref)
            if two:
                dwpost_ref[...] = jnp.zeros_like(dwpost_ref)

        dhv = dh_ref[...]
        xh, r = _rms_stats(x_ref[...])
        dwpre_ref[...] += jnp.sum(dhv * xh, axis=0, keepdims=True)
        dout = d_ref[...] + _rms_bwd(xh, r, wpre_ref[...], dhv)
        do_ref[...] = dout
        if two:
            yh, ry = _rms_stats(y_ref[...])
            dwpost_ref[...] += jnp.sum(dout * yh, axis=0, keepdims=True)
            dy_ref[...] = _rms_bwd(yh, ry, wpost_ref[...], dout).astype(BF16)

    ins = [dres, dh, xin, w_pre] + ([y, w_post] if two else [])
    in_specs = [_row_spec(tr, d)] * 3 + [_vec_spec(d)] + ([_row_spec(tr, d), _vec_spec(d)] if two else [])
    out_specs = [_row_spec(tr, d), _vec_spec(d)] + ([_row_spec(tr, d), _vec_spec(d)] if two else [])
    out_shape = [jax.ShapeDtypeStruct((t, d), F32), jax.ShapeDtypeStruct((1, d), F32)]
    if two:
        out_shape += [jax.ShapeDtypeStruct((t, d), BF16), jax.ShapeDtypeStruct((1, d), F32)]
    return pl.pallas_call(
        body, name=name, grid=(t // tr,),
        in_specs=in_specs, out_specs=out_specs, out_shape=out_shape,
        compiler_params=_cparams(1),
    )(*ins)


_LEVELS = (32, 16, 8, 4, 2, 1)
_N_CUM = 3 + 2 * len(_LEVELS)


def _hgrn_constants():
    c = GLA_CHUNK
    idx = np.arange(c)
    t, r = idx[:, None], idx[None, :]
    mats = [(r <= t), (r > t), np.ones((c, c), bool)]
    lq, lk, masks = [], [], []
    for h in _LEVELS:
        blk, pos = idx // (2 * h), idx % (2 * h)
        mid = blk * 2 * h + h - 1
        upper, lower = pos >= h, pos < h
        lq.append(upper[:, None] & (r > mid[:, None]) & (r <= t))
        lk.append(lower[:, None] & (r > t) & (r <= mid[:, None]))
        masks.append((blk[:, None] == blk[None, :]) & upper[:, None] & lower[None, :])
    cum = np.concatenate(mats + lq + lk, axis=0).astype(np.float32)
    rev = (r >= t).astype(np.float32)
    return (jnp.asarray(cum, BF16), jnp.asarray(rev, BF16), jnp.asarray(np.stack(masks).astype(np.float32)))


def _hgrn_gates(qp, fp, lb):
    sq = _sigmoid(qp)
    q = qp * sq
    sg = _sigmoid(fp)
    f = lb + (1.0 - lb) * sg
    k = 1.0 - f
    logf = jnp.log(jnp.maximum(f, 1e-30))
    return q, sq, sg, f, k, logf


def _hgrn_decays(cum_ref, logf):
    c = GLA_CHUNK
    e = jnp.exp(_dot_exact_l(cum_ref[...], logf))
    part = lambda i: e[i * c:(i + 1) * c]
    n = len(_LEVELS)
    return part(0), part(1), part(2), [part(3 + i) for i in range(n)], [part(3 + n + i) for i in range(n)]


def _hgrn_fwd(proj, lb, nw, n_heads, comms=()):
    t = proj.shape[0]
    aw = n_heads * HEAD
    rb = _tile(t, HGRN_ROWS)
    c = GLA_CHUNK
    n_sub = rb // c
    cum, _, masks = _hgrn_constants()
    lay = _CommLayout(comms)
    hp = HGRN_HEADS if n_heads % HGRN_HEADS == 0 else 1
    wd = hp * HEAD
    grid = (n_heads // hp, t // rb)

    def body(*refs):
        q_ref, f_ref, i_ref, g_ref, lb_ref, nw_ref, cum_ref, m_ref = refs[:8]
        cin = refs[8:8 + lay.n_in]
        a_ref, o_ref, s_ref, sc_ref = refs[8 + lay.n_in:12 + lay.n_in]
        cout = refs[12 + lay.n_in:12 + lay.n_in + lay.n_out]
        st = refs[12 + lay.n_in + lay.n_out]
        csem = refs[13 + lay.n_in + lay.n_out:]
        step = _linear_step(grid)
        lay.run(cin, cout, csem, step, grid[0] * grid[1], post=False)

        @pl.when(pl.program_id(1) == 0)
        def _():
            st[...] = jnp.zeros_like(st)

        lbv = lb_ref[...]
        nwv = nw_ref[...]
        eye = (lax.broadcasted_iota(jnp.int32, (c, c), 0) == lax.broadcasted_iota(jnp.int32, (c, c), 1)).astype(F32)
        heads = range(hp)
        hs = lambda a, h: a[:, h * HEAD:(h + 1) * HEAD]

        def chunk(j, carry):
            rows = pl.ds(pl.multiple_of(j * c, c), c)
            q, _, _, _, k, logf = _hgrn_gates(q_ref[rows, :], f_ref[rows, :], lbv)
            v = i_ref[rows, :]
            eb, ebe, eend, eq, ek = _hgrn_decays(cum_ref, logf)
            qt, kt, qk = q * eb, k * ebe, q * k
            s_in = [st[h] for h in heads]
            for h in heads:
                s_ref[h, j] = s_in[h]
            inter = [_dot(hs(qt, h), s_in[h], "nt") for h in heads]
            for h in heads:
                st[h] = s_in[h] * hs(eend, h)[0:1] + _dot(hs(v, h), hs(kt, h), "tn")
            scores = [eye * jnp.sum(hs(qk, h), axis=-1, keepdims=True) for h in heads]
            for lvl in range(len(_LEVELS)):
                ql, kl = q * eq[lvl], k * ek[lvl]
                for h in heads:
                    scores[h] = scores[h] + m_ref[lvl] * _dot(hs(ql, h), hs(kl, h), "nt")
            gv = g_ref[rows, :]
            gate = nwv * (gv * _sigmoid(gv))
            for h in heads:
                sc_ref[h, rows, :] = scores[h]
                o = inter[h] + _dot(scores[h], hs(v, h))
                o_ref[rows, h * HEAD:(h + 1) * HEAD] = o
                r = lax.rsqrt(jnp.mean(o * o, axis=-1, keepdims=True) + EPS)
                a_ref[rows, h * HEAD:(h + 1) * HEAD] = (o * r * hs(gate, h)).astype(BF16)
            return carry

        lax.fori_loop(0, n_sub, chunk, 0, unroll=HGRN_UNROLL)
        lay.run(cin, cout, csem, step, grid[0] * grid[1], post=True)

    n_hb = n_heads // hp
    col = lambda base: pl.BlockSpec((rb, wd), lambda h, r: (r, base * n_hb + h))
    vec = pl.BlockSpec((1, wd), lambda h, r: (0, h))
    res = pl.pallas_call(
        body, name="hgrn2_fwd", grid=grid,
        in_specs=[col(0), col(1), col(2), col(3), vec, vec,
                  pl.BlockSpec(cum.shape, lambda h, r: (0, 0)), pl.BlockSpec(masks.shape, lambda h, r: (0, 0, 0))] + lay.in_specs,
        out_specs=[pl.BlockSpec((rb, wd), lambda h, r: (r, h)), pl.BlockSpec((rb, wd), lambda h, r: (r, h)),
                   pl.BlockSpec((hp, n_sub, HEAD, HEAD), lambda h, r: (h, r, 0, 0)),
                   pl.BlockSpec((hp, rb, c), lambda h, r: (h, r, 0))] + lay.out_specs,
        out_shape=[jax.ShapeDtypeStruct((t, aw), BF16), jax.ShapeDtypeStruct((t, aw), F32),
                   jax.ShapeDtypeStruct((n_heads, t // c, HEAD, HEAD), F32),
                   jax.ShapeDtypeStruct((n_heads, t, c), F32)] + lay.out_shapes,
        scratch_shapes=[pltpu.VMEM((hp, HEAD, HEAD), F32)] + lay.sem_shapes,
        compiler_params=_cparams(2),
    )(proj, proj, proj, proj, lb, nw, cum, masks, *lay.arrays)
    lay.deliver(res[4:])
    return res[:4]


def _hgrn_bwd(proj, lb, nw, o_raw, states, scores, dab, n_heads, comms=()):
    t = proj.shape[0]
    aw = n_heads * HEAD
    rb = _tile(t, HGRN_ROWS)
    c = GLA_CHUNK
    n_sub = rb // c
    n_rb = t // rb
    cum, rev, masks = _hgrn_constants()
    lay = _CommLayout(comms)
    hp = HGRN_HEADS if n_heads % HGRN_HEADS == 0 else 1
    wd = hp * HEAD
    grid = (n_heads // hp, n_rb)

    def body(*refs):
        q_ref, f_ref, i_ref, g_ref, lb_ref, nw_ref, o_ref, s_ref, sc_ref, da_ref, cum_ref, rev_ref, m_ref = refs[:13]
        cin = refs[13:13 + lay.n_in]
        dq_ref, df_ref, di_ref, dg_ref, dlb_ref, dnw_ref = refs[13 + lay.n_in:19 + lay.n_in]
        cout = refs[19 + lay.n_in:19 + lay.n_in + lay.n_out]
        dst = refs[19 + lay.n_in + lay.n_out]
        csem = refs[20 + lay.n_in + lay.n_out:]
        step = _linear_step(grid)
        lay.run(cin, cout, csem, step, grid[0] * grid[1], post=False)

        @pl.when(pl.program_id(1) == 0)
        def _():
            dst[...] = jnp.zeros_like(dst)
            dlb_ref[...] = jnp.zeros_like(dlb_ref)
            dnw_ref[...] = jnp.zeros_like(dnw_ref)

        lbv = lb_ref[...]
        nwv = nw_ref[...]
        ri = lax.broadcasted_iota(jnp.int32, (c, c), 0)
        ci = lax.broadcasted_iota(jnp.int32, (c, c), 1)
        eye = (ri == ci).astype(F32)
        causal = (ci <= ri).astype(F32)
        last_row = (lax.broadcasted_iota(jnp.int32, (c, wd), 0) == c - 1).astype(F32)
        heads = range(hp)
        hs = lambda a, h: a[:, h * HEAD:(h + 1) * HEAD]
        wide = lambda parts: parts[0] if hp == 1 else jnp.concatenate(parts, axis=1)

        def head_mean(a):
            return wide([jnp.broadcast_to(jnp.mean(hs(a, h), axis=-1, keepdims=True), (c, HEAD)) for h in heads])

        def chunk(jj, carry):
            j = n_sub - 1 - jj
            rows = pl.ds(pl.multiple_of(j * c, c), c)
            qp = q_ref[rows, :]
            q, sq, sg, f, k, logf = _hgrn_gates(qp, f_ref[rows, :], lbv)
            v = i_ref[rows, :]
            gv = g_ref[rows, :]
            eb, ebe, eend, eq, ek = _hgrn_decays(cum_ref, logf)
            s_in = [s_ref[h, j] for h in heads]
            a_sc = [sc_ref[h, rows, :] for h in heads]
            dsn = [dst[h] for h in heads]
            o = o_ref[rows, :]
            r = lax.rsqrt(head_mean(o * o) + EPS)
            oh = o * r
            sgg = _sigmoid(gv)
            sil = gv * sgg
            da = da_ref[rows, :]
            dg_ref[rows, :] = (da * oh * nwv * (sgg * (1.0 + gv * (1.0 - sgg)))).astype(BF16)
            dnw_ref[...] += jnp.sum(da * oh * sil, axis=0, keepdims=True)
            doh = da * nwv * sil
            do = r * (doh - oh * head_mean(doh * oh))
            kt = k * ebe
            qt = q * eb
            d_sc = [_dot(hs(do, h), hs(v, h), "nt") * causal for h in heads]
            dqt = wide([_dot(hs(do, h), s_in[h]) for h in heads])
            dkt = wide([_dot(hs(v, h), dsn[h]) for h in heads])
            for h in heads:
                dst[h] = dsn[h] * hs(eend, h)[0:1] + _dot(hs(do, h), hs(qt, h), "tn")
            di_ref[rows, :] = wide([_dot(a_sc[h], hs(do, h), "tn") + _dot(hs(kt, h), dsn[h], "nt") for h in heads]).astype(BF16)
            diag = wide([jnp.broadcast_to(jnp.sum(d_sc[h] * eye, axis=-1, keepdims=True), (c, HEAD)) for h in heads])
            dq = dqt * eb
            dk = dkt * ebe
            db = q * dq - k * dk
            dq = dq + diag * k
            dk = dk + diag * q
            for lvl in range(len(_LEVELS)):
                ql = (q * eq[lvl]).astype(BF16)
                kl = (k * ek[lvl]).astype(BF16)
                dm = [(m_ref[lvl] * d_sc[h]).astype(BF16) for h in heads]
                gq = wide([_dot(dm[h], hs(kl, h)) for h in heads])
                gk = wide([_dot(dm[h], hs(ql, h), "tn") for h in heads])
                dq = dq + gq * eq[lvl]
                dk = dk + gk * ek[lvl]
                db = db + ql.astype(F32) * gq - kl.astype(F32) * gk
            state_term = wide([jnp.sum(s_in[h] * dsn[h], axis=0, keepdims=True) for h in heads])
            extra = jnp.sum(dkt * kt, axis=0, keepdims=True) + eend[0:1] * state_term
            db = db + last_row * extra
            dlogf = _dot_exact_l(rev_ref[...], db)
            dfv = jnp.where(f > 1e-30, dlogf / f, 0.0) - dk
            df_ref[rows, :] = (dfv * (1.0 - lbv) * sg * (1.0 - sg)).astype(BF16)
            dlb_ref[...] += jnp.sum(dfv * (1.0 - sg), axis=0, keepdims=True)
            dq_ref[rows, :] = (dq * (sq * (1.0 + qp * (1.0 - sq)))).astype(BF16)
            return carry

        lax.fori_loop(0, n_sub, chunk, 0, unroll=HGRN_UNROLL)
        lay.run(cin, cout, csem, step, grid[0] * grid[1], post=True)

    n_hb = n_heads // hp
    col = lambda base: pl.BlockSpec((rb, wd), lambda h, r: (n_rb - 1 - r, base * n_hb + h))
    blk = pl.BlockSpec((rb, wd), lambda h, r: (n_rb - 1 - r, h))
    vec = pl.BlockSpec((1, wd), lambda h, r: (0, h))
    const = lambda a: pl.BlockSpec(a.shape, lambda h, r: (0,) * a.ndim)
    res = pl.pallas_call(
        body, name="hgrn2_bwd", grid=grid,
        in_specs=[col(0), col(1), col(2), col(3), vec, vec, blk,
                  pl.BlockSpec((hp, n_sub, HEAD, HEAD), lambda h, r: (h, n_rb - 1 - r, 0, 0)),
                  pl.BlockSpec((hp, rb, c), lambda h, r: (h, n_rb - 1 - r, 0)),
                  blk, const(cum), const(rev), const(masks)] + lay.in_specs,
        out_specs=[blk, blk, blk, blk, vec, vec] + lay.out_specs,
        out_shape=[jax.ShapeDtypeStruct((t, aw), BF16)] * 4 + [jax.ShapeDtypeStruct((1, aw), F32)] * 2 + lay.out_shapes,
        scratch_shapes=[pltpu.VMEM((hp, HEAD, HEAD), F32)] + lay.sem_shapes,
        compiler_params=_cparams(2),
    )(proj, proj, proj, proj, lb, nw, o_raw, states, scores, dab, cum, rev, masks, *lay.arrays)
    lay.deliver(res[6:])
    return res[:6]


def _lb_fwd(lb_param):
    def body(p_ref, o_ref):
        p = p_ref[...]
        e = jnp.exp(p - jnp.max(p, axis=0, keepdims=True))
        o_ref[...] = e[0:1] / jnp.sum(e, axis=0, keepdims=True)

    return pl.pallas_call(body, name="lb_fwd", out_shape=jax.ShapeDtypeStruct((1, lb_param.shape[1]), F32))(lb_param)


def _lb_bwd(lb_param, dlb):
    def body(p_ref, d_ref, o_ref):
        p = p_ref[...]
        e = jnp.exp(p - jnp.max(p, axis=0, keepdims=True))
        s = e / jnp.sum(e, axis=0, keepdims=True)
        first = (lax.broadcasted_iota(jnp.int32, p.shape, 0) == 0).astype(F32)
        o_ref[...] = d_ref[...] * s[0:1] * (first - s)

    return pl.pallas_call(body, name="lb_bwd", out_shape=jax.ShapeDtypeStruct(lb_param.shape, F32))(lb_param, dlb)


def _gmlp_norm(v, lnw, lnb):
    vf = _gelu(v)
    mu = jnp.mean(vf, axis=-1, keepdims=True)
    cen = vf - mu
    rstd = lax.rsqrt(jnp.mean(cen * cen, axis=-1, keepdims=True) + EPS)
    xh = cen * rstd
    return xh, rstd, xh * lnw + lnb


def _tril(n):
    return (lax.broadcasted_iota(jnp.int32, (n, n), 1) <= lax.broadcasted_iota(jnp.int32, (n, n), 0)).astype(F32)


def _gmlp_fwd(proj, lnw, lnb, w_sp, bs_t, n_groups, col_base):
    t = proj.shape[0]
    bw = n_groups * HEAD
    c = GMLP_CHUNK

    def body(u_ref, v_ref, lnw_ref, lnb_ref, w_ref, bs_ref, o_ref):
        tri = _tril(c)
        uf = _gelu(u_ref[...])
        _, _, vn = _gmlp_norm(v_ref[...], lnw_ref[...], lnb_ref[...])
        for g in range(n_groups):
            cols = slice(g * HEAD, (g + 1) * HEAD)
            z = _dot(w_ref[g] * tri, vn[:, cols]) + bs_ref[:, g:g + 1]
            o_ref[:, cols] = (uf[:, cols] * z).astype(BF16)

    blk = lambda b: pl.BlockSpec((c, bw), lambda n: (n, b))
    const = lambda a: pl.BlockSpec(a.shape, lambda n: (0,) * a.ndim)
    return pl.pallas_call(
        body, name="gmlp_fwd", grid=(t // c,),
        in_specs=[blk(col_base), blk(col_base + 1), const(lnw), const(lnb), const(w_sp), const(bs_t)],
        out_specs=pl.BlockSpec((c, bw), lambda n: (n, 0)),
        out_shape=jax.ShapeDtypeStruct((t, bw), BF16),
        compiler_params=_cparams(1),
    )(proj, proj, lnw, lnb, w_sp, bs_t)


def _gmlp_bwd(proj, lnw, lnb, w_sp, bs_t, dab, n_groups, col_base):
    t = proj.shape[0]
    bw = n_groups * HEAD
    c = GMLP_CHUNK
    n_steps = t // c
    sel = jnp.asarray((np.arange(bw)[:, None] // HEAD == np.arange(n_groups)[None, :]).astype(np.float32), BF16)

    def body(u_ref, v_ref, lnw_ref, lnb_ref, w_ref, bs_ref, d_ref, sel_ref,
             du_ref, dv_ref, dlnw_ref, dlnb_ref, dw_ref, dbs_ref, dz_acc, dvn_scr):
        step = pl.program_id(0)

        @pl.when(step == 0)
        def _():
            dlnw_ref[...] = jnp.zeros_like(dlnw_ref)
            dlnb_ref[...] = jnp.zeros_like(dlnb_ref)
            dw_ref[...] = jnp.zeros_like(dw_ref)
            dz_acc[...] = jnp.zeros_like(dz_acc)

        tri = _tril(c)
        u = u_ref[...]
        v = v_ref[...]
        uf = _gelu(u)
        lnw_v = lnw_ref[...]
        xh, rstd, vn = _gmlp_norm(v, lnw_v, lnb_ref[...])
        dbo = d_ref[...]
        dz = dbo * uf
        dz_acc[...] += dz
        for g in range(n_groups):
            cols = slice(g * HEAD, (g + 1) * HEAD)
            wg = w_ref[g] * tri
            z = _dot(wg, vn[:, cols]) + bs_ref[:, g:g + 1]
            du_ref[:, cols] = (dbo[:, cols] * z * _gelu_grad(u[:, cols])).astype(BF16)
            dvn_scr[:, cols] = _dot(wg, dz[:, cols], "tn")
            dw_ref[g] += tri * _dot(dz[:, cols], vn[:, cols], "nt")
        dvn = dvn_scr[...]
        dlnw_ref[...] += jnp.sum(dvn * xh, axis=0, keepdims=True)
        dlnb_ref[...] += jnp.sum(dvn, axis=0, keepdims=True)
        dxh = dvn * lnw_v
        dvf = rstd * (dxh - jnp.mean(dxh, axis=-1, keepdims=True) - xh * jnp.mean(dxh * xh, axis=-1, keepdims=True))
        dv_ref[...] = (dvf * _gelu_grad(v)).astype(BF16)

        @pl.when(step == n_steps - 1)
        def _():
            dbs_ref[...] = _dot_exact_r(dz_acc[...], sel_ref[...])

    blk = lambda b: pl.BlockSpec((c, bw), lambda n: (n, b))
    const = lambda a: pl.BlockSpec(a.shape, lambda n: (0,) * a.ndim)
    row = pl.BlockSpec((c, bw), lambda n: (n, 0))
    vec = pl.BlockSpec((1, bw), lambda n: (0, 0))
    return pl.pallas_call(
        body, name="gmlp_bwd", grid=(n_steps,),
        in_specs=[blk(col_base), blk(col_base + 1), const(lnw), const(lnb), const(w_sp), const(bs_t), blk(1), const(sel)],
        out_specs=[row, row, vec, vec, const(w_sp), const(bs_t)],
        out_shape=[jax.ShapeDtypeStruct((t, bw), BF16), jax.ShapeDtypeStruct((t, bw), BF16),
                   jax.ShapeDtypeStruct((1, bw), F32), jax.ShapeDtypeStruct((1, bw), F32),
                   jax.ShapeDtypeStruct(w_sp.shape, F32), jax.ShapeDtypeStruct(bs_t.shape, F32)],
        scratch_shapes=[pltpu.VMEM((c, bw), F32), pltpu.VMEM((c, bw), F32)],
        compiler_params=_cparams(1),
    )(proj, proj, lnw, lnb, w_sp, bs_t, dab, sel)


def _pair_sum(name, grad, other, core):
    _, _, r, c = grad.shape
    tr = _rows(r, 6 * c)

    def body(core_ref, g_ref, o_ref, out_ref):
        out_ref[...] = (g_ref[...].astype(F32) + o_ref[...].astype(F32)).astype(BF16)

    return pl.pallas_call(
        body, name=name,
        grid_spec=pltpu.PrefetchScalarGridSpec(
            num_scalar_prefetch=1, grid=(N_CHIP, r // tr),
            in_specs=[pl.BlockSpec((None, None, tr, c), lambda k, i, core_ref: (k, core_ref[0], i, 0)),
                      pl.BlockSpec((None, tr, c), lambda k, i, core_ref: (k, i, 0))],
            out_specs=pl.BlockSpec((None, tr, c), lambda k, i, core_ref: (k, i, 0))),
        out_shape=jax.ShapeDtypeStruct((N_CHIP, r, c), BF16),
        compiler_params=_cparams(2),
    )(core, grad, other)


def _adamw_math(w, g, m, v):
    m = ADAM_B1 * m + (1.0 - ADAM_B1) * g
    v = ADAM_B2 * v + (1.0 - ADAM_B2) * (g * g)
    m_hat = m / (1.0 - ADAM_B1 ** ADAM_STEP)
    v_hat = v / (1.0 - ADAM_B2 ** ADAM_STEP)
    delta = -ADAM_LR * (m_hat / (jnp.sqrt(v_hat) + ADAM_EPS) + ADAM_WD * w)
    return delta, m, v


def _adamw(name, parts, w, m, v):
    n_parts, r, c = parts.shape
    tr = _rows(r, c * (n_parts * parts.dtype.itemsize + 28), mult=8)

    def body(p_ref, w_ref, m_ref, v_ref, g_ref, d_ref, mo_ref, vo_ref):
        g = p_ref[0].astype(F32)
        for i in range(1, n_parts):
            g = g + p_ref[i].astype(F32)
        g_ref[...] = g
        d_ref[...], mo_ref[...], vo_ref[...] = _adamw_math(w_ref[...], g, m_ref[...], v_ref[...])

    row = pl.BlockSpec((tr, c), lambda i: (i, 0))
    return pl.pallas_call(
        body, name=name, grid=(r // tr,),
        in_specs=[pl.BlockSpec((n_parts, tr, c), lambda i: (0, i, 0)), row, row, row],
        out_specs=[row] * 4,
        out_shape=[jax.ShapeDtypeStruct((r, c), F32)] * 4,
        compiler_params=_cparams(1),
    )(parts, w, m, v)


def kernel(x, p, pre_mix_w, w_in, lb_param, a_norm_w, gmlp_ln_w, gmlp_ln_b, w_spatial, b_spatial, w_out, post_mix_w, pre_ffn_w, w_gate, w_up, w_down, post_ffn_w, w_ple, w_ple_gate, post_ple_w, loss_target, m_pre_mix_w, m_w_in, m_lb_param, m_a_norm_w, m_gmlp_ln_w, m_gmlp_ln_b, m_w_spatial, m_b_spatial, m_w_out, m_post_mix_w, m_pre_ffn_w, m_w_gate, m_w_up, m_w_down, m_post_ffn_w, m_w_ple, m_w_ple_gate, m_post_ple_w, v_pre_mix_w, v_w_in, v_lb_param, v_a_norm_w, v_gmlp_ln_w, v_gmlp_ln_b, v_w_spatial, v_b_spatial, v_w_out, v_post_mix_w, v_pre_ffn_w, v_w_gate, v_w_up, v_w_down, v_post_ffn_w, v_w_ple, v_w_ple_gate, v_post_ple_w):
    big_names = ["w_in", "w_out", "w_gate", "w_up", "w_down", "w_ple", "w_ple_gate"]
    small_names = ["pre_mix_w", "lb_param", "a_norm_w", "gmlp_ln_w", "gmlp_ln_b", "w_spatial", "b_spatial",
                   "post_mix_w", "pre_ffn_w", "post_ffn_w", "post_ple_w"]
    all_names = ["pre_mix_w", "w_in", "lb_param", "a_norm_w", "gmlp_ln_w", "gmlp_ln_b", "w_spatial", "b_spatial",
                 "w_out", "post_mix_w", "pre_ffn_w", "w_gate", "w_up", "w_down", "post_ffn_w", "w_ple", "w_ple_gate",
                 "post_ple_w"]
    env = dict(locals())
    W = {n: env[n] for n in all_names}
    M = {n: env["m_" + n] for n in all_names}
    V = {n: env["v_" + n] for n in all_names}

    xs = x[0]
    ps = p[0, 0]
    tgt = loss_target[0]
    t, d = xs.shape
    aw = a_norm_w.shape[1]
    bw = gmlp_ln_w.shape[1]
    n_heads, n_groups = aw // HEAD, bw // HEAD
    core = lax.axis_index("c").astype(jnp.int32).reshape(1)

    transposed = ("w_gate", "w_up")
    local = lambda a, n: jnp.swapaxes(a, 1, 2)[0] if n in transposed else a[0]
    unlocal = lambda a, n: jnp.swapaxes(a[None], 1, 2) if n in transposed else a[None]
    shard = {n: local(W[n], n) for n in big_names}
    bf = {n: _cast_bf16("cast_" + n, shard[n]) for n in big_names}
    ag_in = _ag_comm([bf["w_in"]])
    _comm_only("ag_w_in", [ag_in])
    win_g = ag_in.results[0]
    n_in = bf["w_in"].shape[1]
    ffl = bf["w_gate"].shape[0]
    n_ple = bf["w_ple"].shape[1]
    ple = ps.shape[1]

    TM, TK = 1024, 1024
    tm = _tile(t, TM)
    tn1 = _tile(d, 1024)

    h1 = _rms_fwd("rms_pre_mix", xs, pre_mix_w)
    once = pl.Buffered(1)
    ag_a = _ag_comm([bf["w_gate"]], mid_frac=0.97)
    proj = _matmul(
        "mm_proj",
        [(h1, pl.BlockSpec((tm, d), lambda n, m, k: (m, 0))),
         (win_g, pl.BlockSpec((None, d, n_in), lambda n, m, k: (n, 0, 0), pipeline_mode=once))],
        [(0, 1, "nn", 0)],
        [(jax.ShapeDtypeStruct((t, N_DEV * n_in), F32), pl.BlockSpec((tm, n_in), lambda n, m, k: (m, n)))],
        (N_DEV, t // tm, 1), (tm, n_in), comms=[ag_a])[0]
    wgate_g = ag_a.results[0]
    lb = _lb_fwd(lb_param)
    ag_b = _ag_comm([bf["w_out"]], mid_frac=0.6)
    a_out, o_raw, states, scores = _hgrn_fwd(proj, lb, a_norm_w, n_heads, comms=[ag_b])
    wout_f = ag_b.results[0].reshape(d, d)
    bs_t = b_spatial[0].T
    w_sp = w_spatial[0]
    col_u = (4 * aw) // bw
    b_out = _gmlp_fwd(proj, gmlp_ln_w, gmlp_ln_b, w_sp, bs_t, n_groups, col_u)
    ab = jnp.concatenate([a_out, b_out], axis=1)
    mix = _mm_plain("mm_mix", ab, wout_f, "nn", F32, TM, 1024, d)
    x1, h2 = _resid_rms("resid_mix", xs, mix, post_mix_w, pre_ffn_w)

    def swiglu(accs, gate_v):
        gf = gate_v.astype(F32)
        return accs[0], gf * _sigmoid(gf) * accs[0]

    tmf = _tile(t, 512)
    blk3 = lambda: pl.BlockSpec((None, tmf, ffl), lambda j, m, k: (j, m, 0))
    ag_c = _ag_comm([bf["w_up"]], mid_frac=0.97)
    gate = _matmul(
        "mm_ffn_gate",
        [(h2, pl.BlockSpec((tm, d), lambda j, m, k: (m, 0))),
         (wgate_g, pl.BlockSpec((None, ffl, d), lambda j, m, k: (j, 0, 0)))],
        [(0, 1, "nt", 0)],
        [(jax.ShapeDtypeStruct((N_DEV, t, ffl), BF16), pl.BlockSpec((None, tm, ffl), lambda j, m, k: (j, m, 0)))],
        (N_DEV, t // tm, 1), (tm, ffl), comms=[ag_c])[0]
    wup_g = ag_c.results[0]
    ag_d = _ag_comm([bf["w_down"]], mid_frac=0.97)
    up, act = _matmul(
        "mm_ffn_up",
        [(h2, pl.BlockSpec((tmf, d), lambda j, m, k: (m, 0))),
         (wup_g, pl.BlockSpec((None, ffl, d), lambda j, m, k: (j, 0, 0))),
         (gate, blk3())],
        [(0, 1, "nt", 0)],
        [(jax.ShapeDtypeStruct((N_DEV, t, ffl), BF16), blk3()) for _ in range(2)],
        (N_DEV, t // tmf, 1), (tmf, ffl), epilogue=swiglu, comms=[ag_d])
    wdown_g = ag_d.results[0]
    ag_e = _ag_comm([bf["w_ple_gate"], bf["w_ple"]], mid_frac=0.6)
    tn_d = _tile(d, 2048)
    ff = _matmul(
        "mm_ffn_down",
        [(act, pl.BlockSpec((None, tm, ffl), lambda m, n, k: (k, m, 0))),
         (wdown_g, pl.BlockSpec((None, ffl, tn_d), lambda m, n, k: (k, 0, n)))],
        [(0, 1, "nn", 0)],
        [(jax.ShapeDtypeStruct((t, d), F32), pl.BlockSpec((tm, tn_d), lambda m, n, k: (m, n)))],
        (t // tm, d // tn_d, N_DEV), (tm, tn_d), comms=[ag_e])[0]
    wpg_f = ag_e.results[0].reshape(d, d)
    wple_g = ag_e.results[1]
    x2, x2b = _resid_rms("resid_ffn", x1, ff, post_ffn_w, None)

    pgl = _mm_plain("mm_ple_gate", x2b, wpg_f, "nn", F32, TM, 1024, d)
    pe = _matmul(
        "mm_ple",
        [(ps, pl.BlockSpec((tm, ple), lambda m, n, k: (m, 0))), (wple_g, pl.BlockSpec((None, ple, n_ple), lambda m, n, k: (n, 0, 0)))],
        [(0, 1, "nn", 0)],
        [(jax.ShapeDtypeStruct((t, N_DEV * n_ple), F32), pl.BlockSpec((tm, n_ple), lambda m, n, k: (m, n)))],
        (t // tm, N_DEV, 1), (tm, n_ple))[0]
    loss_part, d3, dpe, dpgl, g_post_ple = _ple_loss("ple_loss", x2, pe, pgl, post_ple_w, tgt)

    tkt = _tile(t, TK)
    g_wple = _matmul(
        "mm_dw_ple",
        [(ps, pl.BlockSpec((tkt, ple), lambda n, k: (k, 0))), (dpe, pl.BlockSpec((tkt, n_ple), lambda n, k: (k, n)))],
        [(0, 1, "tn", 0)],
        [(jax.ShapeDtypeStruct((N_DEV, ple, n_ple), BF16), pl.BlockSpec((None, ple, n_ple), lambda n, k: (n, 0, 0)))],
        (N_DEV, t // tkt), (ple, n_ple))[0]
    g_wpg = _mm_plain("mm_dw_ple_gate", x2b, dpgl, "tn", BF16, TM, 1024, t)

    def by_chip(g):
        return g.reshape((N_CHIP, 2) + g.shape[-2:])

    def pair_sums(names, comm):
        return [_pair_sum("pair_sum_" + n, g, o, core) for n, g, o in zip(names, comm.arrays, comm.results)]

    r1_p = _pair_comm([by_chip(g_wpg.reshape(N_DEV, d // N_DEV, d)), by_chip(g_wple)])
    d2 = _mm_plain("mm_d_x2", dpgl, wpg_f, "nt", F32, TM, 512, d, extra=d3, epilogue=lambda accs, e: [accs[0] + e],
                   comms=[r1_p])
    r2_p = _chip_comm(pair_sums(["w_ple_gate", "w_ple"], r1_p))

    dff, g_post_ffn = _norm_bwd("norm_bwd_ffn", d2, ff, post_ffn_w)
    g_wdown = _matmul(
        "mm_dw_down",
        [(act, pl.BlockSpec((None, t, ffl), lambda j, n, k: (j, 0, 0), pipeline_mode=once)),
         (dff, pl.BlockSpec((t, tn1), lambda j, n, k: (0, n)))],
        [(0, 1, "tn", 0)],
        [(jax.ShapeDtypeStruct((N_DEV, ffl, d), BF16), pl.BlockSpec((None, ffl, tn1), lambda j, n, k: (j, 0, n)))],
        (N_DEV, d // tn1, 1), (ffl, tn1), comms=[r2_p])[0]
    r1_d = _pair_comm([by_chip(g_wdown)])

    def swiglu_bwd(accs, gate_v, up_v):
        dact = accs[0]
        gf = gate_v.astype(F32)
        sg = _sigmoid(gf)
        return dact * up_v.astype(F32) * (sg * (1.0 + gf * (1.0 - sg))), dact * (gf * sg)

    dgate, dup = _matmul(
        "mm_d_act",
        [(dff, pl.BlockSpec((tmf, d), lambda j, m, k: (m, 0))),
         (wdown_g, pl.BlockSpec((None, ffl, d), lambda j, m, k: (j, 0, 0))),
         (gate, blk3()), (up, blk3())],
        [(0, 1, "nt", 0)],
        [(jax.ShapeDtypeStruct((N_DEV, t, ffl), BF16), blk3()) for _ in range(2)],
        (N_DEV, t // tmf, 1), (tmf, ffl), epilogue=swiglu_bwd, comms=[r1_d])
    r2_d = _chip_comm(pair_sums(["w_down"], r1_d))
    tmd = _tile(d, TM)
    def dw_ffn(name, dy, comms):
        return _matmul(
            name,
            [(dy, pl.BlockSpec((None, t, ffl), lambda j, n, k: (j, 0, 0), pipeline_mode=once)),
             (h2, pl.BlockSpec((t, tn1), lambda j, n, k: (0, n)))],
            [(0, 1, "tn", 0)],
            [(jax.ShapeDtypeStruct((N_DEV, ffl, d), BF16), pl.BlockSpec((None, ffl, tn1), lambda j, n, k: (j, 0, n)))],
            (N_DEV, d // tn1, 1), (ffl, tn1), comms=comms)[0]

    g_wgate = dw_ffn("mm_dw_gate", dgate, [r2_d])
    r1_g = _pair_comm([by_chip(g_wgate)])
    g_wup = dw_ffn("mm_dw_up", dup, [r1_g])
    r2_g = _chip_comm(pair_sums(["w_gate"], r1_g))
    r1_u = _pair_comm([by_chip(g_wup)])
    tn1 = _tile(d, 1024)
    dh2 = _matmul(
        "mm_d_h2",
        [(dgate, pl.BlockSpec((None, tm, ffl), lambda m, n, k: (k, m, 0))),
         (wgate_g, pl.BlockSpec((None, ffl, tn1), lambda m, n, k: (k, 0, n))),
         (dup, pl.BlockSpec((None, tm, ffl), lambda m, n, k: (k, m, 0))),
         (wup_g, pl.BlockSpec((None, ffl, tn1), lambda m, n, k: (k, 0, n)))],
        [(0, 1, "nn", 0), (2, 3, "nn", 0)],
        [(jax.ShapeDtypeStruct((t, d), F32), pl.BlockSpec((tm, tn1), lambda m, n, k: (m, n)))],
        (t // tm, d // tn1, N_DEV), (tm, tn1), comms=[r2_g, r1_u])[0]
    r2_u = _chip_comm(pair_sums(["w_up"], r1_u))
    d1, g_pre_ffn, dmix, g_post_mix = _prenorm_bwd("prenorm_bwd_ffn", d2, dh2, x1, pre_ffn_w, mix, post_mix_w)

    g_wout = _mm_plain("mm_dw_out", ab, dmix, "tn", BF16, TM, 1024, t)
    r1_o = _pair_comm([by_chip(g_wout.reshape(N_DEV, d // N_DEV, d))])
    dab = _mm_plain("mm_d_ab", dmix, wout_f, "nt", F32, TM, 1024, d, comms=[r1_o])
    r2_o = _chip_comm(pair_sums(["w_out"], r1_o))
    dq, df, di, dg, dlb, g_a_norm = _hgrn_bwd(proj, lb, a_norm_w, o_raw, states, scores, dab, n_heads, comms=[r2_u])
    du, dv, g_ln_w, g_ln_b, g_wsp, g_bs_t = _gmlp_bwd(proj, gmlp_ln_w, gmlp_ln_b, w_sp, bs_t, dab, n_groups, col_u)
    dproj = jnp.concatenate([dq, df, di, dg, du, dv], axis=1)
    small_grad = {
        "lb_param": _lb_bwd(lb_param, dlb), "a_norm_w": g_a_norm, "gmlp_ln_w": g_ln_w,
        "gmlp_ln_b": g_ln_b, "w_spatial": g_wsp, "b_spatial": g_bs_t.T, "post_mix_w": g_post_mix,
        "pre_ffn_w": g_pre_ffn, "post_ffn_w": g_post_ffn, "post_ple_w": g_post_ple,
    }
    assert small_names[0] == "pre_mix_w"
    pack = lambda get, names: jnp.concatenate([get(n).reshape(-1, LANE) for n in names], axis=0)
    ag_main = _ag_comm([pack(lambda n: small_grad[n], small_names[1:])], mid_frac=0.5)
    g_win = _matmul(
        "mm_dw_in",
        [(h1, pl.BlockSpec((t, tmd), lambda j, m, k: (0, m))),
         (dproj, pl.BlockSpec((t, n_in), lambda j, m, k: (0, j)))],
        [(0, 1, "tn", 0)],
        [(jax.ShapeDtypeStruct((N_DEV, d, n_in), BF16), pl.BlockSpec((None, tmd, n_in), lambda j, m, k: (j, m, 0)))],
        (N_DEV, d // tmd, 1), (tmd, n_in), comms=[ag_main, r2_o])[0]
    r1_in = _pair_comm([by_chip(g_win)])
    _comm_only("rs_pair_w_in", [r1_in])
    r2_in = _chip_comm(pair_sums(["w_in"], r1_in))
    dh1 = _matmul(
        "mm_d_h1",
        [(dproj, pl.BlockSpec((tm, n_in), lambda m, n, k: (m, k))), (win_g, pl.BlockSpec((None, tn1, n_in), lambda m, n, k: (k, n, 0)))],
        [(0, 1, "nt", 0)],
        [(jax.ShapeDtypeStruct((t, d), F32), pl.BlockSpec((tm, tn1), lambda m, n, k: (m, n)))],
        (t // tm, d // tn1, N_DEV), (tm, tn1), comms=[r2_in])[0]
    grad_x, g_pre_mix = _prenorm_bwd("prenorm_bwd_mix", d1, dh1, xs, pre_mix_w)
    ag_pre = _ag_comm([g_pre_mix.reshape(-1, LANE)])
    _comm_only("ag_small_pre", [ag_pre], in_vmem=True)
    g_all = jnp.concatenate([ag_pre.results[0], ag_main.results[0]], axis=1)

    reduced = {
        "w_in": r2_in.results[0], "w_out": r2_o.results[0], "w_gate": r2_g.results[0], "w_up": r2_u.results[0],
        "w_down": r2_d.results[0], "w_ple": r2_p.results[1], "w_ple_gate": r2_p.results[0],
    }
    grads, deltas, new_m, new_v = {}, {}, {}, {}
    for n in big_names:
        res = _adamw("adamw_" + n, reduced[n], shard[n], local(M[n], n), local(V[n], n))
        grads[n], deltas[n], new_m[n], new_v[n] = (unlocal(a, n) for a in res)
    sg, sd, sm, sv = _adamw("adamw_small", g_all, pack(lambda n: W[n], small_names), pack(lambda n: M[n], small_names),
                            pack(lambda n: V[n], small_names))
    off = 0
    for n in small_names:
        rows = W[n].size // LANE
        for src, dst in ((sg, grads), (sd, deltas), (sm, new_m), (sv, new_v)):
            dst[n] = src[off:off + rows].reshape(W[n].shape)
        off += rows

    loss = lax.psum(loss_part[0, 0], ("x", "y", "c"))
    return (loss, grad_x[None], *[grads[n] for n in all_names], *[deltas[n] for n in all_names],
            *[new_m[n] for n in all_names], *[new_v[n] for n in all_names])
```
